```python
import jax, jax.numpy as jnp
from jax import lax
import numpy as np

D_MODEL = 1024
BATCH = 8
SEQ = 4096
DEPTH = 2

GRID_W = 64
CTX_LEN = 256
EPS = 1e-6
N_Q_HEADS = 8
N_KV_HEADS = 2
HEAD_DIM = 64
Q_PER_KV = N_Q_HEADS // N_KV_HEADS
ATTN_WIDTH = N_Q_HEADS * HEAD_DIM
KV_WIDTH = N_KV_HEADS * HEAD_DIM
Q_BLOCK = 128
ROPE_THETA = 10000.0
POOL_WINDOWS = (2, 4, 8, 16)
POOL_GROUP = 128
POOL_WIDTH = POOL_GROUP * len(POOL_WINDOWS)
SG_GROUPS = 4
SG_GROUP_WIDTH = 128
SG_CHUNK = 128
SG_WIDTH = SG_GROUPS * SG_GROUP_WIDTH
CONV_WIDTH = 512
CONV_K = 3
EVEN_IN = ATTN_WIDTH + 2 * KV_WIDTH + POOL_WIDTH
ODD_IN = 2 * SG_WIDTH + 3 * CONV_WIDTH
MIX_WIDTH = ATTN_WIDTH + POOL_WIDTH
N_GROUPS = 4
EXPERTS_PER_GROUP = 4
N_EXPERTS = N_GROUPS * EXPERTS_PER_GROUP
TOP_K = 2
D_EXPERT = 256

kernel_name = 'hybrid_diffusion_attn_pool_sgmlp_conv_hmoe'


def rms_norm(x, gain):
    x32 = x.astype(jnp.float32)
    y = x32 * lax.rsqrt(jnp.mean(x32 * x32, axis=-1, keepdims=True) + EPS)
    return (y * gain.astype(jnp.float32)).astype(x.dtype)


def modulate(x, gain, shift, scale):
    return rms_norm(x, gain) * (1 + scale) + shift


def adaln(cond, mod_w, mod_b):
    m = jax.nn.silu(cond) @ mod_w + mod_b
    return jnp.split(m, 6, axis=-1)


def axial_angles(n_rows):
    row = jnp.repeat(jnp.arange(n_rows, dtype=jnp.float32), GRID_W)
    col = jnp.tile(jnp.arange(GRID_W, dtype=jnp.float32), n_rows)
    half = HEAD_DIM // 2
    inv = ROPE_THETA ** (-jnp.arange(0, half, 2, dtype=jnp.float32) / half)
    return row[:, None] * inv, col[:, None] * inv


def rope_1d(x, ang):
    cos = jnp.cos(ang)[None, :, None, :]
    sin = jnp.sin(ang)[None, :, None, :]
    x1, x2 = jnp.split(x, 2, axis=-1)
    return jnp.concatenate([x1 * cos - x2 * sin, x2 * cos + x1 * sin], axis=-1)


def rope_2d(x, row_ang, col_ang):
    x32 = x.astype(jnp.float32)
    xr, xc = jnp.split(x32, 2, axis=-1)
    return jnp.concatenate([rope_1d(xr, row_ang), rope_1d(xc, col_ang)], axis=-1).astype(x.dtype)


def gqa_attend(q, k, v):
    s = jnp.einsum('bqkgd,bskd->bkgqs', q, k, preferred_element_type=jnp.float32) * (HEAD_DIM ** -0.5)
    p = jax.nn.softmax(s, axis=-1).astype(v.dtype)
    return jnp.einsum('bkgqs,bskd->bqkgd', p, v)


def multiscale_pool(p, pool_w, pool_scale):
    bsz, length, _ = p.shape
    p32 = p.astype(jnp.float32)
    cs = jnp.concatenate([jnp.zeros_like(p32[:, :1]), jnp.cumsum(p32, axis=1)], axis=1)
    t = jnp.arange(length)
    diffs = []
    for g, w in enumerate(POOL_WINDOWS):
        lo = jnp.clip(t - w // 2, 0, length)
        hi = jnp.clip(t + w - w // 2, 0, length)
        sl = slice(g * POOL_GROUP, (g + 1) * POOL_GROUP)
        csg = cs[..., sl]
        mean = (csg[:, hi] - csg[:, lo]) / (hi - lo).astype(jnp.float32)[None, :, None]
        diffs.append(mean - p32[..., sl])
    d = jnp.stack(diffs, axis=2).astype(p.dtype)
    y = jnp.einsum('blgc,gcd->blgd', d, pool_w).reshape(bsz, length, POOL_WIDTH)
    return y * pool_scale


def even_mixer(h, hc, w_in, q_gain, k_gain, pool_w, pool_scale, w_out, row_ang, col_ang, need_ctx_out):
    bsz, seq, _ = h.shape
    n_ctx = hc.shape[1]
    q, k, v, p = jnp.split(h @ w_in, [ATTN_WIDTH, ATTN_WIDTH + KV_WIDTH, ATTN_WIDTH + 2 * KV_WIDTH], axis=-1)
    q = rope_2d(rms_norm(q.reshape(bsz, seq, N_Q_HEADS, HEAD_DIM), q_gain), row_ang, col_ang)
    k = rope_2d(rms_norm(k.reshape(bsz, seq, N_KV_HEADS, HEAD_DIM), k_gain), row_ang, col_ang)
    v = v.reshape(bsz, seq, N_KV_HEADS, HEAD_DIM)
    if need_ctx_out:
        qc, kc, vc, pc = jnp.split(hc @ w_in, [ATTN_WIDTH, ATTN_WIDTH + KV_WIDTH, ATTN_WIDTH + 2 * KV_WIDTH], axis=-1)
    else:
        kc, vc = jnp.split(hc @ w_in[:, ATTN_WIDTH:ATTN_WIDTH + 2 * KV_WIDTH], [KV_WIDTH], axis=-1)
    kc = rms_norm(kc.reshape(bsz, n_ctx, N_KV_HEADS, HEAD_DIM), k_gain)
    vc = vc.reshape(bsz, n_ctx, N_KV_HEADS, HEAD_DIM)
    k_all = jnp.concatenate([kc, k], axis=1)
    v_all = jnp.concatenate([vc, v], axis=1)
    n_blk = seq // Q_BLOCK
    qb = q.reshape(bsz, n_blk, Q_BLOCK, N_KV_HEADS, Q_PER_KV, HEAD_DIM).transpose(1, 0, 2, 3, 4, 5)
    o = lax.map(lambda qq: gqa_attend(qq, k_all, v_all), qb)
    o = o.transpose(1, 0, 2, 3, 4, 5).reshape(bsz, seq, ATTN_WIDTH)
    y = jnp.concatenate([o, multiscale_pool(p, pool_w, pool_scale)], axis=-1) @ w_out
    if need_ctx_out:
        qc = rms_norm(qc.reshape(bsz, n_ctx, N_Q_HEADS, HEAD_DIM), q_gain).reshape(bsz, n_ctx, N_KV_HEADS, Q_PER_KV, HEAD_DIM)
        oc = gqa_attend(qc, kc, vc).reshape(bsz, n_ctx, ATTN_WIDTH)
        yc = jnp.concatenate([oc, multiscale_pool(pc, pool_w, pool_scale)], axis=-1) @ w_out
    else:
        yc = None
    return y, yc


def odd_mixer(h, w_in, sg_gain, sg_w, sg_b, conv_w, w_out):
    bsz, length, _ = h.shape
    u, v, hx, bg, cg = jnp.split(h @ w_in, [SG_WIDTH, 2 * SG_WIDTH, 2 * SG_WIDTH + CONV_WIDTH, 2 * SG_WIDTH + 2 * CONV_WIDTH], axis=-1)
    v = rms_norm(v.reshape(bsz, length // SG_CHUNK, SG_CHUNK, SG_GROUPS, SG_GROUP_WIDTH), sg_gain)
    s = jnp.einsum('gpq,bnqgc->bnpgc', sg_w, v) + sg_b.T[:, :, None]
    y_c = u * s.reshape(bsz, length, SG_WIDTH)
    z = cg * hx
    zc = lax.conv_general_dilated(z, conv_w, window_strides=(1,), padding=((CONV_K // 2, CONV_K // 2),),
                                  dimension_numbers=('NWC', 'WIO', 'NWC'), feature_group_count=CONV_WIDTH)
    y_d = bg * zc
    return jnp.concatenate([y_c, y_d], axis=-1) @ w_out


def hier_moe(h, rg_w, rg_b, re_w, re_b, w_gate, w_up, w_down):
    shape = h.shape
    t = h.reshape(-1, shape[-1])
    t32 = t.astype(jnp.float32)
    n_tok = t.shape[0]
    g_prob = jax.nn.softmax(t32 @ rg_w.astype(jnp.float32) + rg_b.astype(jnp.float32), axis=-1)
    g_p, g_idx = lax.top_k(g_prob, 1)
    e_logits = (t32 @ re_w.astype(jnp.float32) + re_b.astype(jnp.float32)).reshape(n_tok, N_GROUPS, EXPERTS_PER_GROUP)
    e_sel = jnp.take_along_axis(e_logits, g_idx[:, :, None], axis=1)[:, 0]
    top_p, top_i = lax.top_k(jax.nn.softmax(e_sel, axis=-1), TOP_K)
    top_p = top_p / jnp.sum(top_p, axis=-1, keepdims=True)
    weights = g_p * top_p
    expert_id = g_idx * EXPERTS_PER_GROUP + top_i
    comb = jnp.sum(jax.nn.one_hot(expert_id, N_EXPERTS, dtype=jnp.float32) * weights[..., None], axis=1)

    def step(acc, xs):
        wg, wu, wd, cw = xs
        y = (jax.nn.silu(t @ wg) * (t @ wu)) @ wd
        return acc + y.astype(jnp.float32) * cw[:, None], None

    acc, _ = lax.scan(step, jnp.zeros(t.shape, jnp.float32), (w_gate, w_up, w_down, comb.T))
    return acc.astype(h.dtype).reshape(shape)


def setup_inputs(seed: int = 0) -> dict:
    key = jax.random.key(seed)
    ks = jax.random.split(key, 32)
    n_even = (DEPTH + 1) // 2
    n_odd = DEPTH // 2
    f32 = jnp.float32

    def nrm(k, shape, scale):
        return jax.random.normal(k, shape, f32) * scale

    def gain(k, shape, noise=0.05):
        return 1.0 + nrm(k, shape, noise)

    return {
        'x': nrm(ks[0], (BATCH, SEQ, D_MODEL), 1.0),
        'c': nrm(ks[1], (BATCH, D_MODEL), 1.0),
        'ctx': nrm(ks[2], (BATCH, CTX_LEN, D_MODEL), 1.0),
        'c_ctx': nrm(ks[3], (D_MODEL,), 1.0),
        'mod_w': nrm(ks[4], (DEPTH, D_MODEL, 6 * D_MODEL), 0.5 * D_MODEL ** -0.5),
        'mod_b': nrm(ks[5], (DEPTH, 6 * D_MODEL), 0.02),
        'norm1_g': gain(ks[6], (DEPTH, D_MODEL)),
        'norm2_g': gain(ks[7], (DEPTH, D_MODEL)),
        'even_w_in': nrm(ks[8], (n_even, D_MODEL, EVEN_IN), D_MODEL ** -0.5),
        'q_gain': gain(ks[9], (n_even, HEAD_DIM)),
        'k_gain': gain(ks[10], (n_even, HEAD_DIM)),
        'pool_w': nrm(ks[11], (n_even, len(POOL_WINDOWS), POOL_GROUP, POOL_GROUP), POOL_GROUP ** -0.5),
        'pool_scale': gain(ks[12], (n_even, POOL_WIDTH), 0.1),
        'even_w_out': nrm(ks[13], (n_even, MIX_WIDTH, D_MODEL), MIX_WIDTH ** -0.5),
        'odd_w_in': nrm(ks[14], (n_odd, D_MODEL, ODD_IN), D_MODEL ** -0.5),
        'sg_gain': gain(ks[15], (n_odd, SG_GROUPS, SG_GROUP_WIDTH)),
        'sg_w': nrm(ks[16], (n_odd, SG_GROUPS, SG_CHUNK, SG_CHUNK), SG_CHUNK ** -0.5),
        'sg_b': gain(ks[17], (n_odd, SG_GROUPS, SG_CHUNK), 0.1),
        'conv_w': nrm(ks[18], (n_odd, CONV_K, 1, CONV_WIDTH), CONV_K ** -0.5),
        'odd_w_out': nrm(ks[19], (n_odd, MIX_WIDTH, D_MODEL), MIX_WIDTH ** -0.5),
        'router_g_w': nrm(ks[20], (DEPTH, D_MODEL, N_GROUPS), D_MODEL ** -0.5),
        'router_g_b': nrm(ks[21], (DEPTH, N_GROUPS), 0.01),
        'router_e_w': nrm(ks[22], (DEPTH, D_MODEL, N_EXPERTS), D_MODEL ** -0.5),
        'router_e_b': nrm(ks[23], (DEPTH, N_EXPERTS), 0.01),
        'w_gate': nrm(ks[24], (DEPTH, N_EXPERTS, D_MODEL, D_EXPERT), D_MODEL ** -0.5),
        'w_up': nrm(ks[25], (DEPTH, N_EXPERTS, D_MODEL, D_EXPERT), D_MODEL ** -0.5),
        'w_down': nrm(ks[26], (DEPTH, N_EXPERTS, D_EXPERT, D_MODEL), D_EXPERT ** -0.5),
    }


def reference(x, c, ctx, c_ctx, mod_w, mod_b, norm1_g, norm2_g, even_w_in, q_gain, k_gain, pool_w, pool_scale,
              even_w_out, odd_w_in, sg_gain, sg_w, sg_b, conv_w, odd_w_out, router_g_w, router_g_b,
              router_e_w, router_e_b, w_gate, w_up, w_down):
    seq = x.shape[1]
    ROWS = seq // GRID_W
    row_ang, col_ang = axial_angles(ROWS)
    last_even = 2 * ((DEPTH - 1) // 2)
    h_lat, h_ctx = x, ctx
    for i in range(DEPTH):
        j = i // 2
        need_ctx_in = i <= last_even
        need_ctx_out = i < last_even
        sh1, sc1, g1, sh2, sc2, g2 = [m[:, None, :] for m in adaln(c, mod_w[i], mod_b[i])]
        a = modulate(h_lat, norm1_g[i], sh1, sc1)
        if need_ctx_in:
            csh1, csc1, cg1, csh2, csc2, cg2 = adaln(c_ctx, mod_w[i], mod_b[i])
            ac = modulate(h_ctx, norm1_g[i], csh1, csc1)
        if i % 2 == 0:
            y, yc = even_mixer(a, ac, even_w_in[j], q_gain[j], k_gain[j], pool_w[j], pool_scale[j],
                               even_w_out[j], row_ang, col_ang, need_ctx_out)
        else:
            y = odd_mixer(a, odd_w_in[j], sg_gain[j], sg_w[j], sg_b[j], conv_w[j], odd_w_out[j])
            yc = odd_mixer(ac, odd_w_in[j], sg_gain[j], sg_w[j], sg_b[j], conv_w[j], odd_w_out[j]) if need_ctx_out else None
        moe_args = (router_g_w[i], router_g_b[i], router_e_w[i], router_e_b[i], w_gate[i], w_up[i], w_down[i])
        h_lat = h_lat + g1 * y
        h_lat = h_lat + g2 * hier_moe(modulate(h_lat, norm2_g[i], sh2, sc2), *moe_args)
        if need_ctx_out:
            h_ctx = h_ctx + cg1 * yc
            h_ctx = h_ctx + cg2 * hier_moe(modulate(h_ctx, norm2_g[i], csh2, csc2), *moe_args)
    return h_lat
```

```python
import functools

import jax
import jax.numpy as jnp
from jax import lax
from jax.experimental import pallas as pl
from jax.experimental.pallas import tpu as pltpu

F32 = jnp.float32
BF16 = jnp.bfloat16

D_MODEL = 1024
GRID_W = 64
EPS = 1e-6
N_Q_HEADS = 8
N_KV_HEADS = 2
HEAD_DIM = 64
Q_PER_KV = N_Q_HEADS // N_KV_HEADS
ATTN_WIDTH = N_Q_HEADS * HEAD_DIM
KV_WIDTH = N_KV_HEADS * HEAD_DIM
ROPE_THETA = 10000.0
POOL_WINDOWS = (2, 4, 8, 16)
POOL_GROUP = 128
POOL_WIDTH = POOL_GROUP * len(POOL_WINDOWS)
SG_GROUPS = 4
SG_CHUNK = 128
SG_WIDTH = 512
CONV_WIDTH = 512
EVEN_IN = ATTN_WIDTH + 2 * KV_WIDTH + POOL_WIDTH
ODD_IN = 2 * SG_WIDTH + 3 * CONV_WIDTH
N_GROUPS = 4
EXPERTS_PER_GROUP = 4
N_EXPERTS = 16
D_EXPERT = 256

LANES = 128
HALO = 8
ROUTER_LANES = 128
MOD_ROWS = 16
VMEM_LIMIT = 48 * 1024 * 1024


def _params(sem):
    return pltpu.CompilerParams(dimension_semantics=sem, vmem_limit_bytes=VMEM_LIMIT)


def _modulate(x, gain, shift, scale):
    ms = jnp.mean(x * x, axis=-1, keepdims=True)
    return (x * lax.rsqrt(ms + EPS) * gain) * (1.0 + scale) + shift


def _mod_spec(layer, row_fn, which):
    return pl.BlockSpec((None, None, None, 1, D_MODEL),
                        lambda *idx: (layer, row_fn(*idx), which, 0, 0))


def _full(shape):
    return pl.BlockSpec(shape, lambda *idx: (0,) * len(shape))


def _adaln_kernel(c_ref, w_ref, b_ref, o_ref):
    c = c_ref[...]
    s = c * jax.nn.sigmoid(c)
    o_ref[0] = jnp.dot(s, w_ref[0], precision=lax.Precision.HIGHEST,
                       preferred_element_type=F32) + b_ref[0]


def _adaln(cond, mod_w, mod_b):
    depth, d, n = mod_w.shape
    tn = 1024
    return pl.pallas_call(
        _adaln_kernel,
        grid=(depth, n // tn),
        in_specs=[_full((MOD_ROWS, d)),
                  pl.BlockSpec((1, d, tn), lambda l, j: (l, 0, j)),
                  pl.BlockSpec((1, 1, tn), lambda l, j: (l, 0, j))],
        out_specs=pl.BlockSpec((1, MOD_ROWS, tn), lambda l, j: (l, 0, j)),
        out_shape=jax.ShapeDtypeStruct((depth, MOD_ROWS, n), F32),
        compiler_params=_params(("arbitrary", "arbitrary")),
        name="adaln",
    )(cond, mod_w, mod_b.reshape(depth, 1, n))


def _head_norm_rope(z, gain, ones_bd, cos, sin, first_half):
    sq = z * z
    hi = sq.astype(BF16)
    lo = (sq - hi.astype(F32)).astype(BF16)
    ms = (jnp.dot(hi, ones_bd, preferred_element_type=F32)
          + jnp.dot(lo, ones_bd, preferred_element_type=F32))
    zn = z * lax.rsqrt(ms + EPS) * gain
    partner = jnp.where(first_half, pltpu.roll(zn, LANES - 16, 1), pltpu.roll(zn, 16, 1))
    return zn * cos + partner * sin


def _inproj0_kernel(x_ref, gain_ref, sh_ref, sc_ref, w_ref, cos_ref, sin_ref, qg_ref, kg_ref, ones_ref,
                    q_ref, k_ref, v_ref, p_ref):
    a = _modulate(x_ref[0], gain_ref[...], sh_ref[...], sc_ref[...])
    y = jnp.dot(a.astype(BF16), w_ref[...], preferred_element_type=F32)
    cos, sin, ones_bd = cos_ref[...], sin_ref[...], ones_ref[...]
    lane = lax.broadcasted_iota(jnp.int32, cos.shape, 1)
    first_half = (lane % 32) < 16
    for s in range(ATTN_WIDTH // LANES):
        r = _head_norm_rope(y[:, s * LANES:(s + 1) * LANES], qg_ref[...], ones_bd, cos, sin, first_half)
        r = (r * (HEAD_DIM ** -0.5)).astype(BF16)
        q_ref[0, 2 * s] = r[:, :HEAD_DIM]
        q_ref[0, 2 * s + 1] = r[:, HEAD_DIM:]
    kr = _head_norm_rope(y[:, ATTN_WIDTH:ATTN_WIDTH + KV_WIDTH], kg_ref[...], ones_bd, cos, sin,
                         first_half).astype(BF16)
    k_ref[0, 0] = kr[:, :HEAD_DIM]
    k_ref[0, 1] = kr[:, HEAD_DIM:]
    vv = y[:, ATTN_WIDTH + KV_WIDTH:ATTN_WIDTH + 2 * KV_WIDTH].astype(BF16)
    v_ref[0, 0] = vv[:, :HEAD_DIM]
    v_ref[0, 1] = vv[:, HEAD_DIM:]
    p_ref[0] = y[:, ATTN_WIDTH + 2 * KV_WIDTH:]


def _inproj0_ctx_kernel(x_ref, gain_ref, sh_ref, sc_ref, w_ref, kg_ref, ones_ref, k_ref, v_ref):
    a = _modulate(x_ref[0], gain_ref[...], sh_ref[...], sc_ref[...])
    y = jnp.dot(a.astype(BF16), w_ref[...], preferred_element_type=F32)
    z = y[:, :KV_WIDTH]
    sq = z * z
    hi = sq.astype(BF16)
    lo = (sq - hi.astype(F32)).astype(BF16)
    ms = (jnp.dot(hi, ones_ref[...], preferred_element_type=F32)
          + jnp.dot(lo, ones_ref[...], preferred_element_type=F32))
    kr = (z * lax.rsqrt(ms + EPS) * kg_ref[...]).astype(BF16)
    k_ref[0, 0] = kr[:, :HEAD_DIM]
    k_ref[0, 1] = kr[:, HEAD_DIM:]
    vv = y[:, KV_WIDTH:].astype(BF16)
    v_ref[0, 0] = vv[:, :HEAD_DIM]
    v_ref[0, 1] = vv[:, HEAD_DIM:]


def _rope_tables(seq):
    t = jnp.arange(seq)
    row = (t // GRID_W).astype(F32)
    col = (t % GRID_W).astype(F32)
    half = HEAD_DIM // 2
    inv = ROPE_THETA ** (-jnp.arange(0, half, 2, dtype=F32) / half)
    ar, ac = row[:, None] * inv, col[:, None] * inv
    cos = jnp.concatenate([jnp.cos(ar), jnp.cos(ar), jnp.cos(ac), jnp.cos(ac)], axis=-1)
    sin = jnp.concatenate([-jnp.sin(ar), jnp.sin(ar), -jnp.sin(ac), jnp.sin(ac)], axis=-1)
    return jnp.tile(cos, (1, LANES // HEAD_DIM)), jnp.tile(sin, (1, LANES // HEAD_DIM))


def _head_mean_matrix():
    r = jnp.arange(LANES)
    same = (r[:, None] // HEAD_DIM) == (r[None, :] // HEAD_DIM)
    return jnp.where(same, 1.0 / HEAD_DIM, 0.0).astype(BF16)


def _inproj0(x, mods, gain, w_in, q_gain, k_gain, tm):
    bsz, seq, d = x.shape
    cos, sin = _rope_tables(seq)
    qg = jnp.tile(q_gain, LANES // HEAD_DIM).reshape(1, LANES)
    kg = jnp.tile(k_gain, LANES // HEAD_DIM).reshape(1, LANES)
    head = lambda n: pl.BlockSpec((1, n, tm, HEAD_DIM), lambda b, i: (b, 0, i, 0))
    return pl.pallas_call(
        _inproj0_kernel,
        grid=(bsz, seq // tm),
        in_specs=[pl.BlockSpec((1, tm, d), lambda b, i: (b, i, 0)),
                  _full((1, d)),
                  _mod_spec(0, lambda b, i: b, 0),
                  _mod_spec(0, lambda b, i: b, 1),
                  _full((d, EVEN_IN)),
                  pl.BlockSpec((tm, LANES), lambda b, i: (i, 0)),
                  pl.BlockSpec((tm, LANES), lambda b, i: (i, 0)),
                  _full((1, LANES)), _full((1, LANES)), _full((LANES, LANES))],
        out_specs=[head(N_Q_HEADS), head(N_KV_HEADS), head(N_KV_HEADS),
                   pl.BlockSpec((1, tm, POOL_WIDTH), lambda b, i: (b, i, 0))],
        out_shape=[jax.ShapeDtypeStruct((bsz, N_Q_HEADS, seq, HEAD_DIM), BF16),
                   jax.ShapeDtypeStruct((bsz, N_KV_HEADS, seq, HEAD_DIM), BF16),
                   jax.ShapeDtypeStruct((bsz, N_KV_HEADS, seq, HEAD_DIM), BF16),
                   jax.ShapeDtypeStruct((bsz, seq, POOL_WIDTH), F32)],
        compiler_params=_params(("parallel", "parallel")),
        name="inproj0",
    )(x, gain.reshape(1, d), mods, mods, w_in.astype(BF16), cos, sin, qg, kg, _head_mean_matrix())


def _inproj0_ctx(ctx, mods, ctx_row, gain, w_kv, k_gain):
    bsz, n_ctx, d = ctx.shape
    kg = jnp.tile(k_gain, LANES // HEAD_DIM).reshape(1, LANES)
    head = pl.BlockSpec((1, N_KV_HEADS, n_ctx, HEAD_DIM), lambda b: (b, 0, 0, 0))
    return pl.pallas_call(
        _inproj0_ctx_kernel,
        grid=(bsz,),
        in_specs=[pl.BlockSpec((1, n_ctx, d), lambda b: (b, 0, 0)),
                  _full((1, d)),
                  _mod_spec(0, lambda b: ctx_row, 0),
                  _mod_spec(0, lambda b: ctx_row, 1),
                  _full((d, 2 * KV_WIDTH)),
                  _full((1, LANES)), _full((LANES, LANES))],
        out_specs=[head, head],
        out_shape=[jax.ShapeDtypeStruct((bsz, N_KV_HEADS, n_ctx, HEAD_DIM), BF16)] * 2,
        compiler_params=_params(("parallel",)),
        name="inproj0_ctx",
    )(ctx, gain.reshape(1, d), mods, mods, w_kv.astype(BF16), kg, _head_mean_matrix())


def _attn_kernel(q_ref, kl_ref, vl_ref, kc_ref, vc_ref, o_ref, *, tq, tk):
    rows = Q_PER_KV * tq
    q = q_ref[0].reshape(rows, HEAD_DIM)
    seq = kl_ref.shape[2]

    def step(k, v, carry):
        m, l, acc = carry
        s = lax.dot_general(q, k, (((1,), (1,)), ((), ())), preferred_element_type=F32)
        m_new = jnp.maximum(m, jnp.max(s, axis=1, keepdims=True))
        alpha = jnp.exp(m - m_new)
        p = jnp.exp(s - m_new)
        l = alpha * l + jnp.sum(p, axis=1, keepdims=True)
        acc = alpha * acc + jnp.dot(p.astype(BF16), v, preferred_element_type=F32)
        return m_new, l, acc

    carry = (jnp.full((rows, 1), -jnp.inf, F32), jnp.zeros((rows, 1), F32),
             jnp.zeros((rows, HEAD_DIM), F32))
    carry = step(kc_ref[0, 0], vc_ref[0, 0], carry)
    for c in range(seq // tk):
        carry = step(kl_ref[0, 0, c * tk:(c + 1) * tk, :], vl_ref[0, 0, c * tk:(c + 1) * tk, :], carry)
    _, l, acc = carry
    o = acc / l
    o_ref[0] = jnp.concatenate([o[h * tq:(h + 1) * tq] for h in range(Q_PER_KV)], axis=1).astype(BF16)


def _attention(q, k, v, kc, vc, tq, tk):
    bsz, _, seq, _ = q.shape
    n_ctx = kc.shape[2]
    kv_spec = lambda n: pl.BlockSpec((1, 1, n, HEAD_DIM), lambda b, g, i: (b, g, 0, 0))
    return pl.pallas_call(
        functools.partial(_attn_kernel, tq=tq, tk=tk),
        grid=(bsz, N_KV_HEADS, seq // tq),
        in_specs=[pl.BlockSpec((1, Q_PER_KV, tq, HEAD_DIM), lambda b, g, i: (b, g, i, 0)),
                  kv_spec(seq), kv_spec(seq), kv_spec(n_ctx), kv_spec(n_ctx)],
        out_specs=pl.BlockSpec((1, tq, Q_PER_KV * HEAD_DIM), lambda b, g, i: (b, i, g)),
        out_shape=jax.ShapeDtypeStruct((bsz, seq, ATTN_WIDTH), BF16),
        compiler_params=_params(("parallel", "parallel", "parallel")),
        name="attention",
    )(q, k, v, kc, vc)


def _route(logits):
    lane = lax.broadcasted_iota(jnp.int32, logits.shape, 1)
    neg = -jnp.inf
    big = ROUTER_LANES
    is_g = lane < N_GROUPS
    gm = jnp.max(jnp.where(is_g, logits, neg), axis=1, keepdims=True)
    gidx = jnp.min(jnp.where(is_g & (logits == gm), lane, big), axis=1, keepdims=True)
    gden = jnp.sum(jnp.where(is_g, jnp.exp(logits - gm), 0.0), axis=1, keepdims=True)
    g_p = 1.0 / gden
    lo = N_GROUPS + EXPERTS_PER_GROUP * gidx
    sel = (lane >= lo) & (lane < lo + EXPERTS_PER_GROUP)
    e1 = jnp.max(jnp.where(sel, logits, neg), axis=1, keepdims=True)
    i1 = jnp.min(jnp.where(sel & (logits == e1), lane, big), axis=1, keepdims=True)
    rest = sel & (lane != i1)
    e2 = jnp.max(jnp.where(rest, logits, neg), axis=1, keepdims=True)
    i2 = jnp.min(jnp.where(rest & (logits == e2), lane, big), axis=1, keepdims=True)
    p2 = jnp.exp(e2 - e1)
    w1 = 1.0 / (1.0 + p2)
    w2 = p2 / (1.0 + p2)
    return jnp.where(lane == i1, g_p * w1, 0.0) + jnp.where(lane == i2, g_p * w2, 0.0)


def _tail(y, x_res, gate1, gain2, shift2, scale2, rw_hi, rw_lo, rbias, h1_ref, t_ref, comb_ref):
    h1 = x_res + gate1 * y
    h1_ref[0] = h1
    t = _modulate(h1, gain2, shift2, scale2)
    t_hi = t.astype(BF16)
    t_lo = (t - t_hi.astype(F32)).astype(BF16)
    t_ref[0] = t_hi
    logits = (jnp.dot(t_hi, rw_hi, preferred_element_type=F32)
              + jnp.dot(t_lo, rw_hi, preferred_element_type=F32)
              + jnp.dot(t_hi, rw_lo, preferred_element_type=F32)) + rbias
    comb_ref[0] = _route(logits)


def _router_operands(rg_w, rg_b, re_w, re_b):
    d = rg_w.shape[0]
    w = jnp.concatenate([rg_w, re_w, jnp.zeros((d, ROUTER_LANES - N_GROUPS - N_EXPERTS), F32)], axis=1)
    b = jnp.concatenate([rg_b, re_b, jnp.zeros((ROUTER_LANES - N_GROUPS - N_EXPERTS,), F32)])
    w_hi = w.astype(BF16)
    w_lo = (w - w_hi.astype(F32)).astype(BF16)
    return w_hi, w_lo, b.reshape(1, ROUTER_LANES)


def _fill_halo(buf, main_ref, prev_ref, next_ref, tm, i, n_tiles):
    buf[HALO:HALO + tm] = main_ref[0]
    buf[0:HALO] = jnp.where(i > 0, prev_ref[0], 0.0)
    buf[HALO + tm:2 * HALO + tm] = jnp.where(i < n_tiles - 1, next_ref[0], 0.0)


def _halo_specs(tm, seq, width):
    per = tm // HALO
    last = seq // HALO - 1
    return [pl.BlockSpec((1, tm, width), lambda b, i: (b, i, 0)),
            pl.BlockSpec((1, HALO, width), lambda b, i: (b, jnp.maximum(i * per - 1, 0), 0)),
            pl.BlockSpec((1, HALO, width), lambda b, i: (b, jnp.minimum((i + 1) * per, last), 0))]


def _out0_kernel(o_ref, p_ref, pprev_ref, pnext_ref, x_ref, g1_ref, gain2_ref, sh2_ref, sc2_ref,
                 poolw_ref, pscale_ref, wout_ref, rwh_ref, rwl_ref, rb_ref,
                 h1_ref, t_ref, comb_ref, pbuf, *, tm, seq):
    i = pl.program_id(1)
    _fill_halo(pbuf, p_ref, pprev_ref, pnext_ref, tm, i, seq // tm)
    pos = i * tm + lax.broadcasted_iota(jnp.int32, (tm, 1), 0)
    y = jnp.dot(o_ref[0], wout_ref[0:ATTN_WIDTH, :], preferred_element_type=F32)
    for g, w in enumerate(POOL_WINDOWS):
        sl = slice(g * POOL_GROUP, (g + 1) * POOL_GROUP)
        acc = pbuf[HALO - w // 2:HALO - w // 2 + tm, sl]
        for j in range(1 - w // 2, w - w // 2):
            acc = acc + pbuf[HALO + j:HALO + j + tm, sl]
        lo = jnp.clip(pos - w // 2, 0, seq)
        hi = jnp.clip(pos + w - w // 2, 0, seq)
        mean = acc / (hi - lo).astype(F32)
        dlt = (mean - pbuf[HALO:HALO + tm, sl]).astype(BF16)
        yp = jnp.dot(dlt, poolw_ref[g], preferred_element_type=F32) * pscale_ref[:, sl]
        y = y + jnp.dot(yp.astype(BF16), wout_ref[ATTN_WIDTH + g * POOL_GROUP:ATTN_WIDTH + (g + 1) * POOL_GROUP, :],
                        preferred_element_type=F32)
    _tail(y, x_ref[0], g1_ref[...], gain2_ref[...], sh2_ref[...], sc2_ref[...],
          rwh_ref[...], rwl_ref[...], rb_ref[...], h1_ref, t_ref, comb_ref)


def _tail_specs(layer, d):
    by_batch = lambda b, i: b
    ins = [_mod_spec(layer, by_batch, 2), _full((1, d)), _mod_spec(layer, by_batch, 3),
           _mod_spec(layer, by_batch, 4)]
    return ins


def _tail_outs(bsz, seq, d, tm):
    specs = [pl.BlockSpec((1, tm, d), lambda b, i: (b, i, 0)),
             pl.BlockSpec((1, tm, d), lambda b, i: (b, i, 0)),
             pl.BlockSpec((1, tm, ROUTER_LANES), lambda b, i: (b, i, 0))]
    shapes = [jax.ShapeDtypeStruct((bsz, seq, d), F32),
              jax.ShapeDtypeStruct((bsz, seq, d), BF16),
              jax.ShapeDtypeStruct((bsz, seq, ROUTER_LANES), F32)]
    return specs, shapes


def _out0(o, p, x, mods, gain2, pool_w, pool_scale, w_out, router, tm):
    bsz, seq, d = x.shape
    rw_hi, rw_lo, rb = router
    out_specs, out_shapes = _tail_outs(bsz, seq, d, tm)
    return pl.pallas_call(
        functools.partial(_out0_kernel, tm=tm, seq=seq),
        grid=(bsz, seq // tm),
        in_specs=[pl.BlockSpec((1, tm, ATTN_WIDTH), lambda b, i: (b, i, 0))]
        + _halo_specs(tm, seq, POOL_WIDTH)
        + [pl.BlockSpec((1, tm, d), lambda b, i: (b, i, 0))]
        + _tail_specs(0, d)
        + [_full(pool_w.shape), _full((1, POOL_WIDTH)), _full(w_out.shape),
           _full(rw_hi.shape), _full(rw_lo.shape), _full(rb.shape)],
        out_specs=out_specs,
        out_shape=out_shapes,
        scratch_shapes=[pltpu.VMEM((tm + 2 * HALO, POOL_WIDTH), F32)],
        compiler_params=_params(("parallel", "parallel")),
        name="out0",
    )(o, p, p, p, x, mods, gain2.reshape(1, d), mods, mods,
      pool_w.astype(BF16), pool_scale.reshape(1, POOL_WIDTH), w_out.astype(BF16), rw_hi, rw_lo, rb)


def _inproj1_kernel(x_ref, gain_ref, sh_ref, sc_ref, w_ref, sgg_ref, sgw_ref, sgb_ref,
                    yc_ref, z_ref, bg_ref, *, tm):
    a = _modulate(x_ref[0], gain_ref[...], sh_ref[...], sc_ref[...])
    y = jnp.dot(a.astype(BF16), w_ref[...], preferred_element_type=F32)
    for g in range(SG_GROUPS):
        sl = slice(g * LANES, (g + 1) * LANES)
        u = y[:, sl]
        vg = y[:, SG_WIDTH + g * LANES:SG_WIDTH + (g + 1) * LANES]
        ms = jnp.mean(vg * vg, axis=-1, keepdims=True)
        vn = (vg * lax.rsqrt(ms + EPS) * sgg_ref[:, sl]).astype(BF16)
        for c in range(tm // SG_CHUNK):
            rows = slice(c * SG_CHUNK, (c + 1) * SG_CHUNK)
            s = jnp.dot(sgw_ref[g], vn[rows], preferred_element_type=F32) + sgb_ref[g]
            yc_ref[0, rows, sl] = (u[rows] * s).astype(BF16)
    hx = y[:, 2 * SG_WIDTH:2 * SG_WIDTH + CONV_WIDTH]
    bg_ref[0] = y[:, 2 * SG_WIDTH + CONV_WIDTH:2 * SG_WIDTH + 2 * CONV_WIDTH]
    cg = y[:, 2 * SG_WIDTH + 2 * CONV_WIDTH:]
    z_ref[0] = cg * hx


def _inproj1(x, mods, gain, w_in, sg_gain, sg_w, sg_b, tm):
    bsz, seq, d = x.shape
    sgb = jnp.broadcast_to(sg_b[:, :, None], (SG_GROUPS, SG_CHUNK, LANES))
    wide = lambda dt: (pl.BlockSpec((1, tm, SG_WIDTH), lambda b, i: (b, i, 0)),
                       jax.ShapeDtypeStruct((bsz, seq, SG_WIDTH), dt))
    outs = [wide(BF16), wide(F32), wide(F32)]
    return pl.pallas_call(
        functools.partial(_inproj1_kernel, tm=tm),
        grid=(bsz, seq // tm),
        in_specs=[pl.BlockSpec((1, tm, d), lambda b, i: (b, i, 0)),
                  _full((1, d)),
                  _mod_spec(1, lambda b, i: b, 0),
                  _mod_spec(1, lambda b, i: b, 1),
                  _full((d, ODD_IN)),
                  _full((1, SG_WIDTH)), _full(sg_w.shape), _full(sgb.shape)],
        out_specs=[s for s, _ in outs],
        out_shape=[s for _, s in outs],
        compiler_params=_params(("parallel", "parallel")),
        name="inproj1",
    )(x, gain.reshape(1, d), mods, mods, w_in.astype(BF16), sg_gain.reshape(1, SG_WIDTH),
      sg_w.astype(BF16), sgb)


def _out1_kernel(yc_ref, z_ref, zprev_ref, znext_ref, bg_ref, x_ref, g1_ref, gain2_ref, sh2_ref, sc2_ref,
                 convw_ref, wout_ref, rwh_ref, rwl_ref, rb_ref,
                 h1_ref, t_ref, comb_ref, zbuf, *, tm, seq):
    i = pl.program_id(1)
    _fill_halo(zbuf, z_ref, zprev_ref, znext_ref, tm, i, seq // tm)
    zc = (zbuf[HALO - 1:HALO - 1 + tm] * convw_ref[0:1, :]
          + zbuf[HALO:HALO + tm] * convw_ref[1:2, :]
          + zbuf[HALO + 1:HALO + 1 + tm] * convw_ref[2:3, :])
    yd = (bg_ref[0] * zc).astype(BF16)
    y = (jnp.dot(yc_ref[0], wout_ref[0:SG_WIDTH, :], preferred_element_type=F32)
         + jnp.dot(yd, wout_ref[SG_WIDTH:, :], preferred_element_type=F32))
    _tail(y, x_ref[0], g1_ref[...], gain2_ref[...], sh2_ref[...], sc2_ref[...],
          rwh_ref[...], rwl_ref[...], rb_ref[...], h1_ref, t_ref, comb_ref)


def _out1(yc, z, bg, x, mods, gain2, conv_w, w_out, router, tm):
    bsz, seq, d = x.shape
    rw_hi, rw_lo, rb = router
    out_specs, out_shapes = _tail_outs(bsz, seq, d, tm)
    wide = pl.BlockSpec((1, tm, CONV_WIDTH), lambda b, i: (b, i, 0))
    return pl.pallas_call(
        functools.partial(_out1_kernel, tm=tm, seq=seq),
        grid=(bsz, seq // tm),
        in_specs=[wide] + _halo_specs(tm, seq, CONV_WIDTH) + [wide]
        + [pl.BlockSpec((1, tm, d), lambda b, i: (b, i, 0))]
        + _tail_specs(1, d)
        + [_full((3, CONV_WIDTH)), _full(w_out.shape),
           _full(rw_hi.shape), _full(rw_lo.shape), _full(rb.shape)],
        out_specs=out_specs,
        out_shape=out_shapes,
        scratch_shapes=[pltpu.VMEM((tm + 2 * HALO, CONV_WIDTH), F32)],
        compiler_params=_params(("parallel", "parallel")),
        name="out1",
    )(yc, z, z, z, bg, x, mods, gain2.reshape(1, d), mods, mods,
      conv_w.reshape(3, CONV_WIDTH), w_out.astype(BF16), rw_hi, rw_lo, rb)


def _moe_kernel(t_ref, comb_ref, wg_ref, wu_ref, wd_ref, h1_ref, g2_ref, o_ref, acc):
    e = pl.program_id(1)

    @pl.when(e == 0)
    def _():
        acc[...] = jnp.zeros_like(acc)

    x = t_ref[...]
    gt = jnp.dot(x, wg_ref[0], preferred_element_type=F32)
    up = jnp.dot(x, wu_ref[0], preferred_element_type=F32)
    comb = comb_ref[...]
    lane = lax.broadcasted_iota(jnp.int32, comb.shape, 1)
    cw = jnp.sum(jnp.where(lane == N_GROUPS + e, comb, 0.0), axis=1, keepdims=True)
    h = (gt * jax.nn.sigmoid(gt)) * up * cw
    acc[...] += jnp.dot(h.astype(BF16), wd_ref[0], preferred_element_type=F32)

    @pl.when(e == N_EXPERTS - 1)
    def _():
        o_ref[...] = h1_ref[...] + g2_ref[...] * acc[...]


def _moe(t, comb, h1, mods, layer, w_gate, w_up, w_down, seq, tm):
    n, d = t.shape
    per_batch = seq // tm
    return pl.pallas_call(
        _moe_kernel,
        grid=(n // tm, N_EXPERTS),
        in_specs=[pl.BlockSpec((tm, d), lambda i, e: (i, 0)),
                  pl.BlockSpec((tm, ROUTER_LANES), lambda i, e: (i, 0)),
                  pl.BlockSpec((1, d, D_EXPERT), lambda i, e: (e, 0, 0)),
                  pl.BlockSpec((1, d, D_EXPERT), lambda i, e: (e, 0, 0)),
                  pl.BlockSpec((1, D_EXPERT, d), lambda i, e: (e, 0, 0)),
                  pl.BlockSpec((tm, d), lambda i, e: (i, 0)),
                  _mod_spec(layer, lambda i, e: i // per_batch, 5)],
        out_specs=pl.BlockSpec((tm, d), lambda i, e: (i, 0)),
        out_shape=jax.ShapeDtypeStruct((n, d), F32),
        scratch_shapes=[pltpu.VMEM((tm, d), F32)],
        compiler_params=_params(("parallel", "arbitrary")),
        name=f"moe{layer}",
    )(t, comb, w_gate.astype(BF16), w_up.astype(BF16), w_down.astype(BF16), h1, mods)


def kernel(x, c, ctx, c_ctx, mod_w, mod_b, norm1_g, norm2_g, even_w_in, q_gain, k_gain, pool_w, pool_scale,
           even_w_out, odd_w_in, sg_gain, sg_w, sg_b, conv_w, odd_w_out, router_g_w, router_g_b,
           router_e_w, router_e_b, w_gate, w_up, w_down):
    bsz, seq, d = x.shape
    tm = min(512, seq)
    cond = jnp.zeros((MOD_ROWS, d), F32).at[:bsz].set(c).at[bsz].set(c_ctx)
    mods = _adaln(cond, mod_w, mod_b).reshape(mod_w.shape[0], MOD_ROWS, 6, 1, d)

    q, k, v, p = _inproj0(x, mods, norm1_g[0], even_w_in[0], q_gain[0], k_gain[0], tm)
    kc, vc = _inproj0_ctx(ctx, mods, bsz, norm1_g[0], even_w_in[0][:, ATTN_WIDTH:ATTN_WIDTH + 2 * KV_WIDTH],
                          k_gain[0])
    o = _attention(q, k, v, kc, vc, tq=min(128, seq), tk=min(1024, seq))
    router0 = _router_operands(router_g_w[0], router_g_b[0], router_e_w[0], router_e_b[0])
    h1, t, comb = _out0(o, p, x, mods, norm2_g[0], pool_w[0], pool_scale[0], even_w_out[0], router0, tm)
    h = _moe(t.reshape(-1, d), comb.reshape(-1, ROUTER_LANES), h1.reshape(-1, d), mods, 0,
             w_gate[0], w_up[0], w_down[0], seq, tm).reshape(bsz, seq, d)

    yc, z, bg = _inproj1(h, mods, norm1_g[1], odd_w_in[0], sg_gain[0], sg_w[0], sg_b[0], tm)
    router1 = _router_operands(router_g_w[1], router_g_b[1], router_e_w[1], router_e_b[1])
    h1, t, comb = _out1(yc, z, bg, h, mods, norm2_g[1], conv_w[0], odd_w_out[0], router1, tm)
    h = _moe(t.reshape(-1, d), comb.reshape(-1, ROUTER_LANES), h1.reshape(-1, d), mods, 1,
             w_gate[1], w_up[1], w_down[1], seq, tm).reshape(bsz, seq, d)
    return h
```

```python
import functools

import jax
import jax.numpy as jnp
from jax import lax
from jax.experimental import pallas as pl
from jax.experimental.pallas import tpu as pltpu

F32 = jnp.float32
BF16 = jnp.bfloat16

D_MODEL = 1024
GRID_W = 64
EPS = 1e-6
N_Q_HEADS = 8
N_KV_HEADS = 2
HEAD_DIM = 64
Q_PER_KV = N_Q_HEADS // N_KV_HEADS
ATTN_WIDTH = N_Q_HEADS * HEAD_DIM
KV_WIDTH = N_KV_HEADS * HEAD_DIM
ROPE_THETA = 10000.0
POOL_WINDOWS = (2, 4, 8, 16)
POOL_GROUP = 128
POOL_WIDTH = POOL_GROUP * len(POOL_WINDOWS)
SG_GROUPS = 4
SG_CHUNK = 128
SG_WIDTH = 512
CONV_WIDTH = 512
EVEN_IN = ATTN_WIDTH + 2 * KV_WIDTH + POOL_WIDTH
ODD_IN = 2 * SG_WIDTH + 3 * CONV_WIDTH
N_GROUPS = 4
EXPERTS_PER_GROUP = 4
N_EXPERTS = 16
D_EXPERT = 256

LANES = 128
HALO = 8
ROUTER_LANES = 128
MOD_ROWS = 16
VMEM_LIMIT = 48 * 1024 * 1024

PAIRS_PER_GROUP = 6
N_BUCKETS = N_GROUPS * PAIRS_PER_GROUP
BUCKET_ROWS = 32
SORT_TILE = 256
META_LANES = 256
ROW_WIDTH = D_MODEL + ROUTER_LANES
INFO_BUCKET, INFO_W_LO, INFO_W_HI = 0, 1, 2


def _params(sem):
    return pltpu.CompilerParams(dimension_semantics=sem, vmem_limit_bytes=VMEM_LIMIT)


def _modulate(x, gain, shift, scale):
    ms = jnp.mean(x * x, axis=-1, keepdims=True)
    return (x * lax.rsqrt(ms + EPS) * gain) * (1.0 + scale) + shift


def _mod_spec(layer, row_fn, which):
    return pl.BlockSpec((None, None, None, 1, D_MODEL),
                        lambda *idx: (layer, row_fn(*idx), which, 0, 0))


def _full(shape):
    return pl.BlockSpec(shape, lambda *idx: (0,) * len(shape))


def _adaln_kernel(c_ref, w_ref, b_ref, o_ref):
    c = c_ref[...]
    s = c * jax.nn.sigmoid(c)
    o_ref[0] = jnp.dot(s, w_ref[0], precision=lax.Precision.HIGHEST,
                       preferred_element_type=F32) + b_ref[0]


def _adaln(cond, mod_w, mod_b):
    depth, d, n = mod_w.shape
    tn = 1024
    return pl.pallas_call(
        _adaln_kernel,
        grid=(depth, n // tn),
        in_specs=[_full((MOD_ROWS, d)),
                  pl.BlockSpec((1, d, tn), lambda l, j: (l, 0, j)),
                  pl.BlockSpec((1, 1, tn), lambda l, j: (l, 0, j))],
        out_specs=pl.BlockSpec((1, MOD_ROWS, tn), lambda l, j: (l, 0, j)),
        out_shape=jax.ShapeDtypeStruct((depth, MOD_ROWS, n), F32),
        compiler_params=_params(("arbitrary", "arbitrary")),
        name="adaln",
    )(cond, mod_w, mod_b.reshape(depth, 1, n))


def _head_norm_rope(z, gain, ones_bd, cos, sin, first_half):
    sq = z * z
    hi = sq.astype(BF16)
    lo = (sq - hi.astype(F32)).astype(BF16)
    ms = (jnp.dot(hi, ones_bd, preferred_element_type=F32)
          + jnp.dot(lo, ones_bd, preferred_element_type=F32))
    zn = z * lax.rsqrt(ms + EPS) * gain
    partner = jnp.where(first_half, pltpu.roll(zn, LANES - 16, 1), pltpu.roll(zn, 16, 1))
    return zn * cos + partner * sin


def _inproj0_kernel(x_ref, gain_ref, sh_ref, sc_ref, w_ref, cos_ref, sin_ref, qg_ref, kg_ref, ones_ref,
                    q_ref, k_ref, v_ref, p_ref):
    a = _modulate(x_ref[0], gain_ref[...], sh_ref[...], sc_ref[...])
    y = jnp.dot(a.astype(BF16), w_ref[...], preferred_element_type=F32)
    cos, sin, ones_bd = cos_ref[...], sin_ref[...], ones_ref[...]
    lane = lax.broadcasted_iota(jnp.int32, cos.shape, 1)
    first_half = (lane % 32) < 16
    for s in range(ATTN_WIDTH // LANES):
        r = _head_norm_rope(y[:, s * LANES:(s + 1) * LANES], qg_ref[...], ones_bd, cos, sin, first_half)
        r = (r * (HEAD_DIM ** -0.5)).astype(BF16)
        q_ref[0, 2 * s] = r[:, :HEAD_DIM]
        q_ref[0, 2 * s + 1] = r[:, HEAD_DIM:]
    kr = _head_norm_rope(y[:, ATTN_WIDTH:ATTN_WIDTH + KV_WIDTH], kg_ref[...], ones_bd, cos, sin,
                         first_half).astype(BF16)
    k_ref[0, 0] = kr[:, :HEAD_DIM]
    k_ref[0, 1] = kr[:, HEAD_DIM:]
    vv = y[:, ATTN_WIDTH + KV_WIDTH:ATTN_WIDTH + 2 * KV_WIDTH].astype(BF16)
    v_ref[0, 0] = vv[:, :HEAD_DIM]
    v_ref[0, 1] = vv[:, HEAD_DIM:]
    p_ref[0] = y[:, ATTN_WIDTH + 2 * KV_WIDTH:]


def _inproj0_ctx_kernel(x_ref, gain_ref, sh_ref, sc_ref, w_ref, kg_ref, ones_ref, k_ref, v_ref):
    a = _modulate(x_ref[0], gain_ref[...], sh_ref[...], sc_ref[...])
    y = jnp.dot(a.astype(BF16), w_ref[...], preferred_element_type=F32)
    z = y[:, :KV_WIDTH]
    sq = z * z
    hi = sq.astype(BF16)
    lo = (sq - hi.astype(F32)).astype(BF16)
    ms = (jnp.dot(hi, ones_ref[...], preferred_element_type=F32)
          + jnp.dot(lo, ones_ref[...], preferred_element_type=F32))
    kr = (z * lax.rsqrt(ms + EPS) * kg_ref[...]).astype(BF16)
    k_ref[0, 0] = kr[:, :HEAD_DIM]
    k_ref[0, 1] = kr[:, HEAD_DIM:]
    vv = y[:, KV_WIDTH:].astype(BF16)
    v_ref[0, 0] = vv[:, :HEAD_DIM]
    v_ref[0, 1] = vv[:, HEAD_DIM:]


def _rope_tables(seq):
    t = jnp.arange(seq)
    row = (t // GRID_W).astype(F32)
    col = (t % GRID_W).astype(F32)
    half = HEAD_DIM // 2
    inv = ROPE_THETA ** (-jnp.arange(0, half, 2, dtype=F32) / half)
    ar, ac = row[:, None] * inv, col[:, None] * inv
    cos = jnp.concatenate([jnp.cos(ar), jnp.cos(ar), jnp.cos(ac), jnp.cos(ac)], axis=-1)
    sin = jnp.concatenate([-jnp.sin(ar), jnp.sin(ar), -jnp.sin(ac), jnp.sin(ac)], axis=-1)
    return jnp.tile(cos, (1, LANES // HEAD_DIM)), jnp.tile(sin, (1, LANES // HEAD_DIM))


def _head_mean_matrix():
    r = jnp.arange(LANES)
    same = (r[:, None] // HEAD_DIM) == (r[None, :] // HEAD_DIM)
    return jnp.where(same, 1.0 / HEAD_DIM, 0.0).astype(BF16)


def _inproj0(x, mods, gain, w_in, q_gain, k_gain, tm):
    bsz, seq, d = x.shape
    cos, sin = _rope_tables(seq)
    qg = jnp.tile(q_gain, LANES // HEAD_DIM).reshape(1, LANES)
    kg = jnp.tile(k_gain, LANES // HEAD_DIM).reshape(1, LANES)
    head = lambda n: pl.BlockSpec((1, n, tm, HEAD_DIM), lambda b, i: (b, 0, i, 0))
    return pl.pallas_call(
        _inproj0_kernel,
        grid=(bsz, seq // tm),
        in_specs=[pl.BlockSpec((1, tm, d), lambda b, i: (b, i, 0)),
                  _full((1, d)),
                  _mod_spec(0, lambda b, i: b, 0),
                  _mod_spec(0, lambda b, i: b, 1),
                  _full((d, EVEN_IN)),
                  pl.BlockSpec((tm, LANES), lambda b, i: (i, 0)),
                  pl.BlockSpec((tm, LANES), lambda b, i: (i, 0)),
                  _full((1, LANES)), _full((1, LANES)), _full((LANES, LANES))],
        out_specs=[head(N_Q_HEADS), head(N_KV_HEADS), head(N_KV_HEADS),
                   pl.BlockSpec((1, tm, POOL_WIDTH), lambda b, i: (b, i, 0))],
        out_shape=[jax.ShapeDtypeStruct((bsz, N_Q_HEADS, seq, HEAD_DIM), BF16),
                   jax.ShapeDtypeStruct((bsz, N_KV_HEADS, seq, HEAD_DIM), BF16),
                   jax.ShapeDtypeStruct((bsz, N_KV_HEADS, seq, HEAD_DIM), BF16),
                   jax.ShapeDtypeStruct((bsz, seq, POOL_WIDTH), F32)],
        compiler_params=_params(("parallel", "parallel")),
        name="inproj0",
    )(x, gain.reshape(1, d), mods, mods, w_in.astype(BF16), cos, sin, qg, kg, _head_mean_matrix())


def _inproj0_ctx(ctx, mods, ctx_row, gain, w_kv, k_gain):
    bsz, n_ctx, d = ctx.shape
    kg = jnp.tile(k_gain, LANES // HEAD_DIM).reshape(1, LANES)
    head = pl.BlockSpec((1, N_KV_HEADS, n_ctx, HEAD_DIM), lambda b: (b, 0, 0, 0))
    return pl.pallas_call(
        _inproj0_ctx_kernel,
        grid=(bsz,),
        in_specs=[pl.BlockSpec((1, n_ctx, d), lambda b: (b, 0, 0)),
                  _full((1, d)),
                  _mod_spec(0, lambda b: ctx_row, 0),
                  _mod_spec(0, lambda b: ctx_row, 1),
                  _full((d, 2 * KV_WIDTH)),
                  _full((1, LANES)), _full((LANES, LANES))],
        out_specs=[head, head],
        out_shape=[jax.ShapeDtypeStruct((bsz, N_KV_HEADS, n_ctx, HEAD_DIM), BF16)] * 2,
        compiler_params=_params(("parallel",)),
        name="inproj0_ctx",
    )(ctx, gain.reshape(1, d), mods, mods, w_kv.astype(BF16), kg, _head_mean_matrix())


def _attn_kernel(q_ref, kl_ref, vl_ref, kc_ref, vc_ref, o_ref, *, tq, tk):
    rows = Q_PER_KV * tq
    q = q_ref[0].reshape(rows, HEAD_DIM)
    seq = kl_ref.shape[2]

    def step(k, v, carry):
        m, l, acc = carry
        s = lax.dot_general(q, k, (((1,), (1,)), ((), ())), preferred_element_type=F32)
        m_new = jnp.maximum(m, jnp.max(s, axis=1, keepdims=True))
        alpha = jnp.exp(m - m_new)
        p = jnp.exp(s - m_new)
        l = alpha * l + jnp.sum(p, axis=1, keepdims=True)
        acc = alpha * acc + jnp.dot(p.astype(BF16), v, preferred_element_type=F32)
        return m_new, l, acc

    carry = (jnp.full((rows, 1), -jnp.inf, F32), jnp.zeros((rows, 1), F32),
             jnp.zeros((rows, HEAD_DIM), F32))
    carry = step(kc_ref[0, 0], vc_ref[0, 0], carry)
    for c in range(seq // tk):
        carry = step(kl_ref[0, 0, c * tk:(c + 1) * tk, :], vl_ref[0, 0, c * tk:(c + 1) * tk, :], carry)
    _, l, acc = carry
    o = acc / l
    o_ref[0] = jnp.concatenate([o[h * tq:(h + 1) * tq] for h in range(Q_PER_KV)], axis=1).astype(BF16)


def _attention(q, k, v, kc, vc, tq, tk):
    bsz, _, seq, _ = q.shape
    n_ctx = kc.shape[2]
    kv_spec = lambda n: pl.BlockSpec((1, 1, n, HEAD_DIM), lambda b, g, i: (b, g, 0, 0))
    return pl.pallas_call(
        functools.partial(_attn_kernel, tq=tq, tk=tk),
        grid=(bsz, N_KV_HEADS, seq // tq),
        in_specs=[pl.BlockSpec((1, Q_PER_KV, tq, HEAD_DIM), lambda b, g, i: (b, g, i, 0)),
                  kv_spec(seq), kv_spec(seq), kv_spec(n_ctx), kv_spec(n_ctx)],
        out_specs=pl.BlockSpec((1, tq, Q_PER_KV * HEAD_DIM), lambda b, g, i: (b, i, g)),
        out_shape=jax.ShapeDtypeStruct((bsz, seq, ATTN_WIDTH), BF16),
        compiler_params=_params(("parallel", "parallel", "parallel")),
        name="attention",
    )(q, k, v, kc, vc)


def _route(logits):
    lane = lax.broadcasted_iota(jnp.int32, logits.shape, 1)
    neg = -jnp.inf
    big = ROUTER_LANES
    is_g = lane < N_GROUPS
    gm = jnp.max(jnp.where(is_g, logits, neg), axis=1, keepdims=True)
    gidx = jnp.min(jnp.where(is_g & (logits == gm), lane, big), axis=1, keepdims=True)
    gden = jnp.sum(jnp.where(is_g, jnp.exp(logits - gm), 0.0), axis=1, keepdims=True)
    g_p = 1.0 / gden
    first = N_GROUPS + EXPERTS_PER_GROUP * gidx
    sel = (lane >= first) & (lane < first + EXPERTS_PER_GROUP)
    e1 = jnp.max(jnp.where(sel, logits, neg), axis=1, keepdims=True)
    i1 = jnp.min(jnp.where(sel & (logits == e1), lane, big), axis=1, keepdims=True)
    rest = sel & (lane != i1)
    e2 = jnp.max(jnp.where(rest, logits, neg), axis=1, keepdims=True)
    i2 = jnp.min(jnp.where(rest & (logits == e2), lane, big), axis=1, keepdims=True)
    p2 = jnp.exp(e2 - e1)
    w1 = g_p * (1.0 / (1.0 + p2))
    w2 = g_p * (p2 / (1.0 + p2))
    lo = jnp.minimum(i1, i2) - first
    hi = jnp.maximum(i1, i2) - first
    pair = jnp.where(lo == 0, hi - 1, jnp.where(lo == 1, hi + 1, PAIRS_PER_GROUP - 1))
    bucket = (PAIRS_PER_GROUP * gidx + pair).astype(F32)
    w_lo = jnp.where(i1 < i2, w1, w2)
    w_hi = jnp.where(i1 < i2, w2, w1)
    return jnp.where(lane == INFO_BUCKET, bucket,
                     jnp.where(lane == INFO_W_LO, w_lo, jnp.where(lane == INFO_W_HI, w_hi, 0.0)))


def _tail(y, x_res, gate1, gain2, shift2, scale2, rw_hi, rw_lo, rbias, h1_ref, info_ref):
    h1 = x_res + gate1 * y
    h1_ref[0] = h1
    t = _modulate(h1, gain2, shift2, scale2)
    t_hi = t.astype(BF16)
    t_lo = (t - t_hi.astype(F32)).astype(BF16)
    logits = (jnp.dot(t_hi, rw_hi, preferred_element_type=F32)
              + jnp.dot(t_lo, rw_hi, preferred_element_type=F32)
              + jnp.dot(t_hi, rw_lo, preferred_element_type=F32)) + rbias
    info_ref[0] = _route(logits)


def _router_operands(rg_w, rg_b, re_w, re_b):
    d = rg_w.shape[0]
    w = jnp.concatenate([rg_w, re_w, jnp.zeros((d, ROUTER_LANES - N_GROUPS - N_EXPERTS), F32)], axis=1)
    b = jnp.concatenate([rg_b, re_b, jnp.zeros((ROUTER_LANES - N_GROUPS - N_EXPERTS,), F32)])
    w_hi = w.astype(BF16)
    w_lo = (w - w_hi.astype(F32)).astype(BF16)
    return w_hi, w_lo, b.reshape(1, ROUTER_LANES)


def _fill_halo(buf, main_ref, prev_ref, next_ref, tm, i, n_tiles):
    buf[HALO:HALO + tm] = main_ref[0]
    buf[0:HALO] = jnp.where(i > 0, prev_ref[0], 0.0)
    buf[HALO + tm:2 * HALO + tm] = jnp.where(i < n_tiles - 1, next_ref[0], 0.0)


def _halo_specs(tm, seq, width):
    per = tm // HALO
    last = seq // HALO - 1
    return [pl.BlockSpec((1, tm, width), lambda b, i: (b, i, 0)),
            pl.BlockSpec((1, HALO, width), lambda b, i: (b, jnp.maximum(i * per - 1, 0), 0)),
            pl.BlockSpec((1, HALO, width), lambda b, i: (b, jnp.minimum((i + 1) * per, last), 0))]


def _out0_kernel(o_ref, p_ref, pprev_ref, pnext_ref, x_ref, g1_ref, gain2_ref, sh2_ref, sc2_ref,
                 poolw_ref, pscale_ref, wout_ref, rwh_ref, rwl_ref, rb_ref,
                 h1_ref, info_ref, pbuf, *, tm, seq):
    i = pl.program_id(1)
    _fill_halo(pbuf, p_ref, pprev_ref, pnext_ref, tm, i, seq // tm)
    pos = i * tm + lax.broadcasted_iota(jnp.int32, (tm, 1), 0)
    y = jnp.dot(o_ref[0], wout_ref[0:ATTN_WIDTH, :], preferred_element_type=F32)
    for g, w in enumerate(POOL_WINDOWS):
        sl = slice(g * POOL_GROUP, (g + 1) * POOL_GROUP)
        acc = pbuf[HALO - w // 2:HALO - w // 2 + tm, sl]
        for j in range(1 - w // 2, w - w // 2):
            acc = acc + pbuf[HALO + j:HALO + j + tm, sl]
        lo = jnp.clip(pos - w // 2, 0, seq)
        hi = jnp.clip(pos + w - w // 2, 0, seq)
        mean = acc / (hi - lo).astype(F32)
        dlt = (mean - pbuf[HALO:HALO + tm, sl]).astype(BF16)
        yp = jnp.dot(dlt, poolw_ref[g], preferred_element_type=F32) * pscale_ref[:, sl]
        y = y + jnp.dot(yp.astype(BF16), wout_ref[ATTN_WIDTH + g * POOL_GROUP:ATTN_WIDTH + (g + 1) * POOL_GROUP, :],
                        preferred_element_type=F32)
    _tail(y, x_ref[0], g1_ref[...], gain2_ref[...], sh2_ref[...], sc2_ref[...],
          rwh_ref[...], rwl_ref[...], rb_ref[...], h1_ref, info_ref)


def _tail_specs(layer, d):
    by_batch = lambda b, i: b
    ins = [_mod_spec(layer, by_batch, 2), _full((1, d)), _mod_spec(layer, by_batch, 3),
           _mod_spec(layer, by_batch, 4)]
    return ins


def _tail_outs(bsz, seq, d, tm):
    specs = [pl.BlockSpec((1, tm, d), lambda b, i: (b, i, 0)),
             pl.BlockSpec((1, tm, ROUTER_LANES), lambda b, i: (b, i, 0))]
    shapes = [jax.ShapeDtypeStruct((bsz, seq, d), F32),
              jax.ShapeDtypeStruct((bsz, seq, ROUTER_LANES), F32)]
    return specs, shapes


def _out0(o, p, x, mods, gain2, pool_w, pool_scale, w_out, router, tm):
    bsz, seq, d = x.shape
    rw_hi, rw_lo, rb = router
    out_specs, out_shapes = _tail_outs(bsz, seq, d, tm)
    return pl.pallas_call(
        functools.partial(_out0_kernel, tm=tm, seq=seq),
        grid=(bsz, seq // tm),
        in_specs=[pl.BlockSpec((1, tm, ATTN_WIDTH), lambda b, i: (b, i, 0))]
        + _halo_specs(tm, seq, POOL_WIDTH)
        + [pl.BlockSpec((1, tm, d), lambda b, i: (b, i, 0))]
        + _tail_specs(0, d)
        + [_full(pool_w.shape), _full((1, POOL_WIDTH)), _full(w_out.shape),
           _full(rw_hi.shape), _full(rw_lo.shape), _full(rb.shape)],
        out_specs=out_specs,
        out_shape=out_shapes,
        scratch_shapes=[pltpu.VMEM((tm + 2 * HALO, POOL_WIDTH), F32)],
        compiler_params=_params(("parallel", "parallel")),
        name="out0",
    )(o, p, p, p, x, mods, gain2.reshape(1, d), mods, mods,
      pool_w.astype(BF16), pool_scale.reshape(1, POOL_WIDTH), w_out.astype(BF16), rw_hi, rw_lo, rb)


def _inproj1_kernel(x_ref, gain_ref, sh_ref, sc_ref, w_ref, sgg_ref, sgw_ref, sgb_ref,
                    yc_ref, z_ref, bg_ref, *, tm):
    a = _modulate(x_ref[0], gain_ref[...], sh_ref[...], sc_ref[...])
    y = jnp.dot(a.astype(BF16), w_ref[...], preferred_element_type=F32)
    for g in range(SG_GROUPS):
        sl = slice(g * LANES, (g + 1) * LANES)
        u = y[:, sl]
        vg = y[:, SG_WIDTH + g * LANES:SG_WIDTH + (g + 1) * LANES]
        ms = jnp.mean(vg * vg, axis=-1, keepdims=True)
        vn = (vg * lax.rsqrt(ms + EPS) * sgg_ref[:, sl]).astype(BF16)
        for c in range(tm // SG_CHUNK):
            rows = slice(c * SG_CHUNK, (c + 1) * SG_CHUNK)
            s = jnp.dot(sgw_ref[g], vn[rows], preferred_element_type=F32) + sgb_ref[g]
            yc_ref[0, rows, sl] = (u[rows] * s).astype(BF16)
    hx = y[:, 2 * SG_WIDTH:2 * SG_WIDTH + CONV_WIDTH]
    bg_ref[0] = y[:, 2 * SG_WIDTH + CONV_WIDTH:2 * SG_WIDTH + 2 * CONV_WIDTH]
    cg = y[:, 2 * SG_WIDTH + 2 * CONV_WIDTH:]
    z_ref[0] = cg * hx


def _inproj1(x, mods, gain, w_in, sg_gain, sg_w, sg_b, tm):
    bsz, seq, d = x.shape
    sgb = jnp.broadcast_to(sg_b[:, :, None], (SG_GROUPS, SG_CHUNK, LANES))
    wide = lambda dt: (pl.BlockSpec((1, tm, SG_WIDTH), lambda b, i: (b, i, 0)),
                       jax.ShapeDtypeStruct((bsz, seq, SG_WIDTH), dt))
    outs = [wide(BF16), wide(F32), wide(F32)]
    return pl.pallas_call(
        functools.partial(_inproj1_kernel, tm=tm),
        grid=(bsz, seq // tm),
        in_specs=[pl.BlockSpec((1, tm, d), lambda b, i: (b, i, 0)),
                  _full((1, d)),
                  _mod_spec(1, lambda b, i: b, 0),
                  _mod_spec(1, lambda b, i: b, 1),
                  _full((d, ODD_IN)),
                  _full((1, SG_WIDTH)), _full(sg_w.shape), _full(sgb.shape)],
        out_specs=[s for s, _ in outs],
        out_shape=[s for _, s in outs],
        compiler_params=_params(("parallel", "parallel")),
        name="inproj1",
    )(x, gain.reshape(1, d), mods, mods, w_in.astype(BF16), sg_gain.reshape(1, SG_WIDTH),
      sg_w.astype(BF16), sgb)


def _out1_kernel(yc_ref, z_ref, zprev_ref, znext_ref, bg_ref, x_ref, g1_ref, gain2_ref, sh2_ref, sc2_ref,
                 convw_ref, wout_ref, rwh_ref, rwl_ref, rb_ref,
                 h1_ref, info_ref, zbuf, *, tm, seq):
    i = pl.program_id(1)
    _fill_halo(zbuf, z_ref, zprev_ref, znext_ref, tm, i, seq // tm)
    zc = (zbuf[HALO - 1:HALO - 1 + tm] * convw_ref[0:1, :]
          + zbuf[HALO:HALO + tm] * convw_ref[1:2, :]
          + zbuf[HALO + 1:HALO + 1 + tm] * convw_ref[2:3, :])
    yd = (bg_ref[0] * zc).astype(BF16)
    y = (jnp.dot(yc_ref[0], wout_ref[0:SG_WIDTH, :], preferred_element_type=F32)
         + jnp.dot(yd, wout_ref[SG_WIDTH:, :], preferred_element_type=F32))
    _tail(y, x_ref[0], g1_ref[...], gain2_ref[...], sh2_ref[...], sc2_ref[...],
          rwh_ref[...], rwl_ref[...], rb_ref[...], h1_ref, info_ref)


def _out1(yc, z, bg, x, mods, gain2, conv_w, w_out, router, tm):
    bsz, seq, d = x.shape
    rw_hi, rw_lo, rb = router
    out_specs, out_shapes = _tail_outs(bsz, seq, d, tm)
    wide = pl.BlockSpec((1, tm, CONV_WIDTH), lambda b, i: (b, i, 0))
    return pl.pallas_call(
        functools.partial(_out1_kernel, tm=tm, seq=seq),
        grid=(bsz, seq // tm),
        in_specs=[wide] + _halo_specs(tm, seq, CONV_WIDTH) + [wide]
        + [pl.BlockSpec((1, tm, d), lambda b, i: (b, i, 0))]
        + _tail_specs(1, d)
        + [_full((3, CONV_WIDTH)), _full(w_out.shape),
           _full(rw_hi.shape), _full(rw_lo.shape), _full(rb.shape)],
        out_specs=out_specs,
        out_shape=out_shapes,
        scratch_shapes=[pltpu.VMEM((tm + 2 * HALO, CONV_WIDTH), F32)],
        compiler_params=_params(("parallel", "parallel")),
        name="out1",
    )(yc, z, z, z, bg, x, mods, gain2.reshape(1, d), mods, mods,
      conv_w.reshape(3, CONV_WIDTH), w_out.astype(BF16), rw_hi, rw_lo, rb)


def _bucket_onehot(info, sel):
    brow = lax.dot_general(sel, info.astype(BF16), (((1,), (1,)), ((), ())),
                           preferred_element_type=F32)[0:1]
    bid = lax.broadcasted_iota(jnp.int32, (BUCKET_ROWS, info.shape[0]), 0)
    return (bid == brow.astype(jnp.int32)).astype(F32)


def _plan_count_kernel(info_ref, sel_ref, cnt_ref):
    @pl.when(pl.program_id(0) == 0)
    def _():
        cnt_ref[...] = jnp.zeros_like(cnt_ref)

    oh = _bucket_onehot(info_ref[...], sel_ref[...])
    cnt_ref[...] += jnp.sum(oh, axis=1, keepdims=True)


def _plan_pos_kernel(info_ref, sel_ref, cnt_ref, ltri_ref, utri_ref, pos_ref, meta_ref, start_sc, run_sc):
    @pl.when(pl.program_id(0) == 0)
    def _():
        padded = jnp.ceil(cnt_ref[...] * (1.0 / SORT_TILE)) * SORT_TILE
        incl = jnp.dot(ltri_ref[...], padded, precision=lax.Precision.HIGHEST, preferred_element_type=F32)
        start_sc[...] = incl - padded
        run_sc[...] = jnp.zeros_like(run_sc)
        ends = jnp.broadcast_to(incl[:, 0:1], (BUCKET_ROWS, META_LANES))
        bid = lax.broadcasted_iota(jnp.int32, ends.shape, 0)
        row0 = (lax.broadcasted_iota(jnp.int32, ends.shape, 1) * SORT_TILE).astype(F32)
        done = jnp.where((bid < N_BUCKETS) & (ends <= row0), 1.0, 0.0)
        tb = jnp.minimum(jnp.sum(done, axis=0, keepdims=True), N_BUCKETS - 1.0).astype(jnp.int32)
        grp = ((tb >= PAIRS_PER_GROUP).astype(jnp.int32) + (tb >= 2 * PAIRS_PER_GROUP).astype(jnp.int32)
               + (tb >= 3 * PAIRS_PER_GROUP).astype(jnp.int32))
        pair = tb - PAIRS_PER_GROUP * grp
        lo = (pair >= 3).astype(jnp.int32) + (pair >= 5).astype(jnp.int32)
        hi = jnp.where(pair == 0, 1, jnp.where((pair == 1) | (pair == 3), 2, 3))
        n_used = (incl[N_BUCKETS - 1:N_BUCKETS, 0:1] * (1.0 / SORT_TILE)).astype(jnp.int32)
        meta_ref[...] = jnp.zeros_like(meta_ref)
        meta_ref[0:1, :] = EXPERTS_PER_GROUP * grp + lo
        meta_ref[1:2, :] = EXPERTS_PER_GROUP * grp + hi
        meta_ref[2:3, :] = jnp.broadcast_to(n_used, (1, META_LANES))

    oh = _bucket_onehot(info_ref[...], sel_ref[...])
    before = jnp.dot(oh.astype(BF16), utri_ref[...], preferred_element_type=F32)
    base = start_sc[:, 0:1] + run_sc[:, 0:1]
    pos_ref[0] = jnp.sum(oh * (before + base), axis=0, keepdims=True).astype(jnp.int32)
    run_sc[...] += jnp.sum(oh, axis=1, keepdims=True)


def _sort_plan(info, tm):
    n = info.shape[0]
    sel = jnp.zeros((8, ROUTER_LANES), F32).at[0, INFO_BUCKET].set(1.0).astype(BF16)
    info_spec = pl.BlockSpec((tm, ROUTER_LANES), lambda i: (i, 0))
    cnt = pl.pallas_call(
        _plan_count_kernel,
        grid=(n // tm,),
        in_specs=[info_spec, _full(sel.shape)],
        out_specs=_full((BUCKET_ROWS, LANES)),
        out_shape=jax.ShapeDtypeStruct((BUCKET_ROWS, LANES), F32),
        compiler_params=_params(("arbitrary",)),
        name="plan_count",
    )(info, sel)
    r = jnp.arange(BUCKET_ROWS)
    ltri = (r[:, None] >= r[None, :]).astype(F32)
    t = jnp.arange(tm)
    utri = (t[:, None] < t[None, :]).astype(BF16)
    return pl.pallas_call(
        _plan_pos_kernel,
        grid=(n // tm,),
        in_specs=[info_spec, _full(sel.shape), _full(cnt.shape), _full(ltri.shape), _full(utri.shape)],
        out_specs=[pl.BlockSpec((1, 1, tm), lambda i: (i, 0, 0)), _full((8, META_LANES))],
        out_shape=[jax.ShapeDtypeStruct((n // tm, 1, tm), jnp.int32),
                   jax.ShapeDtypeStruct((8, META_LANES), jnp.int32)],
        scratch_shapes=[pltpu.VMEM((BUCKET_ROWS, LANES), F32), pltpu.VMEM((BUCKET_ROWS, LANES), F32)],
        compiler_params=_params(("arbitrary",)),
        name="plan_pos",
    )(info, sel, cnt, ltri, utri)


def _dispatch_kernel(pos_ref, h1_ref, gain2_ref, sh2_ref, sc2_ref, info_ref, xs_in_ref, xs_ref, rowbuf, sem,
                     *, tm, n_steps):
    del xs_in_ref
    step = pl.program_id(0) * pl.num_programs(1) + pl.program_id(1)
    slot = step % 2

    def wait(s):
        pltpu.make_async_copy(rowbuf.at[s], xs_ref.at[pl.ds(0, tm)], sem.at[s]).wait()

    @pl.when(step >= 2)
    def _():
        wait(slot)

    rowbuf[slot, :, 0:D_MODEL] = _modulate(h1_ref[0], gain2_ref[...], sh2_ref[...], sc2_ref[...])
    rowbuf[slot, :, D_MODEL:] = info_ref[0]

    def issue(r, carry):
        pltpu.make_async_copy(rowbuf.at[slot, pl.ds(r, 1)], xs_ref.at[pl.ds(pos_ref[0, 0, r], 1)],
                              sem.at[slot]).start()
        return carry

    lax.fori_loop(0, tm, issue, 0, unroll=8)

    @pl.when(step == n_steps - 1)
    def _():
        wait(slot)
        if n_steps > 1:
            wait(1 - slot)


def _gmoe_kernel(ea_ref, eb_ref, nu_ref, xs_ref, wga_ref, wua_ref, wda_ref, wgb_ref, wub_ref, wdb_ref, ys_ref):
    del ea_ref, eb_ref
    in_use = pl.program_id(0) < nu_ref[0]

    @pl.when(jnp.logical_not(in_use))
    def _():
        ys_ref[...] = jnp.zeros_like(ys_ref)

    @pl.when(in_use)
    def _():
        x = xs_ref[:, 0:D_MODEL].astype(BF16)

        def expert(wg_ref, wu_ref, wd_ref, w):
            gt = jnp.dot(x, wg_ref[0], preferred_element_type=F32)
            up = jnp.dot(x, wu_ref[0], preferred_element_type=F32)
            h = (gt * jax.nn.sigmoid(gt)) * up * w
            return jnp.dot(h.astype(BF16), wd_ref[0], preferred_element_type=F32)

        w_lo = xs_ref[:, D_MODEL + INFO_W_LO:D_MODEL + INFO_W_LO + 1]
        w_hi = xs_ref[:, D_MODEL + INFO_W_HI:D_MODEL + INFO_W_HI + 1]
        ys_ref[...] = expert(wga_ref, wua_ref, wda_ref, w_lo) + expert(wgb_ref, wub_ref, wdb_ref, w_hi)


def _combine_kernel(pos_ref, posn_ref, h1_ref, g2_ref, ys_ref, o_ref, ybuf, sem, *, tm, n_steps):
    step = pl.program_id(0) * pl.num_programs(1) + pl.program_id(1)
    slot = step % 2

    def issue(p_ref, s):
        def body(r, carry):
            pltpu.make_async_copy(ys_ref.at[pl.ds(p_ref[0, 0, r], 1)], ybuf.at[s, pl.ds(r, 1)],
                                  sem.at[s]).start()
            return carry

        lax.fori_loop(0, tm, body, 0, unroll=8)

    @pl.when(step == 0)
    def _():
        issue(pos_ref, slot)

    @pl.when(step + 1 < n_steps)
    def _():
        issue(posn_ref, 1 - slot)

    pltpu.make_async_copy(ys_ref.at[pl.ds(0, tm)], ybuf.at[slot], sem.at[slot]).wait()
    o_ref[0] = h1_ref[0] + g2_ref[...] * ybuf[slot]


def _sparse_moe(h1, info, mods, layer, gain2, w_gate, w_up, w_down, tm):
    bsz, seq, d = h1.shape
    nt = seq // tm
    n_steps = bsz * nt
    n = bsz * seq
    n_sorted = n + N_BUCKETS * SORT_TILE
    n_tiles = n_sorted // SORT_TILE
    assert n % SORT_TILE == 0 and n_tiles <= META_LANES
    pos, meta = _sort_plan(info.reshape(n, ROUTER_LANES), tm)
    pos_spec = lambda off: pl.BlockSpec(
        (1, 1, tm), lambda b, i: (jnp.minimum(b * nt + i + off, n_steps - 1), 0, 0), memory_space=pltpu.SMEM)
    tile = lambda w: pl.BlockSpec((1, tm, w), lambda b, i: (b, i, 0))
    any_spec = pl.BlockSpec(memory_space=pl.ANY)
    by_batch = lambda b, i: b

    xs = pl.pallas_call(
        functools.partial(_dispatch_kernel, tm=tm, n_steps=n_steps),
        grid=(bsz, nt),
        in_specs=[pos_spec(0), tile(d), _full((1, d)), _mod_spec(layer, by_batch, 3),
                  _mod_spec(layer, by_batch, 4), tile(ROUTER_LANES), any_spec],
        out_specs=any_spec,
        out_shape=jax.ShapeDtypeStruct((n_sorted, ROW_WIDTH), F32),
        input_output_aliases={6: 0},
        scratch_shapes=[pltpu.VMEM((2, tm, ROW_WIDTH), F32), pltpu.SemaphoreType.DMA((2,))],
        compiler_params=_params(("arbitrary", "arbitrary")),
        name=f"dispatch{layer}",
    )(pos, h1, gain2.reshape(1, d), mods, mods, info, jnp.zeros((n_sorted, ROW_WIDTH), F32))

    used = lambda j, nu: jnp.minimum(j, nu[0] - 1)
    w_in_spec = lambda which: pl.BlockSpec(
        (1, d, D_EXPERT), lambda j, ea, eb, nu: ((ea, eb)[which][used(j, nu)], 0, 0))
    w_out_spec = lambda which: pl.BlockSpec(
        (1, D_EXPERT, d), lambda j, ea, eb, nu: ((ea, eb)[which][used(j, nu)], 0, 0))
    wg, wu, wd = w_gate.astype(BF16), w_up.astype(BF16), w_down.astype(BF16)
    ys = pl.pallas_call(
        _gmoe_kernel,
        grid_spec=pltpu.PrefetchScalarGridSpec(
            num_scalar_prefetch=3,
            grid=(n_tiles,),
            in_specs=[pl.BlockSpec((SORT_TILE, ROW_WIDTH), lambda j, ea, eb, nu: (used(j, nu), 0)),
                      w_in_spec(0), w_in_spec(0), w_out_spec(0), w_in_spec(1), w_in_spec(1), w_out_spec(1)],
            out_specs=pl.BlockSpec((SORT_TILE, d), lambda j, ea, eb, nu: (j, 0))),
        out_shape=jax.ShapeDtypeStruct((n_sorted, d), F32),
        compiler_params=_params(("arbitrary",)),
        name=f"experts{layer}",
    )(meta[0], meta[1], meta[2, :1], xs, wg, wu, wd, wg, wu, wd)

    return pl.pallas_call(
        functools.partial(_combine_kernel, tm=tm, n_steps=n_steps),
        grid=(bsz, nt),
        in_specs=[pos_spec(0), pos_spec(1), tile(d), _mod_spec(layer, by_batch, 5), any_spec],
        out_specs=tile(d),
        out_shape=jax.ShapeDtypeStruct((bsz, seq, d), F32),
        scratch_shapes=[pltpu.VMEM((2, tm, d), F32), pltpu.SemaphoreType.DMA((2,))],
        compiler_params=_params(("arbitrary", "arbitrary")),
        name=f"combine{layer}",
    )(pos, pos, h1, mods, ys)


def kernel(x, c, ctx, c_ctx, mod_w, mod_b, norm1_g, norm2_g, even_w_in, q_gain, k_gain, pool_w, pool_scale,
           even_w_out, odd_w_in, sg_gain, sg_w, sg_b, conv_w, odd_w_out, router_g_w, router_g_b,
           router_e_w, router_e_b, w_gate, w_up, w_down):
    bsz, seq, d = x.shape
    tm = min(512, seq)
    cond = jnp.zeros((MOD_ROWS, d), F32).at[:bsz].set(c).at[bsz].set(c_ctx)
    mods = _adaln(cond, mod_w, mod_b).reshape(mod_w.shape[0], MOD_ROWS, 6, 1, d)

    q, k, v, p = _inproj0(x, mods, norm1_g[0], even_w_in[0], q_gain[0], k_gain[0], tm)
    kc, vc = _inproj0_ctx(ctx, mods, bsz, norm1_g[0], even_w_in[0][:, ATTN_WIDTH:ATTN_WIDTH + 2 * KV_WIDTH],
                          k_gain[0])
    o = _attention(q, k, v, kc, vc, tq=min(128, seq), tk=min(1024, seq))
    router0 = _router_operands(router_g_w[0], router_g_b[0], router_e_w[0], router_e_b[0])
    h1, info = _out0(o, p, x, mods, norm2_g[0], pool_w[0], pool_scale[0], even_w_out[0], router0, tm)
    h = _sparse_moe(h1, info, mods, 0, norm2_g[0], w_gate[0], w_up[0], w_down[0], tm)

    yc, z, bg = _inproj1(h, mods, norm1_g[1], odd_w_in[0], sg_gain[0], sg_w[0], sg_b[0], tm)
    router1 = _router_operands(router_g_w[1], router_g_b[1], router_e_w[1], router_e_b[1])
    h1, info = _out1(yc, z, bg, h, mods, norm2_g[1], conv_w[0], odd_w_out[0], router1, tm)
    return _sparse_moe(h1, info, mods, 1, norm2_g[1], w_gate[1], w_up[1], w_down[1], tm)
```

```python
import functools

import jax
import jax.numpy as jnp
from jax import lax
from jax.experimental import pallas as pl
from jax.experimental.pallas import tpu as pltpu

F32 = jnp.float32
BF16 = jnp.bfloat16

D_MODEL = 1024
GRID_W = 64
EPS = 1e-6
N_Q_HEADS = 8
N_KV_HEADS = 2
HEAD_DIM = 64
Q_PER_KV = N_Q_HEADS // N_KV_HEADS
ATTN_WIDTH = N_Q_HEADS * HEAD_DIM
KV_WIDTH = N_KV_HEADS * HEAD_DIM
ROPE_THETA = 10000.0
POOL_WINDOWS = (2, 4, 8, 16)
POOL_GROUP = 128
POOL_WIDTH = POOL_GROUP * len(POOL_WINDOWS)
SG_GROUPS = 4
SG_CHUNK = 128
SG_WIDTH = 512
CONV_WIDTH = 512
EVEN_IN = ATTN_WIDTH + 2 * KV_WIDTH + POOL_WIDTH
ODD_IN = 2 * SG_WIDTH + 3 * CONV_WIDTH
N_GROUPS = 4
EXPERTS_PER_GROUP = 4
N_EXPERTS = 16
D_EXPERT = 256

Q_SCALE = HEAD_DIM ** -0.5 * 1.4426950408889634
LANES = 128
HALO = 8
ROUTER_LANES = 128
MOD_ROWS = 16
VMEM_LIMIT = 48 * 1024 * 1024

PAIRS_PER_GROUP = 6
N_BUCKETS = N_GROUPS * PAIRS_PER_GROUP
BUCKET_ROWS = 32
SORT_TILE = 256
META_LANES = 256
META_EXPERT_LO, META_EXPERT_HI, META_N_USED, META_FILL = 0, 1, 2, 3
ROW_WIDTH = D_MODEL + ROUTER_LANES
INFO_BUCKET, INFO_W_LO, INFO_W_HI = 0, 1, 2


def _params(sem):
    return pltpu.CompilerParams(dimension_semantics=sem, vmem_limit_bytes=VMEM_LIMIT)


def _modulate(x, gain, shift, scale):
    ms = jnp.mean(x * x, axis=-1, keepdims=True)
    return (x * lax.rsqrt(ms + EPS) * gain) * (1.0 + scale) + shift


def _mod_spec(layer, row_fn, which):
    return pl.BlockSpec((None, None, None, 1, D_MODEL),
                        lambda *idx: (layer, row_fn(*idx), which, 0, 0))


def _full(shape):
    return pl.BlockSpec(shape, lambda *idx: (0,) * len(shape))


def _adaln_kernel(c_ref, w_ref, b_ref, o_ref):
    c = c_ref[...]
    s = c * jax.nn.sigmoid(c)
    o_ref[0] = jnp.dot(s, w_ref[0], precision=lax.Precision.HIGHEST,
                       preferred_element_type=F32) + b_ref[0]


def _adaln(cond, mod_w, mod_b):
    depth, d, n = mod_w.shape
    tn = 1024
    return pl.pallas_call(
        _adaln_kernel,
        grid=(depth, n // tn),
        in_specs=[_full((MOD_ROWS, d)),
                  pl.BlockSpec((1, d, tn), lambda l, j: (l, 0, j)),
                  pl.BlockSpec((1, 1, tn), lambda l, j: (l, 0, j))],
        out_specs=pl.BlockSpec((1, MOD_ROWS, tn), lambda l, j: (l, 0, j)),
        out_shape=jax.ShapeDtypeStruct((depth, MOD_ROWS, n), F32),
        compiler_params=_params(("arbitrary", "arbitrary")),
        name="adaln",
    )(cond, mod_w, mod_b.reshape(depth, 1, n))


def _head_norm_rope(z, gain, ones_bd, cos, sin, first_half):
    sq = z * z
    hi = sq.astype(BF16)
    lo = (sq - hi.astype(F32)).astype(BF16)
    ms = (jnp.dot(hi, ones_bd, preferred_element_type=F32)
          + jnp.dot(lo, ones_bd, preferred_element_type=F32))
    zn = z * lax.rsqrt(ms + EPS) * gain
    partner = jnp.where(first_half, pltpu.roll(zn, LANES - 16, 1), pltpu.roll(zn, 16, 1))
    return zn * cos + partner * sin


def _inproj0_kernel(x_ref, gain_ref, sh_ref, sc_ref, w_ref, cos_ref, sin_ref, qg_ref, kg_ref, ones_ref,
                    q_ref, k_ref, v_ref, p_ref):
    a = _modulate(x_ref[0], gain_ref[...], sh_ref[...], sc_ref[...])
    y = jnp.dot(a.astype(BF16), w_ref[...], preferred_element_type=F32)
    cos, sin, ones_bd = cos_ref[...], sin_ref[...], ones_ref[...]
    lane = lax.broadcasted_iota(jnp.int32, cos.shape, 1)
    first_half = (lane % 32) < 16
    for s in range(ATTN_WIDTH // LANES):
        r = _head_norm_rope(y[:, s * LANES:(s + 1) * LANES], qg_ref[...], ones_bd, cos, sin, first_half)
        r = (r * Q_SCALE).astype(BF16)
        q_ref[0, 2 * s] = r[:, :HEAD_DIM]
        q_ref[0, 2 * s + 1] = r[:, HEAD_DIM:]
    kr = _head_norm_rope(y[:, ATTN_WIDTH:ATTN_WIDTH + KV_WIDTH], kg_ref[...], ones_bd, cos, sin,
                         first_half).astype(BF16)
    k_ref[0, 0] = kr[:, :HEAD_DIM]
    k_ref[0, 1] = kr[:, HEAD_DIM:]
    vv = y[:, ATTN_WIDTH + KV_WIDTH:ATTN_WIDTH + 2 * KV_WIDTH].astype(BF16)
    v_ref[0, 0] = vv[:, :HEAD_DIM]
    v_ref[0, 1] = vv[:, HEAD_DIM:]
    p_ref[0] = y[:, ATTN_WIDTH + 2 * KV_WIDTH:]


def _inproj0_ctx_kernel(x_ref, gain_ref, sh_ref, sc_ref, w_ref, kg_ref, ones_ref, k_ref, v_ref):
    a = _modulate(x_ref[0], gain_ref[...], sh_ref[...], sc_ref[...])
    y = jnp.dot(a.astype(BF16), w_ref[...], preferred_element_type=F32)
    z = y[:, :KV_WIDTH]
    sq = z * z
    hi = sq.astype(BF16)
    lo = (sq - hi.astype(F32)).astype(BF16)
    ms = (jnp.dot(hi, ones_ref[...], preferred_element_type=F32)
          + jnp.dot(lo, ones_ref[...], preferred_element_type=F32))
    kr = (z * lax.rsqrt(ms + EPS) * kg_ref[...]).astype(BF16)
    k_ref[0, 0] = kr[:, :HEAD_DIM]
    k_ref[0, 1] = kr[:, HEAD_DIM:]
    vv = y[:, KV_WIDTH:].astype(BF16)
    v_ref[0, 0] = vv[:, :HEAD_DIM]
    v_ref[0, 1] = vv[:, HEAD_DIM:]


def _rope_tables(seq):
    t = jnp.arange(seq)
    row = (t // GRID_W).astype(F32)
    col = (t % GRID_W).astype(F32)
    half = HEAD_DIM // 2
    inv = ROPE_THETA ** (-jnp.arange(0, half, 2, dtype=F32) / half)
    ar, ac = row[:, None] * inv, col[:, None] * inv
    cos = jnp.concatenate([jnp.cos(ar), jnp.cos(ar), jnp.cos(ac), jnp.cos(ac)], axis=-1)
    sin = jnp.concatenate([-jnp.sin(ar), jnp.sin(ar), -jnp.sin(ac), jnp.sin(ac)], axis=-1)
    return jnp.tile(cos, (1, LANES // HEAD_DIM)), jnp.tile(sin, (1, LANES // HEAD_DIM))


def _head_mean_matrix():
    r = jnp.arange(LANES)
    same = (r[:, None] // HEAD_DIM) == (r[None, :] // HEAD_DIM)
    return jnp.where(same, 1.0 / HEAD_DIM, 0.0).astype(BF16)


def _inproj0(x, mods, gain, w_in, q_gain, k_gain, tm):
    bsz, seq, d = x.shape
    cos, sin = _rope_tables(seq)
    qg = jnp.tile(q_gain, LANES // HEAD_DIM).reshape(1, LANES)
    kg = jnp.tile(k_gain, LANES // HEAD_DIM).reshape(1, LANES)
    head = lambda n: pl.BlockSpec((1, n, tm, HEAD_DIM), lambda b, i: (b, 0, i, 0))
    return pl.pallas_call(
        _inproj0_kernel,
        grid=(bsz, seq // tm),
        in_specs=[pl.BlockSpec((1, tm, d), lambda b, i: (b, i, 0)),
                  _full((1, d)),
                  _mod_spec(0, lambda b, i: b, 0),
                  _mod_spec(0, lambda b, i: b, 1),
                  _full((d, EVEN_IN)),
                  pl.BlockSpec((tm, LANES), lambda b, i: (i, 0)),
                  pl.BlockSpec((tm, LANES), lambda b, i: (i, 0)),
                  _full((1, LANES)), _full((1, LANES)), _full((LANES, LANES))],
        out_specs=[head(N_Q_HEADS), head(N_KV_HEADS), head(N_KV_HEADS),
                   pl.BlockSpec((1, tm, POOL_WIDTH), lambda b, i: (b, i, 0))],
        out_shape=[jax.ShapeDtypeStruct((bsz, N_Q_HEADS, seq, HEAD_DIM), BF16),
                   jax.ShapeDtypeStruct((bsz, N_KV_HEADS, seq, HEAD_DIM), BF16),
                   jax.ShapeDtypeStruct((bsz, N_KV_HEADS, seq, HEAD_DIM), BF16),
                   jax.ShapeDtypeStruct((bsz, seq, POOL_WIDTH), F32)],
        compiler_params=_params(("parallel", "parallel")),
        name="inproj0",
    )(x, gain.reshape(1, d), mods, mods, w_in.astype(BF16), cos, sin, qg, kg, _head_mean_matrix())


def _inproj0_ctx(ctx, mods, ctx_row, gain, w_kv, k_gain):
    bsz, n_ctx, d = ctx.shape
    kg = jnp.tile(k_gain, LANES // HEAD_DIM).reshape(1, LANES)
    head = pl.BlockSpec((1, N_KV_HEADS, n_ctx, HEAD_DIM), lambda b: (b, 0, 0, 0))
    return pl.pallas_call(
        _inproj0_ctx_kernel,
        grid=(bsz,),
        in_specs=[pl.BlockSpec((1, n_ctx, d), lambda b: (b, 0, 0)),
                  _full((1, d)),
                  _mod_spec(0, lambda b: ctx_row, 0),
                  _mod_spec(0, lambda b: ctx_row, 1),
                  _full((d, 2 * KV_WIDTH)),
                  _full((1, LANES)), _full((LANES, LANES))],
        out_specs=[head, head],
        out_shape=[jax.ShapeDtypeStruct((bsz, N_KV_HEADS, n_ctx, HEAD_DIM), BF16)] * 2,
        compiler_params=_params(("parallel",)),
        name="inproj0_ctx",
    )(ctx, gain.reshape(1, d), mods, mods, w_kv.astype(BF16), kg, _head_mean_matrix())


def _attn_kernel(q_ref, kl_ref, vl_ref, kc_ref, vc_ref, o_ref, *, tq, tk):
    rows = Q_PER_KV * tq
    q = q_ref[0].reshape(rows, HEAD_DIM)
    seq = kl_ref.shape[2]

    def step(k, v, carry):
        m, l, acc = carry
        s = lax.dot_general(q, k, (((1,), (1,)), ((), ())), preferred_element_type=F32)
        m_new = jnp.maximum(m, jnp.max(s, axis=1, keepdims=True))
        alpha = jnp.exp2(m - m_new)
        p = jnp.exp2(s - m_new)
        l = alpha * l + jnp.sum(p, axis=1, keepdims=True)
        acc = alpha * acc + jnp.dot(p.astype(BF16), v, preferred_element_type=F32)
        return m_new, l, acc

    carry = (jnp.full((rows, 1), -jnp.inf, F32), jnp.zeros((rows, 1), F32),
             jnp.zeros((rows, HEAD_DIM), F32))
    carry = step(kc_ref[0, 0], vc_ref[0, 0], carry)
    for c in range(seq // tk):
        carry = step(kl_ref[0, 0, c * tk:(c + 1) * tk, :], vl_ref[0, 0, c * tk:(c + 1) * tk, :], carry)
    _, l, acc = carry
    o = acc / l
    o_ref[0] = jnp.concatenate([o[h * tq:(h + 1) * tq] for h in range(Q_PER_KV)], axis=1).astype(BF16)


def _attention(q, k, v, kc, vc, tq, tk):
    bsz, _, seq, _ = q.shape
    n_ctx = kc.shape[2]
    kv_spec = lambda n: pl.BlockSpec((1, 1, n, HEAD_DIM), lambda b, g, i: (b, g, 0, 0))
    return pl.pallas_call(
        functools.partial(_attn_kernel, tq=tq, tk=tk),
        grid=(bsz, N_KV_HEADS, seq // tq),
        in_specs=[pl.BlockSpec((1, Q_PER_KV, tq, HEAD_DIM), lambda b, g, i: (b, g, i, 0)),
                  kv_spec(seq), kv_spec(seq), kv_spec(n_ctx), kv_spec(n_ctx)],
        out_specs=pl.BlockSpec((1, tq, Q_PER_KV * HEAD_DIM), lambda b, g, i: (b, i, g)),
        out_shape=jax.ShapeDtypeStruct((bsz, seq, ATTN_WIDTH), BF16),
        compiler_params=_params(("parallel", "parallel", "parallel")),
        name="attention",
    )(q, k, v, kc, vc)


def _route(logits):
    lane = lax.broadcasted_iota(jnp.int32, logits.shape, 1)
    neg = -jnp.inf
    big = ROUTER_LANES
    is_g = lane < N_GROUPS
    gm = jnp.max(jnp.where(is_g, logits, neg), axis=1, keepdims=True)
    gidx = jnp.min(jnp.where(is_g & (logits == gm), lane, big), axis=1, keepdims=True)
    gden = jnp.sum(jnp.where(is_g, jnp.exp(logits - gm), 0.0), axis=1, keepdims=True)
    g_p = 1.0 / gden
    first = N_GROUPS + EXPERTS_PER_GROUP * gidx
    sel = (lane >= first) & (lane < first + EXPERTS_PER_GROUP)
    e1 = jnp.max(jnp.where(sel, logits, neg), axis=1, keepdims=True)
    i1 = jnp.min(jnp.where(sel & (logits == e1), lane, big), axis=1, keepdims=True)
    rest = sel & (lane != i1)
    e2 = jnp.max(jnp.where(rest, logits, neg), axis=1, keepdims=True)
    i2 = jnp.min(jnp.where(rest & (logits == e2), lane, big), axis=1, keepdims=True)
    p2 = jnp.exp(e2 - e1)
    w1 = g_p * (1.0 / (1.0 + p2))
    w2 = g_p * (p2 / (1.0 + p2))
    lo = jnp.minimum(i1, i2) - first
    hi = jnp.maximum(i1, i2) - first
    pair = jnp.where(lo == 0, hi - 1, jnp.where(lo == 1, hi + 1, PAIRS_PER_GROUP - 1))
    bucket = (PAIRS_PER_GROUP * gidx + pair).astype(F32)
    w_lo = jnp.where(i1 < i2, w1, w2)
    w_hi = jnp.where(i1 < i2, w2, w1)
    return jnp.where(lane == INFO_BUCKET, bucket,
                     jnp.where(lane == INFO_W_LO, w_lo, jnp.where(lane == INFO_W_HI, w_hi, 0.0)))


def _bucket_onehot(info, sel):
    brow = lax.dot_general(sel, info.astype(BF16), (((1,), (1,)), ((), ())),
                           preferred_element_type=F32)[0:1]
    bid = lax.broadcasted_iota(jnp.int32, (BUCKET_ROWS, info.shape[0]), 0)
    return (bid == brow.astype(jnp.int32)).astype(F32)


def _bucket_selector():
    return jnp.zeros((8, ROUTER_LANES), F32).at[0, INFO_BUCKET].set(1.0).astype(BF16)


def _tail(y, x_res, gate1, gain2, shift2, scale2, rw_hi, rw_lo, rbias, sel, h1_ref, info_ref, cnt_ref):
    h1 = x_res + gate1 * y
    h1_ref[0] = h1
    t = _modulate(h1, gain2, shift2, scale2)
    t_hi = t.astype(BF16)
    t_lo = (t - t_hi.astype(F32)).astype(BF16)
    logits = (jnp.dot(t_hi, rw_hi, preferred_element_type=F32)
              + jnp.dot(t_lo, rw_hi, preferred_element_type=F32)
              + jnp.dot(t_hi, rw_lo, preferred_element_type=F32)) + rbias
    info = _route(logits)
    info_ref[0] = info

    @pl.when((pl.program_id(0) == 0) & (pl.program_id(1) == 0))
    def _():
        cnt_ref[...] = jnp.zeros_like(cnt_ref)

    cnt_ref[...] += jnp.sum(_bucket_onehot(info, sel), axis=1, keepdims=True)


def _router_operands(rg_w, rg_b, re_w, re_b):
    d = rg_w.shape[0]
    w = jnp.concatenate([rg_w, re_w, jnp.zeros((d, ROUTER_LANES - N_GROUPS - N_EXPERTS), F32)], axis=1)
    b = jnp.concatenate([rg_b, re_b, jnp.zeros((ROUTER_LANES - N_GROUPS - N_EXPERTS,), F32)])
    w_hi = w.astype(BF16)
    w_lo = (w - w_hi.astype(F32)).astype(BF16)
    return w_hi, w_lo, b.reshape(1, ROUTER_LANES)


def _fill_halo(buf, main_ref, prev_ref, next_ref, tm, i, n_tiles):
    buf[HALO:HALO + tm] = main_ref[0]
    buf[0:HALO] = jnp.where(i > 0, prev_ref[0], 0.0)
    buf[HALO + tm:2 * HALO + tm] = jnp.where(i < n_tiles - 1, next_ref[0], 0.0)


def _halo_specs(tm, seq, width):
    per = tm // HALO
    last = seq // HALO - 1
    return [pl.BlockSpec((1, tm, width), lambda b, i: (b, i, 0)),
            pl.BlockSpec((1, HALO, width), lambda b, i: (b, jnp.maximum(i * per - 1, 0), 0)),
            pl.BlockSpec((1, HALO, width), lambda b, i: (b, jnp.minimum((i + 1) * per, last), 0))]


def _out0_kernel(o_ref, p_ref, pprev_ref, pnext_ref, x_ref, g1_ref, gain2_ref, sh2_ref, sc2_ref,
                 poolw_ref, pscale_ref, wout_ref, rwh_ref, rwl_ref, rb_ref, sel_ref,
                 h1_ref, info_ref, cnt_ref, pbuf, *, tm, seq):
    i = pl.program_id(1)
    _fill_halo(pbuf, p_ref, pprev_ref, pnext_ref, tm, i, seq // tm)
    pos = i * tm + lax.broadcasted_iota(jnp.int32, (tm, 1), 0)
    y = jnp.dot(o_ref[0], wout_ref[0:ATTN_WIDTH, :], preferred_element_type=F32)
    for g, w in enumerate(POOL_WINDOWS):
        sl = slice(g * POOL_GROUP, (g + 1) * POOL_GROUP)
        acc = pbuf[HALO - w // 2:HALO - w // 2 + tm, sl]
        for j in range(1 - w // 2, w - w // 2):
            acc = acc + pbuf[HALO + j:HALO + j + tm, sl]
        lo = jnp.clip(pos - w // 2, 0, seq)
        hi = jnp.clip(pos + w - w // 2, 0, seq)
        mean = acc / (hi - lo).astype(F32)
        dlt = (mean - pbuf[HALO:HALO + tm, sl]).astype(BF16)
        yp = jnp.dot(dlt, poolw_ref[g], preferred_element_type=F32) * pscale_ref[:, sl]
        y = y + jnp.dot(yp.astype(BF16), wout_ref[ATTN_WIDTH + g * POOL_GROUP:ATTN_WIDTH + (g + 1) * POOL_GROUP, :],
                        preferred_element_type=F32)
    _tail(y, x_ref[0], g1_ref[...], gain2_ref[...], sh2_ref[...], sc2_ref[...],
          rwh_ref[...], rwl_ref[...], rb_ref[...], sel_ref[...], h1_ref, info_ref, cnt_ref)


def _tail_specs(layer, d):
    by_batch = lambda b, i: b
    ins = [_mod_spec(layer, by_batch, 2), _full((1, d)), _mod_spec(layer, by_batch, 3),
           _mod_spec(layer, by_batch, 4)]
    return ins


def _tail_outs(bsz, seq, d, tm):
    specs = [pl.BlockSpec((1, tm, d), lambda b, i: (b, i, 0)),
             pl.BlockSpec((1, tm, ROUTER_LANES), lambda b, i: (b, i, 0)),
             _full((BUCKET_ROWS, LANES))]
    shapes = [jax.ShapeDtypeStruct((bsz, seq, d), F32),
              jax.ShapeDtypeStruct((bsz, seq, ROUTER_LANES), F32),
              jax.ShapeDtypeStruct((BUCKET_ROWS, LANES), F32)]
    return specs, shapes


def _out0(o, p, x, mods, gain2, pool_w, pool_scale, w_out, router, tm):
    bsz, seq, d = x.shape
    rw_hi, rw_lo, rb = router
    out_specs, out_shapes = _tail_outs(bsz, seq, d, tm)
    return pl.pallas_call(
        functools.partial(_out0_kernel, tm=tm, seq=seq),
        grid=(bsz, seq // tm),
        in_specs=[pl.BlockSpec((1, tm, ATTN_WIDTH), lambda b, i: (b, i, 0))]
        + _halo_specs(tm, seq, POOL_WIDTH)
        + [pl.BlockSpec((1, tm, d), lambda b, i: (b, i, 0))]
        + _tail_specs(0, d)
        + [_full(pool_w.shape), _full((1, POOL_WIDTH)), _full(w_out.shape),
           _full(rw_hi.shape), _full(rw_lo.shape), _full(rb.shape), _full((8, ROUTER_LANES))],
        out_specs=out_specs,
        out_shape=out_shapes,
        scratch_shapes=[pltpu.VMEM((tm + 2 * HALO, POOL_WIDTH), F32)],
        compiler_params=_params(("arbitrary", "arbitrary")),
        name="out0",
    )(o, p, p, p, x, mods, gain2.reshape(1, d), mods, mods,
      pool_w.astype(BF16), pool_scale.reshape(1, POOL_WIDTH), w_out.astype(BF16), rw_hi, rw_lo, rb,
      _bucket_selector())


def _inproj1_kernel(x_ref, gain_ref, sh_ref, sc_ref, w_ref, sgg_ref, sgw_ref, sgb_ref,
                    yc_ref, z_ref, bg_ref, *, tm):
    a = _modulate(x_ref[0], gain_ref[...], sh_ref[...], sc_ref[...])
    y = jnp.dot(a.astype(BF16), w_ref[...], preferred_element_type=F32)
    for g in range(SG_GROUPS):
        sl = slice(g * LANES, (g + 1) * LANES)
        u = y[:, sl]
        vg = y[:, SG_WIDTH + g * LANES:SG_WIDTH + (g + 1) * LANES]
        ms = jnp.mean(vg * vg, axis=-1, keepdims=True)
        vn = (vg * lax.rsqrt(ms + EPS) * sgg_ref[:, sl]).astype(BF16)
        for c in range(tm // SG_CHUNK):
            rows = slice(c * SG_CHUNK, (c + 1) * SG_CHUNK)
            s = jnp.dot(sgw_ref[g], vn[rows], preferred_element_type=F32) + sgb_ref[g]
            yc_ref[0, rows, sl] = (u[rows] * s).astype(BF16)
    hx = y[:, 2 * SG_WIDTH:2 * SG_WIDTH + CONV_WIDTH]
    bg_ref[0] = y[:, 2 * SG_WIDTH + CONV_WIDTH:2 * SG_WIDTH + 2 * CONV_WIDTH]
    cg = y[:, 2 * SG_WIDTH + 2 * CONV_WIDTH:]
    z_ref[0] = cg * hx


def _inproj1(x, mods, gain, w_in, sg_gain, sg_w, sg_b, tm):
    bsz, seq, d = x.shape
    sgb = jnp.broadcast_to(sg_b[:, :, None], (SG_GROUPS, SG_CHUNK, LANES))
    wide = lambda dt: (pl.BlockSpec((1, tm, SG_WIDTH), lambda b, i: (b, i, 0)),
                       jax.ShapeDtypeStruct((bsz, seq, SG_WIDTH), dt))
    outs = [wide(BF16), wide(F32), wide(F32)]
    return pl.pallas_call(
        functools.partial(_inproj1_kernel, tm=tm),
        grid=(bsz, seq // tm),
        in_specs=[pl.BlockSpec((1, tm, d), lambda b, i: (b, i, 0)),
                  _full((1, d)),
                  _mod_spec(1, lambda b, i: b, 0),
                  _mod_spec(1, lambda b, i: b, 1),
                  _full((d, ODD_IN)),
                  _full((1, SG_WIDTH)), _full(sg_w.shape), _full(sgb.shape)],
        out_specs=[s for s, _ in outs],
        out_shape=[s for _, s in outs],
        compiler_params=_params(("parallel", "parallel")),
        name="inproj1",
    )(x, gain.reshape(1, d), mods, mods, w_in.astype(BF16), sg_gain.reshape(1, SG_WIDTH),
      sg_w.astype(BF16), sgb)


def _out1_kernel(yc_ref, z_ref, zprev_ref, znext_ref, bg_ref, x_ref, g1_ref, gain2_ref, sh2_ref, sc2_ref,
                 convw_ref, wout_ref, rwh_ref, rwl_ref, rb_ref, sel_ref,
                 h1_ref, info_ref, cnt_ref, zbuf, *, tm, seq):
    i = pl.program_id(1)
    _fill_halo(zbuf, z_ref, zprev_ref, znext_ref, tm, i, seq // tm)
    zc = (zbuf[HALO - 1:HALO - 1 + tm] * convw_ref[0:1, :]
          + zbuf[HALO:HALO + tm] * convw_ref[1:2, :]
          + zbuf[HALO + 1:HALO + 1 + tm] * convw_ref[2:3, :])
    yd = (bg_ref[0] * zc).astype(BF16)
    y = (jnp.dot(yc_ref[0], wout_ref[0:SG_WIDTH, :], preferred_element_type=F32)
         + jnp.dot(yd, wout_ref[SG_WIDTH:, :], preferred_element_type=F32))
    _tail(y, x_ref[0], g1_ref[...], gain2_ref[...], sh2_ref[...], sc2_ref[...],
          rwh_ref[...], rwl_ref[...], rb_ref[...], sel_ref[...], h1_ref, info_ref, cnt_ref)


def _out1(yc, z, bg, x, mods, gain2, conv_w, w_out, router, tm):
    bsz, seq, d = x.shape
    rw_hi, rw_lo, rb = router
    out_specs, out_shapes = _tail_outs(bsz, seq, d, tm)
    wide = pl.BlockSpec((1, tm, CONV_WIDTH), lambda b, i: (b, i, 0))
    return pl.pallas_call(
        functools.partial(_out1_kernel, tm=tm, seq=seq),
        grid=(bsz, seq // tm),
        in_specs=[wide] + _halo_specs(tm, seq, CONV_WIDTH) + [wide]
        + [pl.BlockSpec((1, tm, d), lambda b, i: (b, i, 0))]
        + _tail_specs(1, d)
        + [_full((3, CONV_WIDTH)), _full(w_out.shape),
           _full(rw_hi.shape), _full(rw_lo.shape), _full(rb.shape), _full((8, ROUTER_LANES))],
        out_specs=out_specs,
        out_shape=out_shapes,
        scratch_shapes=[pltpu.VMEM((tm + 2 * HALO, CONV_WIDTH), F32)],
        compiler_params=_params(("arbitrary", "arbitrary")),
        name="out1",
    )(yc, z, z, z, bg, x, mods, gain2.reshape(1, d), mods, mods,
      conv_w.reshape(3, CONV_WIDTH), w_out.astype(BF16), rw_hi, rw_lo, rb, _bucket_selector())


def _plan_pos_kernel(info_ref, sel_ref, cnt_ref, ltri_ref, utri_ref, pos_ref, meta_ref, start_sc, run_sc):
    @pl.when(pl.program_id(0) == 0)
    def _():
        padded = jnp.ceil(cnt_ref[...] * (1.0 / SORT_TILE)) * SORT_TILE
        incl = jnp.dot(ltri_ref[...], padded, precision=lax.Precision.HIGHEST, preferred_element_type=F32)
        start_sc[...] = incl - padded
        run_sc[...] = jnp.zeros_like(run_sc)
        ends = jnp.broadcast_to(incl[:, 0:1], (BUCKET_ROWS, META_LANES))
        bid = lax.broadcasted_iota(jnp.int32, ends.shape, 0)
        tile = lax.broadcasted_iota(jnp.int32, (1, META_LANES), 1)

        def bucket_of(row0):
            done = jnp.where((bid < N_BUCKETS) & (ends <= row0), 1.0, 0.0)
            return jnp.minimum(jnp.sum(done, axis=0, keepdims=True), N_BUCKETS - 1.0).astype(jnp.int32)

        row0 = (tile * SORT_TILE).astype(F32)
        tb = bucket_of(row0)
        grp = ((tb >= PAIRS_PER_GROUP).astype(jnp.int32) + (tb >= 2 * PAIRS_PER_GROUP).astype(jnp.int32)
               + (tb >= 3 * PAIRS_PER_GROUP).astype(jnp.int32))
        pair = tb - PAIRS_PER_GROUP * grp
        lo = (pair >= 3).astype(jnp.int32) + (pair >= 5).astype(jnp.int32)
        hi = jnp.where(pair == 0, 1, jnp.where((pair == 1) | (pair == 3), 2, 3))
        n_used = (incl[N_BUCKETS - 1:N_BUCKETS, 0:1] * (1.0 / SORT_TILE)).astype(jnp.int32)
        fill = (tile >= n_used - 1) | (tb != bucket_of(row0 + SORT_TILE))
        meta_ref[...] = jnp.zeros_like(meta_ref)
        meta_ref[META_EXPERT_LO:META_EXPERT_LO + 1, :] = EXPERTS_PER_GROUP * grp + lo
        meta_ref[META_EXPERT_HI:META_EXPERT_HI + 1, :] = EXPERTS_PER_GROUP * grp + hi
        meta_ref[META_N_USED:META_N_USED + 1, :] = jnp.broadcast_to(n_used, (1, META_LANES))
        meta_ref[META_FILL:META_FILL + 1, :] = fill.astype(jnp.int32)

    oh = _bucket_onehot(info_ref[...], sel_ref[...])
    before = jnp.dot(oh.astype(BF16), utri_ref[...], preferred_element_type=F32)
    base = start_sc[:, 0:1] + run_sc[:, 0:1]
    pos_ref[0] = jnp.sum(oh * (before + base), axis=0, keepdims=True).astype(jnp.int32)
    run_sc[...] += jnp.sum(oh, axis=1, keepdims=True)


def _sort_plan(info, cnt, tm):
    n = info.shape[0]
    sel = _bucket_selector()
    info_spec = pl.BlockSpec((tm, ROUTER_LANES), lambda i: (i, 0))
    r = jnp.arange(BUCKET_ROWS)
    ltri = (r[:, None] >= r[None, :]).astype(F32)
    t = jnp.arange(tm)
    utri = (t[:, None] < t[None, :]).astype(BF16)
    return pl.pallas_call(
        _plan_pos_kernel,
        grid=(n // tm,),
        in_specs=[info_spec, _full(sel.shape), _full(cnt.shape), _full(ltri.shape), _full(utri.shape)],
        out_specs=[pl.BlockSpec((1, 1, tm), lambda i: (i, 0, 0)), _full((8, META_LANES))],
        out_shape=[jax.ShapeDtypeStruct((n // tm, 1, tm), jnp.int32),
                   jax.ShapeDtypeStruct((8, META_LANES), jnp.int32)],
        scratch_shapes=[pltpu.VMEM((BUCKET_ROWS, LANES), F32), pltpu.VMEM((BUCKET_ROWS, LANES), F32)],
        compiler_params=_params(("arbitrary",)),
        name="plan_pos",
    )(info, sel, cnt, ltri, utri)


def _dispatch_kernel(pos_ref, fill_ref, h1_ref, gain2_ref, sh2_ref, sc2_ref, info_ref, xs_ref,
                     rowbuf, zbuf, sem, zsem, *, tm, n_steps, n_tiles):
    step = pl.program_id(0) * pl.num_programs(1) + pl.program_id(1)
    slot = step % 2

    @pl.when(step == 0)
    def _():
        zbuf[...] = jnp.zeros_like(zbuf)
        fill = lambda j: pltpu.make_async_copy(zbuf, xs_ref.at[pl.ds(j * SORT_TILE, SORT_TILE)], zsem)
        for j in range(n_tiles):
            pl.when(fill_ref[0, j] == 1)(lambda j=j: fill(j).start())
        for j in range(n_tiles):
            pl.when(fill_ref[0, j] == 1)(lambda j=j: fill(j).wait())

    def wait(s):
        pltpu.make_async_copy(rowbuf.at[s], xs_ref.at[pl.ds(0, tm)], sem.at[s]).wait()

    @pl.when(step >= 2)
    def _():
        wait(slot)

    rowbuf[slot, :, 0:D_MODEL] = _modulate(h1_ref[0], gain2_ref[...], sh2_ref[...], sc2_ref[...])
    rowbuf[slot, :, D_MODEL:] = info_ref[0]
    for r in range(tm):
        pltpu.make_async_copy(rowbuf.at[slot, pl.ds(r, 1)], xs_ref.at[pl.ds(pos_ref[0, 0, r], 1)],
                              sem.at[slot]).start(priority=r % 2)

    @pl.when(step == n_steps - 1)
    def _():
        wait(slot)
        if n_steps > 1:
            wait(1 - slot)


def _gmoe_kernel(ea_ref, eb_ref, nu_ref, xs_ref, wga_ref, wua_ref, wda_ref, wgb_ref, wub_ref, wdb_ref, ys_ref):
    del ea_ref, eb_ref
    in_use = pl.program_id(0) < nu_ref[0]

    @pl.when(jnp.logical_not(in_use))
    def _():
        ys_ref[...] = jnp.zeros_like(ys_ref)

    @pl.when(in_use)
    def _():
        x = xs_ref[:, 0:D_MODEL].astype(BF16)

        def expert(wg_ref, wu_ref, wd_ref, w):
            gt = jnp.dot(x, wg_ref[0], preferred_element_type=F32)
            up = jnp.dot(x, wu_ref[0], preferred_element_type=F32)
            h = (gt * jax.nn.sigmoid(gt)) * up * w
            return jnp.dot(h.astype(BF16), wd_ref[0], preferred_element_type=F32)

        w_lo = xs_ref[:, D_MODEL + INFO_W_LO:D_MODEL + INFO_W_LO + 1]
        w_hi = xs_ref[:, D_MODEL + INFO_W_HI:D_MODEL + INFO_W_HI + 1]
        ys_ref[...] = expert(wga_ref, wua_ref, wda_ref, w_lo) + expert(wgb_ref, wub_ref, wdb_ref, w_hi)


def _fetch_sorted_rows(ys_ref, pos_ref, posn_ref, ybuf, sem, tm, n_steps):
    step = pl.program_id(0) * pl.num_programs(1) + pl.program_id(1)
    slot = step % 2

    def issue(p_ref, s):
        for r in range(tm):
            pltpu.make_async_copy(ys_ref.at[pl.ds(p_ref[0, 0, r], 1)], ybuf.at[s, pl.ds(r, 1)],
                                  sem.at[s]).start(priority=r % 2)

    @pl.when(step == 0)
    def _():
        issue(pos_ref, slot)

    @pl.when(step + 1 < n_steps)
    def _():
        issue(posn_ref, 1 - slot)

    pltpu.make_async_copy(ys_ref.at[pl.ds(0, tm)], ybuf.at[slot], sem.at[slot]).wait()
    return ybuf[slot]


def _combine_kernel(pos_ref, posn_ref, h1_ref, g2_ref, ys_ref, o_ref, ybuf, sem, *, tm, n_steps):
    rows = _fetch_sorted_rows(ys_ref, pos_ref, posn_ref, ybuf, sem, tm, n_steps)
    o_ref[0] = h1_ref[0] + g2_ref[...] * rows


def _sparse_moe(h1, info, cnt, mods, layer, gain2, w_gate, w_up, w_down, tm):
    bsz, seq, d = h1.shape
    nt = seq // tm
    n_steps = bsz * nt
    n = bsz * seq
    n_sorted = n + N_BUCKETS * SORT_TILE
    n_tiles = n_sorted // SORT_TILE
    assert n % SORT_TILE == 0 and n_tiles <= META_LANES
    pos, meta = _sort_plan(info.reshape(n, ROUTER_LANES), cnt, tm)
    pos_spec = lambda off: pl.BlockSpec(
        (1, 1, tm), lambda b, i: (jnp.minimum(b * nt + i + off, n_steps - 1), 0, 0), memory_space=pltpu.SMEM)
    tile = lambda w: pl.BlockSpec((1, tm, w), lambda b, i: (b, i, 0))
    any_spec = pl.BlockSpec(memory_space=pl.ANY)
    by_batch = lambda b, i: b

    xs = pl.pallas_call(
        functools.partial(_dispatch_kernel, tm=tm, n_steps=n_steps, n_tiles=n_tiles),
        grid=(bsz, nt),
        in_specs=[pos_spec(0), pl.BlockSpec((1, META_LANES), lambda b, i: (0, 0), memory_space=pltpu.SMEM),
                  tile(d), _full((1, d)), _mod_spec(layer, by_batch, 3),
                  _mod_spec(layer, by_batch, 4), tile(ROUTER_LANES)],
        out_specs=any_spec,
        out_shape=jax.ShapeDtypeStruct((n_sorted, ROW_WIDTH), F32),
        scratch_shapes=[pltpu.VMEM((2, tm, ROW_WIDTH), F32), pltpu.VMEM((SORT_TILE, ROW_WIDTH), F32),
                        pltpu.SemaphoreType.DMA((2,)), pltpu.SemaphoreType.DMA(())],
        compiler_params=_params(("arbitrary", "arbitrary")),
        name=f"dispatch{layer}",
    )(pos, meta[META_FILL:META_FILL + 1], h1, gain2.reshape(1, d), mods, mods, info)

    used = lambda j, nu: jnp.minimum(j, nu[0] - 1)
    w_in_spec = lambda which: pl.BlockSpec(
        (1, d, D_EXPERT), lambda j, ea, eb, nu: ((ea, eb)[which][used(j, nu)], 0, 0))
    w_out_spec = lambda which: pl.BlockSpec(
        (1, D_EXPERT, d), lambda j, ea, eb, nu: ((ea, eb)[which][used(j, nu)], 0, 0))
    wg, wu, wd = w_gate.astype(BF16), w_up.astype(BF16), w_down.astype(BF16)
    ys = pl.pallas_call(
        _gmoe_kernel,
        grid_spec=pltpu.PrefetchScalarGridSpec(
            num_scalar_prefetch=3,
            grid=(n_tiles,),
            in_specs=[pl.BlockSpec((SORT_TILE, ROW_WIDTH), lambda j, ea, eb, nu: (used(j, nu), 0)),
                      w_in_spec(0), w_in_spec(0), w_out_spec(0), w_in_spec(1), w_in_spec(1), w_out_spec(1)],
            out_specs=pl.BlockSpec((SORT_TILE, d), lambda j, ea, eb, nu: (j, 0))),
        out_shape=jax.ShapeDtypeStruct((n_sorted, d), F32),
        compiler_params=_params(("arbitrary",)),
        name=f"experts{layer}",
    )(meta[META_EXPERT_LO], meta[META_EXPERT_HI], meta[META_N_USED, :1], xs, wg, wu, wd, wg, wu, wd)

    return pl.pallas_call(
        functools.partial(_combine_kernel, tm=tm, n_steps=n_steps),
        grid=(bsz, nt),
        in_specs=[pos_spec(0), pos_spec(1), tile(d), _mod_spec(layer, by_batch, 5), any_spec],
        out_specs=tile(d),
        out_shape=jax.ShapeDtypeStruct((bsz, seq, d), F32),
        scratch_shapes=[pltpu.VMEM((2, tm, d), F32), pltpu.SemaphoreType.DMA((2,))],
        compiler_params=_params(("arbitrary", "arbitrary")),
        name=f"combine{layer}",
    )(pos, pos, h1, mods, ys)


def kernel(x, c, ctx, c_ctx, mod_w, mod_b, norm1_g, norm2_g, even_w_in, q_gain, k_gain, pool_w, pool_scale,
           even_w_out, odd_w_in, sg_gain, sg_w, sg_b, conv_w, odd_w_out, router_g_w, router_g_b,
           router_e_w, router_e_b, w_gate, w_up, w_down):
    bsz, seq, d = x.shape
    tm = min(512, seq)
    cond = jnp.zeros((MOD_ROWS, d), F32).at[:bsz].set(c).at[bsz].set(c_ctx)
    mods = _adaln(cond, mod_w, mod_b).reshape(mod_w.shape[0], MOD_ROWS, 6, 1, d)

    q, k, v, p = _inproj0(x, mods, norm1_g[0], even_w_in[0], q_gain[0], k_gain[0], tm)
    kc, vc = _inproj0_ctx(ctx, mods, bsz, norm1_g[0], even_w_in[0][:, ATTN_WIDTH:ATTN_WIDTH + 2 * KV_WIDTH],
                          k_gain[0])
    o = _attention(q, k, v, kc, vc, tq=min(256, seq), tk=min(2048, seq))
    router0 = _router_operands(router_g_w[0], router_g_b[0], router_e_w[0], router_e_b[0])
    h1, info, cnt = _out0(o, p, x, mods, norm2_g[0], pool_w[0], pool_scale[0], even_w_out[0], router0, tm)
    h = _sparse_moe(h1, info, cnt, mods, 0, norm2_g[0], w_gate[0], w_up[0], w_down[0], tm)

    yc, z, bg = _inproj1(h, mods, norm1_g[1], odd_w_in[0], sg_gain[0], sg_w[0], sg_b[0], tm)
    router1 = _router_operands(router_g_w[1], router_g_b[1], router_e_w[1], router_e_b[1])
    h1, info, cnt = _out1(yc, z, bg, h, mods, norm2_g[1], conv_w[0], odd_w_out[0], router1, tm)
    return _sparse_moe(h1, info, cnt, mods, 1, norm2_g[1], w_gate[1], w_up[1], w_down[1], tm)
```

```python
import functools

import jax
import jax.numpy as jnp
from jax import lax
from jax.experimental import pallas as pl
from jax.experimental.pallas import tpu as pltpu

F32 = jnp.float32
BF16 = jnp.bfloat16

D_MODEL = 1024
GRID_W = 64
EPS = 1e-6
N_Q_HEADS = 8
N_KV_HEADS = 2
HEAD_DIM = 64
Q_PER_KV = N_Q_HEADS // N_KV_HEADS
ATTN_WIDTH = N_Q_HEADS * HEAD_DIM
KV_WIDTH = N_KV_HEADS * HEAD_DIM
ROPE_THETA = 10000.0
POOL_WINDOWS = (2, 4, 8, 16)
POOL_GROUP = 128
POOL_WIDTH = POOL_GROUP * len(POOL_WINDOWS)
SG_GROUPS = 4
SG_CHUNK = 128
SG_WIDTH = 512
CONV_WIDTH = 512
EVEN_IN = ATTN_WIDTH + 2 * KV_WIDTH + POOL_WIDTH
ODD_IN = 2 * SG_WIDTH + 3 * CONV_WIDTH
N_GROUPS = 4
EXPERTS_PER_GROUP = 4
N_EXPERTS = 16
D_EXPERT = 256

Q_SCALE = HEAD_DIM ** -0.5 * 1.4426950408889634
LANES = 128
HALO = 8
ROUTER_LANES = 128
MOD_ROWS = 16
VMEM_LIMIT = 48 * 1024 * 1024

PAIRS_PER_GROUP = 6
N_BUCKETS = N_GROUPS * PAIRS_PER_GROUP
BUCKET_ROWS = 32
SORT_TILE = 512
META_LANES = 256
META_EXPERT_LO, META_EXPERT_HI, META_N_USED, META_FILL = 0, 1, 2, 3
ROW_WIDTH = D_MODEL + ROUTER_LANES
INFO_BUCKET, INFO_W_LO, INFO_W_HI = 0, 1, 2


def _params(sem):
    return pltpu.CompilerParams(dimension_semantics=sem, vmem_limit_bytes=VMEM_LIMIT)


def _modulate(x, gain, shift, scale):
    ms = jnp.mean(x * x, axis=-1, keepdims=True)
    return (x * lax.rsqrt(ms + EPS) * gain) * (1.0 + scale) + shift


def _mod_spec(layer, row_fn, which):
    return pl.BlockSpec((None, None, None, 1, D_MODEL),
                        lambda *idx: (layer, row_fn(*idx), which, 0, 0))


def _full(shape):
    return pl.BlockSpec(shape, lambda *idx: (0,) * len(shape))


def _adaln_kernel(c_ref, w_ref, b_ref, o_ref):
    c = c_ref[...]
    s = c * jax.nn.sigmoid(c)
    o_ref[0] = jnp.dot(s, w_ref[0], precision=lax.Precision.HIGHEST,
                       preferred_element_type=F32) + b_ref[0]


def _adaln(cond, mod_w, mod_b):
    depth, d, n = mod_w.shape
    tn = 1024
    return pl.pallas_call(
        _adaln_kernel,
        grid=(depth, n // tn),
        in_specs=[_full((MOD_ROWS, d)),
                  pl.BlockSpec((1, d, tn), lambda l, j: (l, 0, j)),
                  pl.BlockSpec((1, 1, tn), lambda l, j: (l, 0, j))],
        out_specs=pl.BlockSpec((1, MOD_ROWS, tn), lambda l, j: (l, 0, j)),
        out_shape=jax.ShapeDtypeStruct((depth, MOD_ROWS, n), F32),
        compiler_params=_params(("arbitrary", "arbitrary")),
        name="adaln",
    )(cond, mod_w, mod_b.reshape(depth, 1, n))


def _head_norm_rope(z, gain, ones_bd, cos, sin, first_half):
    sq = z * z
    hi = sq.astype(BF16)
    lo = (sq - hi.astype(F32)).astype(BF16)
    ms = (jnp.dot(hi, ones_bd, preferred_element_type=F32)
          + jnp.dot(lo, ones_bd, preferred_element_type=F32))
    zn = z * lax.rsqrt(ms + EPS) * gain
    partner = jnp.where(first_half, pltpu.roll(zn, LANES - 16, 1), pltpu.roll(zn, 16, 1))
    return zn * cos + partner * sin


def _inproj0_kernel(x_ref, gain_ref, sh_ref, sc_ref, w_ref, cos_ref, sin_ref, qg_ref, kg_ref, ones_ref,
                    q_ref, k_ref, v_ref, p_ref):
    a = _modulate(x_ref[0], gain_ref[...], sh_ref[...], sc_ref[...])
    y = jnp.dot(a.astype(BF16), w_ref[...], preferred_element_type=F32)
    cos, sin, ones_bd = cos_ref[...], sin_ref[...], ones_ref[...]
    lane = lax.broadcasted_iota(jnp.int32, cos.shape, 1)
    first_half = (lane % 32) < 16
    for s in range(ATTN_WIDTH // LANES):
        r = _head_norm_rope(y[:, s * LANES:(s + 1) * LANES], qg_ref[...], ones_bd, cos, sin, first_half)
        r = (r * Q_SCALE).astype(BF16)
        q_ref[0, 2 * s] = r[:, :HEAD_DIM]
        q_ref[0, 2 * s + 1] = r[:, HEAD_DIM:]
    kr = _head_norm_rope(y[:, ATTN_WIDTH:ATTN_WIDTH + KV_WIDTH], kg_ref[...], ones_bd, cos, sin,
                         first_half).astype(BF16)
    k_ref[0, 0] = kr[:, :HEAD_DIM]
    k_ref[0, 1] = kr[:, HEAD_DIM:]
    vv = y[:, ATTN_WIDTH + KV_WIDTH:ATTN_WIDTH + 2 * KV_WIDTH].astype(BF16)
    v_ref[0, 0] = vv[:, :HEAD_DIM]
    v_ref[0, 1] = vv[:, HEAD_DIM:]
    p_ref[0] = y[:, ATTN_WIDTH + 2 * KV_WIDTH:]


def _inproj0_ctx_kernel(x_ref, gain_ref, sh_ref, sc_ref, w_ref, kg_ref, ones_ref, k_ref, v_ref):
    a = _modulate(x_ref[0], gain_ref[...], sh_ref[...], sc_ref[...])
    y = jnp.dot(a.astype(BF16), w_ref[...], preferred_element_type=F32)
    z = y[:, :KV_WIDTH]
    sq = z * z
    hi = sq.astype(BF16)
    lo = (sq - hi.astype(F32)).astype(BF16)
    ms = (jnp.dot(hi, ones_ref[...], preferred_element_type=F32)
          + jnp.dot(lo, ones_ref[...], preferred_element_type=F32))
    kr = (z * lax.rsqrt(ms + EPS) * kg_ref[...]).astype(BF16)
    k_ref[0, 0] = kr[:, :HEAD_DIM]
    k_ref[0, 1] = kr[:, HEAD_DIM:]
    vv = y[:, KV_WIDTH:].astype(BF16)
    v_ref[0, 0] = vv[:, :HEAD_DIM]
    v_ref[0, 1] = vv[:, HEAD_DIM:]


def _rope_tables(seq):
    t = jnp.arange(seq)
    row = (t // GRID_W).astype(F32)
    col = (t % GRID_W).astype(F32)
    half = HEAD_DIM // 2
    inv = ROPE_THETA ** (-jnp.arange(0, half, 2, dtype=F32) / half)
    ar, ac = row[:, None] * inv, col[:, None] * inv
    cos = jnp.concatenate([jnp.cos(ar), jnp.cos(ar), jnp.cos(ac), jnp.cos(ac)], axis=-1)
    sin = jnp.concatenate([-jnp.sin(ar), jnp.sin(ar), -jnp.sin(ac), jnp.sin(ac)], axis=-1)
    return jnp.tile(cos, (1, LANES // HEAD_DIM)), jnp.tile(sin, (1, LANES // HEAD_DIM))


def _head_mean_matrix():
    r = jnp.arange(LANES)
    same = (r[:, None] // HEAD_DIM) == (r[None, :] // HEAD_DIM)
    return jnp.where(same, 1.0 / HEAD_DIM, 0.0).astype(BF16)


def _inproj0(x, mods, gain, w_in, q_gain, k_gain, tm):
    bsz, seq, d = x.shape
    cos, sin = _rope_tables(seq)
    qg = jnp.tile(q_gain, LANES // HEAD_DIM).reshape(1, LANES)
    kg = jnp.tile(k_gain, LANES // HEAD_DIM).reshape(1, LANES)
    head = lambda n: pl.BlockSpec((1, n, tm, HEAD_DIM), lambda b, i: (b, 0, i, 0))
    return pl.pallas_call(
        _inproj0_kernel,
        grid=(bsz, seq // tm),
        in_specs=[pl.BlockSpec((1, tm, d), lambda b, i: (b, i, 0)),
                  _full((1, d)),
                  _mod_spec(0, lambda b, i: b, 0),
                  _mod_spec(0, lambda b, i: b, 1),
                  _full((d, EVEN_IN)),
                  pl.BlockSpec((tm, LANES), lambda b, i: (i, 0)),
                  pl.BlockSpec((tm, LANES), lambda b, i: (i, 0)),
                  _full((1, LANES)), _full((1, LANES)), _full((LANES, LANES))],
        out_specs=[head(N_Q_HEADS), head(N_KV_HEADS), head(N_KV_HEADS),
                   pl.BlockSpec((1, tm, POOL_WIDTH), lambda b, i: (b, i, 0))],
        out_shape=[jax.ShapeDtypeStruct((bsz, N_Q_HEADS, seq, HEAD_DIM), BF16),
                   jax.ShapeDtypeStruct((bsz, N_KV_HEADS, seq, HEAD_DIM), BF16),
                   jax.ShapeDtypeStruct((bsz, N_KV_HEADS, seq, HEAD_DIM), BF16),
                   jax.ShapeDtypeStruct((bsz, seq, POOL_WIDTH), F32)],
        compiler_params=_params(("parallel", "parallel")),
        name="inproj0",
    )(x, gain.reshape(1, d), mods, mods, w_in.astype(BF16), cos, sin, qg, kg, _head_mean_matrix())


def _inproj0_ctx(ctx, mods, ctx_row, gain, w_kv, k_gain):
    bsz, n_ctx, d = ctx.shape
    kg = jnp.tile(k_gain, LANES // HEAD_DIM).reshape(1, LANES)
    head = pl.BlockSpec((1, N_KV_HEADS, n_ctx, HEAD_DIM), lambda b: (b, 0, 0, 0))
    return pl.pallas_call(
        _inproj0_ctx_kernel,
        grid=(bsz,),
        in_specs=[pl.BlockSpec((1, n_ctx, d), lambda b: (b, 0, 0)),
                  _full((1, d)),
                  _mod_spec(0, lambda b: ctx_row, 0),
                  _mod_spec(0, lambda b: ctx_row, 1),
                  _full((d, 2 * KV_WIDTH)),
                  _full((1, LANES)), _full((LANES, LANES))],
        out_specs=[head, head],
        out_shape=[jax.ShapeDtypeStruct((bsz, N_KV_HEADS, n_ctx, HEAD_DIM), BF16)] * 2,
        compiler_params=_params(("parallel",)),
        name="inproj0_ctx",
    )(ctx, gain.reshape(1, d), mods, mods, w_kv.astype(BF16), kg, _head_mean_matrix())


def _attn_kernel(q_ref, kl_ref, vl_ref, kc_ref, vc_ref, o_ref, *, tq, tk):
    rows = Q_PER_KV * tq
    q = q_ref[0].reshape(rows, HEAD_DIM)
    seq = kl_ref.shape[2]

    def step(k, v, carry):
        m, l, acc = carry
        s = lax.dot_general(q, k, (((1,), (1,)), ((), ())), preferred_element_type=F32)
        m_new = jnp.maximum(m, jnp.max(s, axis=1, keepdims=True))
        alpha = jnp.exp2(m - m_new)
        p = jnp.exp2(s - m_new)
        l = alpha * l + jnp.sum(p, axis=1, keepdims=True)
        acc = alpha * acc + jnp.dot(p.astype(BF16), v, preferred_element_type=F32)
        return m_new, l, acc

    carry = (jnp.full((rows, 1), -jnp.inf, F32), jnp.zeros((rows, 1), F32),
             jnp.zeros((rows, HEAD_DIM), F32))
    carry = step(kc_ref[0, 0], vc_ref[0, 0], carry)
    for c in range(seq // tk):
        carry = step(kl_ref[0, 0, c * tk:(c + 1) * tk, :], vl_ref[0, 0, c * tk:(c + 1) * tk, :], carry)
    _, l, acc = carry
    o = acc / l
    o_ref[0] = jnp.concatenate([o[h * tq:(h + 1) * tq] for h in range(Q_PER_KV)], axis=1).astype(BF16)


def _attention(q, k, v, kc, vc, tq, tk):
    bsz, _, seq, _ = q.shape
    n_ctx = kc.shape[2]
    kv_spec = lambda n: pl.BlockSpec((1, 1, n, HEAD_DIM), lambda b, g, i: (b, g, 0, 0))
    return pl.pallas_call(
        functools.partial(_attn_kernel, tq=tq, tk=tk),
        grid=(bsz, N_KV_HEADS, seq // tq),
        in_specs=[pl.BlockSpec((1, Q_PER_KV, tq, HEAD_DIM), lambda b, g, i: (b, g, i, 0)),
                  kv_spec(seq), kv_spec(seq), kv_spec(n_ctx), kv_spec(n_ctx)],
        out_specs=pl.BlockSpec((1, tq, Q_PER_KV * HEAD_DIM), lambda b, g, i: (b, i, g)),
        out_shape=jax.ShapeDtypeStruct((bsz, seq, ATTN_WIDTH), BF16),
        compiler_params=_params(("parallel", "parallel", "parallel")),
        name="attention",
    )(q, k, v, kc, vc)


def _route(logits):
    lane = lax.broadcasted_iota(jnp.int32, logits.shape, 1)
    neg = -jnp.inf
    big = ROUTER_LANES
    is_g = lane < N_GROUPS
    gm = jnp.max(jnp.where(is_g, logits, neg), axis=1, keepdims=True)
    gidx = jnp.min(jnp.where(is_g & (logits == gm), lane, big), axis=1, keepdims=True)
    gden = jnp.sum(jnp.where(is_g, jnp.exp(logits - gm), 0.0), axis=1, keepdims=True)
    g_p = 1.0 / gden
    first = N_GROUPS + EXPERTS_PER_GROUP * gidx
    sel = (lane >= first) & (lane < first + EXPERTS_PER_GROUP)
    e1 = jnp.max(jnp.where(sel, logits, neg), axis=1, keepdims=True)
    i1 = jnp.min(jnp.where(sel & (logits == e1), lane, big), axis=1, keepdims=True)
    rest = sel & (lane != i1)
    e2 = jnp.max(jnp.where(rest, logits, neg), axis=1, keepdims=True)
    i2 = jnp.min(jnp.where(rest & (logits == e2), lane, big), axis=1, keepdims=True)
    p2 = jnp.exp(e2 - e1)
    w1 = g_p * (1.0 / (1.0 + p2))
    w2 = g_p * (p2 / (1.0 + p2))
    lo = jnp.minimum(i1, i2) - first
    hi = jnp.maximum(i1, i2) - first
    pair = jnp.where(lo == 0, hi - 1, jnp.where(lo == 1, hi + 1, PAIRS_PER_GROUP - 1))
    bucket = (PAIRS_PER_GROUP * gidx + pair).astype(F32)
    w_lo = jnp.where(i1 < i2, w1, w2)
    w_hi = jnp.where(i1 < i2, w2, w1)
    return jnp.where(lane == INFO_BUCKET, bucket,
                     jnp.where(lane == INFO_W_LO, w_lo, jnp.where(lane == INFO_W_HI, w_hi, 0.0)))


def _bucket_onehot(info, sel):
    brow = lax.dot_general(sel, info.astype(BF16), (((1,), (1,)), ((), ())),
                           preferred_element_type=F32)[0:1]
    bid = lax.broadcasted_iota(jnp.int32, (BUCKET_ROWS, info.shape[0]), 0)
    return (bid == brow.astype(jnp.int32)).astype(F32)


def _bucket_selector():
    return jnp.zeros((8, ROUTER_LANES), F32).at[0, INFO_BUCKET].set(1.0).astype(BF16)


def _tail(y, x_res, gate1, gain2, shift2, scale2, rw_hi, rw_lo, rbias, sel, h1_ref, info_ref, cnt_ref):
    h1 = x_res + gate1 * y
    h1_ref[0] = h1
    t = _modulate(h1, gain2, shift2, scale2)
    t_hi = t.astype(BF16)
    t_lo = (t - t_hi.astype(F32)).astype(BF16)
    logits = (jnp.dot(t_hi, rw_hi, preferred_element_type=F32)
              + jnp.dot(t_lo, rw_hi, preferred_element_type=F32)
              + jnp.dot(t_hi, rw_lo, preferred_element_type=F32)) + rbias
    info = _route(logits)
    info_ref[0] = info

    @pl.when((pl.program_id(0) == 0) & (pl.program_id(1) == 0))
    def _():
        cnt_ref[...] = jnp.zeros_like(cnt_ref)

    cnt_ref[...] += jnp.sum(_bucket_onehot(info, sel), axis=1, keepdims=True)


def _router_operands(rg_w, rg_b, re_w, re_b):
    d = rg_w.shape[0]
    w = jnp.concatenate([rg_w, re_w, jnp.zeros((d, ROUTER_LANES - N_GROUPS - N_EXPERTS), F32)], axis=1)
    b = jnp.concatenate([rg_b, re_b, jnp.zeros((ROUTER_LANES - N_GROUPS - N_EXPERTS,), F32)])
    w_hi = w.astype(BF16)
    w_lo = (w - w_hi.astype(F32)).astype(BF16)
    return w_hi, w_lo, b.reshape(1, ROUTER_LANES)


def _fill_halo(buf, main_ref, prev_ref, next_ref, tm, i, n_tiles):
    buf[HALO:HALO + tm] = main_ref[0]
    buf[0:HALO] = jnp.where(i > 0, prev_ref[0], 0.0)
    buf[HALO + tm:2 * HALO + tm] = jnp.where(i < n_tiles - 1, next_ref[0], 0.0)


def _halo_specs(tm, seq, width):
    per = tm // HALO
    last = seq // HALO - 1
    return [pl.BlockSpec((1, tm, width), lambda b, i: (b, i, 0)),
            pl.BlockSpec((1, HALO, width), lambda b, i: (b, jnp.maximum(i * per - 1, 0), 0)),
            pl.BlockSpec((1, HALO, width), lambda b, i: (b, jnp.minimum((i + 1) * per, last), 0))]


def _out0_kernel(o_ref, p_ref, pprev_ref, pnext_ref, x_ref, g1_ref, gain2_ref, sh2_ref, sc2_ref,
                 poolw_ref, pscale_ref, wout_ref, rwh_ref, rwl_ref, rb_ref, sel_ref,
                 h1_ref, info_ref, cnt_ref, pbuf, *, tm, seq):
    i = pl.program_id(1)
    _fill_halo(pbuf, p_ref, pprev_ref, pnext_ref, tm, i, seq // tm)
    pos = i * tm + lax.broadcasted_iota(jnp.int32, (tm, 1), 0)
    y = jnp.dot(o_ref[0], wout_ref[0:ATTN_WIDTH, :], preferred_element_type=F32)
    for g, w in enumerate(POOL_WINDOWS):
        sl = slice(g * POOL_GROUP, (g + 1) * POOL_GROUP)
        acc = pbuf[HALO - w // 2:HALO - w // 2 + tm, sl]
        for j in range(1 - w // 2, w - w // 2):
            acc = acc + pbuf[HALO + j:HALO + j + tm, sl]
        lo = jnp.clip(pos - w // 2, 0, seq)
        hi = jnp.clip(pos + w - w // 2, 0, seq)
        mean = acc / (hi - lo).astype(F32)
        dlt = (mean - pbuf[HALO:HALO + tm, sl]).astype(BF16)
        yp = jnp.dot(dlt, poolw_ref[g], preferred_element_type=F32) * pscale_ref[:, sl]
        y = y + jnp.dot(yp.astype(BF16), wout_ref[ATTN_WIDTH + g * POOL_GROUP:ATTN_WIDTH + (g + 1) * POOL_GROUP, :],
                        preferred_element_type=F32)
    _tail(y, x_ref[0], g1_ref[...], gain2_ref[...], sh2_ref[...], sc2_ref[...],
          rwh_ref[...], rwl_ref[...], rb_ref[...], sel_ref[...], h1_ref, info_ref, cnt_ref)


def _tail_specs(layer, d):
    by_batch = lambda b, i: b
    ins = [_mod_spec(layer, by_batch, 2), _full((1, d)), _mod_spec(layer, by_batch, 3),
           _mod_spec(layer, by_batch, 4)]
    return ins


def _tail_outs(bsz, seq, d, tm):
    specs = [pl.BlockSpec((1, tm, d), lambda b, i: (b, i, 0)),
             pl.BlockSpec((1, tm, ROUTER_LANES), lambda b, i: (b, i, 0)),
             _full((BUCKET_ROWS, LANES))]
    shapes = [jax.ShapeDtypeStruct((bsz, seq, d), F32),
              jax.ShapeDtypeStruct((bsz, seq, ROUTER_LANES), F32),
              jax.ShapeDtypeStruct((BUCKET_ROWS, LANES), F32)]
    return specs, shapes


def _out0(o, p, x, mods, gain2, pool_w, pool_scale, w_out, router, tm):
    bsz, seq, d = x.shape
    rw_hi, rw_lo, rb = router
    out_specs, out_shapes = _tail_outs(bsz, seq, d, tm)
    return pl.pallas_call(
        functools.partial(_out0_kernel, tm=tm, seq=seq),
        grid=(bsz, seq // tm),
        in_specs=[pl.BlockSpec((1, tm, ATTN_WIDTH), lambda b, i: (b, i, 0))]
        + _halo_specs(tm, seq, POOL_WIDTH)
        + [pl.BlockSpec((1, tm, d), lambda b, i: (b, i, 0))]
        + _tail_specs(0, d)
        + [_full(pool_w.shape), _full((1, POOL_WIDTH)), _full(w_out.shape),
           _full(rw_hi.shape), _full(rw_lo.shape), _full(rb.shape), _full((8, ROUTER_LANES))],
        out_specs=out_specs,
        out_shape=out_shapes,
        scratch_shapes=[pltpu.VMEM((tm + 2 * HALO, POOL_WIDTH), F32)],
        compiler_params=_params(("arbitrary", "arbitrary")),
        name="out0",
    )(o, p, p, p, x, mods, gain2.reshape(1, d), mods, mods,
      pool_w.astype(BF16), pool_scale.reshape(1, POOL_WIDTH), w_out.astype(BF16), rw_hi, rw_lo, rb,
      _bucket_selector())


def _inproj1_kernel(pos_ref, posn_ref, h1_ref, g2_ref, gain_ref, sh_ref, sc_ref, w_ref, sgg_ref, sgw_ref, sgb_ref,
                    ys_ref, h_ref, yc_ref, z_ref, bg_ref, ybuf, sem, *, tm, n_steps):
    x = h1_ref[0] + g2_ref[...] * _fetch_sorted_rows(ys_ref, pos_ref, posn_ref, ybuf, sem, tm, n_steps)
    h_ref[0] = x
    a = _modulate(x, gain_ref[...], sh_ref[...], sc_ref[...])
    y = jnp.dot(a.astype(BF16), w_ref[...], preferred_element_type=F32)
    for g in range(SG_GROUPS):
        sl = slice(g * LANES, (g + 1) * LANES)
        u = y[:, sl]
        vg = y[:, SG_WIDTH + g * LANES:SG_WIDTH + (g + 1) * LANES]
        ms = jnp.mean(vg * vg, axis=-1, keepdims=True)
        vn = (vg * lax.rsqrt(ms + EPS) * sgg_ref[:, sl]).astype(BF16)
        for c in range(tm // SG_CHUNK):
            rows = slice(c * SG_CHUNK, (c + 1) * SG_CHUNK)
            s = jnp.dot(sgw_ref[g], vn[rows], preferred_element_type=F32) + sgb_ref[g]
            yc_ref[0, rows, sl] = (u[rows] * s).astype(BF16)
    hx = y[:, 2 * SG_WIDTH:2 * SG_WIDTH + CONV_WIDTH]
    bg_ref[0] = y[:, 2 * SG_WIDTH + CONV_WIDTH:2 * SG_WIDTH + 2 * CONV_WIDTH]
    cg = y[:, 2 * SG_WIDTH + 2 * CONV_WIDTH:]
    z_ref[0] = cg * hx


def _pos_spec(tm, nt, n_steps, ahead):
    return pl.BlockSpec((1, 1, tm), lambda b, i: (jnp.minimum(b * nt + i + ahead, n_steps - 1), 0, 0),
                        memory_space=pltpu.SMEM)


def _inproj1(h1, ys, pos, mods, gain, w_in, sg_gain, sg_w, sg_b, tm):
    bsz, seq, d = h1.shape
    nt = seq // tm
    n_steps = bsz * nt
    sgb = jnp.broadcast_to(sg_b[:, :, None], (SG_GROUPS, SG_CHUNK, LANES))
    by_batch = lambda b, i: b
    wide = lambda w, dt: (pl.BlockSpec((1, tm, w), lambda b, i: (b, i, 0)),
                          jax.ShapeDtypeStruct((bsz, seq, w), dt))
    outs = [wide(d, F32), wide(SG_WIDTH, BF16), wide(CONV_WIDTH, F32), wide(CONV_WIDTH, F32)]
    return pl.pallas_call(
        functools.partial(_inproj1_kernel, tm=tm, n_steps=n_steps),
        grid=(bsz, nt),
        in_specs=[_pos_spec(tm, nt, n_steps, 0), _pos_spec(tm, nt, n_steps, 1),
                  pl.BlockSpec((1, tm, d), lambda b, i: (b, i, 0)),
                  _mod_spec(0, by_batch, 5),
                  _full((1, d)),
                  _mod_spec(1, by_batch, 0),
                  _mod_spec(1, by_batch, 1),
                  _full((d, ODD_IN)),
                  _full((1, SG_WIDTH)), _full(sg_w.shape), _full(sgb.shape),
                  pl.BlockSpec(memory_space=pl.ANY)],
        out_specs=[s for s, _ in outs],
        out_shape=[s for _, s in outs],
        scratch_shapes=[pltpu.VMEM((2, tm, d), F32), pltpu.SemaphoreType.DMA((2,))],
        compiler_params=_params(("arbitrary", "arbitrary")),
        name="inproj1",
    )(pos, pos, h1, mods, gain.reshape(1, d), mods, mods, w_in.astype(BF16), sg_gain.reshape(1, SG_WIDTH),
      sg_w.astype(BF16), sgb, ys)


def _out1_kernel(yc_ref, z_ref, zprev_ref, znext_ref, bg_ref, x_ref, g1_ref, gain2_ref, sh2_ref, sc2_ref,
                 convw_ref, wout_ref, rwh_ref, rwl_ref, rb_ref, sel_ref,
                 h1_ref, info_ref, cnt_ref, zbuf, *, tm, seq):
    i = pl.program_id(1)
    _fill_halo(zbuf, z_ref, zprev_ref, znext_ref, tm, i, seq // tm)
    zc = (zbuf[HALO - 1:HALO - 1 + tm] * convw_ref[0:1, :]
          + zbuf[HALO:HALO + tm] * convw_ref[1:2, :]
          + zbuf[HALO + 1:HALO + 1 + tm] * convw_ref[2:3, :])
    yd = (bg_ref[0] * zc).astype(BF16)
    y = (jnp.dot(yc_ref[0], wout_ref[0:SG_WIDTH, :], preferred_element_type=F32)
         + jnp.dot(yd, wout_ref[SG_WIDTH:, :], preferred_element_type=F32))
    _tail(y, x_ref[0], g1_ref[...], gain2_ref[...], sh2_ref[...], sc2_ref[...],
          rwh_ref[...], rwl_ref[...], rb_ref[...], sel_ref[...], h1_ref, info_ref, cnt_ref)


def _out1(yc, z, bg, x, mods, gain2, conv_w, w_out, router, tm):
    bsz, seq, d = x.shape
    rw_hi, rw_lo, rb = router
    out_specs, out_shapes = _tail_outs(bsz, seq, d, tm)
    wide = pl.BlockSpec((1, tm, CONV_WIDTH), lambda b, i: (b, i, 0))
    return pl.pallas_call(
        functools.partial(_out1_kernel, tm=tm, seq=seq),
        grid=(bsz, seq // tm),
        in_specs=[wide] + _halo_specs(tm, seq, CONV_WIDTH) + [wide]
        + [pl.BlockSpec((1, tm, d), lambda b, i: (b, i, 0))]
        + _tail_specs(1, d)
        + [_full((3, CONV_WIDTH)), _full(w_out.shape),
           _full(rw_hi.shape), _full(rw_lo.shape), _full(rb.shape), _full((8, ROUTER_LANES))],
        out_specs=out_specs,
        out_shape=out_shapes,
        scratch_shapes=[pltpu.VMEM((tm + 2 * HALO, CONV_WIDTH), F32)],
        compiler_params=_params(("arbitrary", "arbitrary")),
        name="out1",
    )(yc, z, z, z, bg, x, mods, gain2.reshape(1, d), mods, mods,
      conv_w.reshape(3, CONV_WIDTH), w_out.astype(BF16), rw_hi, rw_lo, rb, _bucket_selector())


def _plan_pos_kernel(info_ref, sel_ref, cnt_ref, ltri_ref, utri_ref, pos_ref, meta_ref, start_sc, run_sc):
    @pl.when(pl.program_id(0) == 0)
    def _():
        padded = jnp.ceil(cnt_ref[...] * (1.0 / SORT_TILE)) * SORT_TILE
        incl = jnp.dot(ltri_ref[...], padded, precision=lax.Precision.HIGHEST, preferred_element_type=F32)
        start_sc[...] = incl - padded
        run_sc[...] = jnp.zeros_like(run_sc)
        ends = jnp.broadcast_to(incl[:, 0:1], (BUCKET_ROWS, META_LANES))
        bid = lax.broadcasted_iota(jnp.int32, ends.shape, 0)
        tile = lax.broadcasted_iota(jnp.int32, (1, META_LANES), 1)

        def bucket_of(row0):
            done = jnp.where((bid < N_BUCKETS) & (ends <= row0), 1.0, 0.0)
            return jnp.minimum(jnp.sum(done, axis=0, keepdims=True), N_BUCKETS - 1.0).astype(jnp.int32)

        row0 = (tile * SORT_TILE).astype(F32)
        tb = bucket_of(row0)
        grp = ((tb >= PAIRS_PER_GROUP).astype(jnp.int32) + (tb >= 2 * PAIRS_PER_GROUP).astype(jnp.int32)
               + (tb >= 3 * PAIRS_PER_GROUP).astype(jnp.int32))
        pair = tb - PAIRS_PER_GROUP * grp
        lo = (pair >= 3).astype(jnp.int32) + (pair >= 5).astype(jnp.int32)
        hi = jnp.where(pair == 0, 1, jnp.where((pair == 1) | (pair == 3), 2, 3))
        n_used = (incl[N_BUCKETS - 1:N_BUCKETS, 0:1] * (1.0 / SORT_TILE)).astype(jnp.int32)
        fill = (tile >= n_used - 1) | (tb != bucket_of(row0 + SORT_TILE))
        meta_ref[...] = jnp.zeros_like(meta_ref)
        meta_ref[META_EXPERT_LO:META_EXPERT_LO + 1, :] = EXPERTS_PER_GROUP * grp + lo
        meta_ref[META_EXPERT_HI:META_EXPERT_HI + 1, :] = EXPERTS_PER_GROUP * grp + hi
        meta_ref[META_N_USED:META_N_USED + 1, :] = jnp.broadcast_to(n_used, (1, META_LANES))
        meta_ref[META_FILL:META_FILL + 1, :] = fill.astype(jnp.int32)

    oh = _bucket_onehot(info_ref[...], sel_ref[...])
    before = jnp.dot(oh.astype(BF16), utri_ref[...], preferred_element_type=F32)
    base = start_sc[:, 0:1] + run_sc[:, 0:1]
    pos_ref[0] = jnp.sum(oh * (before + base), axis=0, keepdims=True).astype(jnp.int32)
    run_sc[...] += jnp.sum(oh, axis=1, keepdims=True)


def _sort_plan(info, cnt, tm):
    n = info.shape[0]
    sel = _bucket_selector()
    info_spec = pl.BlockSpec((tm, ROUTER_LANES), lambda i: (i, 0))
    r = jnp.arange(BUCKET_ROWS)
    ltri = (r[:, None] >= r[None, :]).astype(F32)
    t = jnp.arange(tm)
    utri = (t[:, None] < t[None, :]).astype(BF16)
    return pl.pallas_call(
        _plan_pos_kernel,
        grid=(n // tm,),
        in_specs=[info_spec, _full(sel.shape), _full(cnt.shape), _full(ltri.shape), _full(utri.shape)],
        out_specs=[pl.BlockSpec((1, 1, tm), lambda i: (i, 0, 0)), _full((8, META_LANES))],
        out_shape=[jax.ShapeDtypeStruct((n // tm, 1, tm), jnp.int32),
                   jax.ShapeDtypeStruct((8, META_LANES), jnp.int32)],
        scratch_shapes=[pltpu.VMEM((BUCKET_ROWS, LANES), F32), pltpu.VMEM((BUCKET_ROWS, LANES), F32)],
        compiler_params=_params(("arbitrary",)),
        name="plan_pos",
    )(info, sel, cnt, ltri, utri)


def _dispatch_kernel(pos_ref, fill_ref, h1_ref, gain2_ref, sh2_ref, sc2_ref, info_ref, xs_ref,
                     rowbuf, zbuf, sem, zsem, *, tm, n_steps, n_tiles):
    step = pl.program_id(0) * pl.num_programs(1) + pl.program_id(1)
    slot = step % 2

    @pl.when(step == 0)
    def _():
        zbuf[...] = jnp.zeros_like(zbuf)
        fill = lambda j: pltpu.make_async_copy(zbuf, xs_ref.at[pl.ds(j * SORT_TILE, SORT_TILE)], zsem)
        for j in range(n_tiles):
            pl.when(fill_ref[0, j] == 1)(lambda j=j: fill(j).start())
        for j in range(n_tiles):
            pl.when(fill_ref[0, j] == 1)(lambda j=j: fill(j).wait())

    def wait(s):
        pltpu.make_async_copy(rowbuf.at[s], xs_ref.at[pl.ds(0, tm)], sem.at[s]).wait()

    def send(s):
        @pl.when(step >= 2)
        def _():
            wait(s)

        rowbuf[s, :, 0:D_MODEL] = _modulate(h1_ref[0], gain2_ref[...], sh2_ref[...], sc2_ref[...])
        rowbuf[s, :, D_MODEL:] = info_ref[0]
        for r in range(tm):
            pltpu.make_async_copy(rowbuf.at[s, pl.ds(r, 1)], xs_ref.at[pl.ds(pos_ref[0, 0, r], 1)],
                                  sem.at[s]).start(priority=r % 2)

    for s in range(2):
        pl.when(slot == s)(functools.partial(send, s))

    @pl.when(step == n_steps - 1)
    def _():
        wait(slot)
        if n_steps > 1:
            wait(1 - slot)


def _gmoe_kernel(ea_ref, eb_ref, nu_ref, xs_ref, wga_ref, wua_ref, wda_ref, wgb_ref, wub_ref, wdb_ref, ys_ref):
    del ea_ref, eb_ref
    in_use = pl.program_id(0) < nu_ref[0]

    @pl.when(jnp.logical_not(in_use))
    def _():
        ys_ref[...] = jnp.zeros_like(ys_ref)

    @pl.when(in_use)
    def _():
        x = xs_ref[:, 0:D_MODEL].astype(BF16)

        def expert(wg_ref, wu_ref, wd_ref, w):
            gt = jnp.dot(x, wg_ref[0].astype(BF16), preferred_element_type=F32)
            up = jnp.dot(x, wu_ref[0].astype(BF16), preferred_element_type=F32)
            h = (gt * jax.nn.sigmoid(gt)) * up * w
            return jnp.dot(h.astype(BF16), wd_ref[0].astype(BF16), preferred_element_type=F32)

        w_lo = xs_ref[:, D_MODEL + INFO_W_LO:D_MODEL + INFO_W_LO + 1]
        w_hi = xs_ref[:, D_MODEL + INFO_W_HI:D_MODEL + INFO_W_HI + 1]
        ys_ref[...] = expert(wga_ref, wua_ref, wda_ref, w_lo) + expert(wgb_ref, wub_ref, wdb_ref, w_hi)


def _fetch_sorted_rows(ys_ref, pos_ref, posn_ref, ybuf, sem, tm, n_steps):
    step = pl.program_id(0) * pl.num_programs(1) + pl.program_id(1)
    slot = step % 2

    def issue(p_ref, s):
        for r in range(tm):
            pltpu.make_async_copy(ys_ref.at[pl.ds(p_ref[0, 0, r], 1)], ybuf.at[s, pl.ds(r, 1)],
                                  sem.at[s]).start(priority=r % 2)

    pl.when(step == 0)(functools.partial(issue, pos_ref, 0))
    for s in range(2):
        pl.when((step + 1 < n_steps) & (slot == s))(functools.partial(issue, posn_ref, 1 - s))
    pltpu.make_async_copy(ys_ref.at[pl.ds(0, tm)], ybuf.at[slot], sem.at[slot]).wait()
    return ybuf[slot]


def _combine_kernel(pos_ref, posn_ref, h1_ref, g2_ref, ys_ref, o_ref, ybuf, sem, *, tm, n_steps):
    rows = _fetch_sorted_rows(ys_ref, pos_ref, posn_ref, ybuf, sem, tm, n_steps)
    o_ref[0] = h1_ref[0] + g2_ref[...] * rows


def _experts_sorted(h1, info, cnt, mods, layer, gain2, w_gate, w_up, w_down, tm):
    bsz, seq, d = h1.shape
    nt = seq // tm
    n_steps = bsz * nt
    n = bsz * seq
    n_sorted = n + N_BUCKETS * SORT_TILE
    n_tiles = n_sorted // SORT_TILE
    assert n % SORT_TILE == 0 and n_tiles <= META_LANES
    pos, meta = _sort_plan(info.reshape(n, ROUTER_LANES), cnt, tm)
    tile = lambda w: pl.BlockSpec((1, tm, w), lambda b, i: (b, i, 0))
    any_spec = pl.BlockSpec(memory_space=pl.ANY)
    by_batch = lambda b, i: b

    xs = pl.pallas_call(
        functools.partial(_dispatch_kernel, tm=tm, n_steps=n_steps, n_tiles=n_tiles),
        grid=(bsz, nt),
        in_specs=[_pos_spec(tm, nt, n_steps, 0),
                  pl.BlockSpec((1, META_LANES), lambda b, i: (0, 0), memory_space=pltpu.SMEM),
                  tile(d), _full((1, d)), _mod_spec(layer, by_batch, 3),
                  _mod_spec(layer, by_batch, 4), tile(ROUTER_LANES)],
        out_specs=any_spec,
        out_shape=jax.ShapeDtypeStruct((n_sorted, ROW_WIDTH), F32),
        scratch_shapes=[pltpu.VMEM((2, tm, ROW_WIDTH), F32), pltpu.VMEM((SORT_TILE, ROW_WIDTH), F32),
                        pltpu.SemaphoreType.DMA((2,)), pltpu.SemaphoreType.DMA(())],
        compiler_params=_params(("arbitrary", "arbitrary")),
        name=f"dispatch{layer}",
    )(pos, meta[META_FILL:META_FILL + 1], h1, gain2.reshape(1, d), mods, mods, info)

    used = lambda j, nu: jnp.minimum(j, nu[0] - 1)
    w_in_spec = lambda which: pl.BlockSpec(
        (1, d, D_EXPERT), lambda j, ea, eb, nu: ((ea, eb)[which][used(j, nu)], 0, 0))
    w_out_spec = lambda which: pl.BlockSpec(
        (1, D_EXPERT, d), lambda j, ea, eb, nu: ((ea, eb)[which][used(j, nu)], 0, 0))
    wg, wu, wd = w_gate, w_up, w_down
    ys = pl.pallas_call(
        _gmoe_kernel,
        grid_spec=pltpu.PrefetchScalarGridSpec(
            num_scalar_prefetch=3,
            grid=(n_tiles,),
            in_specs=[pl.BlockSpec((SORT_TILE, ROW_WIDTH), lambda j, ea, eb, nu: (used(j, nu), 0)),
                      w_in_spec(0), w_in_spec(0), w_out_spec(0), w_in_spec(1), w_in_spec(1), w_out_spec(1)],
            out_specs=pl.BlockSpec((SORT_TILE, d), lambda j, ea, eb, nu: (j, 0))),
        out_shape=jax.ShapeDtypeStruct((n_sorted, d), F32),
        compiler_params=_params(("arbitrary",)),
        name=f"experts{layer}",
    )(meta[META_EXPERT_LO], meta[META_EXPERT_HI], meta[META_N_USED, :1], xs, wg, wu, wd, wg, wu, wd)
    return ys, pos


def _combine(h1, ys, pos, mods, layer, tm):
    bsz, seq, d = h1.shape
    nt = seq // tm
    n_steps = bsz * nt
    tile = lambda w: pl.BlockSpec((1, tm, w), lambda b, i: (b, i, 0))
    return pl.pallas_call(
        functools.partial(_combine_kernel, tm=tm, n_steps=n_steps),
        grid=(bsz, nt),
        in_specs=[_pos_spec(tm, nt, n_steps, 0), _pos_spec(tm, nt, n_steps, 1), tile(d),
                  _mod_spec(layer, lambda b, i: b, 5), pl.BlockSpec(memory_space=pl.ANY)],
        out_specs=tile(d),
        out_shape=jax.ShapeDtypeStruct((bsz, seq, d), F32),
        scratch_shapes=[pltpu.VMEM((2, tm, d), F32), pltpu.SemaphoreType.DMA((2,))],
        compiler_params=_params(("arbitrary", "arbitrary")),
        name=f"combine{layer}",
    )(pos, pos, h1, mods, ys)


def kernel(x, c, ctx, c_ctx, mod_w, mod_b, norm1_g, norm2_g, even_w_in, q_gain, k_gain, pool_w, pool_scale,
           even_w_out, odd_w_in, sg_gain, sg_w, sg_b, conv_w, odd_w_out, router_g_w, router_g_b,
           router_e_w, router_e_b, w_gate, w_up, w_down):
    bsz, seq, d = x.shape
    tm = min(512, seq)
    cond = jnp.zeros((MOD_ROWS, d), F32).at[:bsz].set(c).at[bsz].set(c_ctx)
    mods = _adaln(cond, mod_w, mod_b).reshape(mod_w.shape[0], MOD_ROWS, 6, 1, d)

    q, k, v, p = _inproj0(x, mods, norm1_g[0], even_w_in[0], q_gain[0], k_gain[0], tm)
    kc, vc = _inproj0_ctx(ctx, mods, bsz, norm1_g[0], even_w_in[0][:, ATTN_WIDTH:ATTN_WIDTH + 2 * KV_WIDTH],
                          k_gain[0])
    o = _attention(q, k, v, kc, vc, tq=min(256, seq), tk=min(2048, seq))
    router0 = _router_operands(router_g_w[0], router_g_b[0], router_e_w[0], router_e_b[0])
    h1, info, cnt = _out0(o, p, x, mods, norm2_g[0], pool_w[0], pool_scale[0], even_w_out[0], router0, tm)
    ys, pos = _experts_sorted(h1, info, cnt, mods, 0, norm2_g[0], w_gate[0], w_up[0], w_down[0], tm)

    h, yc, z, bg = _inproj1(h1, ys, pos, mods, norm1_g[1], odd_w_in[0], sg_gain[0], sg_w[0], sg_b[0], tm)
    router1 = _router_operands(router_g_w[1], router_g_b[1], router_e_w[1], router_e_b[1])
    h1, info, cnt = _out1(yc, z, bg, h, mods, norm2_g[1], conv_w[0], odd_w_out[0], router1, tm)
    ys, pos = _experts_sorted(h1, info, cnt, mods, 1, norm2_g[1], w_gate[1], w_up[1], w_down[1], tm)
    return _combine(h1, ys, pos, mods, 1, tm)
```

```python
import functools

import jax
import jax.numpy as jnp
from jax import lax
from jax.experimental import pallas as pl
from jax.experimental.pallas import tpu as pltpu

F32 = jnp.float32
BF16 = jnp.bfloat16

D_MODEL = 1024
GRID_W = 64
EPS = 1e-6
N_Q_HEADS = 8
N_KV_HEADS = 2
HEAD_DIM = 64
Q_PER_KV = N_Q_HEADS // N_KV_HEADS
ATTN_WIDTH = N_Q_HEADS * HEAD_DIM
KV_WIDTH = N_KV_HEADS * HEAD_DIM
ROPE_THETA = 10000.0
POOL_WINDOWS = (2, 4, 8, 16)
POOL_GROUP = 128
POOL_WIDTH = POOL_GROUP * len(POOL_WINDOWS)
SG_GROUPS = 4
SG_CHUNK = 128
SG_WIDTH = 512
CONV_WIDTH = 512
EVEN_IN = ATTN_WIDTH + 2 * KV_WIDTH + POOL_WIDTH
ODD_IN = 2 * SG_WIDTH + 3 * CONV_WIDTH
N_GROUPS = 4
EXPERTS_PER_GROUP = 4
N_EXPERTS = 16
D_EXPERT = 256

Q_SCALE = HEAD_DIM ** -0.5 * 1.4426950408889634
LANES = 128
HALO = 8
ROUTER_LANES = 128
MOD_ROWS = 16
VMEM_LIMIT = 48 * 1024 * 1024

PAIRS_PER_GROUP = 6
N_BUCKETS = N_GROUPS * PAIRS_PER_GROUP
BUCKET_ROWS = 32
SORT_TILE = 512
META_LANES = 256
META_EXPERT_LO, META_EXPERT_HI, META_N_USED, META_FILL = 0, 1, 2, 3
ROW_WIDTH = D_MODEL + ROUTER_LANES
INFO_BUCKET, INFO_W_LO, INFO_W_HI = 0, 1, 2


def _params(sem):
    return pltpu.CompilerParams(dimension_semantics=sem, vmem_limit_bytes=VMEM_LIMIT)


def _modulate(x, gain, shift, scale):
    ms = jnp.mean(x * x, axis=-1, keepdims=True)
    return (x * lax.rsqrt(ms + EPS) * gain) * (1.0 + scale) + shift


def _mod_spec(layer, row_fn, which):
    return pl.BlockSpec((None, None, None, 1, D_MODEL),
                        lambda *idx: (layer, row_fn(*idx), which, 0, 0))


def _full(shape):
    return pl.BlockSpec(shape, lambda *idx: (0,) * len(shape))


def _adaln_kernel(c_ref, w_ref, b_ref, o_ref):
    c = c_ref[...]
    s = c * jax.nn.sigmoid(c)
    o_ref[0] = jnp.dot(s, w_ref[0], precision=lax.Precision.HIGHEST,
                       preferred_element_type=F32) + b_ref[0]


def _adaln(cond, mod_w, mod_b):
    depth, d, n = mod_w.shape
    tn = 1024
    return pl.pallas_call(
        _adaln_kernel,
        grid=(depth, n // tn),
        in_specs=[_full((MOD_ROWS, d)),
                  pl.BlockSpec((1, d, tn), lambda l, j: (l, 0, j)),
                  pl.BlockSpec((1, 1, tn), lambda l, j: (l, 0, j))],
        out_specs=pl.BlockSpec((1, MOD_ROWS, tn), lambda l, j: (l, 0, j)),
        out_shape=jax.ShapeDtypeStruct((depth, MOD_ROWS, n), F32),
        compiler_params=_params(("arbitrary", "arbitrary")),
        name="adaln",
    )(cond, mod_w, mod_b.reshape(depth, 1, n))


def _head_norm_rope(z, gain, ones_bd, cos, sin, first_half):
    sq = z * z
    hi = sq.astype(BF16)
    lo = (sq - hi.astype(F32)).astype(BF16)
    ms = (jnp.dot(hi, ones_bd, preferred_element_type=F32)
          + jnp.dot(lo, ones_bd, preferred_element_type=F32))
    zn = z * lax.rsqrt(ms + EPS) * gain
    partner = jnp.where(first_half, pltpu.roll(zn, LANES - 16, 1), pltpu.roll(zn, 16, 1))
    return zn * cos + partner * sin


def _inproj0_kernel(x_ref, gain_ref, sh_ref, sc_ref, w_ref, cos_ref, sin_ref, qg_ref, kg_ref, ones_ref,
                    q_ref, k_ref, v_ref, p_ref):
    a = _modulate(x_ref[0], gain_ref[...], sh_ref[...], sc_ref[...])
    y = jnp.dot(a.astype(BF16), w_ref[...], preferred_element_type=F32)
    cos, sin, ones_bd = cos_ref[...], sin_ref[...], ones_ref[...]
    lane = lax.broadcasted_iota(jnp.int32, cos.shape, 1)
    first_half = (lane % 32) < 16
    for s in range(ATTN_WIDTH // LANES):
        r = _head_norm_rope(y[:, s * LANES:(s + 1) * LANES], qg_ref[...], ones_bd, cos, sin, first_half)
        r = (r * Q_SCALE).astype(BF16)
        q_ref[0, 2 * s] = r[:, :HEAD_DIM]
        q_ref[0, 2 * s + 1] = r[:, HEAD_DIM:]
    kr = _head_norm_rope(y[:, ATTN_WIDTH:ATTN_WIDTH + KV_WIDTH], kg_ref[...], ones_bd, cos, sin,
                         first_half).astype(BF16)
    k_ref[0, 0] = kr[:, :HEAD_DIM]
    k_ref[0, 1] = kr[:, HEAD_DIM:]
    vv = y[:, ATTN_WIDTH + KV_WIDTH:ATTN_WIDTH + 2 * KV_WIDTH].astype(BF16)
    v_ref[0, 0] = vv[:, :HEAD_DIM]
    v_ref[0, 1] = vv[:, HEAD_DIM:]
    p_ref[0] = y[:, ATTN_WIDTH + 2 * KV_WIDTH:]


def _inproj0_ctx_kernel(x_ref, gain_ref, sh_ref, sc_ref, w_ref, kg_ref, ones_ref, k_ref, v_ref):
    a = _modulate(x_ref[0], gain_ref[...], sh_ref[...], sc_ref[...])
    y = jnp.dot(a.astype(BF16), w_ref[...], preferred_element_type=F32)
    z = y[:, :KV_WIDTH]
    sq = z * z
    hi = sq.astype(BF16)
    lo = (sq - hi.astype(F32)).astype(BF16)
    ms = (jnp.dot(hi, ones_ref[...], preferred_element_type=F32)
          + jnp.dot(lo, ones_ref[...], preferred_element_type=F32))
    kr = (z * lax.rsqrt(ms + EPS) * kg_ref[...]).astype(BF16)
    k_ref[0, 0] = kr[:, :HEAD_DIM]
    k_ref[0, 1] = kr[:, HEAD_DIM:]
    vv = y[:, KV_WIDTH:].astype(BF16)
    v_ref[0, 0] = vv[:, :HEAD_DIM]
    v_ref[0, 1] = vv[:, HEAD_DIM:]


def _rope_tables(seq):
    t = jnp.arange(seq)
    row = (t // GRID_W).astype(F32)
    col = (t % GRID_W).astype(F32)
    half = HEAD_DIM // 2
    inv = ROPE_THETA ** (-jnp.arange(0, half, 2, dtype=F32) / half)
    ar, ac = row[:, None] * inv, col[:, None] * inv
    cos = jnp.concatenate([jnp.cos(ar), jnp.cos(ar), jnp.cos(ac), jnp.cos(ac)], axis=-1)
    sin = jnp.concatenate([-jnp.sin(ar), jnp.sin(ar), -jnp.sin(ac), jnp.sin(ac)], axis=-1)
    return jnp.tile(cos, (1, LANES // HEAD_DIM)), jnp.tile(sin, (1, LANES // HEAD_DIM))


def _head_mean_matrix():
    r = jnp.arange(LANES)
    same = (r[:, None] // HEAD_DIM) == (r[None, :] // HEAD_DIM)
    return jnp.where(same, 1.0 / HEAD_DIM, 0.0).astype(BF16)


def _inproj0(x, mods, gain, w_in, q_gain, k_gain, tm):
    bsz, seq, d = x.shape
    cos, sin = _rope_tables(seq)
    qg = jnp.tile(q_gain, LANES // HEAD_DIM).reshape(1, LANES)
    kg = jnp.tile(k_gain, LANES // HEAD_DIM).reshape(1, LANES)
    head = lambda n: pl.BlockSpec((1, n, tm, HEAD_DIM), lambda b, i: (b, 0, i, 0))
    return pl.pallas_call(
        _inproj0_kernel,
        grid=(bsz, seq // tm),
        in_specs=[pl.BlockSpec((1, tm, d), lambda b, i: (b, i, 0)),
                  _full((1, d)),
                  _mod_spec(0, lambda b, i: b, 0),
                  _mod_spec(0, lambda b, i: b, 1),
                  _full((d, EVEN_IN)),
                  pl.BlockSpec((tm, LANES), lambda b, i: (i, 0)),
                  pl.BlockSpec((tm, LANES), lambda b, i: (i, 0)),
                  _full((1, LANES)), _full((1, LANES)), _full((LANES, LANES))],
        out_specs=[head(N_Q_HEADS), head(N_KV_HEADS), head(N_KV_HEADS),
                   pl.BlockSpec((1, tm, POOL_WIDTH), lambda b, i: (b, i, 0))],
        out_shape=[jax.ShapeDtypeStruct((bsz, N_Q_HEADS, seq, HEAD_DIM), BF16),
                   jax.ShapeDtypeStruct((bsz, N_KV_HEADS, seq, HEAD_DIM), BF16),
                   jax.ShapeDtypeStruct((bsz, N_KV_HEADS, seq, HEAD_DIM), BF16),
                   jax.ShapeDtypeStruct((bsz, seq, POOL_WIDTH), F32)],
        compiler_params=_params(("parallel", "parallel")),
        name="inproj0",
    )(x, gain.reshape(1, d), mods, mods, w_in.astype(BF16), cos, sin, qg, kg, _head_mean_matrix())


def _inproj0_ctx(ctx, mods, ctx_row, gain, w_kv, k_gain):
    bsz, n_ctx, d = ctx.shape
    kg = jnp.tile(k_gain, LANES // HEAD_DIM).reshape(1, LANES)
    head = pl.BlockSpec((1, N_KV_HEADS, n_ctx, HEAD_DIM), lambda b: (b, 0, 0, 0))
    return pl.pallas_call(
        _inproj0_ctx_kernel,
        grid=(bsz,),
        in_specs=[pl.BlockSpec((1, n_ctx, d), lambda b: (b, 0, 0)),
                  _full((1, d)),
                  _mod_spec(0, lambda b: ctx_row, 0),
                  _mod_spec(0, lambda b: ctx_row, 1),
                  _full((d, 2 * KV_WIDTH)),
                  _full((1, LANES)), _full((LANES, LANES))],
        out_specs=[head, head],
        out_shape=[jax.ShapeDtypeStruct((bsz, N_KV_HEADS, n_ctx, HEAD_DIM), BF16)] * 2,
        compiler_params=_params(("parallel",)),
        name="inproj0_ctx",
    )(ctx, gain.reshape(1, d), mods, mods, w_kv.astype(BF16), kg, _head_mean_matrix())


def _attn_kernel(q_ref, kl_ref, vl_ref, kc_ref, vc_ref, o_ref, *, tq, tk):
    rows = Q_PER_KV * tq
    q = q_ref[0].reshape(rows, HEAD_DIM)
    seq = kl_ref.shape[2]

    def step(k, v, carry):
        m, l, acc = carry
        s = lax.dot_general(q, k, (((1,), (1,)), ((), ())), preferred_element_type=F32)
        m_new = jnp.maximum(m, jnp.max(s, axis=1, keepdims=True))
        alpha = jnp.exp2(m - m_new)
        p = jnp.exp2(s - m_new)
        l = alpha * l + jnp.sum(p, axis=1, keepdims=True)
        acc = alpha * acc + jnp.dot(p.astype(BF16), v, preferred_element_type=F32)
        return m_new, l, acc

    carry = (jnp.full((rows, 1), -jnp.inf, F32), jnp.zeros((rows, 1), F32),
             jnp.zeros((rows, HEAD_DIM), F32))
    carry = step(kc_ref[0, 0], vc_ref[0, 0], carry)
    for c in range(seq // tk):
        carry = step(kl_ref[0, 0, c * tk:(c + 1) * tk, :], vl_ref[0, 0, c * tk:(c + 1) * tk, :], carry)
    _, l, acc = carry
    o = acc / l
    o_ref[0] = jnp.concatenate([o[h * tq:(h + 1) * tq] for h in range(Q_PER_KV)], axis=1).astype(BF16)


def _attention(q, k, v, kc, vc, tq, tk):
    bsz, _, seq, _ = q.shape
    n_ctx = kc.shape[2]
    kv_spec = lambda n: pl.BlockSpec((1, 1, n, HEAD_DIM), lambda b, g, i: (b, g, 0, 0))
    return pl.pallas_call(
        functools.partial(_attn_kernel, tq=tq, tk=tk),
        grid=(bsz, N_KV_HEADS, seq // tq),
        in_specs=[pl.BlockSpec((1, Q_PER_KV, tq, HEAD_DIM), lambda b, g, i: (b, g, i, 0)),
                  kv_spec(seq), kv_spec(seq), kv_spec(n_ctx), kv_spec(n_ctx)],
        out_specs=pl.BlockSpec((1, tq, Q_PER_KV * HEAD_DIM), lambda b, g, i: (b, i, g)),
        out_shape=jax.ShapeDtypeStruct((bsz, seq, ATTN_WIDTH), BF16),
        compiler_params=_params(("parallel", "parallel", "parallel")),
        name="attention",
    )(q, k, v, kc, vc)


def _route(logits):
    lane = lax.broadcasted_iota(jnp.int32, logits.shape, 1)
    neg = -jnp.inf
    big = ROUTER_LANES
    is_g = lane < N_GROUPS
    gm = jnp.max(jnp.where(is_g, logits, neg), axis=1, keepdims=True)
    gidx = jnp.min(jnp.where(is_g & (logits == gm), lane, big), axis=1, keepdims=True)
    gden = jnp.sum(jnp.where(is_g, jnp.exp(logits - gm), 0.0), axis=1, keepdims=True)
    g_p = 1.0 / gden
    first = N_GROUPS + EXPERTS_PER_GROUP * gidx
    sel = (lane >= first) & (lane < first + EXPERTS_PER_GROUP)
    e1 = jnp.max(jnp.where(sel, logits, neg), axis=1, keepdims=True)
    i1 = jnp.min(jnp.where(sel & (logits == e1), lane, big), axis=1, keepdims=True)
    rest = sel & (lane != i1)
    e2 = jnp.max(jnp.where(rest, logits, neg), axis=1, keepdims=True)
    i2 = jnp.min(jnp.where(rest & (logits == e2), lane, big), axis=1, keepdims=True)
    p2 = jnp.exp(e2 - e1)
    w1 = g_p * (1.0 / (1.0 + p2))
    w2 = g_p * (p2 / (1.0 + p2))
    lo = jnp.minimum(i1, i2) - first
    hi = jnp.maximum(i1, i2) - first
    pair = jnp.where(lo == 0, hi - 1, jnp.where(lo == 1, hi + 1, PAIRS_PER_GROUP - 1))
    bucket = (PAIRS_PER_GROUP * gidx + pair).astype(F32)
    w_lo = jnp.where(i1 < i2, w1, w2)
    w_hi = jnp.where(i1 < i2, w2, w1)
    return jnp.where(lane == INFO_BUCKET, bucket,
                     jnp.where(lane == INFO_W_LO, w_lo, jnp.where(lane == INFO_W_HI, w_hi, 0.0)))


def _bucket_onehot(info, sel):
    brow = lax.dot_general(sel, info.astype(BF16), (((1,), (1,)), ((), ())),
                           preferred_element_type=F32)[0:1]
    bid = lax.broadcasted_iota(jnp.int32, (BUCKET_ROWS, info.shape[0]), 0)
    return (bid == brow.astype(jnp.int32)).astype(F32)


def _bucket_selector():
    return jnp.zeros((8, ROUTER_LANES), F32).at[0, INFO_BUCKET].set(1.0).astype(BF16)


def _tail(y, x_res, gate1, gain2, shift2, scale2, rw_hi, rw_lo, rbias, sel, h1_ref, info_ref, cnt_ref):
    h1 = x_res + gate1 * y
    h1_ref[0] = h1
    t = _modulate(h1, gain2, shift2, scale2)
    t_hi = t.astype(BF16)
    t_lo = (t - t_hi.astype(F32)).astype(BF16)
    logits = (jnp.dot(t_hi, rw_hi, preferred_element_type=F32)
              + jnp.dot(t_lo, rw_hi, preferred_element_type=F32)
              + jnp.dot(t_hi, rw_lo, preferred_element_type=F32)) + rbias
    info = _route(logits)
    info_ref[0] = info

    @pl.when((pl.program_id(0) == 0) & (pl.program_id(1) == 0))
    def _():
        cnt_ref[...] = jnp.zeros_like(cnt_ref)

    cnt_ref[...] += jnp.sum(_bucket_onehot(info, sel), axis=1, keepdims=True)


def _router_operands(rg_w, rg_b, re_w, re_b):
    d = rg_w.shape[0]
    w = jnp.concatenate([rg_w, re_w, jnp.zeros((d, ROUTER_LANES - N_GROUPS - N_EXPERTS), F32)], axis=1)
    b = jnp.concatenate([rg_b, re_b, jnp.zeros((ROUTER_LANES - N_GROUPS - N_EXPERTS,), F32)])
    w_hi = w.astype(BF16)
    w_lo = (w - w_hi.astype(F32)).astype(BF16)
    return w_hi, w_lo, b.reshape(1, ROUTER_LANES)


def _fill_halo(buf, main_ref, prev_ref, next_ref, tm, i, n_tiles):
    buf[HALO:HALO + tm] = main_ref[0]
    buf[0:HALO] = jnp.where(i > 0, prev_ref[0], 0.0)
    buf[HALO + tm:2 * HALO + tm] = jnp.where(i < n_tiles - 1, next_ref[0], 0.0)


def _halo_specs(tm, seq, width):
    per = tm // HALO
    last = seq // HALO - 1
    return [pl.BlockSpec((1, tm, width), lambda b, i: (b, i, 0)),
            pl.BlockSpec((1, HALO, width), lambda b, i: (b, jnp.maximum(i * per - 1, 0), 0)),
            pl.BlockSpec((1, HALO, width), lambda b, i: (b, jnp.minimum((i + 1) * per, last), 0))]


def _out0_kernel(o_ref, p_ref, pprev_ref, pnext_ref, x_ref, g1_ref, gain2_ref, sh2_ref, sc2_ref,
                 poolw_ref, pscale_ref, wout_ref, rwh_ref, rwl_ref, rb_ref, sel_ref,
                 h1_ref, info_ref, cnt_ref, pbuf, *, tm, seq):
    i = pl.program_id(1)
    _fill_halo(pbuf, p_ref, pprev_ref, pnext_ref, tm, i, seq // tm)
    pos = i * tm + lax.broadcasted_iota(jnp.int32, (tm, 1), 0)
    y = jnp.dot(o_ref[0], wout_ref[0:ATTN_WIDTH, :], preferred_element_type=F32)
    for g, w in enumerate(POOL_WINDOWS):
        sl = slice(g * POOL_GROUP, (g + 1) * POOL_GROUP)
        acc = pbuf[HALO - w // 2:HALO - w // 2 + tm, sl]
        for j in range(1 - w // 2, w - w // 2):
            acc = acc + pbuf[HALO + j:HALO + j + tm, sl]
        lo = jnp.clip(pos - w // 2, 0, seq)
        hi = jnp.clip(pos + w - w // 2, 0, seq)
        mean = acc / (hi - lo).astype(F32)
        dlt = (mean - pbuf[HALO:HALO + tm, sl]).astype(BF16)
        yp = jnp.dot(dlt, poolw_ref[g], preferred_element_type=F32) * pscale_ref[:, sl]
        y = y + jnp.dot(yp.astype(BF16), wout_ref[ATTN_WIDTH + g * POOL_GROUP:ATTN_WIDTH + (g + 1) * POOL_GROUP, :],
                        preferred_element_type=F32)
    _tail(y, x_ref[0], g1_ref[...], gain2_ref[...], sh2_ref[...], sc2_ref[...],
          rwh_ref[...], rwl_ref[...], rb_ref[...], sel_ref[...], h1_ref, info_ref, cnt_ref)


def _tail_specs(layer, d):
    by_batch = lambda b, i: b
    ins = [_mod_spec(layer, by_batch, 2), _full((1, d)), _mod_spec(layer, by_batch, 3),
           _mod_spec(layer, by_batch, 4)]
    return ins


def _tail_outs(bsz, seq, d, tm):
    specs = [pl.BlockSpec((1, tm, d), lambda b, i: (b, i, 0)),
             pl.BlockSpec((1, tm, ROUTER_LANES), lambda b, i: (b, i, 0)),
             _full((BUCKET_ROWS, LANES))]
    shapes = [jax.ShapeDtypeStruct((bsz, seq, d), F32),
              jax.ShapeDtypeStruct((bsz, seq, ROUTER_LANES), F32),
              jax.ShapeDtypeStruct((BUCKET_ROWS, LANES), F32)]
    return specs, shapes


def _out0(o, p, x, mods, gain2, pool_w, pool_scale, w_out, router, tm):
    bsz, seq, d = x.shape
    rw_hi, rw_lo, rb = router
    out_specs, out_shapes = _tail_outs(bsz, seq, d, tm)
    return pl.pallas_call(
        functools.partial(_out0_kernel, tm=tm, seq=seq),
        grid=(bsz, seq // tm),
        in_specs=[pl.BlockSpec((1, tm, ATTN_WIDTH), lambda b, i: (b, i, 0))]
        + _halo_specs(tm, seq, POOL_WIDTH)
        + [pl.BlockSpec((1, tm, d), lambda b, i: (b, i, 0))]
        + _tail_specs(0, d)
        + [_full(pool_w.shape), _full((1, POOL_WIDTH)), _full(w_out.shape),
           _full(rw_hi.shape), _full(rw_lo.shape), _full(rb.shape), _full((8, ROUTER_LANES))],
        out_specs=out_specs,
        out_shape=out_shapes,
        scratch_shapes=[pltpu.VMEM((tm + 2 * HALO, POOL_WIDTH), F32)],
        compiler_params=_params(("arbitrary", "arbitrary")),
        name="out0",
    )(o, p, p, p, x, mods, gain2.reshape(1, d), mods, mods,
      pool_w.astype(BF16), pool_scale.reshape(1, POOL_WIDTH), w_out.astype(BF16), rw_hi, rw_lo, rb,
      _bucket_selector())


def _inproj1_kernel(pos_ref, posn_ref, h1_ref, g2_ref, gain_ref, sh_ref, sc_ref, w_ref, sgg_ref, sgw_ref, sgb_ref,
                    ys_ref, h_ref, yc_ref, z_ref, bg_ref, ybuf, sem, *, tm, n_steps):
    x = h1_ref[0] + g2_ref[...] * _fetch_sorted_rows(ys_ref, pos_ref, posn_ref, ybuf, sem, tm, n_steps)
    h_ref[0] = x
    a = _modulate(x, gain_ref[...], sh_ref[...], sc_ref[...])
    y = jnp.dot(a.astype(BF16), w_ref[...], preferred_element_type=F32)
    for g in range(SG_GROUPS):
        sl = slice(g * LANES, (g + 1) * LANES)
        u = y[:, sl]
        vg = y[:, SG_WIDTH + g * LANES:SG_WIDTH + (g + 1) * LANES]
        ms = jnp.mean(vg * vg, axis=-1, keepdims=True)
        vn = (vg * lax.rsqrt(ms + EPS) * sgg_ref[:, sl]).astype(BF16)
        for c in range(tm // SG_CHUNK):
            rows = slice(c * SG_CHUNK, (c + 1) * SG_CHUNK)
            s = jnp.dot(sgw_ref[g], vn[rows], preferred_element_type=F32) + sgb_ref[g]
            yc_ref[0, rows, sl] = (u[rows] * s).astype(BF16)
    hx = y[:, 2 * SG_WIDTH:2 * SG_WIDTH + CONV_WIDTH]
    bg_ref[0] = y[:, 2 * SG_WIDTH + CONV_WIDTH:2 * SG_WIDTH + 2 * CONV_WIDTH]
    cg = y[:, 2 * SG_WIDTH + 2 * CONV_WIDTH:]
    z_ref[0] = cg * hx


def _pos_spec(tm, nt, n_steps, ahead):
    return pl.BlockSpec((1, 1, tm), lambda b, i: (jnp.minimum(b * nt + i + ahead, n_steps - 1), 0, 0),
                        memory_space=pltpu.SMEM)


def _inproj1(h1, ys, pos, mods, gain, w_in, sg_gain, sg_w, sg_b, tm):
    bsz, seq, d = h1.shape
    nt = seq // tm
    n_steps = bsz * nt
    sgb = jnp.broadcast_to(sg_b[:, :, None], (SG_GROUPS, SG_CHUNK, LANES))
    by_batch = lambda b, i: b
    wide = lambda w, dt: (pl.BlockSpec((1, tm, w), lambda b, i: (b, i, 0)),
                          jax.ShapeDtypeStruct((bsz, seq, w), dt))
    outs = [wide(d, F32), wide(SG_WIDTH, BF16), wide(CONV_WIDTH, F32), wide(CONV_WIDTH, F32)]
    return pl.pallas_call(
        functools.partial(_inproj1_kernel, tm=tm, n_steps=n_steps),
        grid=(bsz, nt),
        in_specs=[_pos_spec(tm, nt, n_steps, 0), _pos_spec(tm, nt, n_steps, 1),
                  pl.BlockSpec((1, tm, d), lambda b, i: (b, i, 0)),
                  _mod_spec(0, by_batch, 5),
                  _full((1, d)),
                  _mod_spec(1, by_batch, 0),
                  _mod_spec(1, by_batch, 1),
                  _full((d, ODD_IN)),
                  _full((1, SG_WIDTH)), _full(sg_w.shape), _full(sgb.shape),
                  pl.BlockSpec(memory_space=pl.ANY)],
        out_specs=[s for s, _ in outs],
        out_shape=[s for _, s in outs],
        scratch_shapes=[pltpu.VMEM((2, tm, d), F32), pltpu.SemaphoreType.DMA((2,))],
        compiler_params=_params(("arbitrary", "arbitrary")),
        name="inproj1",
    )(pos, pos, h1, mods, gain.reshape(1, d), mods, mods, w_in.astype(BF16), sg_gain.reshape(1, SG_WIDTH),
      sg_w.astype(BF16), sgb, ys)


def _out1_kernel(yc_ref, z_ref, zprev_ref, znext_ref, bg_ref, x_ref, g1_ref, gain2_ref, sh2_ref, sc2_ref,
                 convw_ref, wout_ref, rwh_ref, rwl_ref, rb_ref, sel_ref,
                 h1_ref, info_ref, cnt_ref, zbuf, *, tm, seq):
    i = pl.program_id(1)
    _fill_halo(zbuf, z_ref, zprev_ref, znext_ref, tm, i, seq // tm)
    zc = (zbuf[HALO - 1:HALO - 1 + tm] * convw_ref[0:1, :]
          + zbuf[HALO:HALO + tm] * convw_ref[1:2, :]
          + zbuf[HALO + 1:HALO + 1 + tm] * convw_ref[2:3, :])
    yd = (bg_ref[0] * zc).astype(BF16)
    y = (jnp.dot(yc_ref[0], wout_ref[0:SG_WIDTH, :], preferred_element_type=F32)
         + jnp.dot(yd, wout_ref[SG_WIDTH:, :], preferred_element_type=F32))
    _tail(y, x_ref[0], g1_ref[...], gain2_ref[...], sh2_ref[...], sc2_ref[...],
          rwh_ref[...], rwl_ref[...], rb_ref[...], sel_ref[...], h1_ref, info_ref, cnt_ref)


def _out1(yc, z, bg, x, mods, gain2, conv_w, w_out, router, tm):
    bsz, seq, d = x.shape
    rw_hi, rw_lo, rb = router
    out_specs, out_shapes = _tail_outs(bsz, seq, d, tm)
    wide = pl.BlockSpec((1, tm, CONV_WIDTH), lambda b, i: (b, i, 0))
    return pl.pallas_call(
        functools.partial(_out1_kernel, tm=tm, seq=seq),
        grid=(bsz, seq // tm),
        in_specs=[wide] + _halo_specs(tm, seq, CONV_WIDTH) + [wide]
        + [pl.BlockSpec((1, tm, d), lambda b, i: (b, i, 0))]
        + _tail_specs(1, d)
        + [_full((3, CONV_WIDTH)), _full(w_out.shape),
           _full(rw_hi.shape), _full(rw_lo.shape), _full(rb.shape), _full((8, ROUTER_LANES))],
        out_specs=out_specs,
        out_shape=out_shapes,
        scratch_shapes=[pltpu.VMEM((tm + 2 * HALO, CONV_WIDTH), F32)],
        compiler_params=_params(("arbitrary", "arbitrary")),
        name="out1",
    )(yc, z, z, z, bg, x, mods, gain2.reshape(1, d), mods, mods,
      conv_w.reshape(3, CONV_WIDTH), w_out.astype(BF16), rw_hi, rw_lo, rb, _bucket_selector())


def _plan_pos_kernel(info_ref, sel_ref, cnt_ref, ltri_ref, utri_ref, pos_ref, meta_ref, start_sc, run_sc):
    @pl.when(pl.program_id(0) == 0)
    def _():
        padded = jnp.ceil(cnt_ref[...] * (1.0 / SORT_TILE)) * SORT_TILE
        incl = jnp.dot(ltri_ref[...], padded, precision=lax.Precision.HIGHEST, preferred_element_type=F32)
        start_sc[...] = incl - padded
        run_sc[...] = jnp.zeros_like(run_sc)
        ends = jnp.broadcast_to(incl[:, 0:1], (BUCKET_ROWS, META_LANES))
        bid = lax.broadcasted_iota(jnp.int32, ends.shape, 0)
        tile = lax.broadcasted_iota(jnp.int32, (1, META_LANES), 1)

        def bucket_of(row0):
            done = jnp.where((bid < N_BUCKETS) & (ends <= row0), 1.0, 0.0)
            return jnp.minimum(jnp.sum(done, axis=0, keepdims=True), N_BUCKETS - 1.0).astype(jnp.int32)

        row0 = (tile * SORT_TILE).astype(F32)
        tb = bucket_of(row0)
        grp = ((tb >= PAIRS_PER_GROUP).astype(jnp.int32) + (tb >= 2 * PAIRS_PER_GROUP).astype(jnp.int32)
               + (tb >= 3 * PAIRS_PER_GROUP).astype(jnp.int32))
        pair = tb - PAIRS_PER_GROUP * grp
        lo = (pair >= 3).astype(jnp.int32) + (pair >= 5).astype(jnp.int32)
        hi = jnp.where(pair == 0, 1, jnp.where((pair == 1) | (pair == 3), 2, 3))
        n_used = (incl[N_BUCKETS - 1:N_BUCKETS, 0:1] * (1.0 / SORT_TILE)).astype(jnp.int32)
        fill = (tile >= n_used - 1) | (tb != bucket_of(row0 + SORT_TILE))
        meta_ref[...] = jnp.zeros_like(meta_ref)
        meta_ref[META_EXPERT_LO:META_EXPERT_LO + 1, :] = EXPERTS_PER_GROUP * grp + lo
        meta_ref[META_EXPERT_HI:META_EXPERT_HI + 1, :] = EXPERTS_PER_GROUP * grp + hi
        meta_ref[META_N_USED:META_N_USED + 1, :] = jnp.broadcast_to(n_used, (1, META_LANES))
        meta_ref[META_FILL:META_FILL + 1, :] = fill.astype(jnp.int32)

    oh = _bucket_onehot(info_ref[...], sel_ref[...])
    before = jnp.dot(oh.astype(BF16), utri_ref[...], preferred_element_type=F32)
    base = start_sc[:, 0:1] + run_sc[:, 0:1]
    pos_ref[0] = jnp.sum(oh * (before + base), axis=0, keepdims=True).astype(jnp.int32)
    run_sc[...] += jnp.sum(oh, axis=1, keepdims=True)


def _sort_plan(info, cnt, tm):
    n = info.shape[0]
    sel = _bucket_selector()
    info_spec = pl.BlockSpec((tm, ROUTER_LANES), lambda i: (i, 0))
    r = jnp.arange(BUCKET_ROWS)
    ltri = (r[:, None] >= r[None, :]).astype(F32)
    t = jnp.arange(tm)
    utri = (t[:, None] < t[None, :]).astype(BF16)
    return pl.pallas_call(
        _plan_pos_kernel,
        grid=(n // tm,),
        in_specs=[info_spec, _full(sel.shape), _full(cnt.shape), _full(ltri.shape), _full(utri.shape)],
        out_specs=[pl.BlockSpec((1, 1, tm), lambda i: (i, 0, 0)), _full((8, META_LANES))],
        out_shape=[jax.ShapeDtypeStruct((n // tm, 1, tm), jnp.int32),
                   jax.ShapeDtypeStruct((8, META_LANES), jnp.int32)],
        scratch_shapes=[pltpu.VMEM((BUCKET_ROWS, LANES), F32), pltpu.VMEM((BUCKET_ROWS, LANES), F32)],
        compiler_params=_params(("arbitrary",)),
        name="plan_pos",
    )(info, sel, cnt, ltri, utri)


def _dispatch_kernel(pos_ref, fill_ref, h1_ref, gain2_ref, sh2_ref, sc2_ref, info_ref, xs_ref,
                     rowbuf, zbuf, sem, zsem, *, tm, n_steps, n_tiles):
    step = pl.program_id(0) * pl.num_programs(1) + pl.program_id(1)
    slot = step % 2

    @pl.when(step == 0)
    def _():
        zbuf[...] = jnp.zeros_like(zbuf)
        fill = lambda j: pltpu.make_async_copy(zbuf, xs_ref.at[pl.ds(j * SORT_TILE, SORT_TILE), 0], zsem)
        for j in range(n_tiles):
            pl.when(fill_ref[0, j] == 1)(lambda j=j: fill(j).start())
        for j in range(n_tiles):
            pl.when(fill_ref[0, j] == 1)(lambda j=j: fill(j).wait())

    def wait(s):
        pltpu.make_async_copy(rowbuf.at[s], xs_ref.at[pl.ds(0, tm), 0], sem.at[s]).wait()

    def send(s):
        @pl.when(step >= 2)
        def _():
            wait(s)

        rowbuf[s, :, 0:D_MODEL] = _modulate(h1_ref[0], gain2_ref[...], sh2_ref[...], sc2_ref[...])
        rowbuf[s, :, D_MODEL:] = info_ref[0]
        for r in range(tm):
            pltpu.make_async_copy(rowbuf.at[s, pl.ds(r, 1)], xs_ref.at[pos_ref[0, 0, r]],
                                  sem.at[s]).start(priority=r % 2)

    for s in range(2):
        pl.when(slot == s)(functools.partial(send, s))

    @pl.when(step == n_steps - 1)
    def _():
        wait(slot)
        if n_steps > 1:
            wait(1 - slot)


def _gmoe_kernel(ea_ref, eb_ref, nu_ref, xs_ref, wga_ref, wua_ref, wda_ref, wgb_ref, wub_ref, wdb_ref, ys_ref,
                 xbuf, ybuf, zbuf, sem_in, sem_out, zsem):
    del ea_ref, eb_ref
    j = pl.program_id(0)
    n_used = nu_ref[0]
    slot = j % 2
    tile = lambda ref, t: ref.at[pl.ds(t * SORT_TILE, SORT_TILE), 0]
    in_copy = lambda t, s: pltpu.make_async_copy(tile(xs_ref, t), xbuf.at[s], sem_in.at[s])
    out_copy = lambda t, s: pltpu.make_async_copy(ybuf.at[s], tile(ys_ref, t), sem_out.at[s])

    @pl.when(j == 0)
    def _():
        in_copy(0, 0).start()

    @pl.when(j + 1 < n_used)
    def _():
        in_copy(j + 1, 1 - slot).start()

    @pl.when(j < n_used)
    def _():
        in_copy(j, slot).wait()

        @pl.when(j >= 2)
        def _():
            out_copy(j - 2, slot).wait()

        x = xbuf[slot, :, 0:D_MODEL].astype(BF16)

        def expert(wg_ref, wu_ref, wd_ref, w):
            gt = jnp.dot(x, wg_ref[0].astype(BF16), preferred_element_type=F32)
            up = jnp.dot(x, wu_ref[0].astype(BF16), preferred_element_type=F32)
            h = (gt * jax.nn.sigmoid(gt)) * up * w
            return jnp.dot(h.astype(BF16), wd_ref[0].astype(BF16), preferred_element_type=F32)

        w_lo = xbuf[slot, :, D_MODEL + INFO_W_LO:D_MODEL + INFO_W_LO + 1]
        w_hi = xbuf[slot, :, D_MODEL + INFO_W_HI:D_MODEL + INFO_W_HI + 1]
        ybuf[slot] = expert(wga_ref, wua_ref, wda_ref, w_lo) + expert(wgb_ref, wub_ref, wdb_ref, w_hi)
        out_copy(j, slot).start()

        @pl.when(j == n_used - 1)
        def _():
            out_copy(j, slot).wait()

            @pl.when(j >= 1)
            def _():
                out_copy(j - 1, 1 - slot).wait()

    @pl.when(j >= n_used)
    def _():
        @pl.when(j == n_used)
        def _():
            zbuf[...] = jnp.zeros_like(zbuf)

        fill = pltpu.make_async_copy(zbuf, tile(ys_ref, j), zsem)
        fill.start()
        fill.wait()


def _fetch_sorted_rows(ys_ref, pos_ref, posn_ref, ybuf, sem, tm, n_steps):
    step = pl.program_id(0) * pl.num_programs(1) + pl.program_id(1)
    slot = step % 2

    def issue(p_ref, s):
        for r in range(tm):
            pltpu.make_async_copy(ys_ref.at[p_ref[0, 0, r]], ybuf.at[s, pl.ds(r, 1)],
                                  sem.at[s]).start(priority=r % 2)

    pl.when(step == 0)(functools.partial(issue, pos_ref, 0))
    for s in range(2):
        pl.when((step + 1 < n_steps) & (slot == s))(functools.partial(issue, posn_ref, 1 - s))
    pltpu.make_async_copy(ys_ref.at[pl.ds(0, tm), 0], ybuf.at[slot], sem.at[slot]).wait()
    return ybuf[slot]


def _combine_kernel(pos_ref, posn_ref, h1_ref, g2_ref, ys_ref, o_ref, ybuf, sem, *, tm, n_steps):
    rows = _fetch_sorted_rows(ys_ref, pos_ref, posn_ref, ybuf, sem, tm, n_steps)
    o_ref[0] = h1_ref[0] + g2_ref[...] * rows


def _experts_sorted(h1, info, cnt, mods, layer, gain2, w_gate, w_up, w_down, tm):
    bsz, seq, d = h1.shape
    nt = seq // tm
    n_steps = bsz * nt
    n = bsz * seq
    n_sorted = n + N_BUCKETS * SORT_TILE
    n_tiles = n_sorted // SORT_TILE
    assert n % SORT_TILE == 0 and n_tiles <= META_LANES
    pos, meta = _sort_plan(info.reshape(n, ROUTER_LANES), cnt, tm)
    tile = lambda w: pl.BlockSpec((1, tm, w), lambda b, i: (b, i, 0))
    any_spec = pl.BlockSpec(memory_space=pl.ANY)
    by_batch = lambda b, i: b

    xs = pl.pallas_call(
        functools.partial(_dispatch_kernel, tm=tm, n_steps=n_steps, n_tiles=n_tiles),
        grid=(bsz, nt),
        in_specs=[_pos_spec(tm, nt, n_steps, 0),
                  pl.BlockSpec((1, META_LANES), lambda b, i: (0, 0), memory_space=pltpu.SMEM),
                  tile(d), _full((1, d)), _mod_spec(layer, by_batch, 3),
                  _mod_spec(layer, by_batch, 4), tile(ROUTER_LANES)],
        out_specs=any_spec,
        out_shape=jax.ShapeDtypeStruct((n_sorted, 1, ROW_WIDTH), F32),
        scratch_shapes=[pltpu.VMEM((2, tm, ROW_WIDTH), F32), pltpu.VMEM((SORT_TILE, ROW_WIDTH), F32),
                        pltpu.SemaphoreType.DMA((2,)), pltpu.SemaphoreType.DMA(())],
        compiler_params=_params(("arbitrary", "arbitrary")),
        name=f"dispatch{layer}",
    )(pos, meta[META_FILL:META_FILL + 1], h1, gain2.reshape(1, d), mods, mods, info)

    used = lambda j, nu: jnp.minimum(j, nu[0] - 1)
    w_spec = lambda which, shape: pl.BlockSpec(
        (None, 1) + shape, lambda j, ea, eb, nu: (layer, (ea, eb)[which][used(j, nu)], 0, 0))
    w_in, w_out = (d, D_EXPERT), (D_EXPERT, d)
    ys = pl.pallas_call(
        _gmoe_kernel,
        grid_spec=pltpu.PrefetchScalarGridSpec(
            num_scalar_prefetch=3,
            grid=(n_tiles,),
            in_specs=[any_spec, w_spec(0, w_in), w_spec(0, w_in), w_spec(0, w_out),
                      w_spec(1, w_in), w_spec(1, w_in), w_spec(1, w_out)],
            out_specs=any_spec,
            scratch_shapes=[pltpu.VMEM((2, SORT_TILE, ROW_WIDTH), F32), pltpu.VMEM((2, SORT_TILE, d), F32),
                            pltpu.VMEM((SORT_TILE, d), F32), pltpu.SemaphoreType.DMA((2,)),
                            pltpu.SemaphoreType.DMA((2,)), pltpu.SemaphoreType.DMA(())]),
        out_shape=jax.ShapeDtypeStruct((n_sorted, 1, d), F32),
        compiler_params=_params(("arbitrary",)),
        name=f"experts{layer}",
    )(meta[META_EXPERT_LO], meta[META_EXPERT_HI], meta[META_N_USED, :1], xs,
      w_gate, w_up, w_down, w_gate, w_up, w_down)
    return ys, pos


def _combine(h1, ys, pos, mods, layer, tm):
    bsz, seq, d = h1.shape
    nt = seq // tm
    n_steps = bsz * nt
    tile = lambda w: pl.BlockSpec((1, tm, w), lambda b, i: (b, i, 0))
    return pl.pallas_call(
        functools.partial(_combine_kernel, tm=tm, n_steps=n_steps),
        grid=(bsz, nt),
        in_specs=[_pos_spec(tm, nt, n_steps, 0), _pos_spec(tm, nt, n_steps, 1), tile(d),
                  _mod_spec(layer, lambda b, i: b, 5), pl.BlockSpec(memory_space=pl.ANY)],
        out_specs=tile(d),
        out_shape=jax.ShapeDtypeStruct((bsz, seq, d), F32),
        scratch_shapes=[pltpu.VMEM((2, tm, d), F32), pltpu.SemaphoreType.DMA((2,))],
        compiler_params=_params(("arbitrary", "arbitrary")),
        name=f"combine{layer}",
    )(pos, pos, h1, mods, ys)


def kernel(x, c, ctx, c_ctx, mod_w, mod_b, norm1_g, norm2_g, even_w_in, q_gain, k_gain, pool_w, pool_scale,
           even_w_out, odd_w_in, sg_gain, sg_w, sg_b, conv_w, odd_w_out, router_g_w, router_g_b,
           router_e_w, router_e_b, w_gate, w_up, w_down):
    bsz, seq, d = x.shape
    tm = min(512, seq)
    cond = jnp.zeros((MOD_ROWS, d), F32).at[:bsz].set(c).at[bsz].set(c_ctx)
    mods = _adaln(cond, mod_w, mod_b).reshape(mod_w.shape[0], MOD_ROWS, 6, 1, d)

    q, k, v, p = _inproj0(x, mods, norm1_g[0], even_w_in[0], q_gain[0], k_gain[0], tm)
    kc, vc = _inproj0_ctx(ctx, mods, bsz, norm1_g[0], even_w_in[0][:, ATTN_WIDTH:ATTN_WIDTH + 2 * KV_WIDTH],
                          k_gain[0])
    o = _attention(q, k, v, kc, vc, tq=min(256, seq), tk=min(2048, seq))
    router0 = _router_operands(router_g_w[0], router_g_b[0], router_e_w[0], router_e_b[0])
    h1, info, cnt = _out0(o, p, x, mods, norm2_g[0], pool_w[0], pool_scale[0], even_w_out[0], router0, tm)
    ys, pos = _experts_sorted(h1, info, cnt, mods, 0, norm2_g[0], w_gate, w_up, w_down, tm)

    h, yc, z, bg = _inproj1(h1, ys, pos, mods, norm1_g[1], odd_w_in[0], sg_gain[0], sg_w[0], sg_b[0], tm)
    router1 = _router_operands(router_g_w[1], router_g_b[1], router_e_w[1], router_e_b[1])
    h1, info, cnt = _out1(yc, z, bg, h, mods, norm2_g[1], conv_w[0], odd_w_out[0], router1, tm)
    ys, pos = _experts_sorted(h1, info, cnt, mods, 1, norm2_g[1], w_gate, w_up, w_down, tm)
    return _combine(h1, ys, pos, mods, 1, tm)
```

```python
import functools

import jax
import jax.numpy as jnp
from jax import lax
from jax.experimental import pallas as pl
from jax.experimental.pallas import tpu as pltpu

F32 = jnp.float32
BF16 = jnp.bfloat16

D_MODEL = 1024
GRID_W = 64
EPS = 1e-6
N_Q_HEADS = 8
N_KV_HEADS = 2
HEAD_DIM = 64
Q_PER_KV = N_Q_HEADS // N_KV_HEADS
ATTN_WIDTH = N_Q_HEADS * HEAD_DIM
KV_WIDTH = N_KV_HEADS * HEAD_DIM
ROPE_THETA = 10000.0
POOL_WINDOWS = (2, 4, 8, 16)
POOL_GROUP = 128
POOL_WIDTH = POOL_GROUP * len(POOL_WINDOWS)
SG_GROUPS = 4
SG_CHUNK = 128
SG_WIDTH = 512
CONV_WIDTH = 512
EVEN_IN = ATTN_WIDTH + 2 * KV_WIDTH + POOL_WIDTH
ODD_IN = 2 * SG_WIDTH + 3 * CONV_WIDTH
N_GROUPS = 4
EXPERTS_PER_GROUP = 4
N_EXPERTS = 16
D_EXPERT = 256

Q_SCALE = HEAD_DIM ** -0.5 * 1.4426950408889634
SAFE_SOFTMAX_SHIFT = 60.0
LANES = 128
HALO = 8
ROUTER_LANES = 128
MOD_ROWS = 16
VMEM_LIMIT = 48 * 1024 * 1024

PAIRS_PER_GROUP = 6
N_BUCKETS = N_GROUPS * PAIRS_PER_GROUP
BUCKET_ROWS = 32
SORT_TILE = 512
META_LANES = 256
META_EXPERT_LO, META_EXPERT_HI, META_N_USED, META_FILL = 0, 1, 2, 3
ROW_WIDTH = D_MODEL + ROUTER_LANES
INFO_BUCKET, INFO_W_LO, INFO_W_HI = 0, 1, 2


def _params(sem):
    return pltpu.CompilerParams(dimension_semantics=sem, vmem_limit_bytes=VMEM_LIMIT)


def _modulate(x, gain, shift, scale):
    ms = jnp.mean(x * x, axis=-1, keepdims=True)
    return (x * lax.rsqrt(ms + EPS) * gain) * (1.0 + scale) + shift


def _mod_spec(layer, row_fn, which):
    return pl.BlockSpec((None, None, None, 1, D_MODEL),
                        lambda *idx: (layer, row_fn(*idx), which, 0, 0))


def _full(shape):
    return pl.BlockSpec(shape, lambda *idx: (0,) * len(shape))


def _adaln_kernel(c_ref, w_ref, b_ref, o_ref):
    c = c_ref[...]
    s = c * jax.nn.sigmoid(c)
    o_ref[0] = jnp.dot(s, w_ref[0], precision=lax.Precision.HIGHEST,
                       preferred_element_type=F32) + b_ref[0]


def _adaln(cond, mod_w, mod_b):
    depth, d, n = mod_w.shape
    tn = 1024
    return pl.pallas_call(
        _adaln_kernel,
        grid=(depth, n // tn),
        in_specs=[_full((MOD_ROWS, d)),
                  pl.BlockSpec((1, d, tn), lambda l, j: (l, 0, j)),
                  pl.BlockSpec((1, 1, tn), lambda l, j: (l, 0, j))],
        out_specs=pl.BlockSpec((1, MOD_ROWS, tn), lambda l, j: (l, 0, j)),
        out_shape=jax.ShapeDtypeStruct((depth, MOD_ROWS, n), F32),
        compiler_params=_params(("arbitrary", "arbitrary")),
        name="adaln",
    )(cond, mod_w, mod_b.reshape(depth, 1, n))


def _head_norm_rope(z, gain, ones_bd, cos, sin, first_half):
    sq = z * z
    hi = sq.astype(BF16)
    lo = (sq - hi.astype(F32)).astype(BF16)
    ms = (jnp.dot(hi, ones_bd, preferred_element_type=F32)
          + jnp.dot(lo, ones_bd, preferred_element_type=F32))
    zn = z * lax.rsqrt(ms + EPS) * gain
    partner = jnp.where(first_half, pltpu.roll(zn, LANES - 16, 1), pltpu.roll(zn, 16, 1))
    return zn * cos + partner * sin


def _inproj0_kernel(x_ref, gain_ref, sh_ref, sc_ref, w_ref, cos_ref, sin_ref, qg_ref, kg_ref, ones_ref,
                    q_ref, k_ref, v_ref, p_ref):
    a = _modulate(x_ref[0], gain_ref[...], sh_ref[...], sc_ref[...])
    y = jnp.dot(a.astype(BF16), w_ref[...], preferred_element_type=F32)
    cos, sin, ones_bd = cos_ref[...], sin_ref[...], ones_ref[...]
    lane = lax.broadcasted_iota(jnp.int32, cos.shape, 1)
    first_half = (lane % 32) < 16
    for s in range(ATTN_WIDTH // LANES):
        r = _head_norm_rope(y[:, s * LANES:(s + 1) * LANES], qg_ref[...], ones_bd, cos, sin, first_half)
        r = (r * Q_SCALE).astype(BF16)
        q_ref[0, 2 * s] = r[:, :HEAD_DIM]
        q_ref[0, 2 * s + 1] = r[:, HEAD_DIM:]
    kr = _head_norm_rope(y[:, ATTN_WIDTH:ATTN_WIDTH + KV_WIDTH], kg_ref[...], ones_bd, cos, sin,
                         first_half).astype(BF16)
    k_ref[0, 0] = kr[:, :HEAD_DIM]
    k_ref[0, 1] = kr[:, HEAD_DIM:]
    vv = y[:, ATTN_WIDTH + KV_WIDTH:ATTN_WIDTH + 2 * KV_WIDTH].astype(BF16)
    v_ref[0, 0] = vv[:, :HEAD_DIM]
    v_ref[0, 1] = vv[:, HEAD_DIM:]
    p_ref[0] = y[:, ATTN_WIDTH + 2 * KV_WIDTH:]


def _inproj0_ctx_kernel(x_ref, gain_ref, sh_ref, sc_ref, w_ref, kg_ref, ones_ref, k_ref, v_ref):
    a = _modulate(x_ref[0], gain_ref[...], sh_ref[...], sc_ref[...])
    y = jnp.dot(a.astype(BF16), w_ref[...], preferred_element_type=F32)
    z = y[:, :KV_WIDTH]
    sq = z * z
    hi = sq.astype(BF16)
    lo = (sq - hi.astype(F32)).astype(BF16)
    ms = (jnp.dot(hi, ones_ref[...], preferred_element_type=F32)
          + jnp.dot(lo, ones_ref[...], preferred_element_type=F32))
    kr = (z * lax.rsqrt(ms + EPS) * kg_ref[...]).astype(BF16)
    k_ref[0, 0] = kr[:, :HEAD_DIM]
    k_ref[0, 1] = kr[:, HEAD_DIM:]
    vv = y[:, KV_WIDTH:].astype(BF16)
    v_ref[0, 0] = vv[:, :HEAD_DIM]
    v_ref[0, 1] = vv[:, HEAD_DIM:]


def _rope_tables(seq):
    t = jnp.arange(seq)
    row = (t // GRID_W).astype(F32)
    col = (t % GRID_W).astype(F32)
    half = HEAD_DIM // 2
    inv = ROPE_THETA ** (-jnp.arange(0, half, 2, dtype=F32) / half)
    ar, ac = row[:, None] * inv, col[:, None] * inv
    cos = jnp.concatenate([jnp.cos(ar), jnp.cos(ar), jnp.cos(ac), jnp.cos(ac)], axis=-1)
    sin = jnp.concatenate([-jnp.sin(ar), jnp.sin(ar), -jnp.sin(ac), jnp.sin(ac)], axis=-1)
    return jnp.tile(cos, (1, LANES // HEAD_DIM)), jnp.tile(sin, (1, LANES // HEAD_DIM))


def _head_mean_matrix():
    r = jnp.arange(LANES)
    same = (r[:, None] // HEAD_DIM) == (r[None, :] // HEAD_DIM)
    return jnp.where(same, 1.0 / HEAD_DIM, 0.0).astype(BF16)


def _inproj0(x, mods, gain, w_in, q_gain, k_gain, tm):
    bsz, seq, d = x.shape
    cos, sin = _rope_tables(seq)
    qg = jnp.tile(q_gain, LANES // HEAD_DIM).reshape(1, LANES)
    kg = jnp.tile(k_gain, LANES // HEAD_DIM).reshape(1, LANES)
    head = lambda n: pl.BlockSpec((1, n, tm, HEAD_DIM), lambda b, i: (b, 0, i, 0))
    return pl.pallas_call(
        _inproj0_kernel,
        grid=(bsz, seq // tm),
        in_specs=[pl.BlockSpec((1, tm, d), lambda b, i: (b, i, 0)),
                  _full((1, d)),
                  _mod_spec(0, lambda b, i: b, 0),
                  _mod_spec(0, lambda b, i: b, 1),
                  _full((d, EVEN_IN)),
                  pl.BlockSpec((tm, LANES), lambda b, i: (i, 0)),
                  pl.BlockSpec((tm, LANES), lambda b, i: (i, 0)),
                  _full((1, LANES)), _full((1, LANES)), _full((LANES, LANES))],
        out_specs=[head(N_Q_HEADS), head(N_KV_HEADS), head(N_KV_HEADS),
                   pl.BlockSpec((1, tm, POOL_WIDTH), lambda b, i: (b, i, 0))],
        out_shape=[jax.ShapeDtypeStruct((bsz, N_Q_HEADS, seq, HEAD_DIM), BF16),
                   jax.ShapeDtypeStruct((bsz, N_KV_HEADS, seq, HEAD_DIM), BF16),
                   jax.ShapeDtypeStruct((bsz, N_KV_HEADS, seq, HEAD_DIM), BF16),
                   jax.ShapeDtypeStruct((bsz, seq, POOL_WIDTH), F32)],
        compiler_params=_params(("parallel", "parallel")),
        name="inproj0",
    )(x, gain.reshape(1, d), mods, mods, w_in.astype(BF16), cos, sin, qg, kg, _head_mean_matrix())


def _inproj0_ctx(ctx, mods, ctx_row, gain, w_kv, k_gain):
    bsz, n_ctx, d = ctx.shape
    kg = jnp.tile(k_gain, LANES // HEAD_DIM).reshape(1, LANES)
    head = pl.BlockSpec((1, N_KV_HEADS, n_ctx, HEAD_DIM), lambda b: (b, 0, 0, 0))
    return pl.pallas_call(
        _inproj0_ctx_kernel,
        grid=(bsz,),
        in_specs=[pl.BlockSpec((1, n_ctx, d), lambda b: (b, 0, 0)),
                  _full((1, d)),
                  _mod_spec(0, lambda b: ctx_row, 0),
                  _mod_spec(0, lambda b: ctx_row, 1),
                  _full((d, 2 * KV_WIDTH)),
                  _full((1, LANES)), _full((LANES, LANES))],
        out_specs=[head, head],
        out_shape=[jax.ShapeDtypeStruct((bsz, N_KV_HEADS, n_ctx, HEAD_DIM), BF16)] * 2,
        compiler_params=_params(("parallel",)),
        name="inproj0_ctx",
    )(ctx, gain.reshape(1, d), mods, mods, w_kv.astype(BF16), kg, _head_mean_matrix())


def _attn_kernel(q_ref, kl_ref, vl_ref, kc_ref, vc_ref, o_ref, ksq_sc, *, tq, tk):
    rows = Q_PER_KV * tq
    q = q_ref[0].reshape(rows, HEAD_DIM)
    seq = kl_ref.shape[2]
    chunks = [(kc_ref, vc_ref, 0, kc_ref.shape[2])]
    chunks += [(kl_ref, vl_ref, c * tk, tk) for c in range(seq // tk)]
    scores = lambda k: lax.dot_general(q, k, (((1,), (1,)), ((), ())), preferred_element_type=F32)

    @pl.when(pl.program_id(2) == 0)
    def _():
        def largest_sq_norm(k_ref):
            kf = k_ref[0, 0].astype(F32)
            return jnp.max(jnp.sum(kf * kf, axis=1, keepdims=True), axis=0, keepdims=True)

        ksq_sc[...] = jnp.broadcast_to(jnp.maximum(largest_sq_norm(kl_ref), largest_sq_norm(kc_ref)),
                                       ksq_sc.shape)

    qf = q.astype(F32)
    bound = jnp.sqrt(jnp.sum(qf * qf, axis=1, keepdims=True) * ksq_sc[0:1, 0:1])
    safe = jnp.max(bound) <= SAFE_SOFTMAX_SHIFT

    def finish(acc, l):
        o = acc / l
        o_ref[0] = jnp.concatenate([o[h * tq:(h + 1) * tq] for h in range(Q_PER_KV)], axis=1).astype(BF16)

    @pl.when(safe)
    def _():
        l = jnp.zeros((rows, 1), F32)
        acc = jnp.zeros((rows, HEAD_DIM), F32)
        for k_ref, v_ref, start, size in chunks:
            p = jnp.exp2(scores(k_ref[0, 0, start:start + size, :]) - bound)
            l = l + jnp.sum(p, axis=1, keepdims=True)
            acc = acc + jnp.dot(p.astype(BF16), v_ref[0, 0, start:start + size, :], preferred_element_type=F32)
        finish(acc, l)

    @pl.when(jnp.logical_not(safe))
    def _():
        m = jnp.full((rows, 1), -jnp.inf, F32)
        l = jnp.zeros((rows, 1), F32)
        acc = jnp.zeros((rows, HEAD_DIM), F32)
        for k_ref, v_ref, start, size in chunks:
            s = scores(k_ref[0, 0, start:start + size, :])
            m_new = jnp.maximum(m, jnp.max(s, axis=1, keepdims=True))
            alpha = jnp.exp2(m - m_new)
            p = jnp.exp2(s - m_new)
            l = alpha * l + jnp.sum(p, axis=1, keepdims=True)
            acc = alpha * acc + jnp.dot(p.astype(BF16), v_ref[0, 0, start:start + size, :],
                                        preferred_element_type=F32)
            m = m_new
        finish(acc, l)


def _attention(q, k, v, kc, vc, tq, tk):
    bsz, _, seq, _ = q.shape
    n_ctx = kc.shape[2]
    kv_spec = lambda n: pl.BlockSpec((1, 1, n, HEAD_DIM), lambda b, g, i: (b, g, 0, 0))
    return pl.pallas_call(
        functools.partial(_attn_kernel, tq=tq, tk=tk),
        grid=(bsz, N_KV_HEADS, seq // tq),
        in_specs=[pl.BlockSpec((1, Q_PER_KV, tq, HEAD_DIM), lambda b, g, i: (b, g, i, 0)),
                  kv_spec(seq), kv_spec(seq), kv_spec(n_ctx), kv_spec(n_ctx)],
        out_specs=pl.BlockSpec((1, tq, Q_PER_KV * HEAD_DIM), lambda b, g, i: (b, i, g)),
        out_shape=jax.ShapeDtypeStruct((bsz, seq, ATTN_WIDTH), BF16),
        scratch_shapes=[pltpu.VMEM((8, LANES), F32)],
        compiler_params=_params(("arbitrary", "arbitrary", "arbitrary")),
        name="attention",
    )(q, k, v, kc, vc)


def _route(logits):
    lane = lax.broadcasted_iota(jnp.int32, logits.shape, 1).astype(F32)
    neg = -jnp.inf
    big = float(ROUTER_LANES)
    first_index = lambda mask: jnp.min(jnp.where(mask, lane, big), axis=1, keepdims=True)
    is_g = lane < N_GROUPS
    gm = jnp.max(jnp.where(is_g, logits, neg), axis=1, keepdims=True)
    gidx = first_index(is_g & (logits == gm))
    gden = jnp.sum(jnp.where(is_g, jnp.exp(logits - gm), 0.0), axis=1, keepdims=True)
    g_p = 1.0 / gden
    first = N_GROUPS + EXPERTS_PER_GROUP * gidx
    sel = (lane >= first) & (lane < first + EXPERTS_PER_GROUP)
    e1 = jnp.max(jnp.where(sel, logits, neg), axis=1, keepdims=True)
    i1 = first_index(sel & (logits == e1))
    rest = sel & (lane != i1)
    e2 = jnp.max(jnp.where(rest, logits, neg), axis=1, keepdims=True)
    i2 = first_index(rest & (logits == e2))
    p2 = jnp.exp(e2 - e1)
    w1 = g_p * (1.0 / (1.0 + p2))
    w2 = g_p * (p2 / (1.0 + p2))
    lo = jnp.minimum(i1, i2) - first
    hi = jnp.maximum(i1, i2) - first
    pair = jnp.where(lo == 0, hi - 1, jnp.where(lo == 1, hi + 1, PAIRS_PER_GROUP - 1.0))
    bucket = PAIRS_PER_GROUP * gidx + pair
    w_lo = jnp.where(i1 < i2, w1, w2)
    w_hi = jnp.where(i1 < i2, w2, w1)
    return jnp.where(lane == INFO_BUCKET, bucket,
                     jnp.where(lane == INFO_W_LO, w_lo, jnp.where(lane == INFO_W_HI, w_hi, 0.0)))


def _bucket_onehot(info, sel):
    brow = lax.dot_general(sel, info.astype(BF16), (((1,), (1,)), ((), ())),
                           preferred_element_type=F32)[0:1]
    bid = lax.broadcasted_iota(jnp.int32, (BUCKET_ROWS, info.shape[0]), 0)
    return (bid == brow.astype(jnp.int32)).astype(F32)


def _bucket_selector():
    return jnp.zeros((8, ROUTER_LANES), F32).at[0, INFO_BUCKET].set(1.0).astype(BF16)


def _tail(y, x_res, gate1, gain2, shift2, scale2, rw_both, rbias, sel, h1_ref, info_ref, cnt_ref):
    h1 = x_res + gate1 * y
    h1_ref[0] = h1
    t = _modulate(h1, gain2, shift2, scale2)
    t_hi = t.astype(BF16)
    t_lo = (t - t_hi.astype(F32)).astype(BF16)
    both = jnp.dot(t_hi, rw_both, preferred_element_type=F32)
    logits = (both[:, :ROUTER_LANES] + both[:, ROUTER_LANES:]
              + jnp.dot(t_lo, rw_both[:, :ROUTER_LANES], preferred_element_type=F32)) + rbias
    info = _route(logits)
    info_ref[0] = info

    @pl.when((pl.program_id(0) == 0) & (pl.program_id(1) == 0))
    def _():
        cnt_ref[...] = jnp.zeros_like(cnt_ref)

    cnt_ref[...] += jnp.sum(_bucket_onehot(info, sel), axis=1, keepdims=True)


def _router_operands(rg_w, rg_b, re_w, re_b):
    d = rg_w.shape[0]
    w = jnp.concatenate([rg_w, re_w, jnp.zeros((d, ROUTER_LANES - N_GROUPS - N_EXPERTS), F32)], axis=1)
    b = jnp.concatenate([rg_b, re_b, jnp.zeros((ROUTER_LANES - N_GROUPS - N_EXPERTS,), F32)])
    w_hi = w.astype(BF16)
    w_lo = (w - w_hi.astype(F32)).astype(BF16)
    return jnp.concatenate([w_hi, w_lo], axis=1), b.reshape(1, ROUTER_LANES)


def _fill_halo(buf, main_ref, prev_ref, next_ref, tm, i, n_tiles):
    buf[HALO:HALO + tm] = main_ref[0]
    buf[0:HALO] = jnp.where(i > 0, prev_ref[0], 0.0)
    buf[HALO + tm:2 * HALO + tm] = jnp.where(i < n_tiles - 1, next_ref[0], 0.0)


def _halo_specs(tm, seq, width):
    per = tm // HALO
    last = seq // HALO - 1
    return [pl.BlockSpec((1, tm, width), lambda b, i: (b, i, 0)),
            pl.BlockSpec((1, HALO, width), lambda b, i: (b, jnp.maximum(i * per - 1, 0), 0)),
            pl.BlockSpec((1, HALO, width), lambda b, i: (b, jnp.minimum((i + 1) * per, last), 0))]


def _out0_kernel(o_ref, p_ref, pprev_ref, pnext_ref, x_ref, g1_ref, gain2_ref, sh2_ref, sc2_ref,
                 poolw_ref, pscale_ref, wout_ref, rw_ref, rb_ref, sel_ref,
                 h1_ref, info_ref, cnt_ref, pbuf, *, tm, seq):
    i = pl.program_id(1)
    _fill_halo(pbuf, p_ref, pprev_ref, pnext_ref, tm, i, seq // tm)
    pos = i * tm + lax.broadcasted_iota(jnp.int32, (tm, 1), 0)
    pooled = []
    for g, w in enumerate(POOL_WINDOWS):
        sl = slice(g * POOL_GROUP, (g + 1) * POOL_GROUP)
        acc = pbuf[HALO - w // 2:HALO - w // 2 + tm, sl]
        for j in range(1 - w // 2, w - w // 2):
            acc = acc + pbuf[HALO + j:HALO + j + tm, sl]
        lo = jnp.clip(pos - w // 2, 0, seq)
        hi = jnp.clip(pos + w - w // 2, 0, seq)
        mean = acc * (1.0 / (hi - lo).astype(F32))
        dlt = (mean - pbuf[HALO:HALO + tm, sl]).astype(BF16)
        pooled.append((jnp.dot(dlt, poolw_ref[g], preferred_element_type=F32) * pscale_ref[:, sl]).astype(BF16))
    mixed = jnp.concatenate([o_ref[0]] + pooled, axis=1)
    y = jnp.dot(mixed, wout_ref[...], preferred_element_type=F32)
    _tail(y, x_ref[0], g1_ref[...], gain2_ref[...], sh2_ref[...], sc2_ref[...],
          rw_ref[...], rb_ref[...], sel_ref[...], h1_ref, info_ref, cnt_ref)


def _tail_specs(layer, d):
    by_batch = lambda b, i: b
    ins = [_mod_spec(layer, by_batch, 2), _full((1, d)), _mod_spec(layer, by_batch, 3),
           _mod_spec(layer, by_batch, 4)]
    return ins


def _tail_outs(bsz, seq, d, tm):
    specs = [pl.BlockSpec((1, tm, d), lambda b, i: (b, i, 0)),
             pl.BlockSpec((1, tm, ROUTER_LANES), lambda b, i: (b, i, 0)),
             _full((BUCKET_ROWS, LANES))]
    shapes = [jax.ShapeDtypeStruct((bsz, seq, d), F32),
              jax.ShapeDtypeStruct((bsz, seq, ROUTER_LANES), F32),
              jax.ShapeDtypeStruct((BUCKET_ROWS, LANES), F32)]
    return specs, shapes


def _out0(o, p, x, mods, gain2, pool_w, pool_scale, w_out, router, tm):
    bsz, seq, d = x.shape
    rw_both, rb = router
    out_specs, out_shapes = _tail_outs(bsz, seq, d, tm)
    return pl.pallas_call(
        functools.partial(_out0_kernel, tm=tm, seq=seq),
        grid=(bsz, seq // tm),
        in_specs=[pl.BlockSpec((1, tm, ATTN_WIDTH), lambda b, i: (b, i, 0))]
        + _halo_specs(tm, seq, POOL_WIDTH)
        + [pl.BlockSpec((1, tm, d), lambda b, i: (b, i, 0))]
        + _tail_specs(0, d)
        + [_full(pool_w.shape), _full((1, POOL_WIDTH)), _full(w_out.shape),
           _full(rw_both.shape), _full(rb.shape), _full((8, ROUTER_LANES))],
        out_specs=out_specs,
        out_shape=out_shapes,
        scratch_shapes=[pltpu.VMEM((tm + 2 * HALO, POOL_WIDTH), F32)],
        compiler_params=_params(("arbitrary", "arbitrary")),
        name="out0",
    )(o, p, p, p, x, mods, gain2.reshape(1, d), mods, mods,
      pool_w.astype(BF16), pool_scale.reshape(1, POOL_WIDTH), w_out.astype(BF16), rw_both, rb,
      _bucket_selector())


def _inproj1_kernel(pos_ref, posn_ref, h1_ref, g2_ref, gain_ref, sh_ref, sc_ref, w_ref, sgg_ref, sgw_ref, sgb_ref,
                    ys_ref, h_ref, yc_ref, z_ref, bg_ref, ybuf, sem, *, tm, n_steps):
    x = h1_ref[0] + g2_ref[...] * _fetch_sorted_rows(ys_ref, pos_ref, posn_ref, ybuf, sem, tm, n_steps)
    h_ref[0] = x
    a = _modulate(x, gain_ref[...], sh_ref[...], sc_ref[...])
    y = jnp.dot(a.astype(BF16), w_ref[...], preferred_element_type=F32)
    for g in range(SG_GROUPS):
        sl = slice(g * LANES, (g + 1) * LANES)
        u = y[:, sl]
        vg = y[:, SG_WIDTH + g * LANES:SG_WIDTH + (g + 1) * LANES]
        ms = jnp.mean(vg * vg, axis=-1, keepdims=True)
        vn = (vg * lax.rsqrt(ms + EPS) * sgg_ref[:, sl]).astype(BF16)
        for c in range(tm // SG_CHUNK):
            rows = slice(c * SG_CHUNK, (c + 1) * SG_CHUNK)
            s = jnp.dot(sgw_ref[g], vn[rows], preferred_element_type=F32) + sgb_ref[g]
            yc_ref[0, rows, sl] = (u[rows] * s).astype(BF16)
    hx = y[:, 2 * SG_WIDTH:2 * SG_WIDTH + CONV_WIDTH]
    bg_ref[0] = y[:, 2 * SG_WIDTH + CONV_WIDTH:2 * SG_WIDTH + 2 * CONV_WIDTH]
    cg = y[:, 2 * SG_WIDTH + 2 * CONV_WIDTH:]
    z_ref[0] = cg * hx


def _pos_spec(tm, nt, n_steps, ahead):
    return pl.BlockSpec((1, 1, tm), lambda b, i: (jnp.minimum(b * nt + i + ahead, n_steps - 1), 0, 0),
                        memory_space=pltpu.SMEM)


def _inproj1(h1, ys, pos, mods, gain, w_in, sg_gain, sg_w, sg_b, tm):
    bsz, seq, d = h1.shape
    nt = seq // tm
    n_steps = bsz * nt
    sgb = jnp.broadcast_to(sg_b[:, :, None], (SG_GROUPS, SG_CHUNK, LANES))
    by_batch = lambda b, i: b
    wide = lambda w, dt: (pl.BlockSpec((1, tm, w), lambda b, i: (b, i, 0)),
                          jax.ShapeDtypeStruct((bsz, seq, w), dt))
    outs = [wide(d, F32), wide(SG_WIDTH, BF16), wide(CONV_WIDTH, F32), wide(CONV_WIDTH, F32)]
    return pl.pallas_call(
        functools.partial(_inproj1_kernel, tm=tm, n_steps=n_steps),
        grid=(bsz, nt),
        in_specs=[_pos_spec(tm, nt, n_steps, 0), _pos_spec(tm, nt, n_steps, 1),
                  pl.BlockSpec((1, tm, d), lambda b, i: (b, i, 0)),
                  _mod_spec(0, by_batch, 5),
                  _full((1, d)),
                  _mod_spec(1, by_batch, 0),
                  _mod_spec(1, by_batch, 1),
                  _full((d, ODD_IN)),
                  _full((1, SG_WIDTH)), _full(sg_w.shape), _full(sgb.shape),
                  pl.BlockSpec(memory_space=pl.ANY)],
        out_specs=[s for s, _ in outs],
        out_shape=[s for _, s in outs],
        scratch_shapes=[pltpu.VMEM((2, tm, d), F32), pltpu.SemaphoreType.DMA((2,))],
        compiler_params=_params(("arbitrary", "arbitrary")),
        name="inproj1",
    )(pos, pos, h1, mods, gain.reshape(1, d), mods, mods, w_in.astype(BF16), sg_gain.reshape(1, SG_WIDTH),
      sg_w.astype(BF16), sgb, ys)


def _out1_kernel(yc_ref, z_ref, zprev_ref, znext_ref, bg_ref, x_ref, g1_ref, gain2_ref, sh2_ref, sc2_ref,
                 convw_ref, wout_ref, rw_ref, rb_ref, sel_ref,
                 h1_ref, info_ref, cnt_ref, zbuf, *, tm, seq):
    i = pl.program_id(1)
    _fill_halo(zbuf, z_ref, zprev_ref, znext_ref, tm, i, seq // tm)
    zc = (zbuf[HALO - 1:HALO - 1 + tm] * convw_ref[0:1, :]
          + zbuf[HALO:HALO + tm] * convw_ref[1:2, :]
          + zbuf[HALO + 1:HALO + 1 + tm] * convw_ref[2:3, :])
    yd = (bg_ref[0] * zc).astype(BF16)
    y = (jnp.dot(yc_ref[0], wout_ref[0:SG_WIDTH, :], preferred_element_type=F32)
         + jnp.dot(yd, wout_ref[SG_WIDTH:, :], preferred_element_type=F32))
    _tail(y, x_ref[0], g1_ref[...], gain2_ref[...], sh2_ref[...], sc2_ref[...],
          rw_ref[...], rb_ref[...], sel_ref[...], h1_ref, info_ref, cnt_ref)


def _out1(yc, z, bg, x, mods, gain2, conv_w, w_out, router, tm):
    bsz, seq, d = x.shape
    rw_both, rb = router
    out_specs, out_shapes = _tail_outs(bsz, seq, d, tm)
    wide = pl.BlockSpec((1, tm, CONV_WIDTH), lambda b, i: (b, i, 0))
    return pl.pallas_call(
        functools.partial(_out1_kernel, tm=tm, seq=seq),
        grid=(bsz, seq // tm),
        in_specs=[wide] + _halo_specs(tm, seq, CONV_WIDTH) + [wide]
        + [pl.BlockSpec((1, tm, d), lambda b, i: (b, i, 0))]
        + _tail_specs(1, d)
        + [_full((3, CONV_WIDTH)), _full(w_out.shape),
           _full(rw_both.shape), _full(rb.shape), _full((8, ROUTER_LANES))],
        out_specs=out_specs,
        out_shape=out_shapes,
        scratch_shapes=[pltpu.VMEM((tm + 2 * HALO, CONV_WIDTH), F32)],
        compiler_params=_params(("arbitrary", "arbitrary")),
        name="out1",
    )(yc, z, z, z, bg, x, mods, gain2.reshape(1, d), mods, mods,
      conv_w.reshape(3, CONV_WIDTH), w_out.astype(BF16), rw_both, rb, _bucket_selector())


def _plan_pos_kernel(info_ref, sel_ref, cnt_ref, ltri_ref, utri_ref, pos_ref, meta_ref, start_sc, run_sc):
    @pl.when(pl.program_id(0) == 0)
    def _():
        padded = jnp.ceil(cnt_ref[...] * (1.0 / SORT_TILE)) * SORT_TILE
        incl = jnp.dot(ltri_ref[...], padded, precision=lax.Precision.HIGHEST, preferred_element_type=F32)
        start_sc[...] = incl - padded
        run_sc[...] = jnp.zeros_like(run_sc)
        ends = jnp.broadcast_to(incl[:, 0:1], (BUCKET_ROWS, META_LANES))
        bid = lax.broadcasted_iota(jnp.int32, ends.shape, 0)
        tile = lax.broadcasted_iota(jnp.int32, (1, META_LANES), 1)

        def bucket_of(row0):
            done = jnp.where((bid < N_BUCKETS) & (ends <= row0), 1.0, 0.0)
            return jnp.minimum(jnp.sum(done, axis=0, keepdims=True), N_BUCKETS - 1.0).astype(jnp.int32)

        row0 = (tile * SORT_TILE).astype(F32)
        tb = bucket_of(row0)
        grp = ((tb >= PAIRS_PER_GROUP).astype(jnp.int32) + (tb >= 2 * PAIRS_PER_GROUP).astype(jnp.int32)
               + (tb >= 3 * PAIRS_PER_GROUP).astype(jnp.int32))
        pair = tb - PAIRS_PER_GROUP * grp
        lo = (pair >= 3).astype(jnp.int32) + (pair >= 5).astype(jnp.int32)
        hi = jnp.where(pair == 0, 1, jnp.where((pair == 1) | (pair == 3), 2, 3))
        n_used = (incl[N_BUCKETS - 1:N_BUCKETS, 0:1] * (1.0 / SORT_TILE)).astype(jnp.int32)
        fill = (tile >= n_used - 1) | (tb != bucket_of(row0 + SORT_TILE))
        meta_ref[...] = jnp.zeros_like(meta_ref)
        meta_ref[META_EXPERT_LO:META_EXPERT_LO + 1, :] = EXPERTS_PER_GROUP * grp + lo
        meta_ref[META_EXPERT_HI:META_EXPERT_HI + 1, :] = EXPERTS_PER_GROUP * grp + hi
        meta_ref[META_N_USED:META_N_USED + 1, :] = jnp.broadcast_to(n_used, (1, META_LANES))
        meta_ref[META_FILL:META_FILL + 1, :] = fill.astype(jnp.int32)

    oh = _bucket_onehot(info_ref[...], sel_ref[...])
    before = jnp.dot(oh.astype(BF16), utri_ref[...], preferred_element_type=F32)
    base = start_sc[:, 0:1] + run_sc[:, 0:1]
    pos_ref[0] = jnp.sum(oh * (before + base), axis=0, keepdims=True).astype(jnp.int32)
    run_sc[...] += jnp.sum(oh, axis=1, keepdims=True)


def _sort_plan(info, cnt, tm):
    n = info.shape[0]
    sel = _bucket_selector()
    info_spec = pl.BlockSpec((tm, ROUTER_LANES), lambda i: (i, 0))
    r = jnp.arange(BUCKET_ROWS)
    ltri = (r[:, None] >= r[None, :]).astype(F32)
    t = jnp.arange(tm)
    utri = (t[:, None] < t[None, :]).astype(BF16)
    return pl.pallas_call(
        _plan_pos_kernel,
        grid=(n // tm,),
        in_specs=[info_spec, _full(sel.shape), _full(cnt.shape), _full(ltri.shape), _full(utri.shape)],
        out_specs=[pl.BlockSpec((1, 1, tm), lambda i: (i, 0, 0)), _full((8, META_LANES))],
        out_shape=[jax.ShapeDtypeStruct((n // tm, 1, tm), jnp.int32),
                   jax.ShapeDtypeStruct((8, META_LANES), jnp.int32)],
        scratch_shapes=[pltpu.VMEM((BUCKET_ROWS, LANES), F32), pltpu.VMEM((BUCKET_ROWS, LANES), F32)],
        compiler_params=_params(("arbitrary",)),
        name="plan_pos",
    )(info, sel, cnt, ltri, utri)


def _dispatch_kernel(pos_ref, fill_ref, h1_ref, gain2_ref, sh2_ref, sc2_ref, info_ref, xs_ref,
                     rowbuf, zbuf, sem, zsem, *, tm, n_steps, n_tiles):
    step = pl.program_id(0) * pl.num_programs(1) + pl.program_id(1)
    slot = step % 2

    @pl.when(step == 0)
    def _():
        zbuf[...] = jnp.zeros_like(zbuf)
        fill = lambda j: pltpu.make_async_copy(zbuf, xs_ref.at[pl.ds(j * SORT_TILE, SORT_TILE), 0], zsem)
        for j in range(n_tiles):
            pl.when(fill_ref[0, j] == 1)(lambda j=j: fill(j).start())
        for j in range(n_tiles):
            pl.when(fill_ref[0, j] == 1)(lambda j=j: fill(j).wait())

    def wait(s):
        pltpu.make_async_copy(rowbuf.at[s], xs_ref.at[pl.ds(0, tm), 0], sem.at[s]).wait()

    def send(s):
        @pl.when(step >= 2)
        def _():
            wait(s)

        rowbuf[s, :, 0:D_MODEL] = _modulate(h1_ref[0], gain2_ref[...], sh2_ref[...], sc2_ref[...])
        rowbuf[s, :, D_MODEL:] = info_ref[0]
        for r in range(tm):
            pltpu.make_async_copy(rowbuf.at[s, pl.ds(r, 1)], xs_ref.at[pos_ref[0, 0, r]],
                                  sem.at[s]).start(priority=r % 2)

    for s in range(2):
        pl.when(slot == s)(functools.partial(send, s))

    @pl.when(step == n_steps - 1)
    def _():
        wait(slot)
        if n_steps > 1:
            wait(1 - slot)


def _gmoe_kernel(ea_ref, eb_ref, nu_ref, xs_ref, wga_ref, wua_ref, wda_ref, wgb_ref, wub_ref, wdb_ref, ys_ref,
                 xbuf, ybuf, zbuf, sem_in, sem_out, zsem):
    del ea_ref, eb_ref
    j = pl.program_id(0)
    n_used = nu_ref[0]
    slot = j % 2
    tile = lambda ref, t: ref.at[pl.ds(t * SORT_TILE, SORT_TILE), 0]
    in_copy = lambda t, s: pltpu.make_async_copy(tile(xs_ref, t), xbuf.at[s], sem_in.at[s])
    out_copy = lambda t, s: pltpu.make_async_copy(ybuf.at[s], tile(ys_ref, t), sem_out.at[s])

    @pl.when(j == 0)
    def _():
        in_copy(0, 0).start()

    @pl.when(j + 1 < n_used)
    def _():
        in_copy(j + 1, 1 - slot).start()

    @pl.when(j < n_used)
    def _():
        in_copy(j, slot).wait()

        @pl.when(j >= 2)
        def _():
            out_copy(j - 2, slot).wait()

        x = xbuf[slot, :, 0:D_MODEL].astype(BF16)

        def expert(wg_ref, wu_ref, wd_ref, w):
            gt = jnp.dot(x, wg_ref[0].astype(BF16), preferred_element_type=F32)
            up = jnp.dot(x, wu_ref[0].astype(BF16), preferred_element_type=F32)
            h = (gt * jax.nn.sigmoid(gt)) * up * w
            return jnp.dot(h.astype(BF16), wd_ref[0].astype(BF16), preferred_element_type=F32)

        w_lo = xbuf[slot, :, D_MODEL + INFO_W_LO:D_MODEL + INFO_W_LO + 1]
        w_hi = xbuf[slot, :, D_MODEL + INFO_W_HI:D_MODEL + INFO_W_HI + 1]
        ybuf[slot] = expert(wga_ref, wua_ref, wda_ref, w_lo) + expert(wgb_ref, wub_ref, wdb_ref, w_hi)
        out_copy(j, slot).start()

        @pl.when(j == n_used - 1)
        def _():
            out_copy(j, slot).wait()

            @pl.when(j >= 1)
            def _():
                out_copy(j - 1, 1 - slot).wait()

    @pl.when(j >= n_used)
    def _():
        @pl.when(j == n_used)
        def _():
            zbuf[...] = jnp.zeros_like(zbuf)

        fill = pltpu.make_async_copy(zbuf, tile(ys_ref, j), zsem)
        fill.start()
        fill.wait()


def _fetch_sorted_rows(ys_ref, pos_ref, posn_ref, ybuf, sem, tm, n_steps):
    step = pl.program_id(0) * pl.num_programs(1) + pl.program_id(1)
    slot = step % 2

    def issue(p_ref, s):
        for r in range(tm):
            pltpu.make_async_copy(ys_ref.at[p_ref[0, 0, r]], ybuf.at[s, pl.ds(r, 1)],
                                  sem.at[s]).start(priority=r % 2)

    pl.when(step == 0)(functools.partial(issue, pos_ref, 0))
    for s in range(2):
        pl.when((step + 1 < n_steps) & (slot == s))(functools.partial(issue, posn_ref, 1 - s))
    pltpu.make_async_copy(ys_ref.at[pl.ds(0, tm), 0], ybuf.at[slot], sem.at[slot]).wait()
    return ybuf[slot]


def _combine_kernel(pos_ref, posn_ref, h1_ref, g2_ref, ys_ref, o_ref, ybuf, sem, *, tm, n_steps):
    rows = _fetch_sorted_rows(ys_ref, pos_ref, posn_ref, ybuf, sem, tm, n_steps)
    o_ref[0] = h1_ref[0] + g2_ref[...] * rows


def _experts_sorted(h1, info, cnt, mods, layer, gain2, w_gate, w_up, w_down, tm):
    bsz, seq, d = h1.shape
    nt = seq // tm
    n_steps = bsz * nt
    n = bsz * seq
    n_sorted = n + N_BUCKETS * SORT_TILE
    n_tiles = n_sorted // SORT_TILE
    assert n % SORT_TILE == 0 and n_tiles <= META_LANES
    pos, meta = _sort_plan(info.reshape(n, ROUTER_LANES), cnt, tm)
    tile = lambda w: pl.BlockSpec((1, tm, w), lambda b, i: (b, i, 0))
    any_spec = pl.BlockSpec(memory_space=pl.ANY)
    by_batch = lambda b, i: b

    xs = pl.pallas_call(
        functools.partial(_dispatch_kernel, tm=tm, n_steps=n_steps, n_tiles=n_tiles),
        grid=(bsz, nt),
        in_specs=[_pos_spec(tm, nt, n_steps, 0),
                  pl.BlockSpec((1, META_LANES), lambda b, i: (0, 0), memory_space=pltpu.SMEM),
                  tile(d), _full((1, d)), _mod_spec(layer, by_batch, 3),
                  _mod_spec(layer, by_batch, 4), tile(ROUTER_LANES)],
        out_specs=any_spec,
        out_shape=jax.ShapeDtypeStruct((n_sorted, 1, ROW_WIDTH), F32),
        scratch_shapes=[pltpu.VMEM((2, tm, ROW_WIDTH), F32), pltpu.VMEM((SORT_TILE, ROW_WIDTH), F32),
                        pltpu.SemaphoreType.DMA((2,)), pltpu.SemaphoreType.DMA(())],
        compiler_params=_params(("arbitrary", "arbitrary")),
        name=f"dispatch{layer}",
    )(pos, meta[META_FILL:META_FILL + 1], h1, gain2.reshape(1, d), mods, mods, info)

    used = lambda j, nu: jnp.minimum(j, nu[0] - 1)
    w_spec = lambda which, shape: pl.BlockSpec(
        (None, 1) + shape, lambda j, ea, eb, nu: (layer, (ea, eb)[which][used(j, nu)], 0, 0))
    w_in, w_out = (d, D_EXPERT), (D_EXPERT, d)
    ys = pl.pallas_call(
        _gmoe_kernel,
        grid_spec=pltpu.PrefetchScalarGridSpec(
            num_scalar_prefetch=3,
            grid=(n_tiles,),
            in_specs=[any_spec, w_spec(0, w_in), w_spec(0, w_in), w_spec(0, w_out),
                      w_spec(1, w_in), w_spec(1, w_in), w_spec(1, w_out)],
            out_specs=any_spec,
            scratch_shapes=[pltpu.VMEM((2, SORT_TILE, ROW_WIDTH), F32), pltpu.VMEM((2, SORT_TILE, d), F32),
                            pltpu.VMEM((SORT_TILE, d), F32), pltpu.SemaphoreType.DMA((2,)),
                            pltpu.SemaphoreType.DMA((2,)), pltpu.SemaphoreType.DMA(())]),
        out_shape=jax.ShapeDtypeStruct((n_sorted, 1, d), F32),
        compiler_params=_params(("arbitrary",)),
        name=f"experts{layer}",
    )(meta[META_EXPERT_LO], meta[META_EXPERT_HI], meta[META_N_USED, :1], xs,
      w_gate, w_up, w_down, w_gate, w_up, w_down)
    return ys, pos


def _combine(h1, ys, pos, mods, layer, tm):
    bsz, seq, d = h1.shape
    nt = seq // tm
    n_steps = bsz * nt
    tile = lambda w: pl.BlockSpec((1, tm, w), lambda b, i: (b, i, 0))
    return pl.pallas_call(
        functools.partial(_combine_kernel, tm=tm, n_steps=n_steps),
        grid=(bsz, nt),
        in_specs=[_pos_spec(tm, nt, n_steps, 0), _pos_spec(tm, nt, n_steps, 1), tile(d),
                  _mod_spec(layer, lambda b, i: b, 5), pl.BlockSpec(memory_space=pl.ANY)],
        out_specs=tile(d),
        out_shape=jax.ShapeDtypeStruct((bsz, seq, d), F32),
        scratch_shapes=[pltpu.VMEM((2, tm, d), F32), pltpu.SemaphoreType.DMA((2,))],
        compiler_params=_params(("arbitrary", "arbitrary")),
        name=f"combine{layer}",
    )(pos, pos, h1, mods, ys)


def kernel(x, c, ctx, c_ctx, mod_w, mod_b, norm1_g, norm2_g, even_w_in, q_gain, k_gain, pool_w, pool_scale,
           even_w_out, odd_w_in, sg_gain, sg_w, sg_b, conv_w, odd_w_out, router_g_w, router_g_b,
           router_e_w, router_e_b, w_gate, w_up, w_down):
    bsz, seq, d = x.shape
    tm = min(512, seq)
    cond = jnp.zeros((MOD_ROWS, d), F32).at[:bsz].set(c).at[bsz].set(c_ctx)
    mods = _adaln(cond, mod_w, mod_b).reshape(mod_w.shape[0], MOD_ROWS, 6, 1, d)

    q, k, v, p = _inproj0(x, mods, norm1_g[0], even_w_in[0], q_gain[0], k_gain[0], tm)
    kc, vc = _inproj0_ctx(ctx, mods, bsz, norm1_g[0], even_w_in[0][:, ATTN_WIDTH:ATTN_WIDTH + 2 * KV_WIDTH],
                          k_gain[0])
    o = _attention(q, k, v, kc, vc, tq=min(512, seq), tk=min(2048, seq))
    router0 = _router_operands(router_g_w[0], router_g_b[0], router_e_w[0], router_e_b[0])
    h1, info, cnt = _out0(o, p, x, mods, norm2_g[0], pool_w[0], pool_scale[0], even_w_out[0], router0, tm)
    ys, pos = _experts_sorted(h1, info, cnt, mods, 0, norm2_g[0], w_gate, w_up, w_down, tm)

    h, yc, z, bg = _inproj1(h1, ys, pos, mods, norm1_g[1], odd_w_in[0], sg_gain[0], sg_w[0], sg_b[0], tm)
    router1 = _router_operands(router_g_w[1], router_g_b[1], router_e_w[1], router_e_b[1])
    h1, info, cnt = _out1(yc, z, bg, h, mods, norm2_g[1], conv_w[0], odd_w_out[0], router1, tm)
    ys, pos = _experts_sorted(h1, info, cnt, mods, 1, norm2_g[1], w_gate, w_up, w_down, tm)
    return _combine(h1, ys, pos, mods, 1, tm)
```

```python
import functools

import jax
import jax.numpy as jnp
from jax import lax
from jax.experimental import pallas as pl
from jax.experimental.pallas import tpu as pltpu

F32 = jnp.float32
BF16 = jnp.bfloat16

D_MODEL = 1024
GRID_W = 64
EPS = 1e-6
N_Q_HEADS = 8
N_KV_HEADS = 2
HEAD_DIM = 64
Q_PER_KV = N_Q_HEADS // N_KV_HEADS
ATTN_WIDTH = N_Q_HEADS * HEAD_DIM
KV_WIDTH = N_KV_HEADS * HEAD_DIM
ROPE_THETA = 10000.0
POOL_WINDOWS = (2, 4, 8, 16)
POOL_GROUP = 128
POOL_WIDTH = POOL_GROUP * len(POOL_WINDOWS)
SG_GROUPS = 4
SG_CHUNK = 128
SG_WIDTH = 512
CONV_WIDTH = 512
EVEN_IN = ATTN_WIDTH + 2 * KV_WIDTH + POOL_WIDTH
ODD_IN = 2 * SG_WIDTH + 3 * CONV_WIDTH
N_GROUPS = 4
EXPERTS_PER_GROUP = 4
N_EXPERTS = 16
D_EXPERT = 256

Q_SCALE = HEAD_DIM ** -0.5 * 1.4426950408889634
SAFE_SOFTMAX_SHIFT = 60.0
LANES = 128
HALO = 16
ROUTER_LANES = 128
MOD_ROWS = 16
VMEM_LIMIT = 48 * 1024 * 1024

PAIRS_PER_GROUP = 6
N_BUCKETS = N_GROUPS * PAIRS_PER_GROUP
BUCKET_ROWS = 32
SORT_TILE = 512
META_LANES = 256
META_EXPERT_LO, META_EXPERT_HI, META_N_USED, META_FILL, META_FIRST, META_WSLOT, META_NEXT_LO, META_NEXT_HI = range(8)
ROW_WIDTH = D_MODEL + ROUTER_LANES
INFO_BUCKET, INFO_W_LO, INFO_W_HI = 0, 1, 2


def _params(sem):
    return pltpu.CompilerParams(dimension_semantics=sem, vmem_limit_bytes=VMEM_LIMIT)


def _modulate(x, gain, shift, scale):
    ms = jnp.mean(x * x, axis=-1, keepdims=True)
    return (x * lax.rsqrt(ms + EPS) * gain) * (1.0 + scale) + shift


def _mod_spec(layer, row_fn, which):
    return pl.BlockSpec((None, None, None, 1, D_MODEL),
                        lambda *idx: (layer, row_fn(*idx), which, 0, 0))


def _full(shape):
    return pl.BlockSpec(shape, lambda *idx: (0,) * len(shape))


def _adaln_kernel(c_ref, w_ref, b_ref, o_ref):
    c = c_ref[...]
    s = c * jax.nn.sigmoid(c)
    o_ref[0] = jnp.dot(s, w_ref[0], precision=lax.Precision.HIGHEST,
                       preferred_element_type=F32) + b_ref[0]


def _adaln(cond, mod_w, mod_b):
    depth, d, n = mod_w.shape
    tn = 1024
    return pl.pallas_call(
        _adaln_kernel,
        grid=(depth, n // tn),
        in_specs=[_full((MOD_ROWS, d)),
                  pl.BlockSpec((1, d, tn), lambda l, j: (l, 0, j)),
                  pl.BlockSpec((1, 1, tn), lambda l, j: (l, 0, j))],
        out_specs=pl.BlockSpec((1, MOD_ROWS, tn), lambda l, j: (l, 0, j)),
        out_shape=jax.ShapeDtypeStruct((depth, MOD_ROWS, n), F32),
        compiler_params=_params(("arbitrary", "arbitrary")),
        name="adaln",
    )(cond, mod_w, mod_b.reshape(depth, 1, n))


def _head_norm_rope(z, gain, ones_bd, cos, sin, first_half):
    sq = z * z
    hi = sq.astype(BF16)
    lo = (sq - hi.astype(F32)).astype(BF16)
    ms = (jnp.dot(hi, ones_bd, preferred_element_type=F32)
          + jnp.dot(lo, ones_bd, preferred_element_type=F32))
    zn = z * lax.rsqrt(ms + EPS) * gain
    partner = jnp.where(first_half, pltpu.roll(zn, LANES - 16, 1), pltpu.roll(zn, 16, 1))
    return zn * cos + partner * sin


def _inproj0_kernel(x_ref, gain_ref, sh_ref, sc_ref, w_ref, cos_ref, sin_ref, qg_ref, kg_ref, ones_ref,
                    q_ref, k_ref, v_ref, p_ref):
    a = _modulate(x_ref[0], gain_ref[...], sh_ref[...], sc_ref[...])
    y = jnp.dot(a.astype(BF16), w_ref[...], preferred_element_type=F32)
    cos, sin, ones_bd = cos_ref[...], sin_ref[...], ones_ref[...]
    lane = lax.broadcasted_iota(jnp.int32, cos.shape, 1)
    first_half = (lane % 32) < 16
    for s in range(ATTN_WIDTH // LANES):
        r = _head_norm_rope(y[:, s * LANES:(s + 1) * LANES], qg_ref[...], ones_bd, cos, sin, first_half)
        r = (r * Q_SCALE).astype(BF16)
        q_ref[0, 2 * s] = r[:, :HEAD_DIM]
        q_ref[0, 2 * s + 1] = r[:, HEAD_DIM:]
    kr = _head_norm_rope(y[:, ATTN_WIDTH:ATTN_WIDTH + KV_WIDTH], kg_ref[...], ones_bd, cos, sin,
                         first_half).astype(BF16)
    k_ref[0, 0] = kr[:, :HEAD_DIM]
    k_ref[0, 1] = kr[:, HEAD_DIM:]
    vv = y[:, ATTN_WIDTH + KV_WIDTH:ATTN_WIDTH + 2 * KV_WIDTH].astype(BF16)
    v_ref[0, 0] = vv[:, :HEAD_DIM]
    v_ref[0, 1] = vv[:, HEAD_DIM:]
    p_ref[0] = y[:, ATTN_WIDTH + 2 * KV_WIDTH:].astype(BF16)


def _inproj0_ctx_kernel(x_ref, gain_ref, sh_ref, sc_ref, w_ref, kg_ref, ones_ref, k_ref, v_ref):
    a = _modulate(x_ref[0], gain_ref[...], sh_ref[...], sc_ref[...])
    y = jnp.dot(a.astype(BF16), w_ref[...], preferred_element_type=F32)
    z = y[:, :KV_WIDTH]
    sq = z * z
    hi = sq.astype(BF16)
    lo = (sq - hi.astype(F32)).astype(BF16)
    ms = (jnp.dot(hi, ones_ref[...], preferred_element_type=F32)
          + jnp.dot(lo, ones_ref[...], preferred_element_type=F32))
    kr = (z * lax.rsqrt(ms + EPS) * kg_ref[...]).astype(BF16)
    k_ref[0, 0] = kr[:, :HEAD_DIM]
    k_ref[0, 1] = kr[:, HEAD_DIM:]
    vv = y[:, KV_WIDTH:].astype(BF16)
    v_ref[0, 0] = vv[:, :HEAD_DIM]
    v_ref[0, 1] = vv[:, HEAD_DIM:]


def _rope_tables(seq):
    t = jnp.arange(seq)
    row = (t // GRID_W).astype(F32)
    col = (t % GRID_W).astype(F32)
    half = HEAD_DIM // 2
    inv = ROPE_THETA ** (-jnp.arange(0, half, 2, dtype=F32) / half)
    ar, ac = row[:, None] * inv, col[:, None] * inv
    cos = jnp.concatenate([jnp.cos(ar), jnp.cos(ar), jnp.cos(ac), jnp.cos(ac)], axis=-1)
    sin = jnp.concatenate([-jnp.sin(ar), jnp.sin(ar), -jnp.sin(ac), jnp.sin(ac)], axis=-1)
    return jnp.tile(cos, (1, LANES // HEAD_DIM)), jnp.tile(sin, (1, LANES // HEAD_DIM))


def _head_mean_matrix():
    r = jnp.arange(LANES)
    same = (r[:, None] // HEAD_DIM) == (r[None, :] // HEAD_DIM)
    return jnp.where(same, 1.0 / HEAD_DIM, 0.0).astype(BF16)


def _inproj0(x, mods, gain, w_in, q_gain, k_gain, tm):
    bsz, seq, d = x.shape
    cos, sin = _rope_tables(seq)
    qg = jnp.tile(q_gain, LANES // HEAD_DIM).reshape(1, LANES)
    kg = jnp.tile(k_gain, LANES // HEAD_DIM).reshape(1, LANES)
    head = lambda n: pl.BlockSpec((1, n, tm, HEAD_DIM), lambda b, i: (b, 0, i, 0))
    return pl.pallas_call(
        _inproj0_kernel,
        grid=(bsz, seq // tm),
        in_specs=[pl.BlockSpec((1, tm, d), lambda b, i: (b, i, 0)),
                  _full((1, d)),
                  _mod_spec(0, lambda b, i: b, 0),
                  _mod_spec(0, lambda b, i: b, 1),
                  _full((d, EVEN_IN)),
                  pl.BlockSpec((tm, LANES), lambda b, i: (i, 0)),
                  pl.BlockSpec((tm, LANES), lambda b, i: (i, 0)),
                  _full((1, LANES)), _full((1, LANES)), _full((LANES, LANES))],
        out_specs=[head(N_Q_HEADS), head(N_KV_HEADS), head(N_KV_HEADS),
                   pl.BlockSpec((1, tm, POOL_WIDTH), lambda b, i: (b, i, 0))],
        out_shape=[jax.ShapeDtypeStruct((bsz, N_Q_HEADS, seq, HEAD_DIM), BF16),
                   jax.ShapeDtypeStruct((bsz, N_KV_HEADS, seq, HEAD_DIM), BF16),
                   jax.ShapeDtypeStruct((bsz, N_KV_HEADS, seq, HEAD_DIM), BF16),
                   jax.ShapeDtypeStruct((bsz, seq, POOL_WIDTH), BF16)],
        compiler_params=_params(("parallel", "parallel")),
        name="inproj0",
    )(x, gain.reshape(1, d), mods, mods, w_in.astype(BF16), cos, sin, qg, kg, _head_mean_matrix())


def _inproj0_ctx(ctx, mods, ctx_row, gain, w_kv, k_gain):
    bsz, n_ctx, d = ctx.shape
    kg = jnp.tile(k_gain, LANES // HEAD_DIM).reshape(1, LANES)
    head = pl.BlockSpec((1, N_KV_HEADS, n_ctx, HEAD_DIM), lambda b: (b, 0, 0, 0))
    return pl.pallas_call(
        _inproj0_ctx_kernel,
        grid=(bsz,),
        in_specs=[pl.BlockSpec((1, n_ctx, d), lambda b: (b, 0, 0)),
                  _full((1, d)),
                  _mod_spec(0, lambda b: ctx_row, 0),
                  _mod_spec(0, lambda b: ctx_row, 1),
                  _full((d, 2 * KV_WIDTH)),
                  _full((1, LANES)), _full((LANES, LANES))],
        out_specs=[head, head],
        out_shape=[jax.ShapeDtypeStruct((bsz, N_KV_HEADS, n_ctx, HEAD_DIM), BF16)] * 2,
        compiler_params=_params(("parallel",)),
        name="inproj0_ctx",
    )(ctx, gain.reshape(1, d), mods, mods, w_kv.astype(BF16), kg, _head_mean_matrix())


def _attn_kernel(q_ref, kl_ref, vl_ref, kc_ref, vc_ref, o_ref, ksq_sc, *, tq, tk):
    rows = Q_PER_KV * tq
    q = q_ref[0].reshape(rows, HEAD_DIM)
    seq = kl_ref.shape[2]
    chunks = [(kc_ref, vc_ref, 0, kc_ref.shape[2])]
    chunks += [(kl_ref, vl_ref, c * tk, tk) for c in range(seq // tk)]
    scores = lambda k: lax.dot_general(q, k, (((1,), (1,)), ((), ())), preferred_element_type=F32)

    @pl.when(pl.program_id(2) == 0)
    def _():
        def largest_sq_norm(k_ref):
            kf = k_ref[0, 0].astype(F32)
            return jnp.max(jnp.sum(kf * kf, axis=1, keepdims=True), axis=0, keepdims=True)

        ksq_sc[...] = jnp.broadcast_to(jnp.maximum(largest_sq_norm(kl_ref), largest_sq_norm(kc_ref)),
                                       ksq_sc.shape)

    qf = q.astype(F32)
    bound = jnp.sqrt(jnp.sum(qf * qf, axis=1, keepdims=True) * ksq_sc[0:1, 0:1])
    safe = jnp.max(bound) <= SAFE_SOFTMAX_SHIFT

    def finish(acc, l):
        o = acc / l
        o_ref[0] = jnp.concatenate([o[h * tq:(h + 1) * tq] for h in range(Q_PER_KV)], axis=1).astype(BF16)

    @pl.when(safe)
    def _():
        l = jnp.zeros((rows, 1), F32)
        acc = jnp.zeros((rows, HEAD_DIM), F32)
        for k_ref, v_ref, start, size in chunks:
            p = jnp.exp2(scores(k_ref[0, 0, start:start + size, :]) - bound)
            l = l + jnp.sum(p, axis=1, keepdims=True)
            acc = acc + jnp.dot(p.astype(BF16), v_ref[0, 0, start:start + size, :], preferred_element_type=F32)
        finish(acc, l)

    @pl.when(jnp.logical_not(safe))
    def _():
        m = jnp.full((rows, 1), -jnp.inf, F32)
        l = jnp.zeros((rows, 1), F32)
        acc = jnp.zeros((rows, HEAD_DIM), F32)
        for k_ref, v_ref, start, size in chunks:
            s = scores(k_ref[0, 0, start:start + size, :])
            m_new = jnp.maximum(m, jnp.max(s, axis=1, keepdims=True))
            alpha = jnp.exp2(m - m_new)
            p = jnp.exp2(s - m_new)
            l = alpha * l + jnp.sum(p, axis=1, keepdims=True)
            acc = alpha * acc + jnp.dot(p.astype(BF16), v_ref[0, 0, start:start + size, :],
                                        preferred_element_type=F32)
            m = m_new
        finish(acc, l)


def _attention(q, k, v, kc, vc, tq, tk):
    bsz, _, seq, _ = q.shape
    n_ctx = kc.shape[2]
    kv_spec = lambda n: pl.BlockSpec((1, 1, n, HEAD_DIM), lambda b, g, i: (b, g, 0, 0))
    return pl.pallas_call(
        functools.partial(_attn_kernel, tq=tq, tk=tk),
        grid=(bsz, N_KV_HEADS, seq // tq),
        in_specs=[pl.BlockSpec((1, Q_PER_KV, tq, HEAD_DIM), lambda b, g, i: (b, g, i, 0)),
                  kv_spec(seq), kv_spec(seq), kv_spec(n_ctx), kv_spec(n_ctx)],
        out_specs=pl.BlockSpec((1, tq, Q_PER_KV * HEAD_DIM), lambda b, g, i: (b, i, g)),
        out_shape=jax.ShapeDtypeStruct((bsz, seq, ATTN_WIDTH), BF16),
        scratch_shapes=[pltpu.VMEM((8, LANES), F32)],
        compiler_params=_params(("arbitrary", "arbitrary", "arbitrary")),
        name="attention",
    )(q, k, v, kc, vc)


def _route(logits):
    lane = lax.broadcasted_iota(jnp.int32, logits.shape, 1).astype(F32)
    neg = -jnp.inf
    big = float(ROUTER_LANES)
    first_index = lambda mask: jnp.min(jnp.where(mask, lane, big), axis=1, keepdims=True)
    is_g = lane < N_GROUPS
    gm = jnp.max(jnp.where(is_g, logits, neg), axis=1, keepdims=True)
    gidx = first_index(is_g & (logits == gm))
    gden = jnp.sum(jnp.where(is_g, jnp.exp(logits - gm), 0.0), axis=1, keepdims=True)
    g_p = 1.0 / gden
    first = N_GROUPS + EXPERTS_PER_GROUP * gidx
    sel = (lane >= first) & (lane < first + EXPERTS_PER_GROUP)
    e1 = jnp.max(jnp.where(sel, logits, neg), axis=1, keepdims=True)
    i1 = first_index(sel & (logits == e1))
    rest = sel & (lane != i1)
    e2 = jnp.max(jnp.where(rest, logits, neg), axis=1, keepdims=True)
    i2 = first_index(rest & (logits == e2))
    p2 = jnp.exp(e2 - e1)
    w1 = g_p * (1.0 / (1.0 + p2))
    w2 = g_p * (p2 / (1.0 + p2))
    lo = jnp.minimum(i1, i2) - first
    hi = jnp.maximum(i1, i2) - first
    pair = jnp.where(lo == 0, hi - 1, jnp.where(lo == 1, hi + 1, PAIRS_PER_GROUP - 1.0))
    bucket = PAIRS_PER_GROUP * gidx + pair
    w_lo = jnp.where(i1 < i2, w1, w2)
    w_hi = jnp.where(i1 < i2, w2, w1)
    return jnp.where(lane == INFO_BUCKET, bucket,
                     jnp.where(lane == INFO_W_LO, w_lo, jnp.where(lane == INFO_W_HI, w_hi, 0.0)))


def _bucket_onehot(info, sel):
    brow = lax.dot_general(sel, info.astype(BF16), (((1,), (1,)), ((), ())),
                           preferred_element_type=F32)[0:1]
    bid = lax.broadcasted_iota(jnp.int32, (BUCKET_ROWS, info.shape[0]), 0)
    return (bid == brow.astype(jnp.int32)).astype(F32)


def _bucket_selector():
    return jnp.zeros((8, ROUTER_LANES), F32).at[0, INFO_BUCKET].set(1.0).astype(BF16)


def _tail(y, x_res, gate1, gain2, shift2, scale2, rw_both, rbias, sel, h1_ref, info_ref, cnt_ref):
    h1 = x_res + gate1 * y
    h1_ref[0] = h1
    t = _modulate(h1, gain2, shift2, scale2)
    t_hi = t.astype(BF16)
    t_lo = (t - t_hi.astype(F32)).astype(BF16)
    both = jnp.dot(t_hi, rw_both, preferred_element_type=F32)
    logits = (both[:, :ROUTER_LANES] + both[:, ROUTER_LANES:]
              + jnp.dot(t_lo, rw_both[:, :ROUTER_LANES], preferred_element_type=F32)) + rbias
    info = _route(logits)
    info_ref[0] = info

    @pl.when((pl.program_id(0) == 0) & (pl.program_id(1) == 0))
    def _():
        cnt_ref[...] = jnp.zeros_like(cnt_ref)

    cnt_ref[...] += jnp.sum(_bucket_onehot(info, sel), axis=1, keepdims=True)


def _router_operands(rg_w, rg_b, re_w, re_b):
    d = rg_w.shape[0]
    w = jnp.concatenate([rg_w, re_w, jnp.zeros((d, ROUTER_LANES - N_GROUPS - N_EXPERTS), F32)], axis=1)
    b = jnp.concatenate([rg_b, re_b, jnp.zeros((ROUTER_LANES - N_GROUPS - N_EXPERTS,), F32)])
    w_hi = w.astype(BF16)
    w_lo = (w - w_hi.astype(F32)).astype(BF16)
    return jnp.concatenate([w_hi, w_lo], axis=1), b.reshape(1, ROUTER_LANES)


def _fill_halo(buf, main_ref, prev_ref, next_ref, tm, i, n_tiles, halo=HALO):
    buf[halo:halo + tm] = main_ref[0].astype(F32)
    buf[0:halo] = jnp.where(i > 0, prev_ref[0].astype(F32), 0.0)
    buf[halo + tm:2 * halo + tm] = jnp.where(i < n_tiles - 1, next_ref[0].astype(F32), 0.0)


def _halo_specs(tm, seq, width, halo=HALO):
    per = tm // halo
    last = seq // halo - 1
    return [pl.BlockSpec((1, tm, width), lambda b, i: (b, i, 0)),
            pl.BlockSpec((1, halo, width), lambda b, i: (b, jnp.maximum(i * per - 1, 0), 0)),
            pl.BlockSpec((1, halo, width), lambda b, i: (b, jnp.minimum((i + 1) * per, last), 0))]


def _out0_kernel(o_ref, p_ref, pprev_ref, pnext_ref, x_ref, g1_ref, gain2_ref, sh2_ref, sc2_ref,
                 poolw_ref, pscale_ref, wout_ref, rw_ref, rb_ref, sel_ref,
                 h1_ref, info_ref, cnt_ref, pbuf, *, tm, seq):
    i = pl.program_id(1)
    _fill_halo(pbuf, p_ref, pprev_ref, pnext_ref, tm, i, seq // tm)
    pos = i * tm + lax.broadcasted_iota(jnp.int32, (tm, 1), 0)
    pooled = []
    for g, w in enumerate(POOL_WINDOWS):
        sl = slice(g * POOL_GROUP, (g + 1) * POOL_GROUP)
        acc = pbuf[HALO - w // 2:HALO - w // 2 + tm, sl]
        for j in range(1 - w // 2, w - w // 2):
            acc = acc + pbuf[HALO + j:HALO + j + tm, sl]
        lo = jnp.clip(pos - w // 2, 0, seq)
        hi = jnp.clip(pos + w - w // 2, 0, seq)
        mean = acc * (1.0 / (hi - lo).astype(F32))
        dlt = (mean - pbuf[HALO:HALO + tm, sl]).astype(BF16)
        pooled.append((jnp.dot(dlt, poolw_ref[g], preferred_element_type=F32) * pscale_ref[:, sl]).astype(BF16))
    mixed = jnp.concatenate([o_ref[0]] + pooled, axis=1)
    y = jnp.dot(mixed, wout_ref[...], preferred_element_type=F32)
    _tail(y, x_ref[0], g1_ref[...], gain2_ref[...], sh2_ref[...], sc2_ref[...],
          rw_ref[...], rb_ref[...], sel_ref[...], h1_ref, info_ref, cnt_ref)


def _tail_specs(layer, d):
    by_batch = lambda b, i: b
    ins = [_mod_spec(layer, by_batch, 2), _full((1, d)), _mod_spec(layer, by_batch, 3),
           _mod_spec(layer, by_batch, 4)]
    return ins


def _tail_outs(bsz, seq, d, tm):
    specs = [pl.BlockSpec((1, tm, d), lambda b, i: (b, i, 0)),
             pl.BlockSpec((1, tm, ROUTER_LANES), lambda b, i: (b, i, 0)),
             _full((BUCKET_ROWS, LANES))]
    shapes = [jax.ShapeDtypeStruct((bsz, seq, d), F32),
              jax.ShapeDtypeStruct((bsz, seq, ROUTER_LANES), F32),
              jax.ShapeDtypeStruct((BUCKET_ROWS, LANES), F32)]
    return specs, shapes


def _out0(o, p, x, mods, gain2, pool_w, pool_scale, w_out, router, tm):
    bsz, seq, d = x.shape
    rw_both, rb = router
    out_specs, out_shapes = _tail_outs(bsz, seq, d, tm)
    return pl.pallas_call(
        functools.partial(_out0_kernel, tm=tm, seq=seq),
        grid=(bsz, seq // tm),
        in_specs=[pl.BlockSpec((1, tm, ATTN_WIDTH), lambda b, i: (b, i, 0))]
        + _halo_specs(tm, seq, POOL_WIDTH)
        + [pl.BlockSpec((1, tm, d), lambda b, i: (b, i, 0))]
        + _tail_specs(0, d)
        + [_full(pool_w.shape), _full((1, POOL_WIDTH)), _full(w_out.shape),
           _full(rw_both.shape), _full(rb.shape), _full((8, ROUTER_LANES))],
        out_specs=out_specs,
        out_shape=out_shapes,
        scratch_shapes=[pltpu.VMEM((tm + 2 * HALO, POOL_WIDTH), F32)],
        compiler_params=_params(("arbitrary", "arbitrary")),
        name="out0",
    )(o, p, p, p, x, mods, gain2.reshape(1, d), mods, mods,
      pool_w.astype(BF16), pool_scale.reshape(1, POOL_WIDTH), w_out.astype(BF16), rw_both, rb,
      _bucket_selector())


def _inproj1_kernel(pos_ref, posn_ref, h1_ref, g2_ref, gain_ref, sh_ref, sc_ref, w_ref, sgg_ref, sgw_ref, sgb_ref,
                    ys_ref, h_ref, yc_ref, z_ref, bg_ref, ybuf, sem, *, tm, n_steps):
    x = h1_ref[0] + g2_ref[...] * _fetch_sorted_rows(ys_ref, pos_ref, posn_ref, ybuf, sem, tm, n_steps)
    h_ref[0] = x
    a = _modulate(x, gain_ref[...], sh_ref[...], sc_ref[...])
    y = jnp.dot(a.astype(BF16), w_ref[...], preferred_element_type=F32)
    for g in range(SG_GROUPS):
        sl = slice(g * LANES, (g + 1) * LANES)
        u = y[:, sl]
        vg = y[:, SG_WIDTH + g * LANES:SG_WIDTH + (g + 1) * LANES]
        ms = jnp.mean(vg * vg, axis=-1, keepdims=True)
        vn = (vg * lax.rsqrt(ms + EPS) * sgg_ref[:, sl]).astype(BF16)
        for c in range(tm // SG_CHUNK):
            rows = slice(c * SG_CHUNK, (c + 1) * SG_CHUNK)
            s = jnp.dot(sgw_ref[g], vn[rows], preferred_element_type=F32) + sgb_ref[g]
            yc_ref[0, rows, sl] = (u[rows] * s).astype(BF16)
    hx = y[:, 2 * SG_WIDTH:2 * SG_WIDTH + CONV_WIDTH]
    bg_ref[0] = y[:, 2 * SG_WIDTH + CONV_WIDTH:2 * SG_WIDTH + 2 * CONV_WIDTH].astype(BF16)
    cg = y[:, 2 * SG_WIDTH + 2 * CONV_WIDTH:]
    z_ref[0] = (cg * hx).astype(BF16)


def _pos_spec(tm, nt, n_steps, ahead):
    return pl.BlockSpec((1, 1, tm), lambda b, i: (jnp.minimum(b * nt + i + ahead, n_steps - 1), 0, 0),
                        memory_space=pltpu.SMEM)


def _inproj1(h1, ys, pos, mods, gain, w_in, sg_gain, sg_w, sg_b, tm):
    bsz, seq, d = h1.shape
    nt = seq // tm
    n_steps = bsz * nt
    sgb = jnp.broadcast_to(sg_b[:, :, None], (SG_GROUPS, SG_CHUNK, LANES))
    by_batch = lambda b, i: b
    wide = lambda w, dt: (pl.BlockSpec((1, tm, w), lambda b, i: (b, i, 0)),
                          jax.ShapeDtypeStruct((bsz, seq, w), dt))
    outs = [wide(d, F32), wide(SG_WIDTH, BF16), wide(CONV_WIDTH, BF16), wide(CONV_WIDTH, BF16)]
    return pl.pallas_call(
        functools.partial(_inproj1_kernel, tm=tm, n_steps=n_steps),
        grid=(bsz, nt),
        in_specs=[_pos_spec(tm, nt, n_steps, 0), _pos_spec(tm, nt, n_steps, 1),
                  pl.BlockSpec((1, tm, d), lambda b, i: (b, i, 0)),
                  _mod_spec(0, by_batch, 5),
                  _full((1, d)),
                  _mod_spec(1, by_batch, 0),
                  _mod_spec(1, by_batch, 1),
                  _full((d, ODD_IN)),
                  _full((1, SG_WIDTH)), _full(sg_w.shape), _full(sgb.shape),
                  pl.BlockSpec(memory_space=pl.ANY)],
        out_specs=[s for s, _ in outs],
        out_shape=[s for _, s in outs],
        scratch_shapes=[pltpu.VMEM((2, tm, d), F32), pltpu.SemaphoreType.DMA((2,))],
        compiler_params=_params(("arbitrary", "arbitrary")),
        name="inproj1",
    )(pos, pos, h1, mods, gain.reshape(1, d), mods, mods, w_in.astype(BF16), sg_gain.reshape(1, SG_WIDTH),
      sg_w.astype(BF16), sgb, ys)


def _out1_kernel(yc_ref, z_ref, zprev_ref, znext_ref, bg_ref, x_ref, g1_ref, gain2_ref, sh2_ref, sc2_ref,
                 convw_ref, wout_ref, rw_ref, rb_ref, sel_ref,
                 h1_ref, info_ref, cnt_ref, zbuf, *, tm, seq):
    i = pl.program_id(1)
    _fill_halo(zbuf, z_ref, zprev_ref, znext_ref, tm, i, seq // tm, HALO)
    zc = (zbuf[HALO - 1:HALO - 1 + tm] * convw_ref[0:1, :]
          + zbuf[HALO:HALO + tm] * convw_ref[1:2, :]
          + zbuf[HALO + 1:HALO + 1 + tm] * convw_ref[2:3, :])
    yd = (bg_ref[0].astype(F32) * zc).astype(BF16)
    y = (jnp.dot(yc_ref[0], wout_ref[0:SG_WIDTH, :], preferred_element_type=F32)
         + jnp.dot(yd, wout_ref[SG_WIDTH:, :], preferred_element_type=F32))
    _tail(y, x_ref[0], g1_ref[...], gain2_ref[...], sh2_ref[...], sc2_ref[...],
          rw_ref[...], rb_ref[...], sel_ref[...], h1_ref, info_ref, cnt_ref)


def _out1(yc, z, bg, x, mods, gain2, conv_w, w_out, router, tm):
    bsz, seq, d = x.shape
    rw_both, rb = router
    out_specs, out_shapes = _tail_outs(bsz, seq, d, tm)
    wide = pl.BlockSpec((1, tm, CONV_WIDTH), lambda b, i: (b, i, 0))
    return pl.pallas_call(
        functools.partial(_out1_kernel, tm=tm, seq=seq),
        grid=(bsz, seq // tm),
        in_specs=[wide] + _halo_specs(tm, seq, CONV_WIDTH, HALO) + [wide]
        + [pl.BlockSpec((1, tm, d), lambda b, i: (b, i, 0))]
        + _tail_specs(1, d)
        + [_full((3, CONV_WIDTH)), _full(w_out.shape),
           _full(rw_both.shape), _full(rb.shape), _full((8, ROUTER_LANES))],
        out_specs=out_specs,
        out_shape=out_shapes,
        scratch_shapes=[pltpu.VMEM((tm + 2 * HALO, CONV_WIDTH), F32)],
        compiler_params=_params(("arbitrary", "arbitrary")),
        name="out1",
    )(yc, z, z, z, bg, x, mods, gain2.reshape(1, d), mods, mods,
      conv_w.reshape(3, CONV_WIDTH), w_out.astype(BF16), rw_both, rb, _bucket_selector())


def _plan_pos_kernel(info_ref, sel_ref, cnt_ref, ltri_ref, utri_ref, pos_ref, meta_ref, start_sc, run_sc):
    @pl.when(pl.program_id(0) == 0)
    def _():
        padded = jnp.ceil(cnt_ref[...] * (1.0 / SORT_TILE)) * SORT_TILE
        incl = jnp.dot(ltri_ref[...], padded, precision=lax.Precision.HIGHEST, preferred_element_type=F32)
        start_sc[...] = incl - padded
        run_sc[...] = jnp.zeros_like(run_sc)
        ends = jnp.broadcast_to(incl[:, 0:1], (BUCKET_ROWS, META_LANES))
        bid = lax.broadcasted_iota(jnp.int32, ends.shape, 0)
        tile = lax.broadcasted_iota(jnp.int32, (1, META_LANES), 1)

        def bucket_of(row0):
            done = jnp.where((bid < N_BUCKETS) & (ends <= row0), 1.0, 0.0)
            return jnp.minimum(jnp.sum(done, axis=0, keepdims=True), N_BUCKETS - 1.0).astype(jnp.int32)

        def experts_of(bucket):
            grp = ((bucket >= PAIRS_PER_GROUP).astype(jnp.int32)
                   + (bucket >= 2 * PAIRS_PER_GROUP).astype(jnp.int32)
                   + (bucket >= 3 * PAIRS_PER_GROUP).astype(jnp.int32))
            pair = bucket - PAIRS_PER_GROUP * grp
            lo = (pair >= 3).astype(jnp.int32) + (pair >= 5).astype(jnp.int32)
            hi = jnp.where(pair == 0, 1, jnp.where((pair == 1) | (pair == 3), 2, 3))
            return EXPERTS_PER_GROUP * grp + lo, EXPERTS_PER_GROUP * grp + hi

        row0 = (tile * SORT_TILE).astype(F32)
        tb = bucket_of(row0)
        n_used = (incl[N_BUCKETS - 1:N_BUCKETS, 0:1] * (1.0 / SORT_TILE)).astype(jnp.int32)
        fill = (tile >= n_used - 1) | (tb != bucket_of(row0 + SORT_TILE))
        first = (tile == 0) | (tb != bucket_of(row0 - SORT_TILE))
        own_end = jnp.sum(jnp.where(bid == tb, ends, 0.0), axis=0, keepdims=True)
        nonempty = jnp.broadcast_to(padded[:, 0:1], ends.shape) > 0.0
        ordinal = jnp.sum(jnp.where((bid < tb) & nonempty, 1.0, 0.0), axis=0, keepdims=True)
        rows = [None] * 8
        rows[META_EXPERT_LO], rows[META_EXPERT_HI] = experts_of(tb)
        rows[META_N_USED] = jnp.broadcast_to(n_used, (1, META_LANES))
        rows[META_FILL] = fill.astype(jnp.int32)
        rows[META_FIRST] = first.astype(jnp.int32)
        rows[META_WSLOT] = (ordinal - 2.0 * jnp.floor(ordinal * 0.5)).astype(jnp.int32)
        rows[META_NEXT_LO], rows[META_NEXT_HI] = experts_of(bucket_of(own_end))
        for r, row in enumerate(rows):
            meta_ref[r:r + 1, :] = row

    oh = _bucket_onehot(info_ref[...], sel_ref[...])
    before = jnp.dot(oh.astype(BF16), utri_ref[...], preferred_element_type=F32)
    base = start_sc[:, 0:1] + run_sc[:, 0:1]
    pos_ref[0] = jnp.sum(oh * (before + base), axis=0, keepdims=True).astype(jnp.int32)
    run_sc[...] += jnp.sum(oh, axis=1, keepdims=True)


def _sort_plan(info, cnt, tm):
    n = info.shape[0]
    sel = _bucket_selector()
    info_spec = pl.BlockSpec((tm, ROUTER_LANES), lambda i: (i, 0))
    r = jnp.arange(BUCKET_ROWS)
    ltri = (r[:, None] >= r[None, :]).astype(F32)
    t = jnp.arange(tm)
    utri = (t[:, None] < t[None, :]).astype(BF16)
    return pl.pallas_call(
        _plan_pos_kernel,
        grid=(n // tm,),
        in_specs=[info_spec, _full(sel.shape), _full(cnt.shape), _full(ltri.shape), _full(utri.shape)],
        out_specs=[pl.BlockSpec((1, 1, tm), lambda i: (i, 0, 0)), _full((8, META_LANES))],
        out_shape=[jax.ShapeDtypeStruct((n // tm, 1, tm), jnp.int32),
                   jax.ShapeDtypeStruct((8, META_LANES), jnp.int32)],
        scratch_shapes=[pltpu.VMEM((BUCKET_ROWS, LANES), F32), pltpu.VMEM((BUCKET_ROWS, LANES), F32)],
        compiler_params=_params(("arbitrary",)),
        name="plan_pos",
    )(info, sel, cnt, ltri, utri)


def _dispatch_kernel(pos_ref, fill_ref, h1_ref, gain2_ref, sh2_ref, sc2_ref, info_ref, xs_ref,
                     rowbuf, zbuf, sem, zsem, *, tm, n_steps, n_tiles):
    step = pl.program_id(0) * pl.num_programs(1) + pl.program_id(1)
    slot = step % 2

    @pl.when(step == 0)
    def _():
        zbuf[...] = jnp.zeros_like(zbuf)
        fill = lambda j: pltpu.make_async_copy(zbuf, xs_ref.at[pl.ds(j * SORT_TILE, SORT_TILE), 0], zsem)
        for j in range(n_tiles):
            pl.when(fill_ref[0, j] == 1)(lambda j=j: fill(j).start())
        for j in range(n_tiles):
            pl.when(fill_ref[0, j] == 1)(lambda j=j: fill(j).wait())

    def wait(s):
        pltpu.make_async_copy(rowbuf.at[s], xs_ref.at[pl.ds(0, tm), 0], sem.at[s]).wait()

    def send(s):
        @pl.when(step >= 2)
        def _():
            wait(s)

        rowbuf[s, :, 0:D_MODEL] = _modulate(h1_ref[0], gain2_ref[...], sh2_ref[...], sc2_ref[...])
        rowbuf[s, :, D_MODEL:] = info_ref[0]
        for r in range(tm):
            pltpu.make_async_copy(rowbuf.at[s, pl.ds(r, 1)], xs_ref.at[pos_ref[0, 0, r]],
                                  sem.at[s]).start(priority=r % 2)

    for s in range(2):
        pl.when(slot == s)(functools.partial(send, s))

    @pl.when(step == n_steps - 1)
    def _():
        wait(slot)
        if n_steps > 1:
            wait(1 - slot)


def _gmoe_kernel(ea_ref, eb_ref, nu_ref, first_ref, wslot_ref, na_ref, nb_ref, xs_ref, wg_hbm, wu_hbm, wd_hbm,
                 ys_ref, xbuf, ybuf, zbuf, wg_buf, wu_buf, wd_buf, sem_in, sem_out, zsem, wsem, *, layer):
    j = pl.program_id(0)
    n_used = nu_ref[0]
    slot = j % 2
    wslot = wslot_ref[j]
    tile = lambda ref, t: ref.at[pl.ds(t * SORT_TILE, SORT_TILE), 0]
    in_copy = lambda t, s: pltpu.make_async_copy(tile(xs_ref, t), xbuf.at[s], sem_in.at[s])
    out_copy = lambda t, s: pltpu.make_async_copy(ybuf.at[s], tile(ys_ref, t), sem_out.at[s])

    def weight_copies(e_lo, e_hi, s):
        return [pltpu.make_async_copy(hbm.at[layer, e], buf.at[s, which], wsem.at[s])
                for hbm, buf in ((wg_hbm, wg_buf), (wu_hbm, wu_buf), (wd_hbm, wd_buf))
                for which, e in ((0, e_lo), (1, e_hi))]

    @pl.when(j == 0)
    def _():
        in_copy(0, 0).start()
        for c in weight_copies(ea_ref[0], eb_ref[0], 0):
            c.start()

    @pl.when(j + 1 < n_used)
    def _():
        in_copy(j + 1, 1 - slot).start()

    @pl.when(j < n_used)
    def _():
        @pl.when(first_ref[j] == 1)
        def _():
            for c in weight_copies(ea_ref[j], eb_ref[j], wslot):
                c.wait()
            for c in weight_copies(na_ref[j], nb_ref[j], 1 - wslot):
                c.start()

        in_copy(j, slot).wait()

        @pl.when(j >= 2)
        def _():
            out_copy(j - 2, slot).wait()

        x = xbuf[slot, :, 0:D_MODEL].astype(BF16)

        def expert(which, w):
            gt = jnp.dot(x, wg_buf[wslot, which].astype(BF16), preferred_element_type=F32)
            up = jnp.dot(x, wu_buf[wslot, which].astype(BF16), preferred_element_type=F32)
            h = (gt * jax.nn.sigmoid(gt)) * up * w
            return jnp.dot(h.astype(BF16), wd_buf[wslot, which].astype(BF16), preferred_element_type=F32)

        w_lo = xbuf[slot, :, D_MODEL + INFO_W_LO:D_MODEL + INFO_W_LO + 1]
        w_hi = xbuf[slot, :, D_MODEL + INFO_W_HI:D_MODEL + INFO_W_HI + 1]
        ybuf[slot] = expert(0, w_lo) + expert(1, w_hi)
        out_copy(j, slot).start()

        @pl.when(j == n_used - 1)
        def _():
            out_copy(j, slot).wait()

            @pl.when(j >= 1)
            def _():
                out_copy(j - 1, 1 - slot).wait()

            for c in weight_copies(ea_ref[j], eb_ref[j], 1 - wslot):
                c.wait()

    @pl.when(j >= n_used)
    def _():
        @pl.when(j == n_used)
        def _():
            zbuf[...] = jnp.zeros_like(zbuf)

        fill = pltpu.make_async_copy(zbuf, tile(ys_ref, j), zsem)
        fill.start()
        fill.wait()


def _fetch_sorted_rows(ys_ref, pos_ref, posn_ref, ybuf, sem, tm, n_steps):
    step = pl.program_id(0) * pl.num_programs(1) + pl.program_id(1)
    slot = step % 2

    def issue(p_ref, s):
        for r in range(tm):
            pltpu.make_async_copy(ys_ref.at[p_ref[0, 0, r]], ybuf.at[s, pl.ds(r, 1)],
                                  sem.at[s]).start(priority=r % 2)

    pl.when(step == 0)(functools.partial(issue, pos_ref, 0))
    for s in range(2):
        pl.when((step + 1 < n_steps) & (slot == s))(functools.partial(issue, posn_ref, 1 - s))
    pltpu.make_async_copy(ys_ref.at[pl.ds(0, tm), 0], ybuf.at[slot], sem.at[slot]).wait()
    return ybuf[slot]


def _combine_kernel(pos_ref, posn_ref, h1_ref, g2_ref, ys_ref, o_ref, ybuf, sem, *, tm, n_steps):
    rows = _fetch_sorted_rows(ys_ref, pos_ref, posn_ref, ybuf, sem, tm, n_steps)
    o_ref[0] = h1_ref[0] + g2_ref[...] * rows


def _experts_sorted(h1, info, cnt, mods, layer, gain2, w_gate, w_up, w_down, tm):
    bsz, seq, d = h1.shape
    nt = seq // tm
    n_steps = bsz * nt
    n = bsz * seq
    n_sorted = n + N_BUCKETS * SORT_TILE
    n_tiles = n_sorted // SORT_TILE
    assert n % SORT_TILE == 0 and n_tiles <= META_LANES
    pos, meta = _sort_plan(info.reshape(n, ROUTER_LANES), cnt, tm)
    tile = lambda w: pl.BlockSpec((1, tm, w), lambda b, i: (b, i, 0))
    any_spec = pl.BlockSpec(memory_space=pl.ANY)
    by_batch = lambda b, i: b

    xs = pl.pallas_call(
        functools.partial(_dispatch_kernel, tm=tm, n_steps=n_steps, n_tiles=n_tiles),
        grid=(bsz, nt),
        in_specs=[_pos_spec(tm, nt, n_steps, 0),
                  pl.BlockSpec((1, META_LANES), lambda b, i: (0, 0), memory_space=pltpu.SMEM),
                  tile(d), _full((1, d)), _mod_spec(layer, by_batch, 3),
                  _mod_spec(layer, by_batch, 4), tile(ROUTER_LANES)],
        out_specs=any_spec,
        out_shape=jax.ShapeDtypeStruct((n_sorted, 1, ROW_WIDTH), F32),
        scratch_shapes=[pltpu.VMEM((2, tm, ROW_WIDTH), F32), pltpu.VMEM((SORT_TILE, ROW_WIDTH), F32),
                        pltpu.SemaphoreType.DMA((2,)), pltpu.SemaphoreType.DMA(())],
        compiler_params=_params(("arbitrary", "arbitrary")),
        name=f"dispatch{layer}",
    )(pos, meta[META_FILL:META_FILL + 1], h1, gain2.reshape(1, d), mods, mods, info)

    ys = pl.pallas_call(
        functools.partial(_gmoe_kernel, layer=layer),
        grid_spec=pltpu.PrefetchScalarGridSpec(
            num_scalar_prefetch=7,
            grid=(n_tiles,),
            in_specs=[any_spec] * 4,
            out_specs=any_spec,
            scratch_shapes=[pltpu.VMEM((2, SORT_TILE, ROW_WIDTH), F32), pltpu.VMEM((2, SORT_TILE, d), F32),
                            pltpu.VMEM((SORT_TILE, d), F32),
                            pltpu.VMEM((2, 2, d, D_EXPERT), F32), pltpu.VMEM((2, 2, d, D_EXPERT), F32),
                            pltpu.VMEM((2, 2, D_EXPERT, d), F32),
                            pltpu.SemaphoreType.DMA((2,)), pltpu.SemaphoreType.DMA((2,)),
                            pltpu.SemaphoreType.DMA(()), pltpu.SemaphoreType.DMA((2,))]),
        out_shape=jax.ShapeDtypeStruct((n_sorted, 1, d), F32),
        compiler_params=_params(("arbitrary",)),
        name=f"experts{layer}",
    )(meta[META_EXPERT_LO], meta[META_EXPERT_HI], meta[META_N_USED, :1], meta[META_FIRST], meta[META_WSLOT],
      meta[META_NEXT_LO], meta[META_NEXT_HI], xs, w_gate, w_up, w_down)
    return ys, pos


def _combine(h1, ys, pos, mods, layer, tm):
    bsz, seq, d = h1.shape
    nt = seq // tm
    n_steps = bsz * nt
    tile = lambda w: pl.BlockSpec((1, tm, w), lambda b, i: (b, i, 0))
    return pl.pallas_call(
        functools.partial(_combine_kernel, tm=tm, n_steps=n_steps),
        grid=(bsz, nt),
        in_specs=[_pos_spec(tm, nt, n_steps, 0), _pos_spec(tm, nt, n_steps, 1), tile(d),
                  _mod_spec(layer, lambda b, i: b, 5), pl.BlockSpec(memory_space=pl.ANY)],
        out_specs=tile(d),
        out_shape=jax.ShapeDtypeStruct((bsz, seq, d), F32),
        scratch_shapes=[pltpu.VMEM((2, tm, d), F32), pltpu.SemaphoreType.DMA((2,))],
        compiler_params=_params(("arbitrary", "arbitrary")),
        name=f"combine{layer}",
    )(pos, pos, h1, mods, ys)


def kernel(x, c, ctx, c_ctx, mod_w, mod_b, norm1_g, norm2_g, even_w_in, q_gain, k_gain, pool_w, pool_scale,
           even_w_out, odd_w_in, sg_gain, sg_w, sg_b, conv_w, odd_w_out, router_g_w, router_g_b,
           router_e_w, router_e_b, w_gate, w_up, w_down):
    bsz, seq, d = x.shape
    tm = min(512, seq)
    cond = jnp.zeros((MOD_ROWS, d), F32).at[:bsz].set(c).at[bsz].set(c_ctx)
    mods = _adaln(cond, mod_w, mod_b).reshape(mod_w.shape[0], MOD_ROWS, 6, 1, d)

    q, k, v, p = _inproj0(x, mods, norm1_g[0], even_w_in[0], q_gain[0], k_gain[0], tm)
    kc, vc = _inproj0_ctx(ctx, mods, bsz, norm1_g[0], even_w_in[0][:, ATTN_WIDTH:ATTN_WIDTH + 2 * KV_WIDTH],
                          k_gain[0])
    o = _attention(q, k, v, kc, vc, tq=min(512, seq), tk=min(2048, seq))
    router0 = _router_operands(router_g_w[0], router_g_b[0], router_e_w[0], router_e_b[0])
    h1, info, cnt = _out0(o, p, x, mods, norm2_g[0], pool_w[0], pool_scale[0], even_w_out[0], router0, tm)
    ys, pos = _experts_sorted(h1, info, cnt, mods, 0, norm2_g[0], w_gate, w_up, w_down, tm)

    h, yc, z, bg = _inproj1(h1, ys, pos, mods, norm1_g[1], odd_w_in[0], sg_gain[0], sg_w[0], sg_b[0], tm)
    router1 = _router_operands(router_g_w[1], router_g_b[1], router_e_w[1], router_e_b[1])
    h1, info, cnt = _out1(yc, z, bg, h, mods, norm2_g[1], conv_w[0], odd_w_out[0], router1, tm)
    ys, pos = _experts_sorted(h1, info, cnt, mods, 1, norm2_g[1], w_gate, w_up, w_down, tm)
    return _combine(h1, ys, pos, mods, 1, tm)
```

```python
import functools

import jax
import jax.numpy as jnp
from jax import lax
from jax.experimental import pallas as pl
from jax.experimental.pallas import tpu as pltpu

F32 = jnp.float32
BF16 = jnp.bfloat16

D_MODEL = 1024
GRID_W = 64
EPS = 1e-6
N_Q_HEADS = 8
N_KV_HEADS = 2
HEAD_DIM = 64
Q_PER_KV = N_Q_HEADS // N_KV_HEADS
ATTN_WIDTH = N_Q_HEADS * HEAD_DIM
KV_WIDTH = N_KV_HEADS * HEAD_DIM
ROPE_THETA = 10000.0
POOL_WINDOWS = (2, 4, 8, 16)
POOL_GROUP = 128
POOL_WIDTH = POOL_GROUP * len(POOL_WINDOWS)
SG_GROUPS = 4
SG_CHUNK = 128
SG_WIDTH = 512
CONV_WIDTH = 512
EVEN_IN = ATTN_WIDTH + 2 * KV_WIDTH + POOL_WIDTH
ODD_IN = 2 * SG_WIDTH + 3 * CONV_WIDTH
N_GROUPS = 4
EXPERTS_PER_GROUP = 4
N_EXPERTS = 16
D_EXPERT = 256

Q_SCALE = HEAD_DIM ** -0.5 * 1.4426950408889634
SAFE_SOFTMAX_SHIFT = 60.0
LANES = 128
HALO = 16
ROUTER_LANES = 128
MOD_ROWS = 16
VMEM_LIMIT = 48 * 1024 * 1024

PAIRS_PER_GROUP = 6
N_BUCKETS = N_GROUPS * PAIRS_PER_GROUP
BUCKET_ROWS = 32
SORT_TILE = 512
META_LANES = 256
META_EXPERT_LO, META_EXPERT_HI, META_N_USED, META_FILL, META_FIRST, META_WSLOT, META_NEXT_LO, META_NEXT_HI = range(8)
ROW_WIDTH = D_MODEL + ROUTER_LANES
INFO_BUCKET, INFO_W_LO, INFO_W_HI = 0, 1, 2


def _params(sem):
    return pltpu.CompilerParams(dimension_semantics=sem, vmem_limit_bytes=VMEM_LIMIT)


def _modulate(x, gain, shift, scale):
    ms = jnp.mean(x * x, axis=-1, keepdims=True)
    return (x * lax.rsqrt(ms + EPS) * gain) * (1.0 + scale) + shift


def _mod_spec(layer, row_fn, which):
    return pl.BlockSpec((None, None, None, 1, D_MODEL),
                        lambda *idx: (layer, row_fn(*idx), which, 0, 0))


def _full(shape):
    return pl.BlockSpec(shape, lambda *idx: (0,) * len(shape))


def _adaln_kernel(c_ref, w_ref, b_ref, o_ref):
    c = c_ref[...]
    s = c * jax.nn.sigmoid(c)
    o_ref[0] = jnp.dot(s, w_ref[0], precision=lax.Precision.HIGHEST,
                       preferred_element_type=F32) + b_ref[0]


def _adaln(cond, mod_w, mod_b):
    depth, d, n = mod_w.shape
    tn = 1024
    return pl.pallas_call(
        _adaln_kernel,
        grid=(depth, n // tn),
        in_specs=[_full((MOD_ROWS, d)),
                  pl.BlockSpec((1, d, tn), lambda l, j: (l, 0, j)),
                  pl.BlockSpec((1, 1, tn), lambda l, j: (l, 0, j))],
        out_specs=pl.BlockSpec((1, MOD_ROWS, tn), lambda l, j: (l, 0, j)),
        out_shape=jax.ShapeDtypeStruct((depth, MOD_ROWS, n), F32),
        compiler_params=_params(("arbitrary", "arbitrary")),
        name="adaln",
    )(cond, mod_w, mod_b.reshape(depth, 1, n))


def _head_mean_square(z, ones_bd):
    sq = z * z
    hi = sq.astype(BF16)
    lo = (sq - hi.astype(F32)).astype(BF16)
    return jnp.dot(jnp.concatenate([hi, lo], axis=1), ones_bd, preferred_element_type=F32)


def _head_norm_rope(z, gain, ones_bd, cos, sin, first_half):
    zn = z * lax.rsqrt(_head_mean_square(z, ones_bd) + EPS) * gain
    partner = jnp.where(first_half, pltpu.roll(zn, LANES - 16, 1), pltpu.roll(zn, 16, 1))
    return zn * cos + partner * sin


def _inproj0_kernel(x_ref, gain_ref, sh_ref, sc_ref, w_ref, cos_ref, sin_ref, qg_ref, kg_ref, ones_ref,
                    q_ref, k_ref, v_ref, p_ref):
    a = _modulate(x_ref[0], gain_ref[...], sh_ref[...], sc_ref[...])
    y = jnp.dot(a.astype(BF16), w_ref[...], preferred_element_type=F32)
    cos, sin, ones_bd = cos_ref[...], sin_ref[...], ones_ref[...]
    lane = lax.broadcasted_iota(jnp.int32, cos.shape, 1)
    first_half = (lane % 32) < 16
    for s in range(ATTN_WIDTH // LANES):
        r = _head_norm_rope(y[:, s * LANES:(s + 1) * LANES], qg_ref[...], ones_bd, cos, sin, first_half)
        r = (r * Q_SCALE).astype(BF16)
        q_ref[0, 2 * s] = r[:, :HEAD_DIM]
        q_ref[0, 2 * s + 1] = r[:, HEAD_DIM:]
    kr = _head_norm_rope(y[:, ATTN_WIDTH:ATTN_WIDTH + KV_WIDTH], kg_ref[...], ones_bd, cos, sin,
                         first_half).astype(BF16)
    k_ref[0, 0] = kr[:, :HEAD_DIM]
    k_ref[0, 1] = kr[:, HEAD_DIM:]
    vv = y[:, ATTN_WIDTH + KV_WIDTH:ATTN_WIDTH + 2 * KV_WIDTH].astype(BF16)
    v_ref[0, 0] = vv[:, :HEAD_DIM]
    v_ref[0, 1] = vv[:, HEAD_DIM:]
    p_ref[0] = y[:, ATTN_WIDTH + 2 * KV_WIDTH:].astype(BF16)


def _inproj0_ctx_kernel(x_ref, gain_ref, sh_ref, sc_ref, w_ref, kg_ref, ones_ref, k_ref, v_ref):
    a = _modulate(x_ref[0], gain_ref[...], sh_ref[...], sc_ref[...])
    y = jnp.dot(a.astype(BF16), w_ref[...], preferred_element_type=F32)
    z = y[:, :KV_WIDTH]
    kr = (z * lax.rsqrt(_head_mean_square(z, ones_ref[...]) + EPS) * kg_ref[...]).astype(BF16)
    k_ref[0, 0] = kr[:, :HEAD_DIM]
    k_ref[0, 1] = kr[:, HEAD_DIM:]
    vv = y[:, KV_WIDTH:].astype(BF16)
    v_ref[0, 0] = vv[:, :HEAD_DIM]
    v_ref[0, 1] = vv[:, HEAD_DIM:]


def _rope_tables(seq):
    t = jnp.arange(seq)
    row = (t // GRID_W).astype(F32)
    col = (t % GRID_W).astype(F32)
    half = HEAD_DIM // 2
    inv = ROPE_THETA ** (-jnp.arange(0, half, 2, dtype=F32) / half)
    ar, ac = row[:, None] * inv, col[:, None] * inv
    cos = jnp.concatenate([jnp.cos(ar), jnp.cos(ar), jnp.cos(ac), jnp.cos(ac)], axis=-1)
    sin = jnp.concatenate([-jnp.sin(ar), jnp.sin(ar), -jnp.sin(ac), jnp.sin(ac)], axis=-1)
    return jnp.tile(cos, (1, LANES // HEAD_DIM)), jnp.tile(sin, (1, LANES // HEAD_DIM))


def _head_mean_matrix():
    r = jnp.arange(LANES)
    same = (r[:, None] // HEAD_DIM) == (r[None, :] // HEAD_DIM)
    block = jnp.where(same, 1.0 / HEAD_DIM, 0.0).astype(BF16)
    return jnp.concatenate([block, block], axis=0)


def _inproj0(x, mods, gain, w_in, q_gain, k_gain, tm):
    bsz, seq, d = x.shape
    cos, sin = _rope_tables(seq)
    qg = jnp.tile(q_gain, LANES // HEAD_DIM).reshape(1, LANES)
    kg = jnp.tile(k_gain, LANES // HEAD_DIM).reshape(1, LANES)
    head = lambda n: pl.BlockSpec((1, n, tm, HEAD_DIM), lambda b, i: (b, 0, i, 0))
    return pl.pallas_call(
        _inproj0_kernel,
        grid=(bsz, seq // tm),
        in_specs=[pl.BlockSpec((1, tm, d), lambda b, i: (b, i, 0)),
                  _full((1, d)),
                  _mod_spec(0, lambda b, i: b, 0),
                  _mod_spec(0, lambda b, i: b, 1),
                  _full((d, EVEN_IN)),
                  pl.BlockSpec((tm, LANES), lambda b, i: (i, 0)),
                  pl.BlockSpec((tm, LANES), lambda b, i: (i, 0)),
                  _full((1, LANES)), _full((1, LANES)), _full((2 * LANES, LANES))],
        out_specs=[head(N_Q_HEADS), head(N_KV_HEADS), head(N_KV_HEADS),
                   pl.BlockSpec((1, tm, POOL_WIDTH), lambda b, i: (b, i, 0))],
        out_shape=[jax.ShapeDtypeStruct((bsz, N_Q_HEADS, seq, HEAD_DIM), BF16),
                   jax.ShapeDtypeStruct((bsz, N_KV_HEADS, seq, HEAD_DIM), BF16),
                   jax.ShapeDtypeStruct((bsz, N_KV_HEADS, seq, HEAD_DIM), BF16),
                   jax.ShapeDtypeStruct((bsz, seq, POOL_WIDTH), BF16)],
        compiler_params=_params(("parallel", "parallel")),
        name="inproj0",
    )(x, gain.reshape(1, d), mods, mods, w_in.astype(BF16), cos, sin, qg, kg, _head_mean_matrix())


def _inproj0_ctx(ctx, mods, ctx_row, gain, w_kv, k_gain):
    bsz, n_ctx, d = ctx.shape
    kg = jnp.tile(k_gain, LANES // HEAD_DIM).reshape(1, LANES)
    head = pl.BlockSpec((1, N_KV_HEADS, n_ctx, HEAD_DIM), lambda b: (b, 0, 0, 0))
    return pl.pallas_call(
        _inproj0_ctx_kernel,
        grid=(bsz,),
        in_specs=[pl.BlockSpec((1, n_ctx, d), lambda b: (b, 0, 0)),
                  _full((1, d)),
                  _mod_spec(0, lambda b: ctx_row, 0),
                  _mod_spec(0, lambda b: ctx_row, 1),
                  _full((d, 2 * KV_WIDTH)),
                  _full((1, LANES)), _full((2 * LANES, LANES))],
        out_specs=[head, head],
        out_shape=[jax.ShapeDtypeStruct((bsz, N_KV_HEADS, n_ctx, HEAD_DIM), BF16)] * 2,
        compiler_params=_params(("parallel",)),
        name="inproj0_ctx",
    )(ctx, gain.reshape(1, d), mods, mods, w_kv.astype(BF16), kg, _head_mean_matrix())


def _attn_kernel(q_ref, kl_ref, vl_ref, kc_ref, vc_ref, o_ref, ksq_sc, *, tq, tk):
    rows = Q_PER_KV * tq
    q = q_ref[0].reshape(rows, HEAD_DIM)
    seq = kl_ref.shape[2]
    chunks = [(kc_ref, vc_ref, 0, kc_ref.shape[2])]
    chunks += [(kl_ref, vl_ref, c * tk, tk) for c in range(seq // tk)]
    scores = lambda k: lax.dot_general(q, k, (((1,), (1,)), ((), ())), preferred_element_type=F32)

    @pl.when(pl.program_id(2) == 0)
    def _():
        def largest_sq_norm(k_ref):
            kf = k_ref[0, 0].astype(F32)
            return jnp.max(jnp.sum(kf * kf, axis=1, keepdims=True), axis=0, keepdims=True)

        ksq_sc[...] = jnp.broadcast_to(jnp.maximum(largest_sq_norm(kl_ref), largest_sq_norm(kc_ref)),
                                       ksq_sc.shape)

    qf = q.astype(F32)
    bound = jnp.sqrt(jnp.sum(qf * qf, axis=1, keepdims=True) * ksq_sc[0:1, 0:1])
    safe = jnp.max(bound) <= SAFE_SOFTMAX_SHIFT

    def finish(acc, l):
        o = acc / l
        o_ref[0] = jnp.concatenate([o[h * tq:(h + 1) * tq] for h in range(Q_PER_KV)], axis=1).astype(BF16)

    @pl.when(safe)
    def _():
        l = jnp.zeros((rows, 1), F32)
        acc = jnp.zeros((rows, HEAD_DIM), F32)
        for k_ref, v_ref, start, size in chunks:
            p = jnp.exp2(scores(k_ref[0, 0, start:start + size, :]) - bound)
            l = l + jnp.sum(p, axis=1, keepdims=True)
            acc = acc + jnp.dot(p.astype(BF16), v_ref[0, 0, start:start + size, :], preferred_element_type=F32)
        finish(acc, l)

    @pl.when(jnp.logical_not(safe))
    def _():
        m = jnp.full((rows, 1), -jnp.inf, F32)
        l = jnp.zeros((rows, 1), F32)
        acc = jnp.zeros((rows, HEAD_DIM), F32)
        for k_ref, v_ref, start, size in chunks:
            s = scores(k_ref[0, 0, start:start + size, :])
            m_new = jnp.maximum(m, jnp.max(s, axis=1, keepdims=True))
            alpha = jnp.exp2(m - m_new)
            p = jnp.exp2(s - m_new)
            l = alpha * l + jnp.sum(p, axis=1, keepdims=True)
            acc = alpha * acc + jnp.dot(p.astype(BF16), v_ref[0, 0, start:start + size, :],
                                        preferred_element_type=F32)
            m = m_new
        finish(acc, l)


def _attention(q, k, v, kc, vc, tq, tk):
    bsz, _, seq, _ = q.shape
    n_ctx = kc.shape[2]
    kv_spec = lambda n: pl.BlockSpec((1, 1, n, HEAD_DIM), lambda b, g, i: (b, g, 0, 0))
    return pl.pallas_call(
        functools.partial(_attn_kernel, tq=tq, tk=tk),
        grid=(bsz, N_KV_HEADS, seq // tq),
        in_specs=[pl.BlockSpec((1, Q_PER_KV, tq, HEAD_DIM), lambda b, g, i: (b, g, i, 0)),
                  kv_spec(seq), kv_spec(seq), kv_spec(n_ctx), kv_spec(n_ctx)],
        out_specs=pl.BlockSpec((1, tq, Q_PER_KV * HEAD_DIM), lambda b, g, i: (b, i, g)),
        out_shape=jax.ShapeDtypeStruct((bsz, seq, ATTN_WIDTH), BF16),
        scratch_shapes=[pltpu.VMEM((8, LANES), F32)],
        compiler_params=_params(("arbitrary", "arbitrary", "arbitrary")),
        name="attention",
    )(q, k, v, kc, vc)


def _route(logits):
    lane = lax.broadcasted_iota(jnp.int32, logits.shape, 1).astype(F32)
    neg = -jnp.inf
    big = float(ROUTER_LANES)
    first_index = lambda mask: jnp.min(jnp.where(mask, lane, big), axis=1, keepdims=True)
    is_g = lane < N_GROUPS
    gm = jnp.max(jnp.where(is_g, logits, neg), axis=1, keepdims=True)
    gidx = first_index(is_g & (logits == gm))
    gden = jnp.sum(jnp.where(is_g, jnp.exp(logits - gm), 0.0), axis=1, keepdims=True)
    g_p = 1.0 / gden
    first = N_GROUPS + EXPERTS_PER_GROUP * gidx
    sel = (lane >= first) & (lane < first + EXPERTS_PER_GROUP)
    e1 = jnp.max(jnp.where(sel, logits, neg), axis=1, keepdims=True)
    i1 = first_index(sel & (logits == e1))
    rest = sel & (lane != i1)
    e2 = jnp.max(jnp.where(rest, logits, neg), axis=1, keepdims=True)
    i2 = first_index(rest & (logits == e2))
    p2 = jnp.exp(e2 - e1)
    w1 = g_p * (1.0 / (1.0 + p2))
    w2 = g_p * (p2 / (1.0 + p2))
    lo = jnp.minimum(i1, i2) - first
    hi = jnp.maximum(i1, i2) - first
    pair = jnp.where(lo == 0, hi - 1, jnp.where(lo == 1, hi + 1, PAIRS_PER_GROUP - 1.0))
    bucket = PAIRS_PER_GROUP * gidx + pair
    w_lo = jnp.where(i1 < i2, w1, w2)
    w_hi = jnp.where(i1 < i2, w2, w1)
    return jnp.where(lane == INFO_BUCKET, bucket,
                     jnp.where(lane == INFO_W_LO, w_lo, jnp.where(lane == INFO_W_HI, w_hi, 0.0)))


def _bucket_onehot(info, sel):
    brow = lax.dot_general(sel, info.astype(BF16), (((1,), (1,)), ((), ())),
                           preferred_element_type=F32)[0:1]
    bid = lax.broadcasted_iota(jnp.int32, (BUCKET_ROWS, info.shape[0]), 0)
    return (bid == brow.astype(jnp.int32)).astype(F32)


def _bucket_selector():
    return jnp.zeros((8, ROUTER_LANES), F32).at[0, INFO_BUCKET].set(1.0).astype(BF16)


def _tail(y, x_res, gate1, gain2, shift2, scale2, rw_both, rbias, sel, h1_ref, info_ref, cnt_ref):
    h1 = x_res + gate1 * y
    h1_ref[0] = h1
    t = _modulate(h1, gain2, shift2, scale2)
    t_hi = t.astype(BF16)
    t_lo = (t - t_hi.astype(F32)).astype(BF16)
    both = jnp.dot(t_hi, rw_both, preferred_element_type=F32)
    logits = (both[:, :ROUTER_LANES] + both[:, ROUTER_LANES:]
              + jnp.dot(t_lo, rw_both[:, :ROUTER_LANES], preferred_element_type=F32)) + rbias
    info = _route(logits)
    info_ref[0] = info

    @pl.when((pl.program_id(0) == 0) & (pl.program_id(1) == 0))
    def _():
        cnt_ref[...] = jnp.zeros_like(cnt_ref)

    cnt_ref[...] += jnp.sum(_bucket_onehot(info, sel), axis=1, keepdims=True)


def _router_operands(rg_w, rg_b, re_w, re_b):
    d = rg_w.shape[0]
    w = jnp.concatenate([rg_w, re_w, jnp.zeros((d, ROUTER_LANES - N_GROUPS - N_EXPERTS), F32)], axis=1)
    b = jnp.concatenate([rg_b, re_b, jnp.zeros((ROUTER_LANES - N_GROUPS - N_EXPERTS,), F32)])
    w_hi = w.astype(BF16)
    w_lo = (w - w_hi.astype(F32)).astype(BF16)
    return jnp.concatenate([w_hi, w_lo], axis=1), b.reshape(1, ROUTER_LANES)


def _fill_halo(buf, main_ref, prev_ref, next_ref, tm, i, n_tiles, halo=HALO):
    buf[halo:halo + tm] = main_ref[0].astype(F32)
    buf[0:halo] = jnp.where(i > 0, prev_ref[0].astype(F32), 0.0)
    buf[halo + tm:2 * halo + tm] = jnp.where(i < n_tiles - 1, next_ref[0].astype(F32), 0.0)


def _halo_specs(tm, seq, width, halo=HALO):
    per = tm // halo
    last = seq // halo - 1
    return [pl.BlockSpec((1, tm, width), lambda b, i: (b, i, 0)),
            pl.BlockSpec((1, halo, width), lambda b, i: (b, jnp.maximum(i * per - 1, 0), 0)),
            pl.BlockSpec((1, halo, width), lambda b, i: (b, jnp.minimum((i + 1) * per, last), 0))]


def _out0_kernel(o_ref, p_ref, pprev_ref, pnext_ref, x_ref, g1_ref, gain2_ref, sh2_ref, sc2_ref,
                 poolw_ref, pscale_ref, wout_ref, rw_ref, rb_ref, sel_ref,
                 h1_ref, info_ref, cnt_ref, pbuf, *, tm, seq):
    i = pl.program_id(1)
    _fill_halo(pbuf, p_ref, pprev_ref, pnext_ref, tm, i, seq // tm)
    pos = i * tm + lax.broadcasted_iota(jnp.int32, (tm, 1), 0)
    pooled = []
    for g, w in enumerate(POOL_WINDOWS):
        sl = slice(g * POOL_GROUP, (g + 1) * POOL_GROUP)
        acc = pbuf[HALO - w // 2:HALO - w // 2 + tm, sl]
        for j in range(1 - w // 2, w - w // 2):
            acc = acc + pbuf[HALO + j:HALO + j + tm, sl]
        lo = jnp.clip(pos - w // 2, 0, seq)
        hi = jnp.clip(pos + w - w // 2, 0, seq)
        mean = acc * (1.0 / (hi - lo).astype(F32))
        dlt = (mean - pbuf[HALO:HALO + tm, sl]).astype(BF16)
        pooled.append((jnp.dot(dlt, poolw_ref[g], preferred_element_type=F32) * pscale_ref[:, sl]).astype(BF16))
    mixed = jnp.concatenate([o_ref[0]] + pooled, axis=1)
    y = jnp.dot(mixed, wout_ref[...], preferred_element_type=F32)
    _tail(y, x_ref[0], g1_ref[...], gain2_ref[...], sh2_ref[...], sc2_ref[...],
          rw_ref[...], rb_ref[...], sel_ref[...], h1_ref, info_ref, cnt_ref)


def _tail_specs(layer, d):
    by_batch = lambda b, i: b
    ins = [_mod_spec(layer, by_batch, 2), _full((1, d)), _mod_spec(layer, by_batch, 3),
           _mod_spec(layer, by_batch, 4)]
    return ins


def _tail_outs(bsz, seq, d, tm):
    specs = [pl.BlockSpec((1, tm, d), lambda b, i: (b, i, 0)),
             pl.BlockSpec((1, tm, ROUTER_LANES), lambda b, i: (b, i, 0)),
             _full((BUCKET_ROWS, LANES))]
    shapes = [jax.ShapeDtypeStruct((bsz, seq, d), F32),
              jax.ShapeDtypeStruct((bsz, seq, ROUTER_LANES), F32),
              jax.ShapeDtypeStruct((BUCKET_ROWS, LANES), F32)]
    return specs, shapes


def _out0(o, p, x, mods, gain2, pool_w, pool_scale, w_out, router, tm):
    bsz, seq, d = x.shape
    rw_both, rb = router
    out_specs, out_shapes = _tail_outs(bsz, seq, d, tm)
    return pl.pallas_call(
        functools.partial(_out0_kernel, tm=tm, seq=seq),
        grid=(bsz, seq // tm),
        in_specs=[pl.BlockSpec((1, tm, ATTN_WIDTH), lambda b, i: (b, i, 0))]
        + _halo_specs(tm, seq, POOL_WIDTH)
        + [pl.BlockSpec((1, tm, d), lambda b, i: (b, i, 0))]
        + _tail_specs(0, d)
        + [_full(pool_w.shape), _full((1, POOL_WIDTH)), _full(w_out.shape),
           _full(rw_both.shape), _full(rb.shape), _full((8, ROUTER_LANES))],
        out_specs=out_specs,
        out_shape=out_shapes,
        scratch_shapes=[pltpu.VMEM((tm + 2 * HALO, POOL_WIDTH), F32)],
        compiler_params=_params(("arbitrary", "arbitrary")),
        name="out0",
    )(o, p, p, p, x, mods, gain2.reshape(1, d), mods, mods,
      pool_w.astype(BF16), pool_scale.reshape(1, POOL_WIDTH), w_out.astype(BF16), rw_both, rb,
      _bucket_selector())


def _inproj1_kernel(pos_ref, posn_ref, h1_ref, g2_ref, gain_ref, sh_ref, sc_ref, w_ref, sgg_ref, sgw_ref, sgb_ref,
                    ys_ref, h_ref, yc_ref, z_ref, bg_ref, ybuf, sem, *, tm, n_steps):
    x = h1_ref[0] + g2_ref[...] * _fetch_sorted_rows(ys_ref, pos_ref, posn_ref, ybuf, sem, tm, n_steps)
    h_ref[0] = x
    a = _modulate(x, gain_ref[...], sh_ref[...], sc_ref[...])
    y = jnp.dot(a.astype(BF16), w_ref[...], preferred_element_type=F32)
    for g in range(SG_GROUPS):
        sl = slice(g * LANES, (g + 1) * LANES)
        u = y[:, sl]
        vg = y[:, SG_WIDTH + g * LANES:SG_WIDTH + (g + 1) * LANES]
        ms = jnp.mean(vg * vg, axis=-1, keepdims=True)
        vn = (vg * lax.rsqrt(ms + EPS) * sgg_ref[:, sl]).astype(BF16)
        for c in range(tm // SG_CHUNK):
            rows = slice(c * SG_CHUNK, (c + 1) * SG_CHUNK)
            s = jnp.dot(sgw_ref[g], vn[rows], preferred_element_type=F32) + sgb_ref[g]
            yc_ref[0, rows, sl] = (u[rows] * s).astype(BF16)
    hx = y[:, 2 * SG_WIDTH:2 * SG_WIDTH + CONV_WIDTH]
    bg_ref[0] = y[:, 2 * SG_WIDTH + CONV_WIDTH:2 * SG_WIDTH + 2 * CONV_WIDTH].astype(BF16)
    cg = y[:, 2 * SG_WIDTH + 2 * CONV_WIDTH:]
    z_ref[0] = (cg * hx).astype(BF16)


def _pos_spec(tm, nt, n_steps, ahead):
    return pl.BlockSpec((1, 1, tm), lambda b, i: (jnp.minimum(b * nt + i + ahead, n_steps - 1), 0, 0),
                        memory_space=pltpu.SMEM)


def _inproj1(h1, ys, pos, mods, gain, w_in, sg_gain, sg_w, sg_b, tm):
    bsz, seq, d = h1.shape
    nt = seq // tm
    n_steps = bsz * nt
    sgb = jnp.broadcast_to(sg_b[:, :, None], (SG_GROUPS, SG_CHUNK, LANES))
    by_batch = lambda b, i: b
    wide = lambda w, dt: (pl.BlockSpec((1, tm, w), lambda b, i: (b, i, 0)),
                          jax.ShapeDtypeStruct((bsz, seq, w), dt))
    outs = [wide(d, F32), wide(SG_WIDTH, BF16), wide(CONV_WIDTH, BF16), wide(CONV_WIDTH, BF16)]
    return pl.pallas_call(
        functools.partial(_inproj1_kernel, tm=tm, n_steps=n_steps),
        grid=(bsz, nt),
        in_specs=[_pos_spec(tm, nt, n_steps, 0), _pos_spec(tm, nt, n_steps, 1),
                  pl.BlockSpec((1, tm, d), lambda b, i: (b, i, 0)),
                  _mod_spec(0, by_batch, 5),
                  _full((1, d)),
                  _mod_spec(1, by_batch, 0),
                  _mod_spec(1, by_batch, 1),
                  _full((d, ODD_IN)),
                  _full((1, SG_WIDTH)), _full(sg_w.shape), _full(sgb.shape),
                  pl.BlockSpec(memory_space=pl.ANY)],
        out_specs=[s for s, _ in outs],
        out_shape=[s for _, s in outs],
        scratch_shapes=[pltpu.VMEM((2, tm, d), F32), pltpu.SemaphoreType.DMA((2,))],
        compiler_params=_params(("arbitrary", "arbitrary")),
        name="inproj1",
    )(pos, pos, h1, mods, gain.reshape(1, d), mods, mods, w_in.astype(BF16), sg_gain.reshape(1, SG_WIDTH),
      sg_w.astype(BF16), sgb, ys)


def _out1_kernel(yc_ref, z_ref, zprev_ref, znext_ref, bg_ref, x_ref, g1_ref, gain2_ref, sh2_ref, sc2_ref,
                 convw_ref, wout_ref, rw_ref, rb_ref, sel_ref,
                 h1_ref, info_ref, cnt_ref, zbuf, *, tm, seq):
    i = pl.program_id(1)
    _fill_halo(zbuf, z_ref, zprev_ref, znext_ref, tm, i, seq // tm, HALO)
    zc = (zbuf[HALO - 1:HALO - 1 + tm] * convw_ref[0:1, :]
          + zbuf[HALO:HALO + tm] * convw_ref[1:2, :]
          + zbuf[HALO + 1:HALO + 1 + tm] * convw_ref[2:3, :])
    yd = (bg_ref[0].astype(F32) * zc).astype(BF16)
    y = (jnp.dot(yc_ref[0], wout_ref[0:SG_WIDTH, :], preferred_element_type=F32)
         + jnp.dot(yd, wout_ref[SG_WIDTH:, :], preferred_element_type=F32))
    _tail(y, x_ref[0], g1_ref[...], gain2_ref[...], sh2_ref[...], sc2_ref[...],
          rw_ref[...], rb_ref[...], sel_ref[...], h1_ref, info_ref, cnt_ref)


def _out1(yc, z, bg, x, mods, gain2, conv_w, w_out, router, tm):
    bsz, seq, d = x.shape
    rw_both, rb = router
    out_specs, out_shapes = _tail_outs(bsz, seq, d, tm)
    wide = pl.BlockSpec((1, tm, CONV_WIDTH), lambda b, i: (b, i, 0))
    return pl.pallas_call(
        functools.partial(_out1_kernel, tm=tm, seq=seq),
        grid=(bsz, seq // tm),
        in_specs=[wide] + _halo_specs(tm, seq, CONV_WIDTH, HALO) + [wide]
        + [pl.BlockSpec((1, tm, d), lambda b, i: (b, i, 0))]
        + _tail_specs(1, d)
        + [_full((3, CONV_WIDTH)), _full(w_out.shape),
           _full(rw_both.shape), _full(rb.shape), _full((8, ROUTER_LANES))],
        out_specs=out_specs,
        out_shape=out_shapes,
        scratch_shapes=[pltpu.VMEM((tm + 2 * HALO, CONV_WIDTH), F32)],
        compiler_params=_params(("arbitrary", "arbitrary")),
        name="out1",
    )(yc, z, z, z, bg, x, mods, gain2.reshape(1, d), mods, mods,
      conv_w.reshape(3, CONV_WIDTH), w_out.astype(BF16), rw_both, rb, _bucket_selector())


def _plan_pos_kernel(info_ref, sel_ref, cnt_ref, ltri_ref, utri_ref, pos_ref, meta_ref, start_sc, run_sc):
    @pl.when(pl.program_id(0) == 0)
    def _():
        padded = jnp.ceil(cnt_ref[...] * (1.0 / SORT_TILE)) * SORT_TILE
        incl = jnp.dot(ltri_ref[...], padded, precision=lax.Precision.HIGHEST, preferred_element_type=F32)
        start_sc[...] = incl - padded
        run_sc[...] = jnp.zeros_like(run_sc)
        ends = jnp.broadcast_to(incl[:, 0:1], (BUCKET_ROWS, META_LANES))
        bid = lax.broadcasted_iota(jnp.int32, ends.shape, 0)
        tile = lax.broadcasted_iota(jnp.int32, (1, META_LANES), 1)

        def bucket_of(row0):
            done = jnp.where((bid < N_BUCKETS) & (ends <= row0), 1.0, 0.0)
            return jnp.minimum(jnp.sum(done, axis=0, keepdims=True), N_BUCKETS - 1.0).astype(jnp.int32)

        def experts_of(bucket):
            grp = ((bucket >= PAIRS_PER_GROUP).astype(jnp.int32)
                   + (bucket >= 2 * PAIRS_PER_GROUP).astype(jnp.int32)
                   + (bucket >= 3 * PAIRS_PER_GROUP).astype(jnp.int32))
            pair = bucket - PAIRS_PER_GROUP * grp
            lo = (pair >= 3).astype(jnp.int32) + (pair >= 5).astype(jnp.int32)
            hi = jnp.where(pair == 0, 1, jnp.where((pair == 1) | (pair == 3), 2, 3))
            return EXPERTS_PER_GROUP * grp + lo, EXPERTS_PER_GROUP * grp + hi

        row0 = (tile * SORT_TILE).astype(F32)
        tb = bucket_of(row0)
        n_used = (incl[N_BUCKETS - 1:N_BUCKETS, 0:1] * (1.0 / SORT_TILE)).astype(jnp.int32)
        fill = (tile >= n_used - 1) | (tb != bucket_of(row0 + SORT_TILE))
        first = (tile == 0) | (tb != bucket_of(row0 - SORT_TILE))
        own_end = jnp.sum(jnp.where(bid == tb, ends, 0.0), axis=0, keepdims=True)
        nonempty = jnp.broadcast_to(padded[:, 0:1], ends.shape) > 0.0
        ordinal = jnp.sum(jnp.where((bid < tb) & nonempty, 1.0, 0.0), axis=0, keepdims=True)
        rows = [None] * 8
        rows[META_EXPERT_LO], rows[META_EXPERT_HI] = experts_of(tb)
        rows[META_N_USED] = jnp.broadcast_to(n_used, (1, META_LANES))
        rows[META_FILL] = fill.astype(jnp.int32)
        rows[META_FIRST] = first.astype(jnp.int32)
        rows[META_WSLOT] = (ordinal - 2.0 * jnp.floor(ordinal * 0.5)).astype(jnp.int32)
        rows[META_NEXT_LO], rows[META_NEXT_HI] = experts_of(bucket_of(own_end))
        for r, row in enumerate(rows):
            meta_ref[r:r + 1, :] = row

    tm = utri_ref.shape[0]
    base = start_sc[:, 0:1] + run_sc[:, 0:1]
    for k in range(pos_ref.shape[0]):
        oh = _bucket_onehot(info_ref[k * tm:(k + 1) * tm], sel_ref[...])
        before = jnp.dot(oh.astype(BF16), utri_ref[...], preferred_element_type=F32)
        pos_ref[k] = jnp.sum(oh * (before + base), axis=0, keepdims=True).astype(jnp.int32)
        base = base + jnp.sum(oh, axis=1, keepdims=True)
    run_sc[...] = jnp.broadcast_to(base - start_sc[:, 0:1], run_sc.shape)


def _sort_plan(info, cnt, tm):
    n = info.shape[0]
    sel = _bucket_selector()
    sub = 4 if (n // tm) % 4 == 0 else 1
    info_spec = pl.BlockSpec((sub * tm, ROUTER_LANES), lambda i: (i, 0))
    r = jnp.arange(BUCKET_ROWS)
    ltri = (r[:, None] >= r[None, :]).astype(F32)
    t = jnp.arange(tm)
    utri = (t[:, None] < t[None, :]).astype(BF16)
    return pl.pallas_call(
        _plan_pos_kernel,
        grid=(n // (sub * tm),),
        in_specs=[info_spec, _full(sel.shape), _full(cnt.shape), _full(ltri.shape), _full(utri.shape)],
        out_specs=[pl.BlockSpec((sub, 1, tm), lambda i: (i, 0, 0)), _full((8, META_LANES))],
        out_shape=[jax.ShapeDtypeStruct((n // tm, 1, tm), jnp.int32),
                   jax.ShapeDtypeStruct((8, META_LANES), jnp.int32)],
        scratch_shapes=[pltpu.VMEM((BUCKET_ROWS, LANES), F32), pltpu.VMEM((BUCKET_ROWS, LANES), F32)],
        compiler_params=_params(("arbitrary",)),
        name="plan_pos",
    )(info, sel, cnt, ltri, utri)


def _dispatch_kernel(pos_ref, fill_ref, h1_ref, gain2_ref, sh2_ref, sc2_ref, info_ref, xs_ref,
                     rowbuf, zbuf, sem, zsem, *, tm, n_steps, n_tiles):
    step = pl.program_id(0) * pl.num_programs(1) + pl.program_id(1)
    slot = step % 2

    @pl.when(step == 0)
    def _():
        zbuf[...] = jnp.zeros_like(zbuf)
        fill = lambda j: pltpu.make_async_copy(zbuf, xs_ref.at[pl.ds(j * SORT_TILE, SORT_TILE), 0], zsem)
        for j in range(n_tiles):
            pl.when(fill_ref[0, j] == 1)(lambda j=j: fill(j).start())
        for j in range(n_tiles):
            pl.when(fill_ref[0, j] == 1)(lambda j=j: fill(j).wait())

    def wait(s):
        pltpu.make_async_copy(rowbuf.at[s], xs_ref.at[pl.ds(0, tm), 0], sem.at[s]).wait()

    def send(s):
        @pl.when(step >= 2)
        def _():
            wait(s)

        rowbuf[s, :, 0:D_MODEL] = _modulate(h1_ref[0], gain2_ref[...], sh2_ref[...], sc2_ref[...])
        rowbuf[s, :, D_MODEL:] = info_ref[0]
        for r in range(tm):
            pltpu.make_async_copy(rowbuf.at[s, pl.ds(r, 1)], xs_ref.at[pos_ref[0, 0, r]],
                                  sem.at[s]).start(priority=r % 2)

    for s in range(2):
        pl.when(slot == s)(functools.partial(send, s))

    @pl.when(step == n_steps - 1)
    def _():
        wait(slot)
        if n_steps > 1:
            wait(1 - slot)


def _gmoe_kernel(ea_ref, eb_ref, nu_ref, first_ref, wslot_ref, na_ref, nb_ref, xs_ref, wg_hbm, wu_hbm, wd_hbm,
                 ys_ref, xbuf, ybuf, zbuf, wg_buf, wu_buf, wd_buf, sem_in, sem_out, zsem, wsem, *, layer):
    j = pl.program_id(0)
    n_used = nu_ref[0]
    slot = j % 2
    wslot = wslot_ref[j]
    tile = lambda ref, t: ref.at[pl.ds(t * SORT_TILE, SORT_TILE), 0]
    in_copy = lambda t, s: pltpu.make_async_copy(tile(xs_ref, t), xbuf.at[s], sem_in.at[s])
    out_copy = lambda t, s: pltpu.make_async_copy(ybuf.at[s], tile(ys_ref, t), sem_out.at[s])

    def weight_copies(e_lo, e_hi, s):
        return [pltpu.make_async_copy(hbm.at[layer, e], buf.at[s, which], wsem.at[s])
                for hbm, buf in ((wg_hbm, wg_buf), (wu_hbm, wu_buf), (wd_hbm, wd_buf))
                for which, e in ((0, e_lo), (1, e_hi))]

    @pl.when(j == 0)
    def _():
        in_copy(0, 0).start()
        for c in weight_copies(ea_ref[0], eb_ref[0], 0):
            c.start()

    @pl.when(j + 1 < n_used)
    def _():
        in_copy(j + 1, 1 - slot).start()

    @pl.when(j < n_used)
    def _():
        @pl.when(first_ref[j] == 1)
        def _():
            for c in weight_copies(ea_ref[j], eb_ref[j], wslot):
                c.wait()
            for c in weight_copies(na_ref[j], nb_ref[j], 1 - wslot):
                c.start()

        in_copy(j, slot).wait()

        @pl.when(j >= 2)
        def _():
            out_copy(j - 2, slot).wait()

        x = xbuf[slot, :, 0:D_MODEL].astype(BF16)

        def expert(which, w):
            gt = jnp.dot(x, wg_buf[wslot, which].astype(BF16), preferred_element_type=F32)
            up = jnp.dot(x, wu_buf[wslot, which].astype(BF16), preferred_element_type=F32)
            h = (gt * jax.nn.sigmoid(gt)) * up * w
            return jnp.dot(h.astype(BF16), wd_buf[wslot, which].astype(BF16), preferred_element_type=F32)

        w_lo = xbuf[slot, :, D_MODEL + INFO_W_LO:D_MODEL + INFO_W_LO + 1]
        w_hi = xbuf[slot, :, D_MODEL + INFO_W_HI:D_MODEL + INFO_W_HI + 1]
        ybuf[slot] = expert(0, w_lo) + expert(1, w_hi)
        out_copy(j, slot).start()

        @pl.when(j == n_used - 1)
        def _():
            out_copy(j, slot).wait()

            @pl.when(j >= 1)
            def _():
                out_copy(j - 1, 1 - slot).wait()

            for c in weight_copies(ea_ref[j], eb_ref[j], 1 - wslot):
                c.wait()

    @pl.when(j >= n_used)
    def _():
        @pl.when(j == n_used)
        def _():
            zbuf[...] = jnp.zeros_like(zbuf)

        fill = pltpu.make_async_copy(zbuf, tile(ys_ref, j), zsem)
        fill.start()
        fill.wait()


def _fetch_sorted_rows(ys_ref, pos_ref, posn_ref, ybuf, sem, tm, n_steps):
    step = pl.program_id(0) * pl.num_programs(1) + pl.program_id(1)
    slot = step % 2

    def issue(p_ref, s):
        for r in range(tm):
            pltpu.make_async_copy(ys_ref.at[p_ref[0, 0, r]], ybuf.at[s, pl.ds(r, 1)],
                                  sem.at[s]).start(priority=r % 2)

    pl.when(step == 0)(functools.partial(issue, pos_ref, 0))
    for s in range(2):
        pl.when((step + 1 < n_steps) & (slot == s))(functools.partial(issue, posn_ref, 1 - s))
    pltpu.make_async_copy(ys_ref.at[pl.ds(0, tm), 0], ybuf.at[slot], sem.at[slot]).wait()
    return ybuf[slot]


def _combine_kernel(pos_ref, posn_ref, h1_ref, g2_ref, ys_ref, o_ref, ybuf, sem, *, tm, n_steps):
    rows = _fetch_sorted_rows(ys_ref, pos_ref, posn_ref, ybuf, sem, tm, n_steps)
    o_ref[0] = h1_ref[0] + g2_ref[...] * rows


def _experts_sorted(h1, info, cnt, mods, layer, gain2, w_gate, w_up, w_down, tm):
    bsz, seq, d = h1.shape
    nt = seq // tm
    n_steps = bsz * nt
    n = bsz * seq
    n_sorted = n + N_BUCKETS * SORT_TILE
    n_tiles = n_sorted // SORT_TILE
    assert n % SORT_TILE == 0 and n_tiles <= META_LANES
    pos, meta = _sort_plan(info.reshape(n, ROUTER_LANES), cnt, tm)
    tile = lambda w: pl.BlockSpec((1, tm, w), lambda b, i: (b, i, 0))
    any_spec = pl.BlockSpec(memory_space=pl.ANY)
    by_batch = lambda b, i: b

    xs = pl.pallas_call(
        functools.partial(_dispatch_kernel, tm=tm, n_steps=n_steps, n_tiles=n_tiles),
        grid=(bsz, nt),
        in_specs=[_pos_spec(tm, nt, n_steps, 0),
                  pl.BlockSpec((1, META_LANES), lambda b, i: (0, 0), memory_space=pltpu.SMEM),
                  tile(d), _full((1, d)), _mod_spec(layer, by_batch, 3),
                  _mod_spec(layer, by_batch, 4), tile(ROUTER_LANES)],
        out_specs=any_spec,
        out_shape=jax.ShapeDtypeStruct((n_sorted, 1, ROW_WIDTH), F32),
        scratch_shapes=[pltpu.VMEM((2, tm, ROW_WIDTH), F32), pltpu.VMEM((SORT_TILE, ROW_WIDTH), F32),
                        pltpu.SemaphoreType.DMA((2,)), pltpu.SemaphoreType.DMA(())],
        compiler_params=_params(("arbitrary", "arbitrary")),
        name=f"dispatch{layer}",
    )(pos, meta[META_FILL:META_FILL + 1], h1, gain2.reshape(1, d), mods, mods, info)

    ys = pl.pallas_call(
        functools.partial(_gmoe_kernel, layer=layer),
        grid_spec=pltpu.PrefetchScalarGridSpec(
            num_scalar_prefetch=7,
            grid=(n_tiles,),
            in_specs=[any_spec] * 4,
            out_specs=any_spec,
            scratch_shapes=[pltpu.VMEM((2, SORT_TILE, ROW_WIDTH), F32), pltpu.VMEM((2, SORT_TILE, d), F32),
                            pltpu.VMEM((SORT_TILE, d), F32),
                            pltpu.VMEM((2, 2, d, D_EXPERT), F32), pltpu.VMEM((2, 2, d, D_EXPERT), F32),
                            pltpu.VMEM((2, 2, D_EXPERT, d), F32),
                            pltpu.SemaphoreType.DMA((2,)), pltpu.SemaphoreType.DMA((2,)),
                            pltpu.SemaphoreType.DMA(()), pltpu.SemaphoreType.DMA((2,))]),
        out_shape=jax.ShapeDtypeStruct((n_sorted, 1, d), F32),
        compiler_params=_params(("arbitrary",)),
        name=f"experts{layer}",
    )(meta[META_EXPERT_LO], meta[META_EXPERT_HI], meta[META_N_USED, :1], meta[META_FIRST], meta[META_WSLOT],
      meta[META_NEXT_LO], meta[META_NEXT_HI], xs, w_gate, w_up, w_down)
    return ys, pos


def _combine(h1, ys, pos, mods, layer, tm):
    bsz, seq, d = h1.shape
    nt = seq // tm
    n_steps = bsz * nt
    tile = lambda w: pl.BlockSpec((1, tm, w), lambda b, i: (b, i, 0))
    return pl.pallas_call(
        functools.partial(_combine_kernel, tm=tm, n_steps=n_steps),
        grid=(bsz, nt),
        in_specs=[_pos_spec(tm, nt, n_steps, 0), _pos_spec(tm, nt, n_steps, 1), tile(d),
                  _mod_spec(layer, lambda b, i: b, 5), pl.BlockSpec(memory_space=pl.ANY)],
        out_specs=tile(d),
        out_shape=jax.ShapeDtypeStruct((bsz, seq, d), F32),
        scratch_shapes=[pltpu.VMEM((2, tm, d), F32), pltpu.SemaphoreType.DMA((2,))],
        compiler_params=_params(("arbitrary", "arbitrary")),
        name=f"combine{layer}",
    )(pos, pos, h1, mods, ys)


def kernel(x, c, ctx, c_ctx, mod_w, mod_b, norm1_g, norm2_g, even_w_in, q_gain, k_gain, pool_w, pool_scale,
           even_w_out, odd_w_in, sg_gain, sg_w, sg_b, conv_w, odd_w_out, router_g_w, router_g_b,
           router_e_w, router_e_b, w_gate, w_up, w_down):
    bsz, seq, d = x.shape
    tm = min(512, seq)
    cond = jnp.zeros((MOD_ROWS, d), F32).at[:bsz].set(c).at[bsz].set(c_ctx)
    mods = _adaln(cond, mod_w, mod_b).reshape(mod_w.shape[0], MOD_ROWS, 6, 1, d)

    q, k, v, p = _inproj0(x, mods, norm1_g[0], even_w_in[0], q_gain[0], k_gain[0], tm)
    kc, vc = _inproj0_ctx(ctx, mods, bsz, norm1_g[0], even_w_in[0][:, ATTN_WIDTH:ATTN_WIDTH + 2 * KV_WIDTH],
                          k_gain[0])
    o = _attention(q, k, v, kc, vc, tq=min(512, seq), tk=min(2048, seq))
    router0 = _router_operands(router_g_w[0], router_g_b[0], router_e_w[0], router_e_b[0])
    h1, info, cnt = _out0(o, p, x, mods, norm2_g[0], pool_w[0], pool_scale[0], even_w_out[0], router0, tm)
    ys, pos = _experts_sorted(h1, info, cnt, mods, 0, norm2_g[0], w_gate, w_up, w_down, tm)

    h, yc, z, bg = _inproj1(h1, ys, pos, mods, norm1_g[1], odd_w_in[0], sg_gain[0], sg_w[0], sg_b[0], tm)
    router1 = _router_operands(router_g_w[1], router_g_b[1], router_e_w[1], router_e_b[1])
    h1, info, cnt = _out1(yc, z, bg, h, mods, norm2_g[1], conv_w[0], odd_w_out[0], router1, tm)
    ys, pos = _experts_sorted(h1, info, cnt, mods, 1, norm2_g[1], w_gate, w_up, w_down, tm)
    return _combine(h1, ys, pos, mods, 1, tm)
```

```python
import functools

import jax
import jax.numpy as jnp
from jax import lax
from jax.experimental import pallas as pl
from jax.experimental.pallas import tpu as pltpu

F32 = jnp.float32
BF16 = jnp.bfloat16

D_MODEL = 1024
GRID_W = 64
EPS = 1e-6
N_Q_HEADS = 8
N_KV_HEADS = 2
HEAD_DIM = 64
Q_PER_KV = N_Q_HEADS // N_KV_HEADS
ATTN_WIDTH = N_Q_HEADS * HEAD_DIM
KV_WIDTH = N_KV_HEADS * HEAD_DIM
ROPE_THETA = 10000.0
POOL_WINDOWS = (2, 4, 8, 16)
POOL_GROUP = 128
POOL_WIDTH = POOL_GROUP * len(POOL_WINDOWS)
SG_GROUPS = 4
SG_CHUNK = 128
SG_WIDTH = 512
CONV_WIDTH = 512
EVEN_IN = ATTN_WIDTH + 2 * KV_WIDTH + POOL_WIDTH
ODD_IN = 2 * SG_WIDTH + 3 * CONV_WIDTH
N_GROUPS = 4
EXPERTS_PER_GROUP = 4
N_EXPERTS = 16
D_EXPERT = 256

Q_SCALE = HEAD_DIM ** -0.5 * 1.4426950408889634
SAFE_SOFTMAX_SHIFT = 60.0
LANES = 128
HALO = 16
ROUTER_LANES = 128
MOD_ROWS = 16
VMEM_LIMIT = 48 * 1024 * 1024

PAIRS_PER_GROUP = 6
N_BUCKETS = N_GROUPS * PAIRS_PER_GROUP
BUCKET_ROWS = 32
SORT_TILE = 512
META_LANES = 256
META_EXPERT_LO, META_EXPERT_HI, META_N_USED, META_FILL, META_FIRST, META_WSLOT, META_NEXT_LO, META_NEXT_HI = range(8)
ROW_WIDTH = D_MODEL + ROUTER_LANES
INFO_BUCKET, INFO_W_LO, INFO_W_HI = 0, 1, 2


def _params(sem):
    return pltpu.CompilerParams(dimension_semantics=sem, vmem_limit_bytes=VMEM_LIMIT)


def _modulate(x, gain, shift, scale):
    ms = jnp.mean(x * x, axis=-1, keepdims=True)
    return (x * lax.rsqrt(ms + EPS) * gain) * (1.0 + scale) + shift


def _mod_spec(layer, row_fn, which):
    return pl.BlockSpec((None, None, None, 1, D_MODEL),
                        lambda *idx: (layer, row_fn(*idx), which, 0, 0))


def _full(shape):
    return pl.BlockSpec(shape, lambda *idx: (0,) * len(shape))


def _adaln_kernel(c_ref, w_ref, b_ref, o_ref):
    c = c_ref[...]
    s = c * jax.nn.sigmoid(c)
    o_ref[0] = jnp.dot(s, w_ref[0], precision=lax.Precision.HIGHEST,
                       preferred_element_type=F32) + b_ref[0]


def _adaln(cond, mod_w, mod_b):
    depth, d, n = mod_w.shape
    tn = 1024
    return pl.pallas_call(
        _adaln_kernel,
        grid=(depth, n // tn),
        in_specs=[_full((MOD_ROWS, d)),
                  pl.BlockSpec((1, d, tn), lambda l, j: (l, 0, j)),
                  pl.BlockSpec((1, 1, tn), lambda l, j: (l, 0, j))],
        out_specs=pl.BlockSpec((1, MOD_ROWS, tn), lambda l, j: (l, 0, j)),
        out_shape=jax.ShapeDtypeStruct((depth, MOD_ROWS, n), F32),
        compiler_params=_params(("arbitrary", "arbitrary")),
        name="adaln",
    )(cond, mod_w, mod_b.reshape(depth, 1, n))


def _head_mean_square(z, ones_bd):
    sq = z * z
    hi = sq.astype(BF16)
    lo = (sq - hi.astype(F32)).astype(BF16)
    return jnp.dot(jnp.concatenate([hi, lo], axis=1), ones_bd, preferred_element_type=F32)


def _head_norm_rope(z, gain, ones_bd, cos, sin, first_half):
    zn = z * lax.rsqrt(_head_mean_square(z, ones_bd) + EPS) * gain
    partner = jnp.where(first_half, pltpu.roll(zn, LANES - 16, 1), pltpu.roll(zn, 16, 1))
    return zn * cos + partner * sin


def _inproj0_kernel(x_ref, gain_ref, sh_ref, sc_ref, w_ref, cos_ref, sin_ref, qg_ref, kg_ref, ones_ref,
                    q_ref, k_ref, v_ref, p_ref):
    a = _modulate(x_ref[0], gain_ref[...], sh_ref[...], sc_ref[...])
    y = jnp.dot(a.astype(BF16), w_ref[...], preferred_element_type=F32)
    cos, sin, ones_bd = cos_ref[...], sin_ref[...], ones_ref[...]
    lane = lax.broadcasted_iota(jnp.int32, cos.shape, 1)
    first_half = (lane % 32) < 16
    for s in range(ATTN_WIDTH // LANES):
        r = _head_norm_rope(y[:, s * LANES:(s + 1) * LANES], qg_ref[...], ones_bd, cos, sin, first_half)
        r = (r * Q_SCALE).astype(BF16)
        q_ref[0, 2 * s] = r[:, :HEAD_DIM]
        q_ref[0, 2 * s + 1] = r[:, HEAD_DIM:]
    kr = _head_norm_rope(y[:, ATTN_WIDTH:ATTN_WIDTH + KV_WIDTH], kg_ref[...], ones_bd, cos, sin,
                         first_half).astype(BF16)
    k_ref[0, 0] = kr[:, :HEAD_DIM]
    k_ref[0, 1] = kr[:, HEAD_DIM:]
    vv = y[:, ATTN_WIDTH + KV_WIDTH:ATTN_WIDTH + 2 * KV_WIDTH].astype(BF16)
    v_ref[0, 0] = vv[:, :HEAD_DIM]
    v_ref[0, 1] = vv[:, HEAD_DIM:]
    p_ref[0] = y[:, ATTN_WIDTH + 2 * KV_WIDTH:].astype(BF16)


def _inproj0_ctx_kernel(x_ref, gain_ref, sh_ref, sc_ref, w_ref, kg_ref, ones_ref, k_ref, v_ref):
    a = _modulate(x_ref[0], gain_ref[...], sh_ref[...], sc_ref[...])
    y = jnp.dot(a.astype(BF16), w_ref[...], preferred_element_type=F32)
    z = y[:, :KV_WIDTH]
    kr = (z * lax.rsqrt(_head_mean_square(z, ones_ref[...]) + EPS) * kg_ref[...]).astype(BF16)
    k_ref[0, 0] = kr[:, :HEAD_DIM]
    k_ref[0, 1] = kr[:, HEAD_DIM:]
    vv = y[:, KV_WIDTH:].astype(BF16)
    v_ref[0, 0] = vv[:, :HEAD_DIM]
    v_ref[0, 1] = vv[:, HEAD_DIM:]


def _rope_tables(seq):
    t = jnp.arange(seq)
    row = (t // GRID_W).astype(F32)
    col = (t % GRID_W).astype(F32)
    half = HEAD_DIM // 2
    inv = ROPE_THETA ** (-jnp.arange(0, half, 2, dtype=F32) / half)
    ar, ac = row[:, None] * inv, col[:, None] * inv
    cos = jnp.concatenate([jnp.cos(ar), jnp.cos(ar), jnp.cos(ac), jnp.cos(ac)], axis=-1)
    sin = jnp.concatenate([-jnp.sin(ar), jnp.sin(ar), -jnp.sin(ac), jnp.sin(ac)], axis=-1)
    return jnp.tile(cos, (1, LANES // HEAD_DIM)), jnp.tile(sin, (1, LANES // HEAD_DIM))


def _head_mean_matrix():
    r = jnp.arange(LANES)
    same = (r[:, None] // HEAD_DIM) == (r[None, :] // HEAD_DIM)
    block = jnp.where(same, 1.0 / HEAD_DIM, 0.0).astype(BF16)
    return jnp.concatenate([block, block], axis=0)


def _inproj0(x, mods, gain, w_in, q_gain, k_gain, tm):
    bsz, seq, d = x.shape
    cos, sin = _rope_tables(seq)
    qg = jnp.tile(q_gain, LANES // HEAD_DIM).reshape(1, LANES)
    kg = jnp.tile(k_gain, LANES // HEAD_DIM).reshape(1, LANES)
    head = lambda n: pl.BlockSpec((1, n, tm, HEAD_DIM), lambda b, i: (b, 0, i, 0))
    return pl.pallas_call(
        _inproj0_kernel,
        grid=(bsz, seq // tm),
        in_specs=[pl.BlockSpec((1, tm, d), lambda b, i: (b, i, 0)),
                  _full((1, d)),
                  _mod_spec(0, lambda b, i: b, 0),
                  _mod_spec(0, lambda b, i: b, 1),
                  _full((d, EVEN_IN)),
                  pl.BlockSpec((tm, LANES), lambda b, i: (i, 0)),
                  pl.BlockSpec((tm, LANES), lambda b, i: (i, 0)),
                  _full((1, LANES)), _full((1, LANES)), _full((2 * LANES, LANES))],
        out_specs=[head(N_Q_HEADS), head(N_KV_HEADS), head(N_KV_HEADS),
                   pl.BlockSpec((1, tm, POOL_WIDTH), lambda b, i: (b, i, 0))],
        out_shape=[jax.ShapeDtypeStruct((bsz, N_Q_HEADS, seq, HEAD_DIM), BF16),
                   jax.ShapeDtypeStruct((bsz, N_KV_HEADS, seq, HEAD_DIM), BF16),
                   jax.ShapeDtypeStruct((bsz, N_KV_HEADS, seq, HEAD_DIM), BF16),
                   jax.ShapeDtypeStruct((bsz, seq, POOL_WIDTH), BF16)],
        compiler_params=_params(("parallel", "parallel")),
        name="inproj0",
    )(x, gain.reshape(1, d), mods, mods, w_in.astype(BF16), cos, sin, qg, kg, _head_mean_matrix())


def _inproj0_ctx(ctx, mods, ctx_row, gain, w_kv, k_gain):
    bsz, n_ctx, d = ctx.shape
    kg = jnp.tile(k_gain, LANES // HEAD_DIM).reshape(1, LANES)
    head = pl.BlockSpec((1, N_KV_HEADS, n_ctx, HEAD_DIM), lambda b: (b, 0, 0, 0))
    return pl.pallas_call(
        _inproj0_ctx_kernel,
        grid=(bsz,),
        in_specs=[pl.BlockSpec((1, n_ctx, d), lambda b: (b, 0, 0)),
                  _full((1, d)),
                  _mod_spec(0, lambda b: ctx_row, 0),
                  _mod_spec(0, lambda b: ctx_row, 1),
                  _full((d, 2 * KV_WIDTH)),
                  _full((1, LANES)), _full((2 * LANES, LANES))],
        out_specs=[head, head],
        out_shape=[jax.ShapeDtypeStruct((bsz, N_KV_HEADS, n_ctx, HEAD_DIM), BF16)] * 2,
        compiler_params=_params(("parallel",)),
        name="inproj0_ctx",
    )(ctx, gain.reshape(1, d), mods, mods, w_kv.astype(BF16), kg, _head_mean_matrix())


def _attn_kernel(q_ref, kl_ref, vl_ref, kc_ref, vc_ref, o_ref, ksq_sc, *, tq, tk):
    rows = Q_PER_KV * tq
    q = q_ref[0].reshape(rows, HEAD_DIM)
    seq = kl_ref.shape[2]
    chunks = [(kc_ref, vc_ref, 0, kc_ref.shape[2])]
    chunks += [(kl_ref, vl_ref, c * tk, tk) for c in range(seq // tk)]
    scores = lambda k: lax.dot_general(q, k, (((1,), (1,)), ((), ())), preferred_element_type=F32)

    @pl.when(pl.program_id(2) == 0)
    def _():
        def largest_sq_norm(k_ref):
            kf = k_ref[0, 0].astype(F32)
            return jnp.max(jnp.sum(kf * kf, axis=1, keepdims=True), axis=0, keepdims=True)

        ksq_sc[...] = jnp.broadcast_to(jnp.maximum(largest_sq_norm(kl_ref), largest_sq_norm(kc_ref)),
                                       ksq_sc.shape)

    qf = q.astype(F32)
    bound = jnp.sqrt(jnp.sum(qf * qf, axis=1, keepdims=True) * ksq_sc[0:1, 0:1])
    safe = jnp.max(bound) <= SAFE_SOFTMAX_SHIFT

    def finish(acc, l):
        o = acc / l
        o_ref[0] = jnp.concatenate([o[h * tq:(h + 1) * tq] for h in range(Q_PER_KV)], axis=1).astype(BF16)

    @pl.when(safe)
    def _():
        l = jnp.zeros((rows, 1), F32)
        acc = jnp.zeros((rows, HEAD_DIM), F32)
        for k_ref, v_ref, start, size in chunks:
            p = jnp.exp2(scores(k_ref[0, 0, start:start + size, :]) - bound)
            l = l + jnp.sum(p, axis=1, keepdims=True)
            acc = acc + jnp.dot(p.astype(BF16), v_ref[0, 0, start:start + size, :], preferred_element_type=F32)
        finish(acc, l)

    @pl.when(jnp.logical_not(safe))
    def _():
        m = jnp.full((rows, 1), -jnp.inf, F32)
        l = jnp.zeros((rows, 1), F32)
        acc = jnp.zeros((rows, HEAD_DIM), F32)
        for k_ref, v_ref, start, size in chunks:
            s = scores(k_ref[0, 0, start:start + size, :])
            m_new = jnp.maximum(m, jnp.max(s, axis=1, keepdims=True))
            alpha = jnp.exp2(m - m_new)
            p = jnp.exp2(s - m_new)
            l = alpha * l + jnp.sum(p, axis=1, keepdims=True)
            acc = alpha * acc + jnp.dot(p.astype(BF16), v_ref[0, 0, start:start + size, :],
                                        preferred_element_type=F32)
            m = m_new
        finish(acc, l)


def _attention(q, k, v, kc, vc, tq, tk):
    bsz, _, seq, _ = q.shape
    n_ctx = kc.shape[2]
    kv_spec = lambda n: pl.BlockSpec((1, 1, n, HEAD_DIM), lambda b, g, i: (b, g, 0, 0))
    return pl.pallas_call(
        functools.partial(_attn_kernel, tq=tq, tk=tk),
        grid=(bsz, N_KV_HEADS, seq // tq),
        in_specs=[pl.BlockSpec((1, Q_PER_KV, tq, HEAD_DIM), lambda b, g, i: (b, g, i, 0)),
                  kv_spec(seq), kv_spec(seq), kv_spec(n_ctx), kv_spec(n_ctx)],
        out_specs=pl.BlockSpec((1, tq, Q_PER_KV * HEAD_DIM), lambda b, g, i: (b, i, g)),
        out_shape=jax.ShapeDtypeStruct((bsz, seq, ATTN_WIDTH), BF16),
        scratch_shapes=[pltpu.VMEM((8, LANES), F32)],
        compiler_params=_params(("arbitrary", "arbitrary", "arbitrary")),
        name="attention",
    )(q, k, v, kc, vc)


def _route(logits):
    lane = lax.broadcasted_iota(jnp.int32, logits.shape, 1).astype(F32)
    neg = -jnp.inf
    big = float(ROUTER_LANES)
    first_index = lambda mask: jnp.min(jnp.where(mask, lane, big), axis=1, keepdims=True)
    is_g = lane < N_GROUPS
    gm = jnp.max(jnp.where(is_g, logits, neg), axis=1, keepdims=True)
    gidx = first_index(is_g & (logits == gm))
    gden = jnp.sum(jnp.where(is_g, jnp.exp(logits - gm), 0.0), axis=1, keepdims=True)
    g_p = 1.0 / gden
    first = N_GROUPS + EXPERTS_PER_GROUP * gidx
    sel = (lane >= first) & (lane < first + EXPERTS_PER_GROUP)
    e1 = jnp.max(jnp.where(sel, logits, neg), axis=1, keepdims=True)
    i1 = first_index(sel & (logits == e1))
    rest = sel & (lane != i1)
    e2 = jnp.max(jnp.where(rest, logits, neg), axis=1, keepdims=True)
    i2 = first_index(rest & (logits == e2))
    p2 = jnp.exp(e2 - e1)
    w1 = g_p * (1.0 / (1.0 + p2))
    w2 = g_p * (p2 / (1.0 + p2))
    lo = jnp.minimum(i1, i2) - first
    hi = jnp.maximum(i1, i2) - first
    pair = jnp.where(lo == 0, hi - 1, jnp.where(lo == 1, hi + 1, PAIRS_PER_GROUP - 1.0))
    bucket = PAIRS_PER_GROUP * gidx + pair
    w_lo = jnp.where(i1 < i2, w1, w2)
    w_hi = jnp.where(i1 < i2, w2, w1)
    return jnp.where(lane == INFO_BUCKET, bucket,
                     jnp.where(lane == INFO_W_LO, w_lo, jnp.where(lane == INFO_W_HI, w_hi, 0.0)))


def _bucket_onehot(info, sel):
    brow = lax.dot_general(sel, info.astype(BF16), (((1,), (1,)), ((), ())),
                           preferred_element_type=F32)[0:1]
    bid = lax.broadcasted_iota(jnp.int32, (BUCKET_ROWS, info.shape[0]), 0)
    return (bid == brow.astype(jnp.int32)).astype(F32)


def _bucket_selector():
    return jnp.zeros((8, ROUTER_LANES), F32).at[0, INFO_BUCKET].set(1.0).astype(BF16)


def _tail(y, x_res, gate1, gain2, shift2, scale2, rw_both, rbias, sel, h1_ref, info_ref, cnt_ref):
    h1 = x_res + gate1 * y
    h1_ref[0] = h1
    t = _modulate(h1, gain2, shift2, scale2)
    t_hi = t.astype(BF16)
    t_lo = (t - t_hi.astype(F32)).astype(BF16)
    both = jnp.dot(t_hi, rw_both, preferred_element_type=F32)
    logits = (both[:, :ROUTER_LANES] + both[:, ROUTER_LANES:]
              + jnp.dot(t_lo, rw_both[:, :ROUTER_LANES], preferred_element_type=F32)) + rbias
    info = _route(logits)
    info_ref[0] = info

    @pl.when((pl.program_id(0) == 0) & (pl.program_id(1) == 0))
    def _():
        cnt_ref[...] = jnp.zeros_like(cnt_ref)

    cnt_ref[...] += jnp.sum(_bucket_onehot(info, sel), axis=1, keepdims=True)


def _router_operands(rg_w, rg_b, re_w, re_b):
    d = rg_w.shape[0]
    w = jnp.concatenate([rg_w, re_w, jnp.zeros((d, ROUTER_LANES - N_GROUPS - N_EXPERTS), F32)], axis=1)
    b = jnp.concatenate([rg_b, re_b, jnp.zeros((ROUTER_LANES - N_GROUPS - N_EXPERTS,), F32)])
    w_hi = w.astype(BF16)
    w_lo = (w - w_hi.astype(F32)).astype(BF16)
    return jnp.concatenate([w_hi, w_lo], axis=1), b.reshape(1, ROUTER_LANES)


def _fill_halo(buf, main_ref, prev_ref, next_ref, tm, i, n_tiles, halo=HALO):
    buf[halo:halo + tm] = main_ref[0].astype(F32)
    buf[0:halo] = jnp.where(i > 0, prev_ref[0].astype(F32), 0.0)
    buf[halo + tm:2 * halo + tm] = jnp.where(i < n_tiles - 1, next_ref[0].astype(F32), 0.0)


def _halo_specs(tm, seq, width, halo=HALO):
    per = tm // halo
    last = seq // halo - 1
    return [pl.BlockSpec((1, tm, width), lambda b, i: (b, i, 0)),
            pl.BlockSpec((1, halo, width), lambda b, i: (b, jnp.maximum(i * per - 1, 0), 0)),
            pl.BlockSpec((1, halo, width), lambda b, i: (b, jnp.minimum((i + 1) * per, last), 0))]


def _out0_kernel(o_ref, p_ref, pprev_ref, pnext_ref, x_ref, g1_ref, gain2_ref, sh2_ref, sc2_ref,
                 poolw_ref, pscale_ref, wout_ref, rw_ref, rb_ref, sel_ref,
                 h1_ref, info_ref, cnt_ref, pbuf, *, tm, seq):
    i = pl.program_id(1)
    _fill_halo(pbuf, p_ref, pprev_ref, pnext_ref, tm, i, seq // tm)
    pos = i * tm + lax.broadcasted_iota(jnp.int32, (tm, 1), 0)
    pooled = []
    for g, w in enumerate(POOL_WINDOWS):
        sl = slice(g * POOL_GROUP, (g + 1) * POOL_GROUP)
        acc = pbuf[HALO - w // 2:HALO - w // 2 + tm, sl]
        for j in range(1 - w // 2, w - w // 2):
            acc = acc + pbuf[HALO + j:HALO + j + tm, sl]
        lo = jnp.clip(pos - w // 2, 0, seq)
        hi = jnp.clip(pos + w - w // 2, 0, seq)
        mean = acc * (1.0 / (hi - lo).astype(F32))
        dlt = (mean - pbuf[HALO:HALO + tm, sl]).astype(BF16)
        pooled.append((jnp.dot(dlt, poolw_ref[g], preferred_element_type=F32) * pscale_ref[:, sl]).astype(BF16))
    mixed = jnp.concatenate([o_ref[0]] + pooled, axis=1)
    y = jnp.dot(mixed, wout_ref[...], preferred_element_type=F32)
    _tail(y, x_ref[0], g1_ref[...], gain2_ref[...], sh2_ref[...], sc2_ref[...],
          rw_ref[...], rb_ref[...], sel_ref[...], h1_ref, info_ref, cnt_ref)


def _tail_specs(layer, d):
    by_batch = lambda b, i: b
    ins = [_mod_spec(layer, by_batch, 2), _full((1, d)), _mod_spec(layer, by_batch, 3),
           _mod_spec(layer, by_batch, 4)]
    return ins


def _tail_outs(bsz, seq, d, tm):
    specs = [pl.BlockSpec((1, tm, d), lambda b, i: (b, i, 0)),
             pl.BlockSpec((1, tm, ROUTER_LANES), lambda b, i: (b, i, 0)),
             _full((BUCKET_ROWS, LANES))]
    shapes = [jax.ShapeDtypeStruct((bsz, seq, d), F32),
              jax.ShapeDtypeStruct((bsz, seq, ROUTER_LANES), F32),
              jax.ShapeDtypeStruct((BUCKET_ROWS, LANES), F32)]
    return specs, shapes


def _out0(o, p, x, mods, gain2, pool_w, pool_scale, w_out, router, tm):
    bsz, seq, d = x.shape
    rw_both, rb = router
    out_specs, out_shapes = _tail_outs(bsz, seq, d, tm)
    return pl.pallas_call(
        functools.partial(_out0_kernel, tm=tm, seq=seq),
        grid=(bsz, seq // tm),
        in_specs=[pl.BlockSpec((1, tm, ATTN_WIDTH), lambda b, i: (b, i, 0))]
        + _halo_specs(tm, seq, POOL_WIDTH)
        + [pl.BlockSpec((1, tm, d), lambda b, i: (b, i, 0))]
        + _tail_specs(0, d)
        + [_full(pool_w.shape), _full((1, POOL_WIDTH)), _full(w_out.shape),
           _full(rw_both.shape), _full(rb.shape), _full((8, ROUTER_LANES))],
        out_specs=out_specs,
        out_shape=out_shapes,
        scratch_shapes=[pltpu.VMEM((tm + 2 * HALO, POOL_WIDTH), F32)],
        compiler_params=_params(("arbitrary", "arbitrary")),
        name="out0",
    )(o, p, p, p, x, mods, gain2.reshape(1, d), mods, mods,
      pool_w.astype(BF16), pool_scale.reshape(1, POOL_WIDTH), w_out.astype(BF16), rw_both, rb,
      _bucket_selector())


def _inproj1_kernel(pos_ref, posn_ref, h1_ref, g2_ref, gain_ref, sh_ref, sc_ref, w_ref, sgg_ref, sgw_ref, sgb_ref,
                    ys_ref, h_ref, yc_ref, z_ref, bg_ref, ybuf, sem, *, tm, n_steps):
    x = h1_ref[0] + g2_ref[...] * _fetch_sorted_rows(ys_ref, pos_ref, posn_ref, ybuf, sem, tm, n_steps)
    h_ref[0] = x
    a = _modulate(x, gain_ref[...], sh_ref[...], sc_ref[...])
    y = jnp.dot(a.astype(BF16), w_ref[...], preferred_element_type=F32)
    for g in range(SG_GROUPS):
        sl = slice(g * LANES, (g + 1) * LANES)
        u = y[:, sl]
        vg = y[:, SG_WIDTH + g * LANES:SG_WIDTH + (g + 1) * LANES]
        ms = jnp.mean(vg * vg, axis=-1, keepdims=True)
        vn = (vg * lax.rsqrt(ms + EPS) * sgg_ref[:, sl]).astype(BF16)
        for c in range(tm // SG_CHUNK):
            rows = slice(c * SG_CHUNK, (c + 1) * SG_CHUNK)
            s = jnp.dot(sgw_ref[g], vn[rows], preferred_element_type=F32) + sgb_ref[g]
            yc_ref[0, rows, sl] = (u[rows] * s).astype(BF16)
    hx = y[:, 2 * SG_WIDTH:2 * SG_WIDTH + CONV_WIDTH]
    bg_ref[0] = y[:, 2 * SG_WIDTH + CONV_WIDTH:2 * SG_WIDTH + 2 * CONV_WIDTH].astype(BF16)
    cg = y[:, 2 * SG_WIDTH + 2 * CONV_WIDTH:]
    z_ref[0] = (cg * hx).astype(BF16)


def _pos_spec(tm, nt, n_steps, ahead):
    return pl.BlockSpec((1, 1, tm), lambda b, i: (jnp.minimum(b * nt + i + ahead, n_steps - 1), 0, 0),
                        memory_space=pltpu.SMEM)


def _inproj1(h1, ys, pos, mods, gain, w_in, sg_gain, sg_w, sg_b, tm):
    bsz, seq, d = h1.shape
    nt = seq // tm
    n_steps = bsz * nt
    sgb = jnp.broadcast_to(sg_b[:, :, None], (SG_GROUPS, SG_CHUNK, LANES))
    by_batch = lambda b, i: b
    wide = lambda w, dt: (pl.BlockSpec((1, tm, w), lambda b, i: (b, i, 0)),
                          jax.ShapeDtypeStruct((bsz, seq, w), dt))
    outs = [wide(d, F32), wide(SG_WIDTH, BF16), wide(CONV_WIDTH, BF16), wide(CONV_WIDTH, BF16)]
    return pl.pallas_call(
        functools.partial(_inproj1_kernel, tm=tm, n_steps=n_steps),
        grid=(bsz, nt),
        in_specs=[_pos_spec(tm, nt, n_steps, 0), _pos_spec(tm, nt, n_steps, 1),
                  pl.BlockSpec((1, tm, d), lambda b, i: (b, i, 0)),
                  _mod_spec(0, by_batch, 5),
                  _full((1, d)),
                  _mod_spec(1, by_batch, 0),
                  _mod_spec(1, by_batch, 1),
                  _full((d, ODD_IN)),
                  _full((1, SG_WIDTH)), _full(sg_w.shape), _full(sgb.shape),
                  pl.BlockSpec(memory_space=pl.ANY)],
        out_specs=[s for s, _ in outs],
        out_shape=[s for _, s in outs],
        scratch_shapes=[pltpu.VMEM((2, tm, d), F32), pltpu.SemaphoreType.DMA((2,))],
        compiler_params=_params(("arbitrary", "arbitrary")),
        name="inproj1",
    )(pos, pos, h1, mods, gain.reshape(1, d), mods, mods, w_in.astype(BF16), sg_gain.reshape(1, SG_WIDTH),
      sg_w.astype(BF16), sgb, ys)


def _out1_kernel(yc_ref, z_ref, zprev_ref, znext_ref, bg_ref, x_ref, g1_ref, gain2_ref, sh2_ref, sc2_ref,
                 convw_ref, wout_ref, rw_ref, rb_ref, sel_ref,
                 h1_ref, info_ref, cnt_ref, zbuf, *, tm, seq):
    i = pl.program_id(1)
    _fill_halo(zbuf, z_ref, zprev_ref, znext_ref, tm, i, seq // tm, HALO)
    zc = (zbuf[HALO - 1:HALO - 1 + tm] * convw_ref[0:1, :]
          + zbuf[HALO:HALO + tm] * convw_ref[1:2, :]
          + zbuf[HALO + 1:HALO + 1 + tm] * convw_ref[2:3, :])
    yd = (bg_ref[0].astype(F32) * zc).astype(BF16)
    y = (jnp.dot(yc_ref[0], wout_ref[0:SG_WIDTH, :], preferred_element_type=F32)
         + jnp.dot(yd, wout_ref[SG_WIDTH:, :], preferred_element_type=F32))
    _tail(y, x_ref[0], g1_ref[...], gain2_ref[...], sh2_ref[...], sc2_ref[...],
          rw_ref[...], rb_ref[...], sel_ref[...], h1_ref, info_ref, cnt_ref)


def _out1(yc, z, bg, x, mods, gain2, conv_w, w_out, router, tm):
    bsz, seq, d = x.shape
    rw_both, rb = router
    out_specs, out_shapes = _tail_outs(bsz, seq, d, tm)
    wide = pl.BlockSpec((1, tm, CONV_WIDTH), lambda b, i: (b, i, 0))
    return pl.pallas_call(
        functools.partial(_out1_kernel, tm=tm, seq=seq),
        grid=(bsz, seq // tm),
        in_specs=[wide] + _halo_specs(tm, seq, CONV_WIDTH, HALO) + [wide]
        + [pl.BlockSpec((1, tm, d), lambda b, i: (b, i, 0))]
        + _tail_specs(1, d)
        + [_full((3, CONV_WIDTH)), _full(w_out.shape),
           _full(rw_both.shape), _full(rb.shape), _full((8, ROUTER_LANES))],
        out_specs=out_specs,
        out_shape=out_shapes,
        scratch_shapes=[pltpu.VMEM((tm + 2 * HALO, CONV_WIDTH), F32)],
        compiler_params=_params(("arbitrary", "arbitrary")),
        name="out1",
    )(yc, z, z, z, bg, x, mods, gain2.reshape(1, d), mods, mods,
      conv_w.reshape(3, CONV_WIDTH), w_out.astype(BF16), rw_both, rb, _bucket_selector())


def _plan_pos_kernel(info_ref, sel_ref, cnt_ref, ltri_ref, utri_ref, pos_ref, meta_ref, start_sc, run_sc):
    @pl.when(pl.program_id(0) == 0)
    def _():
        padded = jnp.ceil(cnt_ref[...] * (1.0 / SORT_TILE)) * SORT_TILE
        incl = jnp.dot(ltri_ref[...], padded, precision=lax.Precision.HIGHEST, preferred_element_type=F32)
        start_sc[...] = incl - padded
        run_sc[...] = jnp.zeros_like(run_sc)
        ends = jnp.broadcast_to(incl[:, 0:1], (BUCKET_ROWS, META_LANES))
        bid = lax.broadcasted_iota(jnp.int32, ends.shape, 0)
        tile = lax.broadcasted_iota(jnp.int32, (1, META_LANES), 1)

        def bucket_of(row0):
            done = jnp.where((bid < N_BUCKETS) & (ends <= row0), 1.0, 0.0)
            return jnp.minimum(jnp.sum(done, axis=0, keepdims=True), N_BUCKETS - 1.0).astype(jnp.int32)

        def experts_of(bucket):
            grp = ((bucket >= PAIRS_PER_GROUP).astype(jnp.int32)
                   + (bucket >= 2 * PAIRS_PER_GROUP).astype(jnp.int32)
                   + (bucket >= 3 * PAIRS_PER_GROUP).astype(jnp.int32))
            pair = bucket - PAIRS_PER_GROUP * grp
            lo = (pair >= 3).astype(jnp.int32) + (pair >= 5).astype(jnp.int32)
            hi = jnp.where(pair == 0, 1, jnp.where((pair == 1) | (pair == 3), 2, 3))
            return EXPERTS_PER_GROUP * grp + lo, EXPERTS_PER_GROUP * grp + hi

        row0 = (tile * SORT_TILE).astype(F32)
        tb = bucket_of(row0)
        n_used = (incl[N_BUCKETS - 1:N_BUCKETS, 0:1] * (1.0 / SORT_TILE)).astype(jnp.int32)
        fill = (tile >= n_used - 1) | (tb != bucket_of(row0 + SORT_TILE))
        first = (tile == 0) | (tb != bucket_of(row0 - SORT_TILE))
        own_end = jnp.sum(jnp.where(bid == tb, ends, 0.0), axis=0, keepdims=True)
        nonempty = jnp.broadcast_to(padded[:, 0:1], ends.shape) > 0.0
        ordinal = jnp.sum(jnp.where((bid < tb) & nonempty, 1.0, 0.0), axis=0, keepdims=True)
        rows = [None] * 8
        rows[META_EXPERT_LO], rows[META_EXPERT_HI] = experts_of(tb)
        rows[META_N_USED] = jnp.broadcast_to(n_used, (1, META_LANES))
        rows[META_FILL] = fill.astype(jnp.int32)
        rows[META_FIRST] = first.astype(jnp.int32)
        rows[META_WSLOT] = (ordinal - 2.0 * jnp.floor(ordinal * 0.5)).astype(jnp.int32)
        rows[META_NEXT_LO], rows[META_NEXT_HI] = experts_of(bucket_of(own_end))
        for r, row in enumerate(rows):
            meta_ref[r:r + 1, :] = row

    tm = utri_ref.shape[0]
    base = start_sc[:, 0:1] + run_sc[:, 0:1]
    for k in range(pos_ref.shape[0]):
        oh = _bucket_onehot(info_ref[k * tm:(k + 1) * tm], sel_ref[...])
        before = jnp.dot(oh.astype(BF16), utri_ref[...], preferred_element_type=F32)
        pos_ref[k] = jnp.sum(oh * (before + base), axis=0, keepdims=True).astype(jnp.int32)
        base = base + jnp.sum(oh, axis=1, keepdims=True)
    run_sc[...] = jnp.broadcast_to(base - start_sc[:, 0:1], run_sc.shape)


def _sort_plan(info, cnt, tm):
    n = info.shape[0]
    sel = _bucket_selector()
    sub = 4 if (n // tm) % 4 == 0 else 1
    info_spec = pl.BlockSpec((sub * tm, ROUTER_LANES), lambda i: (i, 0))
    r = jnp.arange(BUCKET_ROWS)
    ltri = (r[:, None] >= r[None, :]).astype(F32)
    t = jnp.arange(tm)
    utri = (t[:, None] < t[None, :]).astype(BF16)
    return pl.pallas_call(
        _plan_pos_kernel,
        grid=(n // (sub * tm),),
        in_specs=[info_spec, _full(sel.shape), _full(cnt.shape), _full(ltri.shape), _full(utri.shape)],
        out_specs=[pl.BlockSpec((sub, 1, tm), lambda i: (i, 0, 0)), _full((8, META_LANES))],
        out_shape=[jax.ShapeDtypeStruct((n // tm, 1, tm), jnp.int32),
                   jax.ShapeDtypeStruct((8, META_LANES), jnp.int32)],
        scratch_shapes=[pltpu.VMEM((BUCKET_ROWS, LANES), F32), pltpu.VMEM((BUCKET_ROWS, LANES), F32)],
        compiler_params=_params(("arbitrary",)),
        name="plan_pos",
    )(info, sel, cnt, ltri, utri)


def _dispatch_kernel(pos_ref, fill_ref, h1_ref, gain2_ref, sh2_ref, sc2_ref, info_ref, xs_ref,
                     rowbuf, zbuf, sem, zsem, *, tm, n_steps, n_tiles):
    step = pl.program_id(0) * pl.num_programs(1) + pl.program_id(1)
    slot = step % 2

    @pl.when(step == 0)
    def _():
        zbuf[...] = jnp.zeros_like(zbuf)
        fill = lambda j: pltpu.make_async_copy(zbuf, xs_ref.at[pl.ds(j * SORT_TILE, SORT_TILE), 0], zsem)
        for j in range(n_tiles):
            pl.when(fill_ref[0, j] == 1)(lambda j=j: fill(j).start())
        for j in range(n_tiles):
            pl.when(fill_ref[0, j] == 1)(lambda j=j: fill(j).wait())

    def wait(s):
        pltpu.make_async_copy(rowbuf.at[s], xs_ref.at[pl.ds(0, tm), 0], sem.at[s]).wait()

    def send(s):
        @pl.when(step >= 2)
        def _():
            wait(s)

        rowbuf[s, :, 0:D_MODEL] = _modulate(h1_ref[0], gain2_ref[...], sh2_ref[...], sc2_ref[...])
        rowbuf[s, :, D_MODEL:] = info_ref[0]
        for r in range(tm):
            pltpu.make_async_copy(rowbuf.at[s, pl.ds(r, 1)], xs_ref.at[pos_ref[0, 0, r]],
                                  sem.at[s]).start(priority=r % 2)

    for s in range(2):
        pl.when(slot == s)(functools.partial(send, s))

    @pl.when(step == n_steps - 1)
    def _():
        wait(slot)
        if n_steps > 1:
            wait(1 - slot)


def _gmoe_kernel(ea_ref, eb_ref, nu_ref, first_ref, wslot_ref, na_ref, nb_ref, xs_ref, wg_hbm, wu_hbm, wd_hbm,
                 ys_ref, xbuf, ybuf, zbuf, wg_buf, wu_buf, wd_buf, sem_in, sem_out, zsem, wsem, *, layer):
    j = pl.program_id(0)
    n_used = nu_ref[0]
    slot = j % 2
    wslot = wslot_ref[j]
    tile = lambda ref, t: ref.at[pl.ds(t * SORT_TILE, SORT_TILE), 0]
    in_copy = lambda t, s: pltpu.make_async_copy(tile(xs_ref, t), xbuf.at[s], sem_in.at[s])
    out_copy = lambda t, s: pltpu.make_async_copy(ybuf.at[s], tile(ys_ref, t), sem_out.at[s])

    def weight_copies(e_lo, e_hi, s):
        return [pltpu.make_async_copy(hbm.at[layer, e], buf.at[s, which], wsem.at[s])
                for hbm, buf in ((wg_hbm, wg_buf), (wu_hbm, wu_buf), (wd_hbm, wd_buf))
                for which, e in ((0, e_lo), (1, e_hi))]

    @pl.when(j == 0)
    def _():
        in_copy(0, 0).start()
        for c in weight_copies(ea_ref[0], eb_ref[0], 0):
            c.start()

    @pl.when(j + 1 < n_used)
    def _():
        in_copy(j + 1, 1 - slot).start()

    @pl.when(j < n_used)
    def _():
        @pl.when(first_ref[j] == 1)
        def _():
            for c in weight_copies(ea_ref[j], eb_ref[j], wslot):
                c.wait()
            for c in weight_copies(na_ref[j], nb_ref[j], 1 - wslot):
                c.start()

        in_copy(j, slot).wait()

        @pl.when(j >= 2)
        def _():
            out_copy(j - 2, slot).wait()

        x = xbuf[slot, :, 0:D_MODEL].astype(BF16)

        def expert(which, w):
            gt = jnp.dot(x, wg_buf[wslot, which].astype(BF16), preferred_element_type=F32)
            up = jnp.dot(x, wu_buf[wslot, which].astype(BF16), preferred_element_type=F32)
            h = (gt * jax.nn.sigmoid(gt)) * up * w
            return jnp.dot(h.astype(BF16), wd_buf[wslot, which].astype(BF16), preferred_element_type=F32)

        w_lo = xbuf[slot, :, D_MODEL + INFO_W_LO:D_MODEL + INFO_W_LO + 1]
        w_hi = xbuf[slot, :, D_MODEL + INFO_W_HI:D_MODEL + INFO_W_HI + 1]
        ybuf[slot] = expert(0, w_lo) + expert(1, w_hi)
        out_copy(j, slot).start()

        @pl.when(j == n_used - 1)
        def _():
            out_copy(j, slot).wait()

            @pl.when(j >= 1)
            def _():
                out_copy(j - 1, 1 - slot).wait()

            for c in weight_copies(ea_ref[j], eb_ref[j], 1 - wslot):
                c.wait()

    @pl.when(j >= n_used)
    def _():
        @pl.when(j == n_used)
        def _():
            zbuf[...] = jnp.zeros_like(zbuf)

        fill = pltpu.make_async_copy(zbuf, tile(ys_ref, j), zsem)
        fill.start()
        fill.wait()


def _fetch_sorted_rows(ys_ref, pos_ref, posn_ref, ybuf, sem, tm, n_steps):
    step = pl.program_id(0) * pl.num_programs(1) + pl.program_id(1)
    slot = step % 2

    def issue(p_ref, s):
        for r in range(tm):
            pltpu.make_async_copy(ys_ref.at[p_ref[0, 0, r]], ybuf.at[s, pl.ds(r, 1)],
                                  sem.at[s]).start(priority=r % 2)

    pl.when(step == 0)(functools.partial(issue, pos_ref, 0))
    for s in range(2):
        pl.when((step + 1 < n_steps) & (slot == s))(functools.partial(issue, posn_ref, 1 - s))
    pltpu.make_async_copy(ys_ref.at[pl.ds(0, tm), 0], ybuf.at[slot], sem.at[slot]).wait()
    return ybuf[slot]


def _combine_kernel(pos_ref, posn_ref, h1_ref, g2_ref, ys_ref, o_ref, ybuf, sem, *, tm, n_steps):
    rows = _fetch_sorted_rows(ys_ref, pos_ref, posn_ref, ybuf, sem, tm, n_steps)
    o_ref[0] = h1_ref[0] + g2_ref[...] * rows


def _experts_sorted(h1, info, cnt, mods, layer, gain2, w_gate, w_up, w_down, tm):
    bsz, seq, d = h1.shape
    nt = seq // tm
    n_steps = bsz * nt
    n = bsz * seq
    n_sorted = n + N_BUCKETS * SORT_TILE
    n_tiles = n_sorted // SORT_TILE
    assert n % SORT_TILE == 0 and n_tiles <= META_LANES
    pos, meta = _sort_plan(info.reshape(n, ROUTER_LANES), cnt, tm)
    tile = lambda w: pl.BlockSpec((1, tm, w), lambda b, i: (b, i, 0))
    any_spec = pl.BlockSpec(memory_space=pl.ANY)
    by_batch = lambda b, i: b

    xs = pl.pallas_call(
        functools.partial(_dispatch_kernel, tm=tm, n_steps=n_steps, n_tiles=n_tiles),
        grid=(bsz, nt),
        in_specs=[_pos_spec(tm, nt, n_steps, 0),
                  pl.BlockSpec((1, META_LANES), lambda b, i: (0, 0), memory_space=pltpu.SMEM),
                  tile(d), _full((1, d)), _mod_spec(layer, by_batch, 3),
                  _mod_spec(layer, by_batch, 4), tile(ROUTER_LANES)],
        out_specs=any_spec,
        out_shape=jax.ShapeDtypeStruct((n_sorted, 1, ROW_WIDTH), F32),
        scratch_shapes=[pltpu.VMEM((2, tm, ROW_WIDTH), F32), pltpu.VMEM((SORT_TILE, ROW_WIDTH), F32),
                        pltpu.SemaphoreType.DMA((2,)), pltpu.SemaphoreType.DMA(())],
        compiler_params=_params(("arbitrary", "arbitrary")),
        name=f"dispatch{layer}",
    )(pos, meta[META_FILL:META_FILL + 1], h1, gain2.reshape(1, d), mods, mods, info)

    ys = pl.pallas_call(
        functools.partial(_gmoe_kernel, layer=layer),
        grid_spec=pltpu.PrefetchScalarGridSpec(
            num_scalar_prefetch=7,
            grid=(n_tiles,),
            in_specs=[any_spec] * 4,
            out_specs=any_spec,
            scratch_shapes=[pltpu.VMEM((2, SORT_TILE, ROW_WIDTH), F32), pltpu.VMEM((2, SORT_TILE, d), F32),
                            pltpu.VMEM((SORT_TILE, d), F32),
                            pltpu.VMEM((2, 2, d, D_EXPERT), F32), pltpu.VMEM((2, 2, d, D_EXPERT), F32),
                            pltpu.VMEM((2, 2, D_EXPERT, d), F32),
                            pltpu.SemaphoreType.DMA((2,)), pltpu.SemaphoreType.DMA((2,)),
                            pltpu.SemaphoreType.DMA(()), pltpu.SemaphoreType.DMA((2,))]),
        out_shape=jax.ShapeDtypeStruct((n_sorted, 1, d), F32),
        compiler_params=_params(("arbitrary",)),
        name=f"experts{layer}",
    )(meta[META_EXPERT_LO], meta[META_EXPERT_HI], meta[META_N_USED, :1], meta[META_FIRST], meta[META_WSLOT],
      meta[META_NEXT_LO], meta[META_NEXT_HI], xs, w_gate, w_up, w_down)
    return ys, pos


def _combine(h1, ys, pos, mods, layer, tm):
    bsz, seq, d = h1.shape
    nt = seq // tm
    n_steps = bsz * nt
    tile = lambda w: pl.BlockSpec((1, tm, w), lambda b, i: (b, i, 0))
    return pl.pallas_call(
        functools.partial(_combine_kernel, tm=tm, n_steps=n_steps),
        grid=(bsz, nt),
        in_specs=[_pos_spec(tm, nt, n_steps, 0), _pos_spec(tm, nt, n_steps, 1), tile(d),
                  _mod_spec(layer, lambda b, i: b, 5), pl.BlockSpec(memory_space=pl.ANY)],
        out_specs=tile(d),
        out_shape=jax.ShapeDtypeStruct((bsz, seq, d), F32),
        scratch_shapes=[pltpu.VMEM((2, tm, d), F32), pltpu.SemaphoreType.DMA((2,))],
        compiler_params=_params(("arbitrary", "arbitrary")),
        name=f"combine{layer}",
    )(pos, pos, h1, mods, ys)


def kernel(x, c, ctx, c_ctx, mod_w, mod_b, norm1_g, norm2_g, even_w_in, q_gain, k_gain, pool_w, pool_scale,
           even_w_out, odd_w_in, sg_gain, sg_w, sg_b, conv_w, odd_w_out, router_g_w, router_g_b,
           router_e_w, router_e_b, w_gate, w_up, w_down):
    bsz, seq, d = x.shape
    tm = min(512, seq)
    tm_proj = min(1024, seq)
    cond = jnp.zeros((MOD_ROWS, d), F32).at[:bsz].set(c).at[bsz].set(c_ctx)
    mods = _adaln(cond, mod_w, mod_b).reshape(mod_w.shape[0], MOD_ROWS, 6, 1, d)

    q, k, v, p = _inproj0(x, mods, norm1_g[0], even_w_in[0], q_gain[0], k_gain[0], tm_proj)
    kc, vc = _inproj0_ctx(ctx, mods, bsz, norm1_g[0], even_w_in[0][:, ATTN_WIDTH:ATTN_WIDTH + 2 * KV_WIDTH],
                          k_gain[0])
    o = _attention(q, k, v, kc, vc, tq=min(512, seq), tk=min(2048, seq))
    router0 = _router_operands(router_g_w[0], router_g_b[0], router_e_w[0], router_e_b[0])
    h1, info, cnt = _out0(o, p, x, mods, norm2_g[0], pool_w[0], pool_scale[0], even_w_out[0], router0, tm_proj)
    ys, pos = _experts_sorted(h1, info, cnt, mods, 0, norm2_g[0], w_gate, w_up, w_down, tm)

    h, yc, z, bg = _inproj1(h1, ys, pos, mods, norm1_g[1], odd_w_in[0], sg_gain[0], sg_w[0], sg_b[0], tm)
    router1 = _router_operands(router_g_w[1], router_g_b[1], router_e_w[1], router_e_b[1])
    h1, info, cnt = _out1(yc, z, bg, h, mods, norm2_g[1], conv_w[0], odd_w_out[0], router1, tm_proj)
    ys, pos = _experts_sorted(h1, info, cnt, mods, 1, norm2_g[1], w_gate, w_up, w_down, tm)
    return _combine(h1, ys, pos, mods, 1, tm)
```

```python
import functools

import jax
import jax.numpy as jnp
from jax import lax
from jax.experimental import pallas as pl
from jax.experimental.pallas import tpu as pltpu

F32 = jnp.float32
BF16 = jnp.bfloat16

D_MODEL = 1024
GRID_W = 64
EPS = 1e-6
N_Q_HEADS = 8
N_KV_HEADS = 2
HEAD_DIM = 64
Q_PER_KV = N_Q_HEADS // N_KV_HEADS
ATTN_WIDTH = N_Q_HEADS * HEAD_DIM
KV_WIDTH = N_KV_HEADS * HEAD_DIM
ROPE_THETA = 10000.0
POOL_WINDOWS = (2, 4, 8, 16)
POOL_GROUP = 128
POOL_WIDTH = POOL_GROUP * len(POOL_WINDOWS)
SG_GROUPS = 4
SG_CHUNK = 128
SG_WIDTH = 512
CONV_WIDTH = 512
EVEN_IN = ATTN_WIDTH + 2 * KV_WIDTH + POOL_WIDTH
ODD_IN = 2 * SG_WIDTH + 3 * CONV_WIDTH
N_GROUPS = 4
EXPERTS_PER_GROUP = 4
N_EXPERTS = 16
D_EXPERT = 256

Q_SCALE = HEAD_DIM ** -0.5 * 1.4426950408889634
SAFE_SOFTMAX_SHIFT = 60.0
LANES = 128
HALO = 16
ROUTER_LANES = 128
MOD_ROWS = 16
VMEM_LIMIT = 48 * 1024 * 1024

PAIRS_PER_GROUP = 6
N_BUCKETS = N_GROUPS * PAIRS_PER_GROUP
BUCKET_ROWS = 32
SORT_TILE = 512
META_LANES = 256
META_EXPERT_LO, META_EXPERT_HI, META_N_USED, META_FILL, META_FIRST, META_WSLOT, META_NEXT_LO, META_NEXT_HI = range(8)
ROW_WIDTH = D_MODEL + ROUTER_LANES
INFO_BUCKET, INFO_W_LO, INFO_W_HI = 0, 1, 2


def _params(sem):
    return pltpu.CompilerParams(dimension_semantics=sem, vmem_limit_bytes=VMEM_LIMIT)


def _modulate(x, gain, shift, scale):
    ms = jnp.mean(x * x, axis=-1, keepdims=True)
    return (x * lax.rsqrt(ms + EPS) * gain) * (1.0 + scale) + shift


def _mod_spec(layer, row_fn, which):
    return pl.BlockSpec((None, None, None, 1, D_MODEL),
                        lambda *idx: (layer, row_fn(*idx), which, 0, 0))


def _full(shape):
    return pl.BlockSpec(shape, lambda *idx: (0,) * len(shape))


def _adaln_kernel(c_ref, w_ref, b_ref, o_ref):
    c = c_ref[...]
    s = c * jax.nn.sigmoid(c)
    o_ref[0] = jnp.dot(s, w_ref[0], precision=lax.Precision.HIGHEST,
                       preferred_element_type=F32) + b_ref[0]


def _adaln(cond, mod_w, mod_b):
    depth, d, n = mod_w.shape
    tn = 1024
    return pl.pallas_call(
        _adaln_kernel,
        grid=(depth, n // tn),
        in_specs=[_full((MOD_ROWS, d)),
                  pl.BlockSpec((1, d, tn), lambda l, j: (l, 0, j)),
                  pl.BlockSpec((1, 1, tn), lambda l, j: (l, 0, j))],
        out_specs=pl.BlockSpec((1, MOD_ROWS, tn), lambda l, j: (l, 0, j)),
        out_shape=jax.ShapeDtypeStruct((depth, MOD_ROWS, n), F32),
        compiler_params=_params(("arbitrary", "arbitrary")),
        name="adaln",
    )(cond, mod_w, mod_b.reshape(depth, 1, n))


def _head_mean_square(z, ones_bd):
    sq = z * z
    hi = sq.astype(BF16)
    lo = (sq - hi.astype(F32)).astype(BF16)
    return jnp.dot(jnp.concatenate([hi, lo], axis=1), ones_bd, preferred_element_type=F32)


def _head_norm_rope(z, gain, ones_bd, cos, sin, first_half):
    zn = z * lax.rsqrt(_head_mean_square(z, ones_bd) + EPS) * gain
    partner = jnp.where(first_half, pltpu.roll(zn, LANES - 16, 1), pltpu.roll(zn, 16, 1))
    return zn * cos + partner * sin


def _inproj0_kernel(x_ref, gain_ref, sh_ref, sc_ref, w_ref, cos_ref, sin_ref, qg_ref, kg_ref, ones_ref,
                    q_ref, k_ref, v_ref, p_ref):
    a = _modulate(x_ref[0], gain_ref[...], sh_ref[...], sc_ref[...])
    y = jnp.dot(a.astype(BF16), w_ref[...], preferred_element_type=F32)
    cos, sin, ones_bd = cos_ref[...], sin_ref[...], ones_ref[...]
    lane = lax.broadcasted_iota(jnp.int32, cos.shape, 1)
    first_half = (lane % 32) < 16
    for s in range(ATTN_WIDTH // LANES):
        r = _head_norm_rope(y[:, s * LANES:(s + 1) * LANES], qg_ref[...], ones_bd, cos, sin, first_half)
        r = (r * Q_SCALE).astype(BF16)
        q_ref[0, 2 * s] = r[:, :HEAD_DIM]
        q_ref[0, 2 * s + 1] = r[:, HEAD_DIM:]
    kr = _head_norm_rope(y[:, ATTN_WIDTH:ATTN_WIDTH + KV_WIDTH], kg_ref[...], ones_bd, cos, sin,
                         first_half).astype(BF16)
    k_ref[0, 0] = kr[:, :HEAD_DIM]
    k_ref[0, 1] = kr[:, HEAD_DIM:]
    vv = y[:, ATTN_WIDTH + KV_WIDTH:ATTN_WIDTH + 2 * KV_WIDTH].astype(BF16)
    v_ref[0, 0] = vv[:, :HEAD_DIM]
    v_ref[0, 1] = vv[:, HEAD_DIM:]
    p_ref[0] = y[:, ATTN_WIDTH + 2 * KV_WIDTH:].astype(BF16)


def _inproj0_ctx_kernel(x_ref, gain_ref, sh_ref, sc_ref, w_ref, kg_ref, ones_ref, k_ref, v_ref):
    a = _modulate(x_ref[0], gain_ref[...], sh_ref[...], sc_ref[...])
    y = jnp.dot(a.astype(BF16), w_ref[...], preferred_element_type=F32)
    z = y[:, :KV_WIDTH]
    kr = (z * lax.rsqrt(_head_mean_square(z, ones_ref[...]) + EPS) * kg_ref[...]).astype(BF16)
    k_ref[0, 0] = kr[:, :HEAD_DIM]
    k_ref[0, 1] = kr[:, HEAD_DIM:]
    vv = y[:, KV_WIDTH:].astype(BF16)
    v_ref[0, 0] = vv[:, :HEAD_DIM]
    v_ref[0, 1] = vv[:, HEAD_DIM:]


def _rope_tables(seq):
    t = jnp.arange(seq)
    row = (t // GRID_W).astype(F32)
    col = (t % GRID_W).astype(F32)
    half = HEAD_DIM // 2
    inv = ROPE_THETA ** (-jnp.arange(0, half, 2, dtype=F32) / half)
    ar, ac = row[:, None] * inv, col[:, None] * inv
    cos = jnp.concatenate([jnp.cos(ar), jnp.cos(ar), jnp.cos(ac), jnp.cos(ac)], axis=-1)
    sin = jnp.concatenate([-jnp.sin(ar), jnp.sin(ar), -jnp.sin(ac), jnp.sin(ac)], axis=-1)
    return jnp.tile(cos, (1, LANES // HEAD_DIM)), jnp.tile(sin, (1, LANES // HEAD_DIM))


def _head_mean_matrix():
    r = jnp.arange(LANES)
    same = (r[:, None] // HEAD_DIM) == (r[None, :] // HEAD_DIM)
    block = jnp.where(same, 1.0 / HEAD_DIM, 0.0).astype(BF16)
    return jnp.concatenate([block, block], axis=0)


def _inproj0(x, mods, gain, w_in, q_gain, k_gain, tm):
    bsz, seq, d = x.shape
    cos, sin = _rope_tables(seq)
    qg = jnp.tile(q_gain, LANES // HEAD_DIM).reshape(1, LANES)
    kg = jnp.tile(k_gain, LANES // HEAD_DIM).reshape(1, LANES)
    head = lambda n: pl.BlockSpec((1, n, tm, HEAD_DIM), lambda b, i: (b, 0, i, 0))
    return pl.pallas_call(
        _inproj0_kernel,
        grid=(bsz, seq // tm),
        in_specs=[pl.BlockSpec((1, tm, d), lambda b, i: (b, i, 0)),
                  _full((1, d)),
                  _mod_spec(0, lambda b, i: b, 0),
                  _mod_spec(0, lambda b, i: b, 1),
                  _full((d, EVEN_IN)),
                  pl.BlockSpec((tm, LANES), lambda b, i: (i, 0)),
                  pl.BlockSpec((tm, LANES), lambda b, i: (i, 0)),
                  _full((1, LANES)), _full((1, LANES)), _full((2 * LANES, LANES))],
        out_specs=[head(N_Q_HEADS), head(N_KV_HEADS), head(N_KV_HEADS),
                   pl.BlockSpec((1, tm, POOL_WIDTH), lambda b, i: (b, i, 0))],
        out_shape=[jax.ShapeDtypeStruct((bsz, N_Q_HEADS, seq, HEAD_DIM), BF16),
                   jax.ShapeDtypeStruct((bsz, N_KV_HEADS, seq, HEAD_DIM), BF16),
                   jax.ShapeDtypeStruct((bsz, N_KV_HEADS, seq, HEAD_DIM), BF16),
                   jax.ShapeDtypeStruct((bsz, seq, POOL_WIDTH), BF16)],
        compiler_params=_params(("parallel", "parallel")),
        name="inproj0",
    )(x, gain.reshape(1, d), mods, mods, w_in.astype(BF16), cos, sin, qg, kg, _head_mean_matrix())


def _inproj0_ctx(ctx, mods, ctx_row, gain, w_kv, k_gain):
    bsz, n_ctx, d = ctx.shape
    kg = jnp.tile(k_gain, LANES // HEAD_DIM).reshape(1, LANES)
    head = pl.BlockSpec((1, N_KV_HEADS, n_ctx, HEAD_DIM), lambda b: (b, 0, 0, 0))
    return pl.pallas_call(
        _inproj0_ctx_kernel,
        grid=(bsz,),
        in_specs=[pl.BlockSpec((1, n_ctx, d), lambda b: (b, 0, 0)),
                  _full((1, d)),
                  _mod_spec(0, lambda b: ctx_row, 0),
                  _mod_spec(0, lambda b: ctx_row, 1),
                  _full((d, 2 * KV_WIDTH)),
                  _full((1, LANES)), _full((2 * LANES, LANES))],
        out_specs=[head, head],
        out_shape=[jax.ShapeDtypeStruct((bsz, N_KV_HEADS, n_ctx, HEAD_DIM), BF16)] * 2,
        compiler_params=_params(("parallel",)),
        name="inproj0_ctx",
    )(ctx, gain.reshape(1, d), mods, mods, w_kv.astype(BF16), kg, _head_mean_matrix())


def _attn_kernel(q_ref, kl_ref, vl_ref, kc_ref, vc_ref, o_ref, ksq_sc, *, tq, tk):
    rows = Q_PER_KV * tq
    q = q_ref[0].reshape(rows, HEAD_DIM)
    seq = kl_ref.shape[2]
    chunks = [(kc_ref, vc_ref, 0, kc_ref.shape[2])]
    chunks += [(kl_ref, vl_ref, c * tk, tk) for c in range(seq // tk)]
    scores = lambda k: lax.dot_general(q, k, (((1,), (1,)), ((), ())), preferred_element_type=F32)

    @pl.when(pl.program_id(2) == 0)
    def _():
        def largest_sq_norm(k_ref):
            kf = k_ref[0, 0].astype(F32)
            return jnp.max(jnp.sum(kf * kf, axis=1, keepdims=True), axis=0, keepdims=True)

        ksq_sc[...] = jnp.broadcast_to(jnp.maximum(largest_sq_norm(kl_ref), largest_sq_norm(kc_ref)),
                                       ksq_sc.shape)

    qf = q.astype(F32)
    bound = jnp.sqrt(jnp.sum(qf * qf, axis=1, keepdims=True) * ksq_sc[0:1, 0:1])
    safe = jnp.max(bound) <= SAFE_SOFTMAX_SHIFT

    def finish(acc, l):
        o = acc / l
        o_ref[0] = jnp.concatenate([o[h * tq:(h + 1) * tq] for h in range(Q_PER_KV)], axis=1).astype(BF16)

    @pl.when(safe)
    def _():
        l = jnp.zeros((rows, 1), F32)
        acc = jnp.zeros((rows, HEAD_DIM), F32)
        for k_ref, v_ref, start, size in chunks:
            p = jnp.exp2(scores(k_ref[0, 0, start:start + size, :]) - bound)
            l = l + jnp.sum(p, axis=1, keepdims=True)
            acc = acc + jnp.dot(p.astype(BF16), v_ref[0, 0, start:start + size, :], preferred_element_type=F32)
        finish(acc, l)

    @pl.when(jnp.logical_not(safe))
    def _():
        m = jnp.full((rows, 1), -jnp.inf, F32)
        l = jnp.zeros((rows, 1), F32)
        acc = jnp.zeros((rows, HEAD_DIM), F32)
        for k_ref, v_ref, start, size in chunks:
            s = scores(k_ref[0, 0, start:start + size, :])
            m_new = jnp.maximum(m, jnp.max(s, axis=1, keepdims=True))
            alpha = jnp.exp2(m - m_new)
            p = jnp.exp2(s - m_new)
            l = alpha * l + jnp.sum(p, axis=1, keepdims=True)
            acc = alpha * acc + jnp.dot(p.astype(BF16), v_ref[0, 0, start:start + size, :],
                                        preferred_element_type=F32)
            m = m_new
        finish(acc, l)


def _attention(q, k, v, kc, vc, tq, tk):
    bsz, _, seq, _ = q.shape
    n_ctx = kc.shape[2]
    kv_spec = lambda n: pl.BlockSpec((1, 1, n, HEAD_DIM), lambda b, g, i: (b, g, 0, 0))
    return pl.pallas_call(
        functools.partial(_attn_kernel, tq=tq, tk=tk),
        grid=(bsz, N_KV_HEADS, seq // tq),
        in_specs=[pl.BlockSpec((1, Q_PER_KV, tq, HEAD_DIM), lambda b, g, i: (b, g, i, 0)),
                  kv_spec(seq), kv_spec(seq), kv_spec(n_ctx), kv_spec(n_ctx)],
        out_specs=pl.BlockSpec((1, tq, Q_PER_KV * HEAD_DIM), lambda b, g, i: (b, i, g)),
        out_shape=jax.ShapeDtypeStruct((bsz, seq, ATTN_WIDTH), BF16),
        scratch_shapes=[pltpu.VMEM((8, LANES), F32)],
        compiler_params=_params(("arbitrary", "arbitrary", "arbitrary")),
        name="attention",
    )(q, k, v, kc, vc)


def _route(logits):
    lane = lax.broadcasted_iota(jnp.int32, logits.shape, 1).astype(F32)
    neg = -jnp.inf
    big = float(ROUTER_LANES)
    first_index = lambda mask: jnp.min(jnp.where(mask, lane, big), axis=1, keepdims=True)
    is_g = lane < N_GROUPS
    gm = jnp.max(jnp.where(is_g, logits, neg), axis=1, keepdims=True)
    gidx = first_index(is_g & (logits == gm))
    gden = jnp.sum(jnp.where(is_g, jnp.exp(logits - gm), 0.0), axis=1, keepdims=True)
    g_p = 1.0 / gden
    first = N_GROUPS + EXPERTS_PER_GROUP * gidx
    sel = (lane >= first) & (lane < first + EXPERTS_PER_GROUP)
    e1 = jnp.max(jnp.where(sel, logits, neg), axis=1, keepdims=True)
    i1 = first_index(sel & (logits == e1))
    rest = sel & (lane != i1)
    e2 = jnp.max(jnp.where(rest, logits, neg), axis=1, keepdims=True)
    i2 = first_index(rest & (logits == e2))
    p2 = jnp.exp(e2 - e1)
    w1 = g_p * (1.0 / (1.0 + p2))
    w2 = g_p * (p2 / (1.0 + p2))
    lo = jnp.minimum(i1, i2) - first
    hi = jnp.maximum(i1, i2) - first
    pair = jnp.where(lo == 0, hi - 1, jnp.where(lo == 1, hi + 1, PAIRS_PER_GROUP - 1.0))
    bucket = PAIRS_PER_GROUP * gidx + pair
    w_lo = jnp.where(i1 < i2, w1, w2)
    w_hi = jnp.where(i1 < i2, w2, w1)
    return jnp.where(lane == INFO_BUCKET, bucket,
                     jnp.where(lane == INFO_W_LO, w_lo, jnp.where(lane == INFO_W_HI, w_hi, 0.0)))


def _bucket_onehot(info, sel):
    brow = lax.dot_general(sel, info.astype(BF16), (((1,), (1,)), ((), ())),
                           preferred_element_type=F32)[0:1]
    bid = lax.broadcasted_iota(jnp.int32, (BUCKET_ROWS, info.shape[0]), 0)
    return (bid == brow.astype(jnp.int32)).astype(F32)


def _bucket_selector():
    return jnp.zeros((8, ROUTER_LANES), F32).at[0, INFO_BUCKET].set(1.0).astype(BF16)


def _tail(y, x_res, gate1, gain2, shift2, scale2, rw_both, rbias, sel, h1_ref, info_ref, cnt_ref):
    h1 = x_res + gate1 * y
    h1_ref[0] = h1
    t = _modulate(h1, gain2, shift2, scale2)
    t_hi = t.astype(BF16)
    t_lo = (t - t_hi.astype(F32)).astype(BF16)
    both = jnp.dot(t_hi, rw_both, preferred_element_type=F32)
    logits = (both[:, :ROUTER_LANES] + both[:, ROUTER_LANES:]
              + jnp.dot(t_lo, rw_both[:, :ROUTER_LANES], preferred_element_type=F32)) + rbias
    info = _route(logits)
    info_ref[0] = info

    @pl.when((pl.program_id(0) == 0) & (pl.program_id(1) == 0))
    def _():
        cnt_ref[...] = jnp.zeros_like(cnt_ref)

    cnt_ref[...] += jnp.sum(_bucket_onehot(info, sel), axis=1, keepdims=True)


def _router_operands(rg_w, rg_b, re_w, re_b):
    d = rg_w.shape[0]
    w = jnp.concatenate([rg_w, re_w, jnp.zeros((d, ROUTER_LANES - N_GROUPS - N_EXPERTS), F32)], axis=1)
    b = jnp.concatenate([rg_b, re_b, jnp.zeros((ROUTER_LANES - N_GROUPS - N_EXPERTS,), F32)])
    w_hi = w.astype(BF16)
    w_lo = (w - w_hi.astype(F32)).astype(BF16)
    return jnp.concatenate([w_hi, w_lo], axis=1), b.reshape(1, ROUTER_LANES)


def _fill_halo(buf, main_ref, prev_ref, next_ref, tm, i, n_tiles, halo=HALO):
    buf[halo:halo + tm] = main_ref[0].astype(F32)
    buf[0:halo] = jnp.where(i > 0, prev_ref[0].astype(F32), 0.0)
    buf[halo + tm:2 * halo + tm] = jnp.where(i < n_tiles - 1, next_ref[0].astype(F32), 0.0)


def _halo_specs(tm, seq, width, halo=HALO):
    per = tm // halo
    last = seq // halo - 1
    return [pl.BlockSpec((1, tm, width), lambda b, i: (b, i, 0)),
            pl.BlockSpec((1, halo, width), lambda b, i: (b, jnp.maximum(i * per - 1, 0), 0)),
            pl.BlockSpec((1, halo, width), lambda b, i: (b, jnp.minimum((i + 1) * per, last), 0))]


def _out0_kernel(o_ref, p_ref, pprev_ref, pnext_ref, x_ref, g1_ref, gain2_ref, sh2_ref, sc2_ref,
                 poolw_ref, pscale_ref, wout_ref, rw_ref, rb_ref, sel_ref,
                 h1_ref, info_ref, cnt_ref, pbuf, *, tm, seq):
    i = pl.program_id(1)
    _fill_halo(pbuf, p_ref, pprev_ref, pnext_ref, tm, i, seq // tm)
    pos = i * tm + lax.broadcasted_iota(jnp.int32, (tm, 1), 0)
    pooled = []
    for g, w in enumerate(POOL_WINDOWS):
        sl = slice(g * POOL_GROUP, (g + 1) * POOL_GROUP)
        acc = pbuf[HALO - w // 2:HALO - w // 2 + tm, sl]
        for j in range(1 - w // 2, w - w // 2):
            acc = acc + pbuf[HALO + j:HALO + j + tm, sl]
        lo = jnp.clip(pos - w // 2, 0, seq)
        hi = jnp.clip(pos + w - w // 2, 0, seq)
        mean = acc * (1.0 / (hi - lo).astype(F32))
        dlt = (mean - pbuf[HALO:HALO + tm, sl]).astype(BF16)
        pooled.append((jnp.dot(dlt, poolw_ref[g], preferred_element_type=F32) * pscale_ref[:, sl]).astype(BF16))
    mixed = jnp.concatenate([o_ref[0]] + pooled, axis=1)
    y = jnp.dot(mixed, wout_ref[...], preferred_element_type=F32)
    _tail(y, x_ref[0], g1_ref[...], gain2_ref[...], sh2_ref[...], sc2_ref[...],
          rw_ref[...], rb_ref[...], sel_ref[...], h1_ref, info_ref, cnt_ref)


def _tail_specs(layer, d):
    by_batch = lambda b, i: b
    ins = [_mod_spec(layer, by_batch, 2), _full((1, d)), _mod_spec(layer, by_batch, 3),
           _mod_spec(layer, by_batch, 4)]
    return ins


def _tail_outs(bsz, seq, d, tm):
    specs = [pl.BlockSpec((1, tm, d), lambda b, i: (b, i, 0)),
             pl.BlockSpec((1, tm, ROUTER_LANES), lambda b, i: (b, i, 0)),
             _full((BUCKET_ROWS, LANES))]
    shapes = [jax.ShapeDtypeStruct((bsz, seq, d), F32),
              jax.ShapeDtypeStruct((bsz, seq, ROUTER_LANES), F32),
              jax.ShapeDtypeStruct((BUCKET_ROWS, LANES), F32)]
    return specs, shapes


def _out0(o, p, x, mods, gain2, pool_w, pool_scale, w_out, router, tm):
    bsz, seq, d = x.shape
    rw_both, rb = router
    out_specs, out_shapes = _tail_outs(bsz, seq, d, tm)
    return pl.pallas_call(
        functools.partial(_out0_kernel, tm=tm, seq=seq),
        grid=(bsz, seq // tm),
        in_specs=[pl.BlockSpec((1, tm, ATTN_WIDTH), lambda b, i: (b, i, 0))]
        + _halo_specs(tm, seq, POOL_WIDTH)
        + [pl.BlockSpec((1, tm, d), lambda b, i: (b, i, 0))]
        + _tail_specs(0, d)
        + [_full(pool_w.shape), _full((1, POOL_WIDTH)), _full(w_out.shape),
           _full(rw_both.shape), _full(rb.shape), _full((8, ROUTER_LANES))],
        out_specs=out_specs,
        out_shape=out_shapes,
        scratch_shapes=[pltpu.VMEM((tm + 2 * HALO, POOL_WIDTH), F32)],
        compiler_params=_params(("arbitrary", "arbitrary")),
        name="out0",
    )(o, p, p, p, x, mods, gain2.reshape(1, d), mods, mods,
      pool_w.astype(BF16), pool_scale.reshape(1, POOL_WIDTH), w_out.astype(BF16), rw_both, rb,
      _bucket_selector())


def _inproj1_kernel(pos_ref, posn_ref, h1_ref, g2_ref, gain_ref, sh_ref, sc_ref, w_ref, sgg_ref, sgw_ref, sgb_ref,
                    ys_ref, h_ref, yc_ref, z_ref, bg_ref, ybuf, sem, *, tm, n_steps):
    x = h1_ref[0] + g2_ref[...] * _fetch_sorted_rows(ys_ref, pos_ref, posn_ref, ybuf, sem, tm, n_steps)
    h_ref[0] = x
    a = _modulate(x, gain_ref[...], sh_ref[...], sc_ref[...])
    y = jnp.dot(a.astype(BF16), w_ref[...], preferred_element_type=F32)
    for g in range(SG_GROUPS):
        sl = slice(g * LANES, (g + 1) * LANES)
        u = y[:, sl]
        vg = y[:, SG_WIDTH + g * LANES:SG_WIDTH + (g + 1) * LANES]
        ms = jnp.mean(vg * vg, axis=-1, keepdims=True)
        vn = (vg * lax.rsqrt(ms + EPS) * sgg_ref[:, sl]).astype(BF16)
        for c in range(tm // SG_CHUNK):
            rows = slice(c * SG_CHUNK, (c + 1) * SG_CHUNK)
            s = jnp.dot(sgw_ref[g], vn[rows], preferred_element_type=F32) + sgb_ref[g]
            yc_ref[0, rows, sl] = (u[rows] * s).astype(BF16)
    hx = y[:, 2 * SG_WIDTH:2 * SG_WIDTH + CONV_WIDTH]
    bg_ref[0] = y[:, 2 * SG_WIDTH + CONV_WIDTH:2 * SG_WIDTH + 2 * CONV_WIDTH].astype(BF16)
    cg = y[:, 2 * SG_WIDTH + 2 * CONV_WIDTH:]
    z_ref[0] = (cg * hx).astype(BF16)


def _pos_spec(tm, nt, n_steps, ahead):
    return pl.BlockSpec((1, 1, tm), lambda b, i: (jnp.minimum(b * nt + i + ahead, n_steps - 1), 0, 0),
                        memory_space=pltpu.SMEM)


def _inproj1(h1, ys, pos, mods, gain, w_in, sg_gain, sg_w, sg_b, tm):
    bsz, seq, d = h1.shape
    nt = seq // tm
    n_steps = bsz * nt
    pos = pos.reshape(n_steps, 1, tm)
    sgb = jnp.broadcast_to(sg_b[:, :, None], (SG_GROUPS, SG_CHUNK, LANES))
    by_batch = lambda b, i: b
    wide = lambda w, dt: (pl.BlockSpec((1, tm, w), lambda b, i: (b, i, 0)),
                          jax.ShapeDtypeStruct((bsz, seq, w), dt))
    outs = [wide(d, F32), wide(SG_WIDTH, BF16), wide(CONV_WIDTH, BF16), wide(CONV_WIDTH, BF16)]
    return pl.pallas_call(
        functools.partial(_inproj1_kernel, tm=tm, n_steps=n_steps),
        grid=(bsz, nt),
        in_specs=[_pos_spec(tm, nt, n_steps, 0), _pos_spec(tm, nt, n_steps, 1),
                  pl.BlockSpec((1, tm, d), lambda b, i: (b, i, 0)),
                  _mod_spec(0, by_batch, 5),
                  _full((1, d)),
                  _mod_spec(1, by_batch, 0),
                  _mod_spec(1, by_batch, 1),
                  _full((d, ODD_IN)),
                  _full((1, SG_WIDTH)), _full(sg_w.shape), _full(sgb.shape),
                  pl.BlockSpec(memory_space=pl.ANY)],
        out_specs=[s for s, _ in outs],
        out_shape=[s for _, s in outs],
        scratch_shapes=[pltpu.VMEM((2, tm, d), F32), pltpu.SemaphoreType.DMA((2,))],
        compiler_params=_params(("arbitrary", "arbitrary")),
        name="inproj1",
    )(pos, pos, h1, mods, gain.reshape(1, d), mods, mods, w_in.astype(BF16), sg_gain.reshape(1, SG_WIDTH),
      sg_w.astype(BF16), sgb, ys)


def _out1_kernel(yc_ref, z_ref, zprev_ref, znext_ref, bg_ref, x_ref, g1_ref, gain2_ref, sh2_ref, sc2_ref,
                 convw_ref, wout_ref, rw_ref, rb_ref, sel_ref,
                 h1_ref, info_ref, cnt_ref, zbuf, *, tm, seq):
    i = pl.program_id(1)
    _fill_halo(zbuf, z_ref, zprev_ref, znext_ref, tm, i, seq // tm, HALO)
    zc = (zbuf[HALO - 1:HALO - 1 + tm] * convw_ref[0:1, :]
          + zbuf[HALO:HALO + tm] * convw_ref[1:2, :]
          + zbuf[HALO + 1:HALO + 1 + tm] * convw_ref[2:3, :])
    yd = (bg_ref[0].astype(F32) * zc).astype(BF16)
    y = (jnp.dot(yc_ref[0], wout_ref[0:SG_WIDTH, :], preferred_element_type=F32)
         + jnp.dot(yd, wout_ref[SG_WIDTH:, :], preferred_element_type=F32))
    _tail(y, x_ref[0], g1_ref[...], gain2_ref[...], sh2_ref[...], sc2_ref[...],
          rw_ref[...], rb_ref[...], sel_ref[...], h1_ref, info_ref, cnt_ref)


def _out1(yc, z, bg, x, mods, gain2, conv_w, w_out, router, tm):
    bsz, seq, d = x.shape
    rw_both, rb = router
    out_specs, out_shapes = _tail_outs(bsz, seq, d, tm)
    wide = pl.BlockSpec((1, tm, CONV_WIDTH), lambda b, i: (b, i, 0))
    return pl.pallas_call(
        functools.partial(_out1_kernel, tm=tm, seq=seq),
        grid=(bsz, seq // tm),
        in_specs=[wide] + _halo_specs(tm, seq, CONV_WIDTH, HALO) + [wide]
        + [pl.BlockSpec((1, tm, d), lambda b, i: (b, i, 0))]
        + _tail_specs(1, d)
        + [_full((3, CONV_WIDTH)), _full(w_out.shape),
           _full(rw_both.shape), _full(rb.shape), _full((8, ROUTER_LANES))],
        out_specs=out_specs,
        out_shape=out_shapes,
        scratch_shapes=[pltpu.VMEM((tm + 2 * HALO, CONV_WIDTH), F32)],
        compiler_params=_params(("arbitrary", "arbitrary")),
        name="out1",
    )(yc, z, z, z, bg, x, mods, gain2.reshape(1, d), mods, mods,
      conv_w.reshape(3, CONV_WIDTH), w_out.astype(BF16), rw_both, rb, _bucket_selector())


def _plan_pos_kernel(info_ref, sel_ref, cnt_ref, ltri_ref, utri_ref, pos_ref, meta_ref, start_sc, run_sc):
    @pl.when(pl.program_id(0) == 0)
    def _():
        padded = jnp.ceil(cnt_ref[...] * (1.0 / SORT_TILE)) * SORT_TILE
        incl = jnp.dot(ltri_ref[...], padded, precision=lax.Precision.HIGHEST, preferred_element_type=F32)
        start_sc[...] = incl - padded
        run_sc[...] = jnp.zeros_like(run_sc)
        ends = jnp.broadcast_to(incl[:, 0:1], (BUCKET_ROWS, META_LANES))
        bid = lax.broadcasted_iota(jnp.int32, ends.shape, 0)
        tile = lax.broadcasted_iota(jnp.int32, (1, META_LANES), 1)

        def bucket_of(row0):
            done = jnp.where((bid < N_BUCKETS) & (ends <= row0), 1.0, 0.0)
            return jnp.minimum(jnp.sum(done, axis=0, keepdims=True), N_BUCKETS - 1.0).astype(jnp.int32)

        def experts_of(bucket):
            grp = ((bucket >= PAIRS_PER_GROUP).astype(jnp.int32)
                   + (bucket >= 2 * PAIRS_PER_GROUP).astype(jnp.int32)
                   + (bucket >= 3 * PAIRS_PER_GROUP).astype(jnp.int32))
            pair = bucket - PAIRS_PER_GROUP * grp
            lo = (pair >= 3).astype(jnp.int32) + (pair >= 5).astype(jnp.int32)
            hi = jnp.where(pair == 0, 1, jnp.where((pair == 1) | (pair == 3), 2, 3))
            return EXPERTS_PER_GROUP * grp + lo, EXPERTS_PER_GROUP * grp + hi

        row0 = (tile * SORT_TILE).astype(F32)
        tb = bucket_of(row0)
        n_used = (incl[N_BUCKETS - 1:N_BUCKETS, 0:1] * (1.0 / SORT_TILE)).astype(jnp.int32)
        fill = (tile >= n_used - 1) | (tb != bucket_of(row0 + SORT_TILE))
        first = (tile == 0) | (tb != bucket_of(row0 - SORT_TILE))
        own_end = jnp.sum(jnp.where(bid == tb, ends, 0.0), axis=0, keepdims=True)
        nonempty = jnp.broadcast_to(padded[:, 0:1], ends.shape) > 0.0
        ordinal = jnp.sum(jnp.where((bid < tb) & nonempty, 1.0, 0.0), axis=0, keepdims=True)
        rows = [None] * 8
        rows[META_EXPERT_LO], rows[META_EXPERT_HI] = experts_of(tb)
        rows[META_N_USED] = jnp.broadcast_to(n_used, (1, META_LANES))
        rows[META_FILL] = fill.astype(jnp.int32)
        rows[META_FIRST] = first.astype(jnp.int32)
        rows[META_WSLOT] = (ordinal - 2.0 * jnp.floor(ordinal * 0.5)).astype(jnp.int32)
        rows[META_NEXT_LO], rows[META_NEXT_HI] = experts_of(bucket_of(own_end))
        for r, row in enumerate(rows):
            meta_ref[r:r + 1, :] = row

    tm = utri_ref.shape[0]
    base = start_sc[:, 0:1] + run_sc[:, 0:1]
    for k in range(pos_ref.shape[0]):
        oh = _bucket_onehot(info_ref[k * tm:(k + 1) * tm], sel_ref[...])
        before = jnp.dot(oh.astype(BF16), utri_ref[...], preferred_element_type=F32)
        pos_ref[k] = jnp.sum(oh * (before + base), axis=0, keepdims=True).astype(jnp.int32)
        base = base + jnp.sum(oh, axis=1, keepdims=True)
    run_sc[...] = jnp.broadcast_to(base - start_sc[:, 0:1], run_sc.shape)


def _sort_plan(info, cnt, tm):
    n = info.shape[0]
    sel = _bucket_selector()
    sub = 4 if (n // tm) % 4 == 0 else 1
    info_spec = pl.BlockSpec((sub * tm, ROUTER_LANES), lambda i: (i, 0))
    r = jnp.arange(BUCKET_ROWS)
    ltri = (r[:, None] >= r[None, :]).astype(F32)
    t = jnp.arange(tm)
    utri = (t[:, None] < t[None, :]).astype(BF16)
    return pl.pallas_call(
        _plan_pos_kernel,
        grid=(n // (sub * tm),),
        in_specs=[info_spec, _full(sel.shape), _full(cnt.shape), _full(ltri.shape), _full(utri.shape)],
        out_specs=[pl.BlockSpec((sub, 1, tm), lambda i: (i, 0, 0)), _full((8, META_LANES))],
        out_shape=[jax.ShapeDtypeStruct((n // tm, 1, tm), jnp.int32),
                   jax.ShapeDtypeStruct((8, META_LANES), jnp.int32)],
        scratch_shapes=[pltpu.VMEM((BUCKET_ROWS, LANES), F32), pltpu.VMEM((BUCKET_ROWS, LANES), F32)],
        compiler_params=_params(("arbitrary",)),
        name="plan_pos",
    )(info, sel, cnt, ltri, utri)


def _dispatch_kernel(pos_ref, fill_ref, h1_ref, gain2_ref, sh2_ref, sc2_ref, info_ref, xs_ref,
                     rowbuf, zbuf, sem, zsem, *, tm, n_steps, n_tiles):
    step = pl.program_id(0) * pl.num_programs(1) + pl.program_id(1)
    slot = step % 2

    @pl.when(step == 0)
    def _():
        zbuf[...] = jnp.zeros_like(zbuf)
        fill = lambda j: pltpu.make_async_copy(zbuf, xs_ref.at[pl.ds(j * SORT_TILE, SORT_TILE), 0], zsem)
        for j in range(n_tiles):
            pl.when(fill_ref[0, j] == 1)(lambda j=j: fill(j).start())
        for j in range(n_tiles):
            pl.when(fill_ref[0, j] == 1)(lambda j=j: fill(j).wait())

    def wait(s):
        pltpu.make_async_copy(rowbuf.at[s], xs_ref.at[pl.ds(0, tm), 0], sem.at[s]).wait()

    def send(s):
        @pl.when(step >= 2)
        def _():
            wait(s)

        rowbuf[s, :, 0:D_MODEL] = _modulate(h1_ref[0], gain2_ref[...], sh2_ref[...], sc2_ref[...])
        rowbuf[s, :, D_MODEL:] = info_ref[0]
        for r in range(tm):
            pltpu.make_async_copy(rowbuf.at[s, pl.ds(r, 1)], xs_ref.at[pos_ref[0, 0, r]],
                                  sem.at[s]).start(priority=r % 2)

    for s in range(2):
        pl.when(slot == s)(functools.partial(send, s))

    @pl.when(step == n_steps - 1)
    def _():
        wait(slot)
        if n_steps > 1:
            wait(1 - slot)


def _gmoe_kernel(ea_ref, eb_ref, nu_ref, first_ref, wslot_ref, na_ref, nb_ref, xs_ref, wg_hbm, wu_hbm, wd_hbm,
                 ys_ref, xbuf, ybuf, zbuf, wg_buf, wu_buf, wd_buf, sem_in, sem_out, zsem, wsem, *, layer):
    j = pl.program_id(0)
    n_used = nu_ref[0]
    slot = j % 2
    wslot = wslot_ref[j]
    tile = lambda ref, t: ref.at[pl.ds(t * SORT_TILE, SORT_TILE), 0]
    in_copy = lambda t, s: pltpu.make_async_copy(tile(xs_ref, t), xbuf.at[s], sem_in.at[s])
    out_copy = lambda t, s: pltpu.make_async_copy(ybuf.at[s], tile(ys_ref, t), sem_out.at[s])

    def weight_copies(e_lo, e_hi, s):
        return [pltpu.make_async_copy(hbm.at[layer, e], buf.at[s, which], wsem.at[s])
                for hbm, buf in ((wg_hbm, wg_buf), (wu_hbm, wu_buf), (wd_hbm, wd_buf))
                for which, e in ((0, e_lo), (1, e_hi))]

    @pl.when(j == 0)
    def _():
        in_copy(0, 0).start()
        for c in weight_copies(ea_ref[0], eb_ref[0], 0):
            c.start()

    @pl.when(j + 1 < n_used)
    def _():
        in_copy(j + 1, 1 - slot).start()

    @pl.when(j < n_used)
    def _():
        @pl.when(first_ref[j] == 1)
        def _():
            for c in weight_copies(ea_ref[j], eb_ref[j], wslot):
                c.wait()
            for c in weight_copies(na_ref[j], nb_ref[j], 1 - wslot):
                c.start()

        in_copy(j, slot).wait()

        @pl.when(j >= 2)
        def _():
            out_copy(j - 2, slot).wait()

        x = xbuf[slot, :, 0:D_MODEL].astype(BF16)

        def expert(which, w):
            gt = jnp.dot(x, wg_buf[wslot, which].astype(BF16), preferred_element_type=F32)
            up = jnp.dot(x, wu_buf[wslot, which].astype(BF16), preferred_element_type=F32)
            h = (gt * jax.nn.sigmoid(gt)) * up * w
            return jnp.dot(h.astype(BF16), wd_buf[wslot, which].astype(BF16), preferred_element_type=F32)

        w_lo = xbuf[slot, :, D_MODEL + INFO_W_LO:D_MODEL + INFO_W_LO + 1]
        w_hi = xbuf[slot, :, D_MODEL + INFO_W_HI:D_MODEL + INFO_W_HI + 1]
        ybuf[slot] = expert(0, w_lo) + expert(1, w_hi)
        out_copy(j, slot).start()

        @pl.when(j == n_used - 1)
        def _():
            out_copy(j, slot).wait()

            @pl.when(j >= 1)
            def _():
                out_copy(j - 1, 1 - slot).wait()

            for c in weight_copies(ea_ref[j], eb_ref[j], 1 - wslot):
                c.wait()

    @pl.when(j >= n_used)
    def _():
        @pl.when(j == n_used)
        def _():
            zbuf[...] = jnp.zeros_like(zbuf)

        fill = pltpu.make_async_copy(zbuf, tile(ys_ref, j), zsem)
        fill.start()
        fill.wait()


def _fetch_sorted_rows(ys_ref, pos_ref, posn_ref, ybuf, sem, tm, n_steps):
    step = pl.program_id(0) * pl.num_programs(1) + pl.program_id(1)
    slot = step % 2

    def issue(p_ref, s):
        for r in range(tm):
            pltpu.make_async_copy(ys_ref.at[p_ref[0, 0, r]], ybuf.at[s, pl.ds(r, 1)],
                                  sem.at[s]).start(priority=r % 2)

    pl.when(step == 0)(functools.partial(issue, pos_ref, 0))
    for s in range(2):
        pl.when((step + 1 < n_steps) & (slot == s))(functools.partial(issue, posn_ref, 1 - s))
    pltpu.make_async_copy(ys_ref.at[pl.ds(0, tm), 0], ybuf.at[slot], sem.at[slot]).wait()
    return ybuf[slot]


def _combine_kernel(pos_ref, posn_ref, h1_ref, g2_ref, ys_ref, o_ref, ybuf, sem, *, tm, n_steps):
    rows = _fetch_sorted_rows(ys_ref, pos_ref, posn_ref, ybuf, sem, tm, n_steps)
    o_ref[0] = h1_ref[0] + g2_ref[...] * rows


def _experts_sorted(h1, info, cnt, mods, layer, gain2, w_gate, w_up, w_down, tm):
    bsz, seq, d = h1.shape
    nt = seq // tm
    n_steps = bsz * nt
    n = bsz * seq
    n_sorted = n + N_BUCKETS * SORT_TILE
    n_tiles = n_sorted // SORT_TILE
    assert n % SORT_TILE == 0 and n_tiles <= META_LANES
    pos, meta = _sort_plan(info.reshape(n, ROUTER_LANES), cnt, tm)
    tile = lambda w: pl.BlockSpec((1, tm, w), lambda b, i: (b, i, 0))
    any_spec = pl.BlockSpec(memory_space=pl.ANY)
    by_batch = lambda b, i: b

    xs = pl.pallas_call(
        functools.partial(_dispatch_kernel, tm=tm, n_steps=n_steps, n_tiles=n_tiles),
        grid=(bsz, nt),
        in_specs=[_pos_spec(tm, nt, n_steps, 0),
                  pl.BlockSpec((1, META_LANES), lambda b, i: (0, 0), memory_space=pltpu.SMEM),
                  tile(d), _full((1, d)), _mod_spec(layer, by_batch, 3),
                  _mod_spec(layer, by_batch, 4), tile(ROUTER_LANES)],
        out_specs=any_spec,
        out_shape=jax.ShapeDtypeStruct((n_sorted, 1, ROW_WIDTH), F32),
        scratch_shapes=[pltpu.VMEM((2, tm, ROW_WIDTH), F32), pltpu.VMEM((SORT_TILE, ROW_WIDTH), F32),
                        pltpu.SemaphoreType.DMA((2,)), pltpu.SemaphoreType.DMA(())],
        compiler_params=_params(("arbitrary", "arbitrary")),
        name=f"dispatch{layer}",
    )(pos, meta[META_FILL:META_FILL + 1], h1, gain2.reshape(1, d), mods, mods, info)

    ys = pl.pallas_call(
        functools.partial(_gmoe_kernel, layer=layer),
        grid_spec=pltpu.PrefetchScalarGridSpec(
            num_scalar_prefetch=7,
            grid=(n_tiles,),
            in_specs=[any_spec] * 4,
            out_specs=any_spec,
            scratch_shapes=[pltpu.VMEM((2, SORT_TILE, ROW_WIDTH), F32), pltpu.VMEM((2, SORT_TILE, d), F32),
                            pltpu.VMEM((SORT_TILE, d), F32),
                            pltpu.VMEM((2, 2, d, D_EXPERT), F32), pltpu.VMEM((2, 2, d, D_EXPERT), F32),
                            pltpu.VMEM((2, 2, D_EXPERT, d), F32),
                            pltpu.SemaphoreType.DMA((2,)), pltpu.SemaphoreType.DMA((2,)),
                            pltpu.SemaphoreType.DMA(()), pltpu.SemaphoreType.DMA((2,))]),
        out_shape=jax.ShapeDtypeStruct((n_sorted, 1, d), F32),
        compiler_params=_params(("arbitrary",)),
        name=f"experts{layer}",
    )(meta[META_EXPERT_LO], meta[META_EXPERT_HI], meta[META_N_USED, :1], meta[META_FIRST], meta[META_WSLOT],
      meta[META_NEXT_LO], meta[META_NEXT_HI], xs, w_gate, w_up, w_down)
    return ys, pos


def _combine(h1, ys, pos, mods, layer, tm):
    bsz, seq, d = h1.shape
    nt = seq // tm
    n_steps = bsz * nt
    tile = lambda w: pl.BlockSpec((1, tm, w), lambda b, i: (b, i, 0))
    return pl.pallas_call(
        functools.partial(_combine_kernel, tm=tm, n_steps=n_steps),
        grid=(bsz, nt),
        in_specs=[_pos_spec(tm, nt, n_steps, 0), _pos_spec(tm, nt, n_steps, 1), tile(d),
                  _mod_spec(layer, lambda b, i: b, 5), pl.BlockSpec(memory_space=pl.ANY)],
        out_specs=tile(d),
        out_shape=jax.ShapeDtypeStruct((bsz, seq, d), F32),
        scratch_shapes=[pltpu.VMEM((2, tm, d), F32), pltpu.SemaphoreType.DMA((2,))],
        compiler_params=_params(("arbitrary", "arbitrary")),
        name=f"combine{layer}",
    )(pos, pos, h1, mods, ys)


def kernel(x, c, ctx, c_ctx, mod_w, mod_b, norm1_g, norm2_g, even_w_in, q_gain, k_gain, pool_w, pool_scale,
           even_w_out, odd_w_in, sg_gain, sg_w, sg_b, conv_w, odd_w_out, router_g_w, router_g_b,
           router_e_w, router_e_b, w_gate, w_up, w_down):
    bsz, seq, d = x.shape
    tm = min(512, seq)
    tm_proj = min(1024, seq)
    cond = jnp.zeros((MOD_ROWS, d), F32).at[:bsz].set(c).at[bsz].set(c_ctx)
    mods = _adaln(cond, mod_w, mod_b).reshape(mod_w.shape[0], MOD_ROWS, 6, 1, d)

    q, k, v, p = _inproj0(x, mods, norm1_g[0], even_w_in[0], q_gain[0], k_gain[0], tm_proj)
    kc, vc = _inproj0_ctx(ctx, mods, bsz, norm1_g[0], even_w_in[0][:, ATTN_WIDTH:ATTN_WIDTH + 2 * KV_WIDTH],
                          k_gain[0])
    o = _attention(q, k, v, kc, vc, tq=min(512, seq), tk=min(2048, seq))
    router0 = _router_operands(router_g_w[0], router_g_b[0], router_e_w[0], router_e_b[0])
    h1, info, cnt = _out0(o, p, x, mods, norm2_g[0], pool_w[0], pool_scale[0], even_w_out[0], router0, tm_proj)
    ys, pos = _experts_sorted(h1, info, cnt, mods, 0, norm2_g[0], w_gate, w_up, w_down, tm)

    h, yc, z, bg = _inproj1(h1, ys, pos, mods, norm1_g[1], odd_w_in[0], sg_gain[0], sg_w[0], sg_b[0], tm_proj)
    router1 = _router_operands(router_g_w[1], router_g_b[1], router_e_w[1], router_e_b[1])
    h1, info, cnt = _out1(yc, z, bg, h, mods, norm2_g[1], conv_w[0], odd_w_out[0], router1, tm_proj)
    ys, pos = _experts_sorted(h1, info, cnt, mods, 1, norm2_g[1], w_gate, w_up, w_down, tm)
    return _combine(h1, ys, pos, mods, 1, tm)
```

```python
import functools

import jax
import jax.numpy as jnp
from jax import lax
from jax.experimental import pallas as pl
from jax.experimental.pallas import tpu as pltpu

F32 = jnp.float32
BF16 = jnp.bfloat16

D_MODEL = 1024
GRID_W = 64
EPS = 1e-6
N_Q_HEADS = 8
N_KV_HEADS = 2
HEAD_DIM = 64
Q_PER_KV = N_Q_HEADS // N_KV_HEADS
ATTN_WIDTH = N_Q_HEADS * HEAD_DIM
KV_WIDTH = N_KV_HEADS * HEAD_DIM
ROPE_THETA = 10000.0
POOL_WINDOWS = (2, 4, 8, 16)
POOL_GROUP = 128
POOL_WIDTH = POOL_GROUP * len(POOL_WINDOWS)
SG_GROUPS = 4
SG_CHUNK = 128
SG_WIDTH = 512
CONV_WIDTH = 512
EVEN_IN = ATTN_WIDTH + 2 * KV_WIDTH + POOL_WIDTH
ODD_IN = 2 * SG_WIDTH + 3 * CONV_WIDTH
N_GROUPS = 4
EXPERTS_PER_GROUP = 4
N_EXPERTS = 16
D_EXPERT = 256

Q_SCALE = HEAD_DIM ** -0.5 * 1.4426950408889634
SAFE_SOFTMAX_SHIFT = 60.0
SCORE_BOUND_MARGIN = 1.02
LANES = 128
HALO = 16
ROUTER_LANES = 128
MOD_ROWS = 16
VMEM_LIMIT = 48 * 1024 * 1024

PAIRS_PER_GROUP = 6
N_BUCKETS = N_GROUPS * PAIRS_PER_GROUP
BUCKET_ROWS = 32
SORT_TILE = 512
META_LANES = 256
META_EXPERT_LO, META_EXPERT_HI, META_N_USED, META_FILL, META_FIRST, META_WSLOT, META_NEXT_LO, META_NEXT_HI = range(8)
ROW_WIDTH = D_MODEL + ROUTER_LANES
INFO_BUCKET, INFO_W_LO, INFO_W_HI = 0, 1, 2


def _params(sem):
    return pltpu.CompilerParams(dimension_semantics=sem, vmem_limit_bytes=VMEM_LIMIT)


def _modulate(x, gain, shift, scale):
    ms = jnp.mean(x * x, axis=-1, keepdims=True)
    return (x * lax.rsqrt(ms + EPS) * gain) * (1.0 + scale) + shift


def _mod_spec(layer, row_fn, which):
    return pl.BlockSpec((None, None, None, 1, D_MODEL),
                        lambda *idx: (layer, row_fn(*idx), which, 0, 0))


def _full(shape):
    return pl.BlockSpec(shape, lambda *idx: (0,) * len(shape))


def _adaln_kernel(c_ref, w_ref, b_ref, o_ref):
    c = c_ref[...]
    s = c * jax.nn.sigmoid(c)
    o_ref[0] = jnp.dot(s, w_ref[0], precision=lax.Precision.HIGHEST,
                       preferred_element_type=F32) + b_ref[0]


def _adaln(cond, mod_w, mod_b):
    depth, d, n = mod_w.shape
    tn = 1024
    return pl.pallas_call(
        _adaln_kernel,
        grid=(depth, n // tn),
        in_specs=[_full((MOD_ROWS, d)),
                  pl.BlockSpec((1, d, tn), lambda l, j: (l, 0, j)),
                  pl.BlockSpec((1, 1, tn), lambda l, j: (l, 0, j))],
        out_specs=pl.BlockSpec((1, MOD_ROWS, tn), lambda l, j: (l, 0, j)),
        out_shape=jax.ShapeDtypeStruct((depth, MOD_ROWS, n), F32),
        compiler_params=_params(("arbitrary", "arbitrary")),
        name="adaln",
    )(cond, mod_w, mod_b.reshape(depth, 1, n))


def _head_mean_square(z, ones_bd):
    sq = z * z
    hi = sq.astype(BF16)
    lo = (sq - hi.astype(F32)).astype(BF16)
    return jnp.dot(jnp.concatenate([hi, lo], axis=1), ones_bd, preferred_element_type=F32)


def _head_norm_rope(z, gain, ones_bd, cos, sin, first_half):
    zn = z * lax.rsqrt(_head_mean_square(z, ones_bd) + EPS) * gain
    partner = jnp.where(first_half, pltpu.roll(zn, LANES - 16, 1), pltpu.roll(zn, 16, 1))
    return zn * cos + partner * sin


def _inproj0_kernel(x_ref, gain_ref, sh_ref, sc_ref, w_ref, cos_ref, sin_ref, qg_ref, kg_ref, ones_ref,
                    q_ref, k_ref, v_ref, p_ref):
    a = _modulate(x_ref[0], gain_ref[...], sh_ref[...], sc_ref[...])
    y = jnp.dot(a.astype(BF16), w_ref[...], preferred_element_type=F32)
    cos, sin, ones_bd = cos_ref[...], sin_ref[...], ones_ref[...]
    lane = lax.broadcasted_iota(jnp.int32, cos.shape, 1)
    first_half = (lane % 32) < 16
    for s in range(ATTN_WIDTH // LANES):
        r = _head_norm_rope(y[:, s * LANES:(s + 1) * LANES], qg_ref[...], ones_bd, cos, sin, first_half)
        r = (r * Q_SCALE).astype(BF16)
        q_ref[0, 2 * s] = r[:, :HEAD_DIM]
        q_ref[0, 2 * s + 1] = r[:, HEAD_DIM:]
    kr = _head_norm_rope(y[:, ATTN_WIDTH:ATTN_WIDTH + KV_WIDTH], kg_ref[...], ones_bd, cos, sin,
                         first_half).astype(BF16)
    k_ref[0, 0] = kr[:, :HEAD_DIM]
    k_ref[0, 1] = kr[:, HEAD_DIM:]
    vv = y[:, ATTN_WIDTH + KV_WIDTH:ATTN_WIDTH + 2 * KV_WIDTH].astype(BF16)
    v_ref[0, 0] = vv[:, :HEAD_DIM]
    v_ref[0, 1] = vv[:, HEAD_DIM:]
    p_ref[0] = y[:, ATTN_WIDTH + 2 * KV_WIDTH:].astype(BF16)


def _inproj0_ctx_kernel(x_ref, gain_ref, sh_ref, sc_ref, w_ref, kg_ref, ones_ref, k_ref, v_ref):
    a = _modulate(x_ref[0], gain_ref[...], sh_ref[...], sc_ref[...])
    y = jnp.dot(a.astype(BF16), w_ref[...], preferred_element_type=F32)
    z = y[:, :KV_WIDTH]
    kr = (z * lax.rsqrt(_head_mean_square(z, ones_ref[...]) + EPS) * kg_ref[...]).astype(BF16)
    k_ref[0, 0] = kr[:, :HEAD_DIM]
    k_ref[0, 1] = kr[:, HEAD_DIM:]
    vv = y[:, KV_WIDTH:].astype(BF16)
    v_ref[0, 0] = vv[:, :HEAD_DIM]
    v_ref[0, 1] = vv[:, HEAD_DIM:]


def _rope_tables(seq):
    t = jnp.arange(seq)
    row = (t // GRID_W).astype(F32)
    col = (t % GRID_W).astype(F32)
    half = HEAD_DIM // 2
    inv = ROPE_THETA ** (-jnp.arange(0, half, 2, dtype=F32) / half)
    ar, ac = row[:, None] * inv, col[:, None] * inv
    cos = jnp.concatenate([jnp.cos(ar), jnp.cos(ar), jnp.cos(ac), jnp.cos(ac)], axis=-1)
    sin = jnp.concatenate([-jnp.sin(ar), jnp.sin(ar), -jnp.sin(ac), jnp.sin(ac)], axis=-1)
    return jnp.tile(cos, (1, LANES // HEAD_DIM)), jnp.tile(sin, (1, LANES // HEAD_DIM))


def _head_mean_matrix():
    r = jnp.arange(LANES)
    same = (r[:, None] // HEAD_DIM) == (r[None, :] // HEAD_DIM)
    block = jnp.where(same, 1.0 / HEAD_DIM, 0.0).astype(BF16)
    return jnp.concatenate([block, block], axis=0)


def _inproj0(x, mods, gain, w_in, q_gain, k_gain, tm):
    bsz, seq, d = x.shape
    cos, sin = _rope_tables(seq)
    qg = jnp.tile(q_gain, LANES // HEAD_DIM).reshape(1, LANES)
    kg = jnp.tile(k_gain, LANES // HEAD_DIM).reshape(1, LANES)
    head = lambda n: pl.BlockSpec((1, n, tm, HEAD_DIM), lambda b, i: (b, 0, i, 0))
    return pl.pallas_call(
        _inproj0_kernel,
        grid=(bsz, seq // tm),
        in_specs=[pl.BlockSpec((1, tm, d), lambda b, i: (b, i, 0)),
                  _full((1, d)),
                  _mod_spec(0, lambda b, i: b, 0),
                  _mod_spec(0, lambda b, i: b, 1),
                  _full((d, EVEN_IN)),
                  pl.BlockSpec((tm, LANES), lambda b, i: (i, 0)),
                  pl.BlockSpec((tm, LANES), lambda b, i: (i, 0)),
                  _full((1, LANES)), _full((1, LANES)), _full((2 * LANES, LANES))],
        out_specs=[head(N_Q_HEADS), head(N_KV_HEADS), head(N_KV_HEADS),
                   pl.BlockSpec((1, tm, POOL_WIDTH), lambda b, i: (b, i, 0))],
        out_shape=[jax.ShapeDtypeStruct((bsz, N_Q_HEADS, seq, HEAD_DIM), BF16),
                   jax.ShapeDtypeStruct((bsz, N_KV_HEADS, seq, HEAD_DIM), BF16),
                   jax.ShapeDtypeStruct((bsz, N_KV_HEADS, seq, HEAD_DIM), BF16),
                   jax.ShapeDtypeStruct((bsz, seq, POOL_WIDTH), BF16)],
        compiler_params=_params(("parallel", "parallel")),
        name="inproj0",
    )(x, gain.reshape(1, d), mods, mods, w_in.astype(BF16), cos, sin, qg, kg, _head_mean_matrix())


def _inproj0_ctx(ctx, mods, ctx_row, gain, w_kv, k_gain):
    bsz, n_ctx, d = ctx.shape
    kg = jnp.tile(k_gain, LANES // HEAD_DIM).reshape(1, LANES)
    head = pl.BlockSpec((1, N_KV_HEADS, n_ctx, HEAD_DIM), lambda b: (b, 0, 0, 0))
    return pl.pallas_call(
        _inproj0_ctx_kernel,
        grid=(bsz,),
        in_specs=[pl.BlockSpec((1, n_ctx, d), lambda b: (b, 0, 0)),
                  _full((1, d)),
                  _mod_spec(0, lambda b: ctx_row, 0),
                  _mod_spec(0, lambda b: ctx_row, 1),
                  _full((d, 2 * KV_WIDTH)),
                  _full((1, LANES)), _full((2 * LANES, LANES))],
        out_specs=[head, head],
        out_shape=[jax.ShapeDtypeStruct((bsz, N_KV_HEADS, n_ctx, HEAD_DIM), BF16)] * 2,
        compiler_params=_params(("parallel",)),
        name="inproj0_ctx",
    )(ctx, gain.reshape(1, d), mods, mods, w_kv.astype(BF16), kg, _head_mean_matrix())


def _attn_kernel(shift_ref, q_ref, kl_ref, vl_ref, kc_ref, vc_ref, o_ref, *, tq, tk):
    rows = Q_PER_KV * tq
    q = q_ref[0].reshape(rows, HEAD_DIM)
    seq = kl_ref.shape[2]
    chunks = [(kc_ref, vc_ref, 0, kc_ref.shape[2])]
    chunks += [(kl_ref, vl_ref, c * tk, tk) for c in range(seq // tk)]
    scores = lambda k: lax.dot_general(q, k, (((1,), (1,)), ((), ())), preferred_element_type=F32)

    bound = shift_ref[0, 0]
    safe = bound <= SAFE_SOFTMAX_SHIFT

    def finish(acc, l):
        o = acc / l
        o_ref[0] = jnp.concatenate([o[h * tq:(h + 1) * tq] for h in range(Q_PER_KV)], axis=1).astype(BF16)

    @pl.when(safe)
    def _():
        l = jnp.zeros((rows, 1), F32)
        acc = jnp.zeros((rows, HEAD_DIM), F32)
        for k_ref, v_ref, start, size in chunks:
            p = jnp.exp2(scores(k_ref[0, 0, start:start + size, :]) - bound)
            l = l + jnp.sum(p, axis=1, keepdims=True)
            acc = acc + jnp.dot(p.astype(BF16), v_ref[0, 0, start:start + size, :], preferred_element_type=F32)
        finish(acc, l)

    @pl.when(jnp.logical_not(safe))
    def _():
        m = jnp.full((rows, 1), -jnp.inf, F32)
        l = jnp.zeros((rows, 1), F32)
        acc = jnp.zeros((rows, HEAD_DIM), F32)
        for k_ref, v_ref, start, size in chunks:
            s = scores(k_ref[0, 0, start:start + size, :])
            m_new = jnp.maximum(m, jnp.max(s, axis=1, keepdims=True))
            alpha = jnp.exp2(m - m_new)
            p = jnp.exp2(s - m_new)
            l = alpha * l + jnp.sum(p, axis=1, keepdims=True)
            acc = alpha * acc + jnp.dot(p.astype(BF16), v_ref[0, 0, start:start + size, :],
                                        preferred_element_type=F32)
            m = m_new
        finish(acc, l)


def _score_bound(q_gain, k_gain):
    bound = HEAD_DIM * Q_SCALE * jnp.max(jnp.abs(q_gain)) * jnp.max(jnp.abs(k_gain)) * SCORE_BOUND_MARGIN
    return bound.reshape(1, 1).astype(F32)


def _attention(q, k, v, kc, vc, q_gain, k_gain, tq, tk):
    bsz, _, seq, _ = q.shape
    n_ctx = kc.shape[2]
    kv_spec = lambda n: pl.BlockSpec((1, 1, n, HEAD_DIM), lambda b, g, i: (b, g, 0, 0))
    return pl.pallas_call(
        functools.partial(_attn_kernel, tq=tq, tk=tk),
        grid=(bsz, N_KV_HEADS, seq // tq),
        in_specs=[pl.BlockSpec((1, 1), lambda b, g, i: (0, 0), memory_space=pltpu.SMEM),
                  pl.BlockSpec((1, Q_PER_KV, tq, HEAD_DIM), lambda b, g, i: (b, g, i, 0)),
                  kv_spec(seq), kv_spec(seq), kv_spec(n_ctx), kv_spec(n_ctx)],
        out_specs=pl.BlockSpec((1, tq, Q_PER_KV * HEAD_DIM), lambda b, g, i: (b, i, g)),
        out_shape=jax.ShapeDtypeStruct((bsz, seq, ATTN_WIDTH), BF16),
        compiler_params=_params(("parallel", "parallel", "parallel")),
        name="attention",
    )(_score_bound(q_gain, k_gain), q, k, v, kc, vc)


def _route(logits):
    lane = lax.broadcasted_iota(jnp.int32, logits.shape, 1).astype(F32)
    neg = -jnp.inf
    big = float(ROUTER_LANES)
    first_index = lambda mask: jnp.min(jnp.where(mask, lane, big), axis=1, keepdims=True)
    is_g = lane < N_GROUPS
    gm = jnp.max(jnp.where(is_g, logits, neg), axis=1, keepdims=True)
    gidx = first_index(is_g & (logits == gm))
    gden = jnp.sum(jnp.where(is_g, jnp.exp(logits - gm), 0.0), axis=1, keepdims=True)
    g_p = 1.0 / gden
    first = N_GROUPS + EXPERTS_PER_GROUP * gidx
    sel = (lane >= first) & (lane < first + EXPERTS_PER_GROUP)
    e1 = jnp.max(jnp.where(sel, logits, neg), axis=1, keepdims=True)
    i1 = first_index(sel & (logits == e1))
    rest = sel & (lane != i1)
    e2 = jnp.max(jnp.where(rest, logits, neg), axis=1, keepdims=True)
    i2 = first_index(rest & (logits == e2))
    p2 = jnp.exp(e2 - e1)
    w1 = g_p * (1.0 / (1.0 + p2))
    w2 = g_p * (p2 / (1.0 + p2))
    lo = jnp.minimum(i1, i2) - first
    hi = jnp.maximum(i1, i2) - first
    pair = jnp.where(lo == 0, hi - 1, jnp.where(lo == 1, hi + 1, PAIRS_PER_GROUP - 1.0))
    bucket = PAIRS_PER_GROUP * gidx + pair
    w_lo = jnp.where(i1 < i2, w1, w2)
    w_hi = jnp.where(i1 < i2, w2, w1)
    return jnp.where(lane == INFO_BUCKET, bucket,
                     jnp.where(lane == INFO_W_LO, w_lo, jnp.where(lane == INFO_W_HI, w_hi, 0.0)))


def _bucket_onehot(info, sel):
    brow = lax.dot_general(sel, info.astype(BF16), (((1,), (1,)), ((), ())),
                           preferred_element_type=F32)[0:1]
    bid = lax.broadcasted_iota(jnp.int32, (BUCKET_ROWS, info.shape[0]), 0)
    return (bid == brow.astype(jnp.int32)).astype(F32)


def _bucket_selector():
    return jnp.zeros((8, ROUTER_LANES), F32).at[0, INFO_BUCKET].set(1.0).astype(BF16)


def _tail(y, x_res, gate1, gain2, shift2, scale2, rw_both, rbias, sel, h1_ref, info_ref, cnt_ref):
    h1 = x_res + gate1 * y
    h1_ref[0] = h1
    t = _modulate(h1, gain2, shift2, scale2)
    t_hi = t.astype(BF16)
    t_lo = (t - t_hi.astype(F32)).astype(BF16)
    both = jnp.dot(t_hi, rw_both, preferred_element_type=F32)
    logits = (both[:, :ROUTER_LANES] + both[:, ROUTER_LANES:]
              + jnp.dot(t_lo, rw_both[:, :ROUTER_LANES], preferred_element_type=F32)) + rbias
    info = _route(logits)
    info_ref[0] = info

    @pl.when((pl.program_id(0) == 0) & (pl.program_id(1) == 0))
    def _():
        cnt_ref[...] = jnp.zeros_like(cnt_ref)

    cnt_ref[...] += jnp.sum(_bucket_onehot(info, sel), axis=1, keepdims=True)


def _router_operands(rg_w, rg_b, re_w, re_b):
    d = rg_w.shape[0]
    w = jnp.concatenate([rg_w, re_w, jnp.zeros((d, ROUTER_LANES - N_GROUPS - N_EXPERTS), F32)], axis=1)
    b = jnp.concatenate([rg_b, re_b, jnp.zeros((ROUTER_LANES - N_GROUPS - N_EXPERTS,), F32)])
    w_hi = w.astype(BF16)
    w_lo = (w - w_hi.astype(F32)).astype(BF16)
    return jnp.concatenate([w_hi, w_lo], axis=1), b.reshape(1, ROUTER_LANES)


def _fill_halo(buf, main_ref, prev_ref, next_ref, tm, i, n_tiles, halo=HALO):
    buf[halo:halo + tm] = main_ref[0].astype(F32)
    buf[0:halo] = jnp.where(i > 0, prev_ref[0].astype(F32), 0.0)
    buf[halo + tm:2 * halo + tm] = jnp.where(i < n_tiles - 1, next_ref[0].astype(F32), 0.0)


def _halo_specs(tm, seq, width, halo=HALO):
    per = tm // halo
    last = seq // halo - 1
    return [pl.BlockSpec((1, tm, width), lambda b, i: (b, i, 0)),
            pl.BlockSpec((1, halo, width), lambda b, i: (b, jnp.maximum(i * per - 1, 0), 0)),
            pl.BlockSpec((1, halo, width), lambda b, i: (b, jnp.minimum((i + 1) * per, last), 0))]


def _out0_kernel(o_ref, p_ref, pprev_ref, pnext_ref, x_ref, g1_ref, gain2_ref, sh2_ref, sc2_ref,
                 poolw_ref, pscale_ref, wout_ref, rw_ref, rb_ref, sel_ref,
                 h1_ref, info_ref, cnt_ref, pbuf, *, tm, seq):
    i = pl.program_id(1)
    _fill_halo(pbuf, p_ref, pprev_ref, pnext_ref, tm, i, seq // tm)
    pos = i * tm + lax.broadcasted_iota(jnp.int32, (tm, 1), 0)
    pooled = []
    for g, w in enumerate(POOL_WINDOWS):
        sl = slice(g * POOL_GROUP, (g + 1) * POOL_GROUP)
        acc = pbuf[HALO - w // 2:HALO - w // 2 + tm, sl]
        for j in range(1 - w // 2, w - w // 2):
            acc = acc + pbuf[HALO + j:HALO + j + tm, sl]
        lo = jnp.clip(pos - w // 2, 0, seq)
        hi = jnp.clip(pos + w - w // 2, 0, seq)
        mean = acc * (1.0 / (hi - lo).astype(F32))
        dlt = (mean - pbuf[HALO:HALO + tm, sl]).astype(BF16)
        pooled.append((jnp.dot(dlt, poolw_ref[g], preferred_element_type=F32) * pscale_ref[:, sl]).astype(BF16))
    mixed = jnp.concatenate([o_ref[0]] + pooled, axis=1)
    y = jnp.dot(mixed, wout_ref[...], preferred_element_type=F32)
    _tail(y, x_ref[0], g1_ref[...], gain2_ref[...], sh2_ref[...], sc2_ref[...],
          rw_ref[...], rb_ref[...], sel_ref[...], h1_ref, info_ref, cnt_ref)


def _tail_specs(layer, d):
    by_batch = lambda b, i: b
    ins = [_mod_spec(layer, by_batch, 2), _full((1, d)), _mod_spec(layer, by_batch, 3),
           _mod_spec(layer, by_batch, 4)]
    return ins


def _tail_outs(bsz, seq, d, tm):
    specs = [pl.BlockSpec((1, tm, d), lambda b, i: (b, i, 0)),
             pl.BlockSpec((1, tm, ROUTER_LANES), lambda b, i: (b, i, 0)),
             _full((BUCKET_ROWS, LANES))]
    shapes = [jax.ShapeDtypeStruct((bsz, seq, d), F32),
              jax.ShapeDtypeStruct((bsz, seq, ROUTER_LANES), F32),
              jax.ShapeDtypeStruct((BUCKET_ROWS, LANES), F32)]
    return specs, shapes


def _out0(o, p, x, mods, gain2, pool_w, pool_scale, w_out, router, tm):
    bsz, seq, d = x.shape
    rw_both, rb = router
    out_specs, out_shapes = _tail_outs(bsz, seq, d, tm)
    return pl.pallas_call(
        functools.partial(_out0_kernel, tm=tm, seq=seq),
        grid=(bsz, seq // tm),
        in_specs=[pl.BlockSpec((1, tm, ATTN_WIDTH), lambda b, i: (b, i, 0))]
        + _halo_specs(tm, seq, POOL_WIDTH)
        + [pl.BlockSpec((1, tm, d), lambda b, i: (b, i, 0))]
        + _tail_specs(0, d)
        + [_full(pool_w.shape), _full((1, POOL_WIDTH)), _full(w_out.shape),
           _full(rw_both.shape), _full(rb.shape), _full((8, ROUTER_LANES))],
        out_specs=out_specs,
        out_shape=out_shapes,
        scratch_shapes=[pltpu.VMEM((tm + 2 * HALO, POOL_WIDTH), F32)],
        compiler_params=_params(("arbitrary", "arbitrary")),
        name="out0",
    )(o, p, p, p, x, mods, gain2.reshape(1, d), mods, mods,
      pool_w.astype(BF16), pool_scale.reshape(1, POOL_WIDTH), w_out.astype(BF16), rw_both, rb,
      _bucket_selector())


def _inproj1_kernel(pos_ref, posn_ref, h1_ref, g2_ref, gain_ref, sh_ref, sc_ref, w_ref, sgg_ref, sgw_ref, sgb_ref,
                    ys_ref, h_ref, yc_ref, z_ref, bg_ref, ybuf, sem, *, tm, n_steps):
    x = h1_ref[0] + g2_ref[...] * _fetch_sorted_rows(ys_ref, pos_ref, posn_ref, ybuf, sem, tm, n_steps)
    h_ref[0] = x
    a = _modulate(x, gain_ref[...], sh_ref[...], sc_ref[...])
    y = jnp.dot(a.astype(BF16), w_ref[...], preferred_element_type=F32)
    for g in range(SG_GROUPS):
        sl = slice(g * LANES, (g + 1) * LANES)
        u = y[:, sl]
        vg = y[:, SG_WIDTH + g * LANES:SG_WIDTH + (g + 1) * LANES]
        ms = jnp.mean(vg * vg, axis=-1, keepdims=True)
        vn = (vg * lax.rsqrt(ms + EPS) * sgg_ref[:, sl]).astype(BF16)
        for c in range(tm // SG_CHUNK):
            rows = slice(c * SG_CHUNK, (c + 1) * SG_CHUNK)
            s = jnp.dot(sgw_ref[g], vn[rows], preferred_element_type=F32) + sgb_ref[g]
            yc_ref[0, rows, sl] = (u[rows] * s).astype(BF16)
    hx = y[:, 2 * SG_WIDTH:2 * SG_WIDTH + CONV_WIDTH]
    bg_ref[0] = y[:, 2 * SG_WIDTH + CONV_WIDTH:2 * SG_WIDTH + 2 * CONV_WIDTH].astype(BF16)
    cg = y[:, 2 * SG_WIDTH + 2 * CONV_WIDTH:]
    z_ref[0] = (cg * hx).astype(BF16)


def _pos_spec(tm, nt, n_steps, ahead):
    return pl.BlockSpec((1, 1, tm), lambda b, i: (jnp.minimum(b * nt + i + ahead, n_steps - 1), 0, 0),
                        memory_space=pltpu.SMEM)


def _inproj1(h1, ys, pos, mods, gain, w_in, sg_gain, sg_w, sg_b, tm):
    bsz, seq, d = h1.shape
    nt = seq // tm
    n_steps = bsz * nt
    sgb = jnp.broadcast_to(sg_b[:, :, None], (SG_GROUPS, SG_CHUNK, LANES))
    by_batch = lambda b, i: b
    wide = lambda w, dt: (pl.BlockSpec((1, tm, w), lambda b, i: (b, i, 0)),
                          jax.ShapeDtypeStruct((bsz, seq, w), dt))
    outs = [wide(d, F32), wide(SG_WIDTH, BF16), wide(CONV_WIDTH, BF16), wide(CONV_WIDTH, BF16)]
    return pl.pallas_call(
        functools.partial(_inproj1_kernel, tm=tm, n_steps=n_steps),
        grid=(bsz, nt),
        in_specs=[_pos_spec(tm, nt, n_steps, 0), _pos_spec(tm, nt, n_steps, 1),
                  pl.BlockSpec((1, tm, d), lambda b, i: (b, i, 0)),
                  _mod_spec(0, by_batch, 5),
                  _full((1, d)),
                  _mod_spec(1, by_batch, 0),
                  _mod_spec(1, by_batch, 1),
                  _full((d, ODD_IN)),
                  _full((1, SG_WIDTH)), _full(sg_w.shape), _full(sgb.shape),
                  pl.BlockSpec(memory_space=pl.ANY)],
        out_specs=[s for s, _ in outs],
        out_shape=[s for _, s in outs],
        scratch_shapes=[pltpu.VMEM((2, tm, d), F32), pltpu.SemaphoreType.DMA((2,))],
        compiler_params=_params(("arbitrary", "arbitrary")),
        name="inproj1",
    )(pos, pos, h1, mods, gain.reshape(1, d), mods, mods, w_in.astype(BF16), sg_gain.reshape(1, SG_WIDTH),
      sg_w.astype(BF16), sgb, ys)


def _out1_kernel(yc_ref, z_ref, zprev_ref, znext_ref, bg_ref, x_ref, g1_ref, gain2_ref, sh2_ref, sc2_ref,
                 convw_ref, wout_ref, rw_ref, rb_ref, sel_ref,
                 h1_ref, info_ref, cnt_ref, zbuf, *, tm, seq):
    i = pl.program_id(1)
    _fill_halo(zbuf, z_ref, zprev_ref, znext_ref, tm, i, seq // tm, HALO)
    zc = (zbuf[HALO - 1:HALO - 1 + tm] * convw_ref[0:1, :]
          + zbuf[HALO:HALO + tm] * convw_ref[1:2, :]
          + zbuf[HALO + 1:HALO + 1 + tm] * convw_ref[2:3, :])
    yd = (bg_ref[0].astype(F32) * zc).astype(BF16)
    y = (jnp.dot(yc_ref[0], wout_ref[0:SG_WIDTH, :], preferred_element_type=F32)
         + jnp.dot(yd, wout_ref[SG_WIDTH:, :], preferred_element_type=F32))
    _tail(y, x_ref[0], g1_ref[...], gain2_ref[...], sh2_ref[...], sc2_ref[...],
          rw_ref[...], rb_ref[...], sel_ref[...], h1_ref, info_ref, cnt_ref)


def _out1(yc, z, bg, x, mods, gain2, conv_w, w_out, router, tm):
    bsz, seq, d = x.shape
    rw_both, rb = router
    out_specs, out_shapes = _tail_outs(bsz, seq, d, tm)
    wide = pl.BlockSpec((1, tm, CONV_WIDTH), lambda b, i: (b, i, 0))
    return pl.pallas_call(
        functools.partial(_out1_kernel, tm=tm, seq=seq),
        grid=(bsz, seq // tm),
        in_specs=[wide] + _halo_specs(tm, seq, CONV_WIDTH, HALO) + [wide]
        + [pl.BlockSpec((1, tm, d), lambda b, i: (b, i, 0))]
        + _tail_specs(1, d)
        + [_full((3, CONV_WIDTH)), _full(w_out.shape),
           _full(rw_both.shape), _full(rb.shape), _full((8, ROUTER_LANES))],
        out_specs=out_specs,
        out_shape=out_shapes,
        scratch_shapes=[pltpu.VMEM((tm + 2 * HALO, CONV_WIDTH), F32)],
        compiler_params=_params(("arbitrary", "arbitrary")),
        name="out1",
    )(yc, z, z, z, bg, x, mods, gain2.reshape(1, d), mods, mods,
      conv_w.reshape(3, CONV_WIDTH), w_out.astype(BF16), rw_both, rb, _bucket_selector())


def _plan_pos_kernel(info_ref, sel_ref, cnt_ref, ltri_ref, utri_ref, pos_ref, meta_ref, start_sc, run_sc):
    @pl.when(pl.program_id(0) == 0)
    def _():
        padded = jnp.ceil(cnt_ref[...] * (1.0 / SORT_TILE)) * SORT_TILE
        incl = jnp.dot(ltri_ref[...], padded, precision=lax.Precision.HIGHEST, preferred_element_type=F32)
        start_sc[...] = incl - padded
        run_sc[...] = jnp.zeros_like(run_sc)
        ends = jnp.broadcast_to(incl[:, 0:1], (BUCKET_ROWS, META_LANES))
        bid = lax.broadcasted_iota(jnp.int32, ends.shape, 0)
        tile = lax.broadcasted_iota(jnp.int32, (1, META_LANES), 1)

        def bucket_of(row0):
            done = jnp.where((bid < N_BUCKETS) & (ends <= row0), 1.0, 0.0)
            return jnp.minimum(jnp.sum(done, axis=0, keepdims=True), N_BUCKETS - 1.0).astype(jnp.int32)

        def experts_of(bucket):
            grp = ((bucket >= PAIRS_PER_GROUP).astype(jnp.int32)
                   + (bucket >= 2 * PAIRS_PER_GROUP).astype(jnp.int32)
                   + (bucket >= 3 * PAIRS_PER_GROUP).astype(jnp.int32))
            pair = bucket - PAIRS_PER_GROUP * grp
            lo = (pair >= 3).astype(jnp.int32) + (pair >= 5).astype(jnp.int32)
            hi = jnp.where(pair == 0, 1, jnp.where((pair == 1) | (pair == 3), 2, 3))
            return EXPERTS_PER_GROUP * grp + lo, EXPERTS_PER_GROUP * grp + hi

        row0 = (tile * SORT_TILE).astype(F32)
        tb = bucket_of(row0)
        n_used = (incl[N_BUCKETS - 1:N_BUCKETS, 0:1] * (1.0 / SORT_TILE)).astype(jnp.int32)
        fill = (tile >= n_used - 1) | (tb != bucket_of(row0 + SORT_TILE))
        first = (tile == 0) | (tb != bucket_of(row0 - SORT_TILE))
        own_end = jnp.sum(jnp.where(bid == tb, ends, 0.0), axis=0, keepdims=True)
        nonempty = jnp.broadcast_to(padded[:, 0:1], ends.shape) > 0.0
        ordinal = jnp.sum(jnp.where((bid < tb) & nonempty, 1.0, 0.0), axis=0, keepdims=True)
        rows = [None] * 8
        rows[META_EXPERT_LO], rows[META_EXPERT_HI] = experts_of(tb)
        rows[META_N_USED] = jnp.broadcast_to(n_used, (1, META_LANES))
        rows[META_FILL] = fill.astype(jnp.int32)
        rows[META_FIRST] = first.astype(jnp.int32)
        rows[META_WSLOT] = (ordinal - 2.0 * jnp.floor(ordinal * 0.5)).astype(jnp.int32)
        rows[META_NEXT_LO], rows[META_NEXT_HI] = experts_of(bucket_of(own_end))
        for r, row in enumerate(rows):
            meta_ref[r:r + 1, :] = row

    tm = utri_ref.shape[0]
    base = start_sc[:, 0:1] + run_sc[:, 0:1]
    for k in range(pos_ref.shape[0]):
        oh = _bucket_onehot(info_ref[k * tm:(k + 1) * tm], sel_ref[...])
        before = jnp.dot(oh.astype(BF16), utri_ref[...], preferred_element_type=F32)
        pos_ref[k] = jnp.sum(oh * (before + base), axis=0, keepdims=True).astype(jnp.int32)
        base = base + jnp.sum(oh, axis=1, keepdims=True)
    run_sc[...] = jnp.broadcast_to(base - start_sc[:, 0:1], run_sc.shape)


def _sort_plan(info, cnt, tm):
    n = info.shape[0]
    sel = _bucket_selector()
    sub = 4 if (n // tm) % 4 == 0 else 1
    info_spec = pl.BlockSpec((sub * tm, ROUTER_LANES), lambda i: (i, 0))
    r = jnp.arange(BUCKET_ROWS)
    ltri = (r[:, None] >= r[None, :]).astype(F32)
    t = jnp.arange(tm)
    utri = (t[:, None] < t[None, :]).astype(BF16)
    return pl.pallas_call(
        _plan_pos_kernel,
        grid=(n // (sub * tm),),
        in_specs=[info_spec, _full(sel.shape), _full(cnt.shape), _full(ltri.shape), _full(utri.shape)],
        out_specs=[pl.BlockSpec((sub, 1, tm), lambda i: (i, 0, 0)), _full((8, META_LANES))],
        out_shape=[jax.ShapeDtypeStruct((n // tm, 1, tm), jnp.int32),
                   jax.ShapeDtypeStruct((8, META_LANES), jnp.int32)],
        scratch_shapes=[pltpu.VMEM((BUCKET_ROWS, LANES), F32), pltpu.VMEM((BUCKET_ROWS, LANES), F32)],
        compiler_params=_params(("arbitrary",)),
        name="plan_pos",
    )(info, sel, cnt, ltri, utri)


def _dispatch_kernel(pos_ref, fill_ref, h1_ref, gain2_ref, sh2_ref, sc2_ref, info_ref, xs_ref,
                     rowbuf, zbuf, sem, zsem, *, tm, n_steps, n_tiles):
    step = pl.program_id(0) * pl.num_programs(1) + pl.program_id(1)
    slot = step % 2

    @pl.when(step == 0)
    def _():
        zbuf[...] = jnp.zeros_like(zbuf)
        fill = lambda j: pltpu.make_async_copy(zbuf, xs_ref.at[pl.ds(j * SORT_TILE, SORT_TILE), 0], zsem)
        for j in range(n_tiles):
            pl.when(fill_ref[0, j] == 1)(lambda j=j: fill(j).start())
        for j in range(n_tiles):
            pl.when(fill_ref[0, j] == 1)(lambda j=j: fill(j).wait())

    def wait(s):
        pltpu.make_async_copy(rowbuf.at[s], xs_ref.at[pl.ds(0, tm), 0], sem.at[s]).wait()

    def send(s):
        @pl.when(step >= 2)
        def _():
            wait(s)

        rowbuf[s, :, 0:D_MODEL] = _modulate(h1_ref[0], gain2_ref[...], sh2_ref[...], sc2_ref[...])
        rowbuf[s, :, D_MODEL:] = info_ref[0]
        for r in range(tm):
            pltpu.make_async_copy(rowbuf.at[s, pl.ds(r, 1)], xs_ref.at[pos_ref[0, 0, r]],
                                  sem.at[s]).start(priority=r % 2)

    for s in range(2):
        pl.when(slot == s)(functools.partial(send, s))

    @pl.when(step == n_steps - 1)
    def _():
        wait(slot)
        if n_steps > 1:
            wait(1 - slot)


def _gmoe_kernel(ea_ref, eb_ref, nu_ref, first_ref, wslot_ref, na_ref, nb_ref, xs_ref, wg_hbm, wu_hbm, wd_hbm,
                 ys_ref, xbuf, ybuf, zbuf, wg_buf, wu_buf, wd_buf, sem_in, sem_out, zsem, wsem, *, layer):
    j = pl.program_id(0)
    n_used = nu_ref[0]
    slot = j % 2
    wslot = wslot_ref[j]
    tile = lambda ref, t: ref.at[pl.ds(t * SORT_TILE, SORT_TILE), 0]
    in_copy = lambda t, s: pltpu.make_async_copy(tile(xs_ref, t), xbuf.at[s], sem_in.at[s])
    out_copy = lambda t, s: pltpu.make_async_copy(ybuf.at[s], tile(ys_ref, t), sem_out.at[s])

    def weight_copies(e_lo, e_hi, s):
        return [pltpu.make_async_copy(hbm.at[layer, e], buf.at[s, which], wsem.at[s])
                for hbm, buf in ((wg_hbm, wg_buf), (wu_hbm, wu_buf), (wd_hbm, wd_buf))
                for which, e in ((0, e_lo), (1, e_hi))]

    @pl.when(j == 0)
    def _():
        in_copy(0, 0).start()
        for c in weight_copies(ea_ref[0], eb_ref[0], 0):
            c.start()

    @pl.when(j + 1 < n_used)
    def _():
        in_copy(j + 1, 1 - slot).start()

    @pl.when(j < n_used)
    def _():
        @pl.when(first_ref[j] == 1)
        def _():
            for c in weight_copies(ea_ref[j], eb_ref[j], wslot):
                c.wait()
            for c in weight_copies(na_ref[j], nb_ref[j], 1 - wslot):
                c.start()

        in_copy(j, slot).wait()

        @pl.when(j >= 2)
        def _():
            out_copy(j - 2, slot).wait()

        x = xbuf[slot, :, 0:D_MODEL].astype(BF16)

        def expert(which, w):
            gt = jnp.dot(x, wg_buf[wslot, which].astype(BF16), preferred_element_type=F32)
            up = jnp.dot(x, wu_buf[wslot, which].astype(BF16), preferred_element_type=F32)
            h = (gt * jax.nn.sigmoid(gt)) * up * w
            return jnp.dot(h.astype(BF16), wd_buf[wslot, which].astype(BF16), preferred_element_type=F32)

        w_lo = xbuf[slot, :, D_MODEL + INFO_W_LO:D_MODEL + INFO_W_LO + 1]
        w_hi = xbuf[slot, :, D_MODEL + INFO_W_HI:D_MODEL + INFO_W_HI + 1]
        ybuf[slot] = expert(0, w_lo) + expert(1, w_hi)
        out_copy(j, slot).start()

        @pl.when(j == n_used - 1)
        def _():
            out_copy(j, slot).wait()

            @pl.when(j >= 1)
            def _():
                out_copy(j - 1, 1 - slot).wait()

            for c in weight_copies(ea_ref[j], eb_ref[j], 1 - wslot):
                c.wait()

    @pl.when(j >= n_used)
    def _():
        @pl.when(j == n_used)
        def _():
            zbuf[...] = jnp.zeros_like(zbuf)

        fill = pltpu.make_async_copy(zbuf, tile(ys_ref, j), zsem)
        fill.start()
        fill.wait()


def _fetch_sorted_rows(ys_ref, pos_ref, posn_ref, ybuf, sem, tm, n_steps):
    step = pl.program_id(0) * pl.num_programs(1) + pl.program_id(1)
    slot = step % 2

    def issue(p_ref, s):
        for r in range(tm):
            pltpu.make_async_copy(ys_ref.at[p_ref[0, 0, r]], ybuf.at[s, pl.ds(r, 1)],
                                  sem.at[s]).start(priority=r % 2)

    pl.when(step == 0)(functools.partial(issue, pos_ref, 0))
    for s in range(2):
        pl.when((step + 1 < n_steps) & (slot == s))(functools.partial(issue, posn_ref, 1 - s))
    pltpu.make_async_copy(ys_ref.at[pl.ds(0, tm), 0], ybuf.at[slot], sem.at[slot]).wait()
    return ybuf[slot]


def _combine_kernel(pos_ref, posn_ref, h1_ref, g2_ref, ys_ref, o_ref, ybuf, sem, *, tm, n_steps):
    rows = _fetch_sorted_rows(ys_ref, pos_ref, posn_ref, ybuf, sem, tm, n_steps)
    o_ref[0] = h1_ref[0] + g2_ref[...] * rows


def _experts_sorted(h1, info, cnt, mods, layer, gain2, w_gate, w_up, w_down, tm):
    bsz, seq, d = h1.shape
    nt = seq // tm
    n_steps = bsz * nt
    n = bsz * seq
    n_sorted = n + N_BUCKETS * SORT_TILE
    n_tiles = n_sorted // SORT_TILE
    assert n % SORT_TILE == 0 and n_tiles <= META_LANES
    pos, meta = _sort_plan(info.reshape(n, ROUTER_LANES), cnt, tm)
    tile = lambda w: pl.BlockSpec((1, tm, w), lambda b, i: (b, i, 0))
    any_spec = pl.BlockSpec(memory_space=pl.ANY)
    by_batch = lambda b, i: b

    xs = pl.pallas_call(
        functools.partial(_dispatch_kernel, tm=tm, n_steps=n_steps, n_tiles=n_tiles),
        grid=(bsz, nt),
        in_specs=[_pos_spec(tm, nt, n_steps, 0),
                  pl.BlockSpec((1, META_LANES), lambda b, i: (0, 0), memory_space=pltpu.SMEM),
                  tile(d), _full((1, d)), _mod_spec(layer, by_batch, 3),
                  _mod_spec(layer, by_batch, 4), tile(ROUTER_LANES)],
        out_specs=any_spec,
        out_shape=jax.ShapeDtypeStruct((n_sorted, 1, ROW_WIDTH), F32),
        scratch_shapes=[pltpu.VMEM((2, tm, ROW_WIDTH), F32), pltpu.VMEM((SORT_TILE, ROW_WIDTH), F32),
                        pltpu.SemaphoreType.DMA((2,)), pltpu.SemaphoreType.DMA(())],
        compiler_params=_params(("arbitrary", "arbitrary")),
        name=f"dispatch{layer}",
    )(pos, meta[META_FILL:META_FILL + 1], h1, gain2.reshape(1, d), mods, mods, info)

    ys = pl.pallas_call(
        functools.partial(_gmoe_kernel, layer=layer),
        grid_spec=pltpu.PrefetchScalarGridSpec(
            num_scalar_prefetch=7,
            grid=(n_tiles,),
            in_specs=[any_spec] * 4,
            out_specs=any_spec,
            scratch_shapes=[pltpu.VMEM((2, SORT_TILE, ROW_WIDTH), F32), pltpu.VMEM((2, SORT_TILE, d), F32),
                            pltpu.VMEM((SORT_TILE, d), F32),
                            pltpu.VMEM((2, 2, d, D_EXPERT), F32), pltpu.VMEM((2, 2, d, D_EXPERT), F32),
                            pltpu.VMEM((2, 2, D_EXPERT, d), F32),
                            pltpu.SemaphoreType.DMA((2,)), pltpu.SemaphoreType.DMA((2,)),
                            pltpu.SemaphoreType.DMA(()), pltpu.SemaphoreType.DMA((2,))]),
        out_shape=jax.ShapeDtypeStruct((n_sorted, 1, d), F32),
        compiler_params=_params(("arbitrary",)),
        name=f"experts{layer}",
    )(meta[META_EXPERT_LO], meta[META_EXPERT_HI], meta[META_N_USED, :1], meta[META_FIRST], meta[META_WSLOT],
      meta[META_NEXT_LO], meta[META_NEXT_HI], xs, w_gate, w_up, w_down)
    return ys, pos


def _combine(h1, ys, pos, mods, layer, tm):
    bsz, seq, d = h1.shape
    nt = seq // tm
    n_steps = bsz * nt
    tile = lambda w: pl.BlockSpec((1, tm, w), lambda b, i: (b, i, 0))
    return pl.pallas_call(
        functools.partial(_combine_kernel, tm=tm, n_steps=n_steps),
        grid=(bsz, nt),
        in_specs=[_pos_spec(tm, nt, n_steps, 0), _pos_spec(tm, nt, n_steps, 1), tile(d),
                  _mod_spec(layer, lambda b, i: b, 5), pl.BlockSpec(memory_space=pl.ANY)],
        out_specs=tile(d),
        out_shape=jax.ShapeDtypeStruct((bsz, seq, d), F32),
        scratch_shapes=[pltpu.VMEM((2, tm, d), F32), pltpu.SemaphoreType.DMA((2,))],
        compiler_params=_params(("arbitrary", "arbitrary")),
        name=f"combine{layer}",
    )(pos, pos, h1, mods, ys)


def kernel(x, c, ctx, c_ctx, mod_w, mod_b, norm1_g, norm2_g, even_w_in, q_gain, k_gain, pool_w, pool_scale,
           even_w_out, odd_w_in, sg_gain, sg_w, sg_b, conv_w, odd_w_out, router_g_w, router_g_b,
           router_e_w, router_e_b, w_gate, w_up, w_down):
    bsz, seq, d = x.shape
    tm = min(512, seq)
    tm_proj = min(1024, seq)
    cond = jnp.zeros((MOD_ROWS, d), F32).at[:bsz].set(c).at[bsz].set(c_ctx)
    mods = _adaln(cond, mod_w, mod_b).reshape(mod_w.shape[0], MOD_ROWS, 6, 1, d)

    q, k, v, p = _inproj0(x, mods, norm1_g[0], even_w_in[0], q_gain[0], k_gain[0], tm_proj)
    kc, vc = _inproj0_ctx(ctx, mods, bsz, norm1_g[0], even_w_in[0][:, ATTN_WIDTH:ATTN_WIDTH + 2 * KV_WIDTH],
                          k_gain[0])
    o = _attention(q, k, v, kc, vc, q_gain[0], k_gain[0], tq=min(512, seq), tk=min(2048, seq))
    router0 = _router_operands(router_g_w[0], router_g_b[0], router_e_w[0], router_e_b[0])
    h1, info, cnt = _out0(o, p, x, mods, norm2_g[0], pool_w[0], pool_scale[0], even_w_out[0], router0, tm_proj)
    ys, pos = _experts_sorted(h1, info, cnt, mods, 0, norm2_g[0], w_gate, w_up, w_down, tm)

    h, yc, z, bg = _inproj1(h1, ys, pos, mods, norm1_g[1], odd_w_in[0], sg_gain[0], sg_w[0], sg_b[0], tm)
    router1 = _router_operands(router_g_w[1], router_g_b[1], router_e_w[1], router_e_b[1])
    h1, info, cnt = _out1(yc, z, bg, h, mods, norm2_g[1], conv_w[0], odd_w_out[0], router1, tm_proj)
    ys, pos = _experts_sorted(h1, info, cnt, mods, 1, norm2_g[1], w_gate, w_up, w_down, tm)
    return _combine(h1, ys, pos, mods, 1, tm)
```

```python
import functools

import jax
import jax.numpy as jnp
from jax import lax
from jax.experimental import pallas as pl
from jax.experimental.pallas import tpu as pltpu

F32 = jnp.float32
BF16 = jnp.bfloat16

D_MODEL = 1024
GRID_W = 64
EPS = 1e-6
N_Q_HEADS = 8
N_KV_HEADS = 2
HEAD_DIM = 64
Q_PER_KV = N_Q_HEADS // N_KV_HEADS
ATTN_WIDTH = N_Q_HEADS * HEAD_DIM
KV_WIDTH = N_KV_HEADS * HEAD_DIM
ROPE_THETA = 10000.0
POOL_WINDOWS = (2, 4, 8, 16)
POOL_GROUP = 128
POOL_WIDTH = POOL_GROUP * len(POOL_WINDOWS)
SG_GROUPS = 4
SG_CHUNK = 128
SG_WIDTH = 512
CONV_WIDTH = 512
EVEN_IN = ATTN_WIDTH + 2 * KV_WIDTH + POOL_WIDTH
ODD_IN = 2 * SG_WIDTH + 3 * CONV_WIDTH
N_GROUPS = 4
EXPERTS_PER_GROUP = 4
N_EXPERTS = 16
D_EXPERT = 256

Q_SCALE = HEAD_DIM ** -0.5 * 1.4426950408889634
SAFE_SOFTMAX_SHIFT = 60.0
SCORE_BOUND_MARGIN = 1.02
LANES = 128
HALO = 16
POOL_BLOCK = 128
ROUTER_LANES = 128
MOD_ROWS = 16
VMEM_LIMIT = 48 * 1024 * 1024

PAIRS_PER_GROUP = 6
N_BUCKETS = N_GROUPS * PAIRS_PER_GROUP
BUCKET_ROWS = 32
SORT_TILE = 512
META_LANES = 256
META_EXPERT_LO, META_EXPERT_HI, META_N_USED, META_FILL, META_FIRST, META_WSLOT, META_NEXT_LO, META_NEXT_HI = range(8)
ROW_WIDTH = D_MODEL + ROUTER_LANES
INFO_BUCKET, INFO_W_LO, INFO_W_HI = 0, 1, 2


def _params(sem):
    return pltpu.CompilerParams(dimension_semantics=sem, vmem_limit_bytes=VMEM_LIMIT)


def _modulate(x, gain, shift, scale):
    ms = jnp.mean(x * x, axis=-1, keepdims=True)
    return (x * lax.rsqrt(ms + EPS) * gain) * (1.0 + scale) + shift


def _mod_spec(layer, row_fn, which):
    return pl.BlockSpec((None, None, None, 1, D_MODEL),
                        lambda *idx: (layer, row_fn(*idx), which, 0, 0))


def _full(shape):
    return pl.BlockSpec(shape, lambda *idx: (0,) * len(shape))


def _adaln_kernel(c_ref, w_ref, b_ref, o_ref):
    c = c_ref[...]
    s = c * jax.nn.sigmoid(c)
    o_ref[0] = jnp.dot(s, w_ref[0], precision=lax.Precision.HIGHEST,
                       preferred_element_type=F32) + b_ref[0]


def _adaln(cond, mod_w, mod_b):
    depth, d, n = mod_w.shape
    tn = 1024
    return pl.pallas_call(
        _adaln_kernel,
        grid=(depth, n // tn),
        in_specs=[_full((MOD_ROWS, d)),
                  pl.BlockSpec((1, d, tn), lambda l, j: (l, 0, j)),
                  pl.BlockSpec((1, 1, tn), lambda l, j: (l, 0, j))],
        out_specs=pl.BlockSpec((1, MOD_ROWS, tn), lambda l, j: (l, 0, j)),
        out_shape=jax.ShapeDtypeStruct((depth, MOD_ROWS, n), F32),
        compiler_params=_params(("arbitrary", "arbitrary")),
        name="adaln",
    )(cond, mod_w, mod_b.reshape(depth, 1, n))


def _head_mean_square(z, ones_bd):
    sq = z * z
    hi = sq.astype(BF16)
    lo = (sq - hi.astype(F32)).astype(BF16)
    return jnp.dot(jnp.concatenate([hi, lo], axis=1), ones_bd, preferred_element_type=F32)


def _head_norm_rope(z, gain, ones_bd, cos, sin, first_half):
    zn = z * lax.rsqrt(_head_mean_square(z, ones_bd) + EPS) * gain
    partner = jnp.where(first_half, pltpu.roll(zn, LANES - 16, 1), pltpu.roll(zn, 16, 1))
    return zn * cos + partner * sin


def _inproj0_kernel(x_ref, gain_ref, sh_ref, sc_ref, w_ref, cos_ref, sin_ref, qg_ref, kg_ref, ones_ref,
                    q_ref, k_ref, v_ref, p_ref):
    a = _modulate(x_ref[0], gain_ref[...], sh_ref[...], sc_ref[...])
    y = jnp.dot(a.astype(BF16), w_ref[...], preferred_element_type=F32)
    cos, sin, ones_bd = cos_ref[...], sin_ref[...], ones_ref[...]
    lane = lax.broadcasted_iota(jnp.int32, cos.shape, 1)
    first_half = (lane % 32) < 16
    for s in range(ATTN_WIDTH // LANES):
        r = _head_norm_rope(y[:, s * LANES:(s + 1) * LANES], qg_ref[...], ones_bd, cos, sin, first_half)
        r = (r * Q_SCALE).astype(BF16)
        q_ref[0, 2 * s] = r[:, :HEAD_DIM]
        q_ref[0, 2 * s + 1] = r[:, HEAD_DIM:]
    kr = _head_norm_rope(y[:, ATTN_WIDTH:ATTN_WIDTH + KV_WIDTH], kg_ref[...], ones_bd, cos, sin,
                         first_half).astype(BF16)
    k_ref[0, 0] = kr[:, :HEAD_DIM]
    k_ref[0, 1] = kr[:, HEAD_DIM:]
    vv = y[:, ATTN_WIDTH + KV_WIDTH:ATTN_WIDTH + 2 * KV_WIDTH].astype(BF16)
    v_ref[0, 0] = vv[:, :HEAD_DIM]
    v_ref[0, 1] = vv[:, HEAD_DIM:]
    p_ref[0] = y[:, ATTN_WIDTH + 2 * KV_WIDTH:].astype(BF16)


def _inproj0_ctx_kernel(x_ref, gain_ref, sh_ref, sc_ref, w_ref, kg_ref, ones_ref, k_ref, v_ref):
    a = _modulate(x_ref[0], gain_ref[...], sh_ref[...], sc_ref[...])
    y = jnp.dot(a.astype(BF16), w_ref[...], preferred_element_type=F32)
    z = y[:, :KV_WIDTH]
    kr = (z * lax.rsqrt(_head_mean_square(z, ones_ref[...]) + EPS) * kg_ref[...]).astype(BF16)
    k_ref[0, 0] = kr[:, :HEAD_DIM]
    k_ref[0, 1] = kr[:, HEAD_DIM:]
    vv = y[:, KV_WIDTH:].astype(BF16)
    v_ref[0, 0] = vv[:, :HEAD_DIM]
    v_ref[0, 1] = vv[:, HEAD_DIM:]


def _rope_tables(seq):
    t = jnp.arange(seq)
    row = (t // GRID_W).astype(F32)
    col = (t % GRID_W).astype(F32)
    half = HEAD_DIM // 2
    inv = ROPE_THETA ** (-jnp.arange(0, half, 2, dtype=F32) / half)
    ar, ac = row[:, None] * inv, col[:, None] * inv
    cos = jnp.concatenate([jnp.cos(ar), jnp.cos(ar), jnp.cos(ac), jnp.cos(ac)], axis=-1)
    sin = jnp.concatenate([-jnp.sin(ar), jnp.sin(ar), -jnp.sin(ac), jnp.sin(ac)], axis=-1)
    return jnp.tile(cos, (1, LANES // HEAD_DIM)), jnp.tile(sin, (1, LANES // HEAD_DIM))


def _head_mean_matrix():
    r = jnp.arange(LANES)
    same = (r[:, None] // HEAD_DIM) == (r[None, :] // HEAD_DIM)
    block = jnp.where(same, 1.0 / HEAD_DIM, 0.0).astype(BF16)
    return jnp.concatenate([block, block], axis=0)


def _inproj0(x, mods, gain, w_in, q_gain, k_gain, tm):
    bsz, seq, d = x.shape
    cos, sin = _rope_tables(seq)
    qg = jnp.tile(q_gain, LANES // HEAD_DIM).reshape(1, LANES)
    kg = jnp.tile(k_gain, LANES // HEAD_DIM).reshape(1, LANES)
    head = lambda n: pl.BlockSpec((1, n, tm, HEAD_DIM), lambda b, i: (b, 0, i, 0))
    return pl.pallas_call(
        _inproj0_kernel,
        grid=(bsz, seq // tm),
        in_specs=[pl.BlockSpec((1, tm, d), lambda b, i: (b, i, 0)),
                  _full((1, d)),
                  _mod_spec(0, lambda b, i: b, 0),
                  _mod_spec(0, lambda b, i: b, 1),
                  _full((d, EVEN_IN)),
                  pl.BlockSpec((tm, LANES), lambda b, i: (i, 0)),
                  pl.BlockSpec((tm, LANES), lambda b, i: (i, 0)),
                  _full((1, LANES)), _full((1, LANES)), _full((2 * LANES, LANES))],
        out_specs=[head(N_Q_HEADS), head(N_KV_HEADS), head(N_KV_HEADS),
                   pl.BlockSpec((1, tm, POOL_WIDTH), lambda b, i: (b, i, 0))],
        out_shape=[jax.ShapeDtypeStruct((bsz, N_Q_HEADS, seq, HEAD_DIM), BF16),
                   jax.ShapeDtypeStruct((bsz, N_KV_HEADS, seq, HEAD_DIM), BF16),
                   jax.ShapeDtypeStruct((bsz, N_KV_HEADS, seq, HEAD_DIM), BF16),
                   jax.ShapeDtypeStruct((bsz, seq, POOL_WIDTH), BF16)],
        compiler_params=_params(("parallel", "parallel")),
        name="inproj0",
    )(x, gain.reshape(1, d), mods, mods, w_in.astype(BF16), cos, sin, qg, kg, _head_mean_matrix())


def _inproj0_ctx(ctx, mods, ctx_row, gain, w_kv, k_gain):
    bsz, n_ctx, d = ctx.shape
    kg = jnp.tile(k_gain, LANES // HEAD_DIM).reshape(1, LANES)
    head = pl.BlockSpec((1, N_KV_HEADS, n_ctx, HEAD_DIM), lambda b: (b, 0, 0, 0))
    return pl.pallas_call(
        _inproj0_ctx_kernel,
        grid=(bsz,),
        in_specs=[pl.BlockSpec((1, n_ctx, d), lambda b: (b, 0, 0)),
                  _full((1, d)),
                  _mod_spec(0, lambda b: ctx_row, 0),
                  _mod_spec(0, lambda b: ctx_row, 1),
                  _full((d, 2 * KV_WIDTH)),
                  _full((1, LANES)), _full((2 * LANES, LANES))],
        out_specs=[head, head],
        out_shape=[jax.ShapeDtypeStruct((bsz, N_KV_HEADS, n_ctx, HEAD_DIM), BF16)] * 2,
        compiler_params=_params(("parallel",)),
        name="inproj0_ctx",
    )(ctx, gain.reshape(1, d), mods, mods, w_kv.astype(BF16), kg, _head_mean_matrix())


def _attn_kernel(shift_ref, q_ref, kl_ref, vl_ref, kc_ref, vc_ref, o_ref, *, tq, tk):
    rows = Q_PER_KV * tq
    q = q_ref[0].reshape(rows, HEAD_DIM)
    seq = kl_ref.shape[2]
    chunks = [(kc_ref, vc_ref, 0, kc_ref.shape[2])]
    chunks += [(kl_ref, vl_ref, c * tk, tk) for c in range(seq // tk)]
    scores = lambda k: lax.dot_general(q, k, (((1,), (1,)), ((), ())), preferred_element_type=F32)

    bound = shift_ref[0, 0]
    safe = bound <= SAFE_SOFTMAX_SHIFT

    def finish(acc, l):
        o = acc / l
        o_ref[0] = jnp.concatenate([o[h * tq:(h + 1) * tq] for h in range(Q_PER_KV)], axis=1).astype(BF16)

    @pl.when(safe)
    def _():
        l = jnp.zeros((rows, 1), F32)
        acc = jnp.zeros((rows, HEAD_DIM), F32)
        for k_ref, v_ref, start, size in chunks:
            p = jnp.exp2(scores(k_ref[0, 0, start:start + size, :]) - bound)
            l = l + jnp.sum(p, axis=1, keepdims=True)
            acc = acc + jnp.dot(p.astype(BF16), v_ref[0, 0, start:start + size, :], preferred_element_type=F32)
        finish(acc, l)

    @pl.when(jnp.logical_not(safe))
    def _():
        m = jnp.full((rows, 1), -jnp.inf, F32)
        l = jnp.zeros((rows, 1), F32)
        acc = jnp.zeros((rows, HEAD_DIM), F32)
        for k_ref, v_ref, start, size in chunks:
            s = scores(k_ref[0, 0, start:start + size, :])
            m_new = jnp.maximum(m, jnp.max(s, axis=1, keepdims=True))
            alpha = jnp.exp2(m - m_new)
            p = jnp.exp2(s - m_new)
            l = alpha * l + jnp.sum(p, axis=1, keepdims=True)
            acc = alpha * acc + jnp.dot(p.astype(BF16), v_ref[0, 0, start:start + size, :],
                                        preferred_element_type=F32)
            m = m_new
        finish(acc, l)


def _score_bound(q_gain, k_gain):
    bound = HEAD_DIM * Q_SCALE * jnp.max(jnp.abs(q_gain)) * jnp.max(jnp.abs(k_gain)) * SCORE_BOUND_MARGIN
    return bound.reshape(1, 1).astype(F32)


def _attention(q, k, v, kc, vc, q_gain, k_gain, tq, tk):
    bsz, _, seq, _ = q.shape
    n_ctx = kc.shape[2]
    kv_spec = lambda n: pl.BlockSpec((1, 1, n, HEAD_DIM), lambda b, g, i: (b, g, 0, 0))
    return pl.pallas_call(
        functools.partial(_attn_kernel, tq=tq, tk=tk),
        grid=(bsz, N_KV_HEADS, seq // tq),
        in_specs=[pl.BlockSpec((1, 1), lambda b, g, i: (0, 0), memory_space=pltpu.SMEM),
                  pl.BlockSpec((1, Q_PER_KV, tq, HEAD_DIM), lambda b, g, i: (b, g, i, 0)),
                  kv_spec(seq), kv_spec(seq), kv_spec(n_ctx), kv_spec(n_ctx)],
        out_specs=pl.BlockSpec((1, tq, Q_PER_KV * HEAD_DIM), lambda b, g, i: (b, i, g)),
        out_shape=jax.ShapeDtypeStruct((bsz, seq, ATTN_WIDTH), BF16),
        compiler_params=_params(("parallel", "parallel", "parallel")),
        name="attention",
    )(_score_bound(q_gain, k_gain), q, k, v, kc, vc)


def _route(logits):
    lane = lax.broadcasted_iota(jnp.int32, logits.shape, 1).astype(F32)
    neg = -jnp.inf
    big = float(ROUTER_LANES)
    first_index = lambda mask: jnp.min(jnp.where(mask, lane, big), axis=1, keepdims=True)
    is_g = lane < N_GROUPS
    gm = jnp.max(jnp.where(is_g, logits, neg), axis=1, keepdims=True)
    gidx = first_index(is_g & (logits == gm))
    gden = jnp.sum(jnp.where(is_g, jnp.exp(logits - gm), 0.0), axis=1, keepdims=True)
    g_p = 1.0 / gden
    first = N_GROUPS + EXPERTS_PER_GROUP * gidx
    sel = (lane >= first) & (lane < first + EXPERTS_PER_GROUP)
    e1 = jnp.max(jnp.where(sel, logits, neg), axis=1, keepdims=True)
    i1 = first_index(sel & (logits == e1))
    rest = sel & (lane != i1)
    e2 = jnp.max(jnp.where(rest, logits, neg), axis=1, keepdims=True)
    i2 = first_index(rest & (logits == e2))
    p2 = jnp.exp(e2 - e1)
    w1 = g_p * (1.0 / (1.0 + p2))
    w2 = g_p * (p2 / (1.0 + p2))
    lo = jnp.minimum(i1, i2) - first
    hi = jnp.maximum(i1, i2) - first
    pair = jnp.where(lo == 0, hi - 1, jnp.where(lo == 1, hi + 1, PAIRS_PER_GROUP - 1.0))
    bucket = PAIRS_PER_GROUP * gidx + pair
    w_lo = jnp.where(i1 < i2, w1, w2)
    w_hi = jnp.where(i1 < i2, w2, w1)
    return jnp.where(lane == INFO_BUCKET, bucket,
                     jnp.where(lane == INFO_W_LO, w_lo, jnp.where(lane == INFO_W_HI, w_hi, 0.0)))


def _bucket_onehot(info, sel):
    brow = lax.dot_general(sel, info.astype(BF16), (((1,), (1,)), ((), ())),
                           preferred_element_type=F32)[0:1]
    bid = lax.broadcasted_iota(jnp.int32, (BUCKET_ROWS, info.shape[0]), 0)
    return (bid == brow.astype(jnp.int32)).astype(F32)


def _bucket_selector():
    return jnp.zeros((8, ROUTER_LANES), F32).at[0, INFO_BUCKET].set(1.0).astype(BF16)


def _tail(y, x_res, gate1, gain2, shift2, scale2, rw_both, rbias, sel, h1_ref, info_ref, cnt_ref):
    h1 = x_res + gate1 * y
    h1_ref[0] = h1
    t = _modulate(h1, gain2, shift2, scale2)
    t_hi = t.astype(BF16)
    t_lo = (t - t_hi.astype(F32)).astype(BF16)
    both = jnp.dot(t_hi, rw_both, preferred_element_type=F32)
    logits = (both[:, :ROUTER_LANES] + both[:, ROUTER_LANES:]
              + jnp.dot(t_lo, rw_both[:, :ROUTER_LANES], preferred_element_type=F32)) + rbias
    info = _route(logits)
    info_ref[0] = info

    @pl.when((pl.program_id(0) == 0) & (pl.program_id(1) == 0))
    def _():
        cnt_ref[...] = jnp.zeros_like(cnt_ref)

    cnt_ref[...] += jnp.sum(_bucket_onehot(info, sel), axis=1, keepdims=True)


def _router_operands(rg_w, rg_b, re_w, re_b):
    d = rg_w.shape[0]
    w = jnp.concatenate([rg_w, re_w, jnp.zeros((d, ROUTER_LANES - N_GROUPS - N_EXPERTS), F32)], axis=1)
    b = jnp.concatenate([rg_b, re_b, jnp.zeros((ROUTER_LANES - N_GROUPS - N_EXPERTS,), F32)])
    w_hi = w.astype(BF16)
    w_lo = (w - w_hi.astype(F32)).astype(BF16)
    return jnp.concatenate([w_hi, w_lo], axis=1), b.reshape(1, ROUTER_LANES)


def _fill_halo(buf, main_ref, prev_ref, next_ref, tm, i, n_tiles, halo=HALO):
    zero = jnp.zeros((), buf.dtype)
    buf[halo:halo + tm] = main_ref[0].astype(buf.dtype)
    buf[0:halo] = jnp.where(i > 0, prev_ref[0].astype(buf.dtype), zero)
    buf[halo + tm:2 * halo + tm] = jnp.where(i < n_tiles - 1, next_ref[0].astype(buf.dtype), zero)


def _halo_specs(tm, seq, width, halo=HALO):
    per = tm // halo
    last = seq // halo - 1
    return [pl.BlockSpec((1, tm, width), lambda b, i: (b, i, 0)),
            pl.BlockSpec((1, halo, width), lambda b, i: (b, jnp.maximum(i * per - 1, 0), 0)),
            pl.BlockSpec((1, halo, width), lambda b, i: (b, jnp.minimum((i + 1) * per, last), 0))]


def _pool_bands():
    r = jnp.arange(POOL_BLOCK)[:, None]
    c = jnp.arange(POOL_BLOCK + 2 * HALO)[None, :]
    return jnp.stack([((c >= r + HALO - w // 2) & (c < r + HALO + w - w // 2)) for w in POOL_WINDOWS]).astype(BF16)


def _out0_kernel(o_ref, p_ref, pprev_ref, pnext_ref, x_ref, g1_ref, gain2_ref, sh2_ref, sc2_ref,
                 poolw_ref, pscale_ref, band_ref, wout_ref, rw_ref, rb_ref, sel_ref,
                 h1_ref, info_ref, cnt_ref, pbuf, *, tm, seq):
    i = pl.program_id(1)
    _fill_halo(pbuf, p_ref, pprev_ref, pnext_ref, tm, i, seq // tm)
    pos = i * tm + lax.broadcasted_iota(jnp.int32, (tm, 1), 0)
    pooled = []
    for g, w in enumerate(POOL_WINDOWS):
        sl = slice(g * POOL_GROUP, (g + 1) * POOL_GROUP)
        acc = jnp.concatenate(
            [jnp.dot(band_ref[g], pbuf[b:b + POOL_BLOCK + 2 * HALO, sl], preferred_element_type=F32)
             for b in range(0, tm, POOL_BLOCK)], axis=0)
        lo = jnp.clip(pos - w // 2, 0, seq)
        hi = jnp.clip(pos + w - w // 2, 0, seq)
        mean = acc * (1.0 / (hi - lo).astype(F32))
        dlt = (mean - p_ref[0, :, sl].astype(F32)).astype(BF16)
        pooled.append((jnp.dot(dlt, poolw_ref[g], preferred_element_type=F32) * pscale_ref[:, sl]).astype(BF16))
    mixed = jnp.concatenate([o_ref[0]] + pooled, axis=1)
    y = jnp.dot(mixed, wout_ref[...], preferred_element_type=F32)
    _tail(y, x_ref[0], g1_ref[...], gain2_ref[...], sh2_ref[...], sc2_ref[...],
          rw_ref[...], rb_ref[...], sel_ref[...], h1_ref, info_ref, cnt_ref)


def _tail_specs(layer, d):
    by_batch = lambda b, i: b
    ins = [_mod_spec(layer, by_batch, 2), _full((1, d)), _mod_spec(layer, by_batch, 3),
           _mod_spec(layer, by_batch, 4)]
    return ins


def _tail_outs(bsz, seq, d, tm):
    specs = [pl.BlockSpec((1, tm, d), lambda b, i: (b, i, 0)),
             pl.BlockSpec((1, tm, ROUTER_LANES), lambda b, i: (b, i, 0)),
             _full((BUCKET_ROWS, LANES))]
    shapes = [jax.ShapeDtypeStruct((bsz, seq, d), F32),
              jax.ShapeDtypeStruct((bsz, seq, ROUTER_LANES), F32),
              jax.ShapeDtypeStruct((BUCKET_ROWS, LANES), F32)]
    return specs, shapes


def _out0(o, p, x, mods, gain2, pool_w, pool_scale, w_out, router, tm):
    bsz, seq, d = x.shape
    rw_both, rb = router
    bands = _pool_bands()
    out_specs, out_shapes = _tail_outs(bsz, seq, d, tm)
    return pl.pallas_call(
        functools.partial(_out0_kernel, tm=tm, seq=seq),
        grid=(bsz, seq // tm),
        in_specs=[pl.BlockSpec((1, tm, ATTN_WIDTH), lambda b, i: (b, i, 0))]
        + _halo_specs(tm, seq, POOL_WIDTH)
        + [pl.BlockSpec((1, tm, d), lambda b, i: (b, i, 0))]
        + _tail_specs(0, d)
        + [_full(pool_w.shape), _full((1, POOL_WIDTH)), _full(bands.shape), _full(w_out.shape),
           _full(rw_both.shape), _full(rb.shape), _full((8, ROUTER_LANES))],
        out_specs=out_specs,
        out_shape=out_shapes,
        scratch_shapes=[pltpu.VMEM((tm + 2 * HALO, POOL_WIDTH), BF16)],
        compiler_params=_params(("arbitrary", "arbitrary")),
        name="out0",
    )(o, p, p, p, x, mods, gain2.reshape(1, d), mods, mods,
      pool_w.astype(BF16), pool_scale.reshape(1, POOL_WIDTH), bands, w_out.astype(BF16), rw_both, rb,
      _bucket_selector())


def _inproj1_kernel(pos_ref, posn_ref, h1_ref, g2_ref, gain_ref, sh_ref, sc_ref, w_ref, sgg_ref, sgw_ref, sgb_ref,
                    ys_ref, h_ref, yc_ref, z_ref, bg_ref, ybuf, sem, *, tm, n_steps):
    x = h1_ref[0] + g2_ref[...] * _fetch_sorted_rows(ys_ref, pos_ref, posn_ref, ybuf, sem, tm, n_steps)
    h_ref[0] = x
    a = _modulate(x, gain_ref[...], sh_ref[...], sc_ref[...])
    y = jnp.dot(a.astype(BF16), w_ref[...], preferred_element_type=F32)
    for g in range(SG_GROUPS):
        sl = slice(g * LANES, (g + 1) * LANES)
        u = y[:, sl]
        vg = y[:, SG_WIDTH + g * LANES:SG_WIDTH + (g + 1) * LANES]
        ms = jnp.mean(vg * vg, axis=-1, keepdims=True)
        vn = (vg * lax.rsqrt(ms + EPS) * sgg_ref[:, sl]).astype(BF16)
        for c in range(tm // SG_CHUNK):
            rows = slice(c * SG_CHUNK, (c + 1) * SG_CHUNK)
            s = jnp.dot(sgw_ref[g], vn[rows], preferred_element_type=F32) + sgb_ref[g]
            yc_ref[0, rows, sl] = (u[rows] * s).astype(BF16)
    hx = y[:, 2 * SG_WIDTH:2 * SG_WIDTH + CONV_WIDTH]
    bg_ref[0] = y[:, 2 * SG_WIDTH + CONV_WIDTH:2 * SG_WIDTH + 2 * CONV_WIDTH].astype(BF16)
    cg = y[:, 2 * SG_WIDTH + 2 * CONV_WIDTH:]
    z_ref[0] = (cg * hx).astype(BF16)


def _pos_spec(tm, nt, n_steps, ahead):
    return pl.BlockSpec((1, 1, tm), lambda b, i: (jnp.minimum(b * nt + i + ahead, n_steps - 1), 0, 0),
                        memory_space=pltpu.SMEM)


def _inproj1(h1, ys, pos, mods, gain, w_in, sg_gain, sg_w, sg_b, tm):
    bsz, seq, d = h1.shape
    nt = seq // tm
    n_steps = bsz * nt
    sgb = jnp.broadcast_to(sg_b[:, :, None], (SG_GROUPS, SG_CHUNK, LANES))
    by_batch = lambda b, i: b
    wide = lambda w, dt: (pl.BlockSpec((1, tm, w), lambda b, i: (b, i, 0)),
                          jax.ShapeDtypeStruct((bsz, seq, w), dt))
    outs = [wide(d, F32), wide(SG_WIDTH, BF16), wide(CONV_WIDTH, BF16), wide(CONV_WIDTH, BF16)]
    return pl.pallas_call(
        functools.partial(_inproj1_kernel, tm=tm, n_steps=n_steps),
        grid=(bsz, nt),
        in_specs=[_pos_spec(tm, nt, n_steps, 0), _pos_spec(tm, nt, n_steps, 1),
                  pl.BlockSpec((1, tm, d), lambda b, i: (b, i, 0)),
                  _mod_spec(0, by_batch, 5),
                  _full((1, d)),
                  _mod_spec(1, by_batch, 0),
                  _mod_spec(1, by_batch, 1),
                  _full((d, ODD_IN)),
                  _full((1, SG_WIDTH)), _full(sg_w.shape), _full(sgb.shape),
                  pl.BlockSpec(memory_space=pl.ANY)],
        out_specs=[s for s, _ in outs],
        out_shape=[s for _, s in outs],
        scratch_shapes=[pltpu.VMEM((2, tm, d), F32), pltpu.SemaphoreType.DMA((2,))],
        compiler_params=_params(("arbitrary", "arbitrary")),
        name="inproj1",
    )(pos, pos, h1, mods, gain.reshape(1, d), mods, mods, w_in.astype(BF16), sg_gain.reshape(1, SG_WIDTH),
      sg_w.astype(BF16), sgb, ys)


def _out1_kernel(yc_ref, z_ref, zprev_ref, znext_ref, bg_ref, x_ref, g1_ref, gain2_ref, sh2_ref, sc2_ref,
                 convw_ref, wout_ref, rw_ref, rb_ref, sel_ref,
                 h1_ref, info_ref, cnt_ref, zbuf, *, tm, seq):
    i = pl.program_id(1)
    _fill_halo(zbuf, z_ref, zprev_ref, znext_ref, tm, i, seq // tm, HALO)
    zc = (zbuf[HALO - 1:HALO - 1 + tm] * convw_ref[0:1, :]
          + zbuf[HALO:HALO + tm] * convw_ref[1:2, :]
          + zbuf[HALO + 1:HALO + 1 + tm] * convw_ref[2:3, :])
    yd = (bg_ref[0].astype(F32) * zc).astype(BF16)
    y = (jnp.dot(yc_ref[0], wout_ref[0:SG_WIDTH, :], preferred_element_type=F32)
         + jnp.dot(yd, wout_ref[SG_WIDTH:, :], preferred_element_type=F32))
    _tail(y, x_ref[0], g1_ref[...], gain2_ref[...], sh2_ref[...], sc2_ref[...],
          rw_ref[...], rb_ref[...], sel_ref[...], h1_ref, info_ref, cnt_ref)


def _out1(yc, z, bg, x, mods, gain2, conv_w, w_out, router, tm):
    bsz, seq, d = x.shape
    rw_both, rb = router
    out_specs, out_shapes = _tail_outs(bsz, seq, d, tm)
    wide = pl.BlockSpec((1, tm, CONV_WIDTH), lambda b, i: (b, i, 0))
    return pl.pallas_call(
        functools.partial(_out1_kernel, tm=tm, seq=seq),
        grid=(bsz, seq // tm),
        in_specs=[wide] + _halo_specs(tm, seq, CONV_WIDTH, HALO) + [wide]
        + [pl.BlockSpec((1, tm, d), lambda b, i: (b, i, 0))]
        + _tail_specs(1, d)
        + [_full((3, CONV_WIDTH)), _full(w_out.shape),
           _full(rw_both.shape), _full(rb.shape), _full((8, ROUTER_LANES))],
        out_specs=out_specs,
        out_shape=out_shapes,
        scratch_shapes=[pltpu.VMEM((tm + 2 * HALO, CONV_WIDTH), F32)],
        compiler_params=_params(("arbitrary", "arbitrary")),
        name="out1",
    )(yc, z, z, z, bg, x, mods, gain2.reshape(1, d), mods, mods,
      conv_w.reshape(3, CONV_WIDTH), w_out.astype(BF16), rw_both, rb, _bucket_selector())


def _plan_pos_kernel(info_ref, sel_ref, cnt_ref, ltri_ref, utri_ref, pos_ref, meta_ref, start_sc, run_sc):
    @pl.when(pl.program_id(0) == 0)
    def _():
        padded = jnp.ceil(cnt_ref[...] * (1.0 / SORT_TILE)) * SORT_TILE
        incl = jnp.dot(ltri_ref[...], padded, precision=lax.Precision.HIGHEST, preferred_element_type=F32)
        start_sc[...] = incl - padded
        run_sc[...] = jnp.zeros_like(run_sc)
        ends = jnp.broadcast_to(incl[:, 0:1], (BUCKET_ROWS, META_LANES))
        bid = lax.broadcasted_iota(jnp.int32, ends.shape, 0)
        tile = lax.broadcasted_iota(jnp.int32, (1, META_LANES), 1)

        def bucket_of(row0):
            done = jnp.where((bid < N_BUCKETS) & (ends <= row0), 1.0, 0.0)
            return jnp.minimum(jnp.sum(done, axis=0, keepdims=True), N_BUCKETS - 1.0).astype(jnp.int32)

        def experts_of(bucket):
            grp = ((bucket >= PAIRS_PER_GROUP).astype(jnp.int32)
                   + (bucket >= 2 * PAIRS_PER_GROUP).astype(jnp.int32)
                   + (bucket >= 3 * PAIRS_PER_GROUP).astype(jnp.int32))
            pair = bucket - PAIRS_PER_GROUP * grp
            lo = (pair >= 3).astype(jnp.int32) + (pair >= 5).astype(jnp.int32)
            hi = jnp.where(pair == 0, 1, jnp.where((pair == 1) | (pair == 3), 2, 3))
            return EXPERTS_PER_GROUP * grp + lo, EXPERTS_PER_GROUP * grp + hi

        row0 = (tile * SORT_TILE).astype(F32)
        tb = bucket_of(row0)
        n_used = (incl[N_BUCKETS - 1:N_BUCKETS, 0:1] * (1.0 / SORT_TILE)).astype(jnp.int32)
        fill = (tile >= n_used - 1) | (tb != bucket_of(row0 + SORT_TILE))
        first = (tile == 0) | (tb != bucket_of(row0 - SORT_TILE))
        own_end = jnp.sum(jnp.where(bid == tb, ends, 0.0), axis=0, keepdims=True)
        nonempty = jnp.broadcast_to(padded[:, 0:1], ends.shape) > 0.0
        ordinal = jnp.sum(jnp.where((bid < tb) & nonempty, 1.0, 0.0), axis=0, keepdims=True)
        rows = [None] * 8
        rows[META_EXPERT_LO], rows[META_EXPERT_HI] = experts_of(tb)
        rows[META_N_USED] = jnp.broadcast_to(n_used, (1, META_LANES))
        rows[META_FILL] = fill.astype(jnp.int32)
        rows[META_FIRST] = first.astype(jnp.int32)
        rows[META_WSLOT] = (ordinal - 2.0 * jnp.floor(ordinal * 0.5)).astype(jnp.int32)
        rows[META_NEXT_LO], rows[META_NEXT_HI] = experts_of(bucket_of(own_end))
        for r, row in enumerate(rows):
            meta_ref[r:r + 1, :] = row

    tm = utri_ref.shape[0]
    base = start_sc[:, 0:1] + run_sc[:, 0:1]
    for k in range(pos_ref.shape[0]):
        oh = _bucket_onehot(info_ref[k * tm:(k + 1) * tm], sel_ref[...])
        before = jnp.dot(oh.astype(BF16), utri_ref[...], preferred_element_type=F32)
        pos_ref[k] = jnp.sum(oh * (before + base), axis=0, keepdims=True).astype(jnp.int32)
        base = base + jnp.sum(oh, axis=1, keepdims=True)
    run_sc[...] = jnp.broadcast_to(base - start_sc[:, 0:1], run_sc.shape)


def _sort_plan(info, cnt, tm):
    n = info.shape[0]
    sel = _bucket_selector()
    sub = 4 if (n // tm) % 4 == 0 else 1
    info_spec = pl.BlockSpec((sub * tm, ROUTER_LANES), lambda i: (i, 0))
    r = jnp.arange(BUCKET_ROWS)
    ltri = (r[:, None] >= r[None, :]).astype(F32)
    t = jnp.arange(tm)
    utri = (t[:, None] < t[None, :]).astype(BF16)
    return pl.pallas_call(
        _plan_pos_kernel,
        grid=(n // (sub * tm),),
        in_specs=[info_spec, _full(sel.shape), _full(cnt.shape), _full(ltri.shape), _full(utri.shape)],
        out_specs=[pl.BlockSpec((sub, 1, tm), lambda i: (i, 0, 0)), _full((8, META_LANES))],
        out_shape=[jax.ShapeDtypeStruct((n // tm, 1, tm), jnp.int32),
                   jax.ShapeDtypeStruct((8, META_LANES), jnp.int32)],
        scratch_shapes=[pltpu.VMEM((BUCKET_ROWS, LANES), F32), pltpu.VMEM((BUCKET_ROWS, LANES), F32)],
        compiler_params=_params(("arbitrary",)),
        name="plan_pos",
    )(info, sel, cnt, ltri, utri)


def _dispatch_kernel(pos_ref, fill_ref, h1_ref, gain2_ref, sh2_ref, sc2_ref, info_ref, xs_ref,
                     rowbuf, zbuf, sem, zsem, *, tm, n_steps, n_tiles):
    step = pl.program_id(0) * pl.num_programs(1) + pl.program_id(1)
    slot = step % 2

    @pl.when(step == 0)
    def _():
        zbuf[...] = jnp.zeros_like(zbuf)
        fill = lambda j: pltpu.make_async_copy(zbuf, xs_ref.at[pl.ds(j * SORT_TILE, SORT_TILE), 0], zsem)
        for j in range(n_tiles):
            pl.when(fill_ref[0, j] == 1)(lambda j=j: fill(j).start())
        for j in range(n_tiles):
            pl.when(fill_ref[0, j] == 1)(lambda j=j: fill(j).wait())

    def wait(s):
        pltpu.make_async_copy(rowbuf.at[s], xs_ref.at[pl.ds(0, tm), 0], sem.at[s]).wait()

    def send(s):
        @pl.when(step >= 2)
        def _():
            wait(s)

        rowbuf[s, :, 0:D_MODEL] = _modulate(h1_ref[0], gain2_ref[...], sh2_ref[...], sc2_ref[...])
        rowbuf[s, :, D_MODEL:] = info_ref[0]
        for r in range(tm):
            pltpu.make_async_copy(rowbuf.at[s, pl.ds(r, 1)], xs_ref.at[pos_ref[0, 0, r]],
                                  sem.at[s]).start(priority=r % 2)

    for s in range(2):
        pl.when(slot == s)(functools.partial(send, s))

    @pl.when(step == n_steps - 1)
    def _():
        wait(slot)
        if n_steps > 1:
            wait(1 - slot)


def _gmoe_kernel(ea_ref, eb_ref, nu_ref, first_ref, wslot_ref, na_ref, nb_ref, xs_ref, wg_hbm, wu_hbm, wd_hbm,
                 ys_ref, xbuf, ybuf, zbuf, wg_buf, wu_buf, wd_buf, sem_in, sem_out, zsem, wsem, *, layer):
    j = pl.program_id(0)
    n_used = nu_ref[0]
    slot = j % 2
    wslot = wslot_ref[j]
    tile = lambda ref, t: ref.at[pl.ds(t * SORT_TILE, SORT_TILE), 0]
    in_copy = lambda t, s: pltpu.make_async_copy(tile(xs_ref, t), xbuf.at[s], sem_in.at[s])
    out_copy = lambda t, s: pltpu.make_async_copy(ybuf.at[s], tile(ys_ref, t), sem_out.at[s])

    def weight_copies(e_lo, e_hi, s):
        return [pltpu.make_async_copy(hbm.at[layer, e], buf.at[s, which], wsem.at[s])
                for hbm, buf in ((wg_hbm, wg_buf), (wu_hbm, wu_buf), (wd_hbm, wd_buf))
                for which, e in ((0, e_lo), (1, e_hi))]

    @pl.when(j == 0)
    def _():
        in_copy(0, 0).start()
        for c in weight_copies(ea_ref[0], eb_ref[0], 0):
            c.start()

    @pl.when(j + 1 < n_used)
    def _():
        in_copy(j + 1, 1 - slot).start()

    @pl.when(j < n_used)
    def _():
        @pl.when(first_ref[j] == 1)
        def _():
            for c in weight_copies(ea_ref[j], eb_ref[j], wslot):
                c.wait()
            for c in weight_copies(na_ref[j], nb_ref[j], 1 - wslot):
                c.start()

        in_copy(j, slot).wait()

        @pl.when(j >= 2)
        def _():
            out_copy(j - 2, slot).wait()

        x = xbuf[slot, :, 0:D_MODEL].astype(BF16)

        def expert(which, w):
            gt = jnp.dot(x, wg_buf[wslot, which], preferred_element_type=F32)
            up = jnp.dot(x, wu_buf[wslot, which], preferred_element_type=F32)
            h = (gt * jax.nn.sigmoid(gt)) * up * w
            return jnp.dot(h.astype(BF16), wd_buf[wslot, which], preferred_element_type=F32)

        w_lo = xbuf[slot, :, D_MODEL + INFO_W_LO:D_MODEL + INFO_W_LO + 1]
        w_hi = xbuf[slot, :, D_MODEL + INFO_W_HI:D_MODEL + INFO_W_HI + 1]
        ybuf[slot] = expert(0, w_lo) + expert(1, w_hi)
        out_copy(j, slot).start()

        @pl.when(j == n_used - 1)
        def _():
            out_copy(j, slot).wait()

            @pl.when(j >= 1)
            def _():
                out_copy(j - 1, 1 - slot).wait()

            for c in weight_copies(ea_ref[j], eb_ref[j], 1 - wslot):
                c.wait()

    @pl.when(j >= n_used)
    def _():
        @pl.when(j == n_used)
        def _():
            zbuf[...] = jnp.zeros_like(zbuf)

        fill = pltpu.make_async_copy(zbuf, tile(ys_ref, j), zsem)
        fill.start()
        fill.wait()


def _fetch_sorted_rows(ys_ref, pos_ref, posn_ref, ybuf, sem, tm, n_steps):
    step = pl.program_id(0) * pl.num_programs(1) + pl.program_id(1)
    slot = step % 2

    def issue(p_ref, s):
        for r in range(tm):
            pltpu.make_async_copy(ys_ref.at[p_ref[0, 0, r]], ybuf.at[s, pl.ds(r, 1)],
                                  sem.at[s]).start(priority=r % 2)

    pl.when(step == 0)(functools.partial(issue, pos_ref, 0))
    for s in range(2):
        pl.when((step + 1 < n_steps) & (slot == s))(functools.partial(issue, posn_ref, 1 - s))
    pltpu.make_async_copy(ys_ref.at[pl.ds(0, tm), 0], ybuf.at[slot], sem.at[slot]).wait()
    return ybuf[slot]


def _combine_kernel(pos_ref, posn_ref, h1_ref, g2_ref, ys_ref, o_ref, ybuf, sem, *, tm, n_steps):
    rows = _fetch_sorted_rows(ys_ref, pos_ref, posn_ref, ybuf, sem, tm, n_steps)
    o_ref[0] = h1_ref[0] + g2_ref[...] * rows


def _experts_sorted(h1, info, cnt, mods, layer, gain2, w_gate, w_up, w_down, tm):
    bsz, seq, d = h1.shape
    nt = seq // tm
    n_steps = bsz * nt
    n = bsz * seq
    n_sorted = n + N_BUCKETS * SORT_TILE
    n_tiles = n_sorted // SORT_TILE
    assert n % SORT_TILE == 0 and n_tiles <= META_LANES
    pos, meta = _sort_plan(info.reshape(n, ROUTER_LANES), cnt, tm)
    tile = lambda w: pl.BlockSpec((1, tm, w), lambda b, i: (b, i, 0))
    any_spec = pl.BlockSpec(memory_space=pl.ANY)
    by_batch = lambda b, i: b

    xs = pl.pallas_call(
        functools.partial(_dispatch_kernel, tm=tm, n_steps=n_steps, n_tiles=n_tiles),
        grid=(bsz, nt),
        in_specs=[_pos_spec(tm, nt, n_steps, 0),
                  pl.BlockSpec((1, META_LANES), lambda b, i: (0, 0), memory_space=pltpu.SMEM),
                  tile(d), _full((1, d)), _mod_spec(layer, by_batch, 3),
                  _mod_spec(layer, by_batch, 4), tile(ROUTER_LANES)],
        out_specs=any_spec,
        out_shape=jax.ShapeDtypeStruct((n_sorted, 1, ROW_WIDTH), F32),
        scratch_shapes=[pltpu.VMEM((2, tm, ROW_WIDTH), F32), pltpu.VMEM((SORT_TILE, ROW_WIDTH), F32),
                        pltpu.SemaphoreType.DMA((2,)), pltpu.SemaphoreType.DMA(())],
        compiler_params=_params(("arbitrary", "arbitrary")),
        name=f"dispatch{layer}",
    )(pos, meta[META_FILL:META_FILL + 1], h1, gain2.reshape(1, d), mods, mods, info)

    ys = pl.pallas_call(
        functools.partial(_gmoe_kernel, layer=layer),
        grid_spec=pltpu.PrefetchScalarGridSpec(
            num_scalar_prefetch=7,
            grid=(n_tiles,),
            in_specs=[any_spec] * 4,
            out_specs=any_spec,
            scratch_shapes=[pltpu.VMEM((2, SORT_TILE, ROW_WIDTH), F32), pltpu.VMEM((2, SORT_TILE, d), F32),
                            pltpu.VMEM((SORT_TILE, d), F32),
                            pltpu.VMEM((2, 2, d, D_EXPERT), BF16), pltpu.VMEM((2, 2, d, D_EXPERT), BF16),
                            pltpu.VMEM((2, 2, D_EXPERT, d), BF16),
                            pltpu.SemaphoreType.DMA((2,)), pltpu.SemaphoreType.DMA((2,)),
                            pltpu.SemaphoreType.DMA(()), pltpu.SemaphoreType.DMA((2,))]),
        out_shape=jax.ShapeDtypeStruct((n_sorted, 1, d), F32),
        compiler_params=_params(("arbitrary",)),
        name=f"experts{layer}",
    )(meta[META_EXPERT_LO], meta[META_EXPERT_HI], meta[META_N_USED, :1], meta[META_FIRST], meta[META_WSLOT],
      meta[META_NEXT_LO], meta[META_NEXT_HI], xs, w_gate, w_up, w_down)
    return ys, pos


def _combine(h1, ys, pos, mods, layer, tm):
    bsz, seq, d = h1.shape
    nt = seq // tm
    n_steps = bsz * nt
    tile = lambda w: pl.BlockSpec((1, tm, w), lambda b, i: (b, i, 0))
    return pl.pallas_call(
        functools.partial(_combine_kernel, tm=tm, n_steps=n_steps),
        grid=(bsz, nt),
        in_specs=[_pos_spec(tm, nt, n_steps, 0), _pos_spec(tm, nt, n_steps, 1), tile(d),
                  _mod_spec(layer, lambda b, i: b, 5), pl.BlockSpec(memory_space=pl.ANY)],
        out_specs=tile(d),
        out_shape=jax.ShapeDtypeStruct((bsz, seq, d), F32),
        scratch_shapes=[pltpu.VMEM((2, tm, d), F32), pltpu.SemaphoreType.DMA((2,))],
        compiler_params=_params(("arbitrary", "arbitrary")),
        name=f"combine{layer}",
    )(pos, pos, h1, mods, ys)


def kernel(x, c, ctx, c_ctx, mod_w, mod_b, norm1_g, norm2_g, even_w_in, q_gain, k_gain, pool_w, pool_scale,
           even_w_out, odd_w_in, sg_gain, sg_w, sg_b, conv_w, odd_w_out, router_g_w, router_g_b,
           router_e_w, router_e_b, w_gate, w_up, w_down):
    bsz, seq, d = x.shape
    tm = min(512, seq)
    tm_proj = min(1024, seq)
    cond = jnp.zeros((MOD_ROWS, d), F32).at[:bsz].set(c).at[bsz].set(c_ctx)
    mods = _adaln(cond, mod_w, mod_b).reshape(mod_w.shape[0], MOD_ROWS, 6, 1, d)
    w_gate, w_up, w_down = w_gate.astype(BF16), w_up.astype(BF16), w_down.astype(BF16)

    q, k, v, p = _inproj0(x, mods, norm1_g[0], even_w_in[0], q_gain[0], k_gain[0], tm_proj)
    kc, vc = _inproj0_ctx(ctx, mods, bsz, norm1_g[0], even_w_in[0][:, ATTN_WIDTH:ATTN_WIDTH + 2 * KV_WIDTH],
                          k_gain[0])
    o = _attention(q, k, v, kc, vc, q_gain[0], k_gain[0], tq=min(512, seq), tk=min(2048, seq))
    router0 = _router_operands(router_g_w[0], router_g_b[0], router_e_w[0], router_e_b[0])
    h1, info, cnt = _out0(o, p, x, mods, norm2_g[0], pool_w[0], pool_scale[0], even_w_out[0], router0, tm_proj)
    ys, pos = _experts_sorted(h1, info, cnt, mods, 0, norm2_g[0], w_gate, w_up, w_down, tm)

    h, yc, z, bg = _inproj1(h1, ys, pos, mods, norm1_g[1], odd_w_in[0], sg_gain[0], sg_w[0], sg_b[0], tm)
    router1 = _router_operands(router_g_w[1], router_g_b[1], router_e_w[1], router_e_b[1])
    h1, info, cnt = _out1(yc, z, bg, h, mods, norm2_g[1], conv_w[0], odd_w_out[0], router1, tm_proj)
    ys, pos = _experts_sorted(h1, info, cnt, mods, 1, norm2_g[1], w_gate, w_up, w_down, tm)
    return _combine(h1, ys, pos, mods, 1, tm)
```

```python
import functools

import jax
import jax.numpy as jnp
from jax import lax
from jax.experimental import pallas as pl
from jax.experimental.pallas import tpu as pltpu

F32 = jnp.float32
BF16 = jnp.bfloat16

D_MODEL = 1024
GRID_W = 64
EPS = 1e-6
N_Q_HEADS = 8
N_KV_HEADS = 2
HEAD_DIM = 64
Q_PER_KV = N_Q_HEADS // N_KV_HEADS
ATTN_WIDTH = N_Q_HEADS * HEAD_DIM
KV_WIDTH = N_KV_HEADS * HEAD_DIM
ROPE_THETA = 10000.0
POOL_WINDOWS = (2, 4, 8, 16)
POOL_GROUP = 128
POOL_WIDTH = POOL_GROUP * len(POOL_WINDOWS)
SG_GROUPS = 4
SG_CHUNK = 128
SG_WIDTH = 512
CONV_WIDTH = 512
EVEN_IN = ATTN_WIDTH + 2 * KV_WIDTH + POOL_WIDTH
ODD_IN = 2 * SG_WIDTH + 3 * CONV_WIDTH
N_GROUPS = 4
EXPERTS_PER_GROUP = 4
N_EXPERTS = 16
D_EXPERT = 256

Q_SCALE = HEAD_DIM ** -0.5 * 1.4426950408889634
SAFE_SOFTMAX_SHIFT = 60.0
SCORE_BOUND_MARGIN = 1.02
LANES = 128
HALO = 16
POOL_BLOCK = 128
ROUTER_LANES = 128
MOD_ROWS = 16
VMEM_LIMIT = 48 * 1024 * 1024

PAIRS_PER_GROUP = 6
N_BUCKETS = N_GROUPS * PAIRS_PER_GROUP
BUCKET_ROWS = 32
SORT_TILE = 512
META_LANES = 256
META_EXPERT_LO, META_EXPERT_HI, META_N_USED, META_FILL, META_FIRST, META_WSLOT, META_NEXT_LO, META_NEXT_HI = range(8)
ROW_WIDTH = D_MODEL + ROUTER_LANES
INFO_BUCKET, INFO_W_LO, INFO_W_HI = 0, 1, 2


def _params(sem):
    return pltpu.CompilerParams(dimension_semantics=sem, vmem_limit_bytes=VMEM_LIMIT)


def _modulate(x, gain, shift, scale):
    ms = jnp.mean(x * x, axis=-1, keepdims=True)
    return (x * lax.rsqrt(ms + EPS) * gain) * (1.0 + scale) + shift


def _mod_spec(layer, row_fn, which):
    return pl.BlockSpec((None, None, None, 1, D_MODEL),
                        lambda *idx: (layer, row_fn(*idx), which, 0, 0))


def _full(shape):
    return pl.BlockSpec(shape, lambda *idx: (0,) * len(shape))


def _adaln_kernel(c_ref, w_ref, b_ref, o_ref):
    c = c_ref[...]
    s = c * jax.nn.sigmoid(c)
    o_ref[0] = jnp.dot(s, w_ref[0], precision=lax.Precision.HIGHEST,
                       preferred_element_type=F32) + b_ref[0]


def _adaln(cond, mod_w, mod_b):
    depth, d, n = mod_w.shape
    tn = 1024
    return pl.pallas_call(
        _adaln_kernel,
        grid=(depth, n // tn),
        in_specs=[_full((MOD_ROWS, d)),
                  pl.BlockSpec((1, d, tn), lambda l, j: (l, 0, j)),
                  pl.BlockSpec((1, 1, tn), lambda l, j: (l, 0, j))],
        out_specs=pl.BlockSpec((1, MOD_ROWS, tn), lambda l, j: (l, 0, j)),
        out_shape=jax.ShapeDtypeStruct((depth, MOD_ROWS, n), F32),
        compiler_params=_params(("arbitrary", "arbitrary")),
        name="adaln",
    )(cond, mod_w, mod_b.reshape(depth, 1, n))


def _head_mean_square(z, ones_bd):
    sq = z * z
    hi = sq.astype(BF16)
    lo = (sq - hi.astype(F32)).astype(BF16)
    return jnp.dot(jnp.concatenate([hi, lo], axis=1), ones_bd, preferred_element_type=F32)


def _head_norm_rope(z, gain, ones_bd, cos, sin, first_half):
    zn = z * lax.rsqrt(_head_mean_square(z, ones_bd) + EPS) * gain
    partner = jnp.where(first_half, pltpu.roll(zn, LANES - 16, 1), pltpu.roll(zn, 16, 1))
    return zn * cos + partner * sin


def _inproj0_kernel(x_ref, gain_ref, sh_ref, sc_ref, w_ref, cos_ref, sin_ref, qg_ref, kg_ref, ones_ref,
                    q_ref, k_ref, v_ref, p_ref):
    a = _modulate(x_ref[0], gain_ref[...], sh_ref[...], sc_ref[...])
    y = jnp.dot(a.astype(BF16), w_ref[...], preferred_element_type=F32)
    cos, sin, ones_bd = cos_ref[...], sin_ref[...], ones_ref[...]
    lane = lax.broadcasted_iota(jnp.int32, cos.shape, 1)
    first_half = (lane % 32) < 16
    for s in range(ATTN_WIDTH // LANES):
        r = _head_norm_rope(y[:, s * LANES:(s + 1) * LANES], qg_ref[...], ones_bd, cos, sin, first_half)
        r = (r * Q_SCALE).astype(BF16)
        q_ref[0, 2 * s] = r[:, :HEAD_DIM]
        q_ref[0, 2 * s + 1] = r[:, HEAD_DIM:]
    kr = _head_norm_rope(y[:, ATTN_WIDTH:ATTN_WIDTH + KV_WIDTH], kg_ref[...], ones_bd, cos, sin,
                         first_half).astype(BF16)
    k_ref[0, 0] = kr[:, :HEAD_DIM]
    k_ref[0, 1] = kr[:, HEAD_DIM:]
    vv = y[:, ATTN_WIDTH + KV_WIDTH:ATTN_WIDTH + 2 * KV_WIDTH].astype(BF16)
    v_ref[0, 0] = vv[:, :HEAD_DIM]
    v_ref[0, 1] = vv[:, HEAD_DIM:]
    p_ref[0] = y[:, ATTN_WIDTH + 2 * KV_WIDTH:].astype(BF16)


def _inproj0_ctx_kernel(x_ref, gain_ref, sh_ref, sc_ref, w_ref, kg_ref, ones_ref, k_ref, v_ref):
    a = _modulate(x_ref[0], gain_ref[...], sh_ref[...], sc_ref[...])
    y = jnp.dot(a.astype(BF16), w_ref[...], preferred_element_type=F32)
    z = y[:, :KV_WIDTH]
    kr = (z * lax.rsqrt(_head_mean_square(z, ones_ref[...]) + EPS) * kg_ref[...]).astype(BF16)
    k_ref[0, 0] = kr[:, :HEAD_DIM]
    k_ref[0, 1] = kr[:, HEAD_DIM:]
    vv = y[:, KV_WIDTH:].astype(BF16)
    v_ref[0, 0] = vv[:, :HEAD_DIM]
    v_ref[0, 1] = vv[:, HEAD_DIM:]


def _rope_tables(seq):
    t = jnp.arange(seq)
    row = (t // GRID_W).astype(F32)
    col = (t % GRID_W).astype(F32)
    half = HEAD_DIM // 2
    inv = ROPE_THETA ** (-jnp.arange(0, half, 2, dtype=F32) / half)
    ar, ac = row[:, None] * inv, col[:, None] * inv
    cos = jnp.concatenate([jnp.cos(ar), jnp.cos(ar), jnp.cos(ac), jnp.cos(ac)], axis=-1)
    sin = jnp.concatenate([-jnp.sin(ar), jnp.sin(ar), -jnp.sin(ac), jnp.sin(ac)], axis=-1)
    return jnp.tile(cos, (1, LANES // HEAD_DIM)), jnp.tile(sin, (1, LANES // HEAD_DIM))


def _head_mean_matrix():
    r = jnp.arange(LANES)
    same = (r[:, None] // HEAD_DIM) == (r[None, :] // HEAD_DIM)
    block = jnp.where(same, 1.0 / HEAD_DIM, 0.0).astype(BF16)
    return jnp.concatenate([block, block], axis=0)


def _inproj0(x, mods, gain, w_in, q_gain, k_gain, tm):
    bsz, seq, d = x.shape
    cos, sin = _rope_tables(seq)
    qg = jnp.tile(q_gain, LANES // HEAD_DIM).reshape(1, LANES)
    kg = jnp.tile(k_gain, LANES // HEAD_DIM).reshape(1, LANES)
    head = lambda n: pl.BlockSpec((1, n, tm, HEAD_DIM), lambda b, i: (b, 0, i, 0))
    return pl.pallas_call(
        _inproj0_kernel,
        grid=(bsz, seq // tm),
        in_specs=[pl.BlockSpec((1, tm, d), lambda b, i: (b, i, 0)),
                  _full((1, d)),
                  _mod_spec(0, lambda b, i: b, 0),
                  _mod_spec(0, lambda b, i: b, 1),
                  _full((d, EVEN_IN)),
                  pl.BlockSpec((tm, LANES), lambda b, i: (i, 0)),
                  pl.BlockSpec((tm, LANES), lambda b, i: (i, 0)),
                  _full((1, LANES)), _full((1, LANES)), _full((2 * LANES, LANES))],
        out_specs=[head(N_Q_HEADS), head(N_KV_HEADS), head(N_KV_HEADS),
                   pl.BlockSpec((1, tm, POOL_WIDTH), lambda b, i: (b, i, 0))],
        out_shape=[jax.ShapeDtypeStruct((bsz, N_Q_HEADS, seq, HEAD_DIM), BF16),
                   jax.ShapeDtypeStruct((bsz, N_KV_HEADS, seq, HEAD_DIM), BF16),
                   jax.ShapeDtypeStruct((bsz, N_KV_HEADS, seq, HEAD_DIM), BF16),
                   jax.ShapeDtypeStruct((bsz, seq, POOL_WIDTH), BF16)],
        compiler_params=_params(("parallel", "parallel")),
        name="inproj0",
    )(x, gain.reshape(1, d), mods, mods, w_in.astype(BF16), cos, sin, qg, kg, _head_mean_matrix())


def _inproj0_ctx(ctx, mods, ctx_row, gain, w_kv, k_gain):
    bsz, n_ctx, d = ctx.shape
    kg = jnp.tile(k_gain, LANES // HEAD_DIM).reshape(1, LANES)
    head = pl.BlockSpec((1, N_KV_HEADS, n_ctx, HEAD_DIM), lambda b: (b, 0, 0, 0))
    return pl.pallas_call(
        _inproj0_ctx_kernel,
        grid=(bsz,),
        in_specs=[pl.BlockSpec((1, n_ctx, d), lambda b: (b, 0, 0)),
                  _full((1, d)),
                  _mod_spec(0, lambda b: ctx_row, 0),
                  _mod_spec(0, lambda b: ctx_row, 1),
                  _full((d, 2 * KV_WIDTH)),
                  _full((1, LANES)), _full((2 * LANES, LANES))],
        out_specs=[head, head],
        out_shape=[jax.ShapeDtypeStruct((bsz, N_KV_HEADS, n_ctx, HEAD_DIM), BF16)] * 2,
        compiler_params=_params(("parallel",)),
        name="inproj0_ctx",
    )(ctx, gain.reshape(1, d), mods, mods, w_kv.astype(BF16), kg, _head_mean_matrix())


def _attn_kernel(shift_ref, q_ref, kl_ref, vl_ref, kc_ref, vc_ref, o_ref, *, tq, tk):
    rows = Q_PER_KV * tq
    q = q_ref[0].reshape(rows, HEAD_DIM)
    seq = kl_ref.shape[2]
    chunks = [(kc_ref, vc_ref, 0, kc_ref.shape[2])]
    chunks += [(kl_ref, vl_ref, c * tk, tk) for c in range(seq // tk)]
    scores = lambda k: lax.dot_general(q, k, (((1,), (1,)), ((), ())), preferred_element_type=F32)

    bound = shift_ref[0, 0]
    safe = bound <= SAFE_SOFTMAX_SHIFT

    def finish(acc, l):
        o = acc / l
        o_ref[0] = jnp.concatenate([o[h * tq:(h + 1) * tq] for h in range(Q_PER_KV)], axis=1).astype(BF16)

    @pl.when(safe)
    def _():
        l = jnp.zeros((rows, 1), F32)
        acc = jnp.zeros((rows, HEAD_DIM), F32)
        for k_ref, v_ref, start, size in chunks:
            p = jnp.exp2(scores(k_ref[0, 0, start:start + size, :]) - bound)
            l = l + jnp.sum(p, axis=1, keepdims=True)
            acc = acc + jnp.dot(p.astype(BF16), v_ref[0, 0, start:start + size, :], preferred_element_type=F32)
        finish(acc, l)

    @pl.when(jnp.logical_not(safe))
    def _():
        m = jnp.full((rows, 1), -jnp.inf, F32)
        l = jnp.zeros((rows, 1), F32)
        acc = jnp.zeros((rows, HEAD_DIM), F32)
        for k_ref, v_ref, start, size in chunks:
            s = scores(k_ref[0, 0, start:start + size, :])
            m_new = jnp.maximum(m, jnp.max(s, axis=1, keepdims=True))
            alpha = jnp.exp2(m - m_new)
            p = jnp.exp2(s - m_new)
            l = alpha * l + jnp.sum(p, axis=1, keepdims=True)
            acc = alpha * acc + jnp.dot(p.astype(BF16), v_ref[0, 0, start:start + size, :],
                                        preferred_element_type=F32)
            m = m_new
        finish(acc, l)


def _score_bound(q_gain, k_gain):
    bound = HEAD_DIM * Q_SCALE * jnp.max(jnp.abs(q_gain)) * jnp.max(jnp.abs(k_gain)) * SCORE_BOUND_MARGIN
    return bound.reshape(1, 1).astype(F32)


def _attention(q, k, v, kc, vc, q_gain, k_gain, tq, tk):
    bsz, _, seq, _ = q.shape
    n_ctx = kc.shape[2]
    kv_spec = lambda n: pl.BlockSpec((1, 1, n, HEAD_DIM), lambda b, g, i: (b, g, 0, 0))
    return pl.pallas_call(
        functools.partial(_attn_kernel, tq=tq, tk=tk),
        grid=(bsz, N_KV_HEADS, seq // tq),
        in_specs=[pl.BlockSpec((1, 1), lambda b, g, i: (0, 0), memory_space=pltpu.SMEM),
                  pl.BlockSpec((1, Q_PER_KV, tq, HEAD_DIM), lambda b, g, i: (b, g, i, 0)),
                  kv_spec(seq), kv_spec(seq), kv_spec(n_ctx), kv_spec(n_ctx)],
        out_specs=pl.BlockSpec((1, tq, Q_PER_KV * HEAD_DIM), lambda b, g, i: (b, i, g)),
        out_shape=jax.ShapeDtypeStruct((bsz, seq, ATTN_WIDTH), BF16),
        compiler_params=_params(("parallel", "parallel", "parallel")),
        name="attention",
    )(_score_bound(q_gain, k_gain), q, k, v, kc, vc)


def _route(logits):
    lane = lax.broadcasted_iota(jnp.int32, logits.shape, 1).astype(F32)
    neg = -jnp.inf
    big = float(ROUTER_LANES)
    first_index = lambda mask: jnp.min(jnp.where(mask, lane, big), axis=1, keepdims=True)
    is_g = lane < N_GROUPS
    gm = jnp.max(jnp.where(is_g, logits, neg), axis=1, keepdims=True)
    gidx = first_index(is_g & (logits == gm))
    gden = jnp.sum(jnp.where(is_g, jnp.exp(logits - gm), 0.0), axis=1, keepdims=True)
    g_p = 1.0 / gden
    first = N_GROUPS + EXPERTS_PER_GROUP * gidx
    sel = (lane >= first) & (lane < first + EXPERTS_PER_GROUP)
    e1 = jnp.max(jnp.where(sel, logits, neg), axis=1, keepdims=True)
    i1 = first_index(sel & (logits == e1))
    rest = sel & (lane != i1)
    e2 = jnp.max(jnp.where(rest, logits, neg), axis=1, keepdims=True)
    i2 = first_index(rest & (logits == e2))
    p2 = jnp.exp(e2 - e1)
    w1 = g_p * (1.0 / (1.0 + p2))
    w2 = g_p * (p2 / (1.0 + p2))
    lo = jnp.minimum(i1, i2) - first
    hi = jnp.maximum(i1, i2) - first
    pair = jnp.where(lo == 0, hi - 1, jnp.where(lo == 1, hi + 1, PAIRS_PER_GROUP - 1.0))
    bucket = PAIRS_PER_GROUP * gidx + pair
    w_lo = jnp.where(i1 < i2, w1, w2)
    w_hi = jnp.where(i1 < i2, w2, w1)
    return jnp.where(lane == INFO_BUCKET, bucket,
                     jnp.where(lane == INFO_W_LO, w_lo, jnp.where(lane == INFO_W_HI, w_hi, 0.0)))


def _bucket_onehot(info, sel):
    brow = lax.dot_general(sel, info.astype(BF16), (((1,), (1,)), ((), ())),
                           preferred_element_type=F32)[0:1]
    bid = lax.broadcasted_iota(jnp.int32, (BUCKET_ROWS, info.shape[0]), 0)
    return (bid == brow.astype(jnp.int32)).astype(F32)


def _bucket_selector():
    return jnp.zeros((8, ROUTER_LANES), F32).at[0, INFO_BUCKET].set(1.0).astype(BF16)


def _tail(y, x_res, gate1, gain2, shift2, scale2, rw_both, rbias, sel, h1_ref, info_ref, cnt_ref):
    h1 = x_res + gate1 * y
    h1_ref[0] = h1
    t = _modulate(h1, gain2, shift2, scale2)
    t_hi = t.astype(BF16)
    t_lo = (t - t_hi.astype(F32)).astype(BF16)
    both = jnp.dot(t_hi, rw_both, preferred_element_type=F32)
    logits = (both[:, :ROUTER_LANES] + both[:, ROUTER_LANES:]
              + jnp.dot(t_lo, rw_both[:, :ROUTER_LANES], preferred_element_type=F32)) + rbias
    info = _route(logits)
    info_ref[0] = info

    @pl.when((pl.program_id(0) == 0) & (pl.program_id(1) == 0))
    def _():
        cnt_ref[...] = jnp.zeros_like(cnt_ref)

    cnt_ref[...] += jnp.sum(_bucket_onehot(info, sel), axis=1, keepdims=True)


def _router_operands(rg_w, rg_b, re_w, re_b):
    d = rg_w.shape[0]
    w = jnp.concatenate([rg_w, re_w, jnp.zeros((d, ROUTER_LANES - N_GROUPS - N_EXPERTS), F32)], axis=1)
    b = jnp.concatenate([rg_b, re_b, jnp.zeros((ROUTER_LANES - N_GROUPS - N_EXPERTS,), F32)])
    w_hi = w.astype(BF16)
    w_lo = (w - w_hi.astype(F32)).astype(BF16)
    return jnp.concatenate([w_hi, w_lo], axis=1), b.reshape(1, ROUTER_LANES)


def _fill_halo(buf, main_ref, prev_ref, next_ref, tm, i, n_tiles, halo=HALO):
    zero = jnp.zeros((), buf.dtype)
    buf[halo:halo + tm] = main_ref[0].astype(buf.dtype)
    buf[0:halo] = jnp.where(i > 0, prev_ref[0].astype(buf.dtype), zero)
    buf[halo + tm:2 * halo + tm] = jnp.where(i < n_tiles - 1, next_ref[0].astype(buf.dtype), zero)


def _halo_specs(tm, seq, width, halo=HALO):
    per = tm // halo
    last = seq // halo - 1
    return [pl.BlockSpec((1, tm, width), lambda b, i: (b, i, 0)),
            pl.BlockSpec((1, halo, width), lambda b, i: (b, jnp.maximum(i * per - 1, 0), 0)),
            pl.BlockSpec((1, halo, width), lambda b, i: (b, jnp.minimum((i + 1) * per, last), 0))]


def _pool_bands():
    r = jnp.arange(POOL_BLOCK)[:, None]
    c = jnp.arange(POOL_BLOCK + 2 * HALO)[None, :]
    return jnp.stack([((c >= r + HALO - w // 2) & (c < r + HALO + w - w // 2)) for w in POOL_WINDOWS]).astype(BF16)


def _out0_kernel(o_ref, p_ref, pprev_ref, pnext_ref, x_ref, g1_ref, gain2_ref, sh2_ref, sc2_ref,
                 poolw_ref, pscale_ref, band_ref, wout_ref, rw_ref, rb_ref, sel_ref,
                 h1_ref, info_ref, cnt_ref, pbuf, *, tm, seq):
    i = pl.program_id(1)
    _fill_halo(pbuf, p_ref, pprev_ref, pnext_ref, tm, i, seq // tm)
    pos = i * tm + lax.broadcasted_iota(jnp.int32, (tm, 1), 0)
    pooled = []
    for g, w in enumerate(POOL_WINDOWS):
        sl = slice(g * POOL_GROUP, (g + 1) * POOL_GROUP)
        acc = jnp.concatenate(
            [jnp.dot(band_ref[g], pbuf[b:b + POOL_BLOCK + 2 * HALO, sl], preferred_element_type=F32)
             for b in range(0, tm, POOL_BLOCK)], axis=0)
        lo = jnp.clip(pos - w // 2, 0, seq)
        hi = jnp.clip(pos + w - w // 2, 0, seq)
        mean = acc * (1.0 / (hi - lo).astype(F32))
        dlt = (mean - p_ref[0, :, sl].astype(F32)).astype(BF16)
        pooled.append((jnp.dot(dlt, poolw_ref[g], preferred_element_type=F32) * pscale_ref[:, sl]).astype(BF16))
    mixed = jnp.concatenate([o_ref[0]] + pooled, axis=1)
    y = jnp.dot(mixed, wout_ref[...], preferred_element_type=F32)
    _tail(y, x_ref[0], g1_ref[...], gain2_ref[...], sh2_ref[...], sc2_ref[...],
          rw_ref[...], rb_ref[...], sel_ref[...], h1_ref, info_ref, cnt_ref)


def _tail_specs(layer, d):
    by_batch = lambda b, i: b
    ins = [_mod_spec(layer, by_batch, 2), _full((1, d)), _mod_spec(layer, by_batch, 3),
           _mod_spec(layer, by_batch, 4)]
    return ins


def _tail_outs(bsz, seq, d, tm):
    specs = [pl.BlockSpec((1, tm, d), lambda b, i: (b, i, 0)),
             pl.BlockSpec((1, tm, ROUTER_LANES), lambda b, i: (b, i, 0)),
             _full((BUCKET_ROWS, LANES))]
    shapes = [jax.ShapeDtypeStruct((bsz, seq, d), F32),
              jax.ShapeDtypeStruct((bsz, seq, ROUTER_LANES), F32),
              jax.ShapeDtypeStruct((BUCKET_ROWS, LANES), F32)]
    return specs, shapes


def _out0(o, p, x, mods, gain2, pool_w, pool_scale, w_out, router, tm):
    bsz, seq, d = x.shape
    rw_both, rb = router
    bands = _pool_bands()
    out_specs, out_shapes = _tail_outs(bsz, seq, d, tm)
    return pl.pallas_call(
        functools.partial(_out0_kernel, tm=tm, seq=seq),
        grid=(bsz, seq // tm),
        in_specs=[pl.BlockSpec((1, tm, ATTN_WIDTH), lambda b, i: (b, i, 0))]
        + _halo_specs(tm, seq, POOL_WIDTH)
        + [pl.BlockSpec((1, tm, d), lambda b, i: (b, i, 0))]
        + _tail_specs(0, d)
        + [_full(pool_w.shape), _full((1, POOL_WIDTH)), _full(bands.shape), _full(w_out.shape),
           _full(rw_both.shape), _full(rb.shape), _full((8, ROUTER_LANES))],
        out_specs=out_specs,
        out_shape=out_shapes,
        scratch_shapes=[pltpu.VMEM((tm + 2 * HALO, POOL_WIDTH), BF16)],
        compiler_params=_params(("arbitrary", "arbitrary")),
        name="out0",
    )(o, p, p, p, x, mods, gain2.reshape(1, d), mods, mods,
      pool_w.astype(BF16), pool_scale.reshape(1, POOL_WIDTH), bands, w_out.astype(BF16), rw_both, rb,
      _bucket_selector())


def _inproj1_kernel(pos_ref, posn_ref, h1_ref, g2_ref, gain_ref, sh_ref, sc_ref, w_ref, sgg_ref, sgw_ref, sgb_ref,
                    ys_ref, h_ref, yc_ref, z_ref, bg_ref, ybuf, sem, *, tm, n_steps):
    x = h1_ref[0] + g2_ref[...] * _fetch_sorted_rows(ys_ref, pos_ref, posn_ref, ybuf, sem, tm, n_steps)
    h_ref[0] = x
    a = _modulate(x, gain_ref[...], sh_ref[...], sc_ref[...])
    y = jnp.dot(a.astype(BF16), w_ref[...], preferred_element_type=F32)
    for g in range(SG_GROUPS):
        sl = slice(g * LANES, (g + 1) * LANES)
        u = y[:, sl]
        vg = y[:, SG_WIDTH + g * LANES:SG_WIDTH + (g + 1) * LANES]
        ms = jnp.mean(vg * vg, axis=-1, keepdims=True)
        vn = (vg * lax.rsqrt(ms + EPS) * sgg_ref[:, sl]).astype(BF16)
        for c in range(tm // SG_CHUNK):
            rows = slice(c * SG_CHUNK, (c + 1) * SG_CHUNK)
            s = jnp.dot(sgw_ref[g], vn[rows], preferred_element_type=F32) + sgb_ref[g]
            yc_ref[0, rows, sl] = (u[rows] * s).astype(BF16)
    hx = y[:, 2 * SG_WIDTH:2 * SG_WIDTH + CONV_WIDTH]
    bg_ref[0] = y[:, 2 * SG_WIDTH + CONV_WIDTH:2 * SG_WIDTH + 2 * CONV_WIDTH].astype(BF16)
    cg = y[:, 2 * SG_WIDTH + 2 * CONV_WIDTH:]
    z_ref[0] = (cg * hx).astype(BF16)


def _pos_spec(tm, nt, n_steps, ahead):
    return pl.BlockSpec((1, 1, tm), lambda b, i: (jnp.minimum(b * nt + i + ahead, n_steps - 1), 0, 0),
                        memory_space=pltpu.SMEM)


def _inproj1(h1, ys, pos, mods, gain, w_in, sg_gain, sg_w, sg_b, tm):
    bsz, seq, d = h1.shape
    nt = seq // tm
    n_steps = bsz * nt
    sgb = jnp.broadcast_to(sg_b[:, :, None], (SG_GROUPS, SG_CHUNK, LANES))
    by_batch = lambda b, i: b
    wide = lambda w, dt: (pl.BlockSpec((1, tm, w), lambda b, i: (b, i, 0)),
                          jax.ShapeDtypeStruct((bsz, seq, w), dt))
    outs = [wide(d, F32), wide(SG_WIDTH, BF16), wide(CONV_WIDTH, BF16), wide(CONV_WIDTH, BF16)]
    return pl.pallas_call(
        functools.partial(_inproj1_kernel, tm=tm, n_steps=n_steps),
        grid=(bsz, nt),
        in_specs=[_pos_spec(tm, nt, n_steps, 0), _pos_spec(tm, nt, n_steps, 1),
                  pl.BlockSpec((1, tm, d), lambda b, i: (b, i, 0)),
                  _mod_spec(0, by_batch, 5),
                  _full((1, d)),
                  _mod_spec(1, by_batch, 0),
                  _mod_spec(1, by_batch, 1),
                  _full((d, ODD_IN)),
                  _full((1, SG_WIDTH)), _full(sg_w.shape), _full(sgb.shape),
                  pl.BlockSpec(memory_space=pl.ANY)],
        out_specs=[s for s, _ in outs],
        out_shape=[s for _, s in outs],
        scratch_shapes=[pltpu.VMEM((2, tm, d), F32), pltpu.SemaphoreType.DMA((2,))],
        compiler_params=_params(("arbitrary", "arbitrary")),
        name="inproj1",
    )(pos, pos, h1, mods, gain.reshape(1, d), mods, mods, w_in.astype(BF16), sg_gain.reshape(1, SG_WIDTH),
      sg_w.astype(BF16), sgb, ys)


def _out1_kernel(yc_ref, z_ref, zprev_ref, znext_ref, bg_ref, x_ref, g1_ref, gain2_ref, sh2_ref, sc2_ref,
                 convw_ref, wout_ref, rw_ref, rb_ref, sel_ref,
                 h1_ref, info_ref, cnt_ref, zbuf, *, tm, seq):
    i = pl.program_id(1)
    _fill_halo(zbuf, z_ref, zprev_ref, znext_ref, tm, i, seq // tm, HALO)
    zc = (zbuf[HALO - 1:HALO - 1 + tm] * convw_ref[0:1, :]
          + zbuf[HALO:HALO + tm] * convw_ref[1:2, :]
          + zbuf[HALO + 1:HALO + 1 + tm] * convw_ref[2:3, :])
    yd = (bg_ref[0].astype(F32) * zc).astype(BF16)
    y = (jnp.dot(yc_ref[0], wout_ref[0:SG_WIDTH, :], preferred_element_type=F32)
         + jnp.dot(yd, wout_ref[SG_WIDTH:, :], preferred_element_type=F32))
    _tail(y, x_ref[0], g1_ref[...], gain2_ref[...], sh2_ref[...], sc2_ref[...],
          rw_ref[...], rb_ref[...], sel_ref[...], h1_ref, info_ref, cnt_ref)


def _out1(yc, z, bg, x, mods, gain2, conv_w, w_out, router, tm):
    bsz, seq, d = x.shape
    rw_both, rb = router
    out_specs, out_shapes = _tail_outs(bsz, seq, d, tm)
    wide = pl.BlockSpec((1, tm, CONV_WIDTH), lambda b, i: (b, i, 0))
    return pl.pallas_call(
        functools.partial(_out1_kernel, tm=tm, seq=seq),
        grid=(bsz, seq // tm),
        in_specs=[wide] + _halo_specs(tm, seq, CONV_WIDTH, HALO) + [wide]
        + [pl.BlockSpec((1, tm, d), lambda b, i: (b, i, 0))]
        + _tail_specs(1, d)
        + [_full((3, CONV_WIDTH)), _full(w_out.shape),
           _full(rw_both.shape), _full(rb.shape), _full((8, ROUTER_LANES))],
        out_specs=out_specs,
        out_shape=out_shapes,
        scratch_shapes=[pltpu.VMEM((tm + 2 * HALO, CONV_WIDTH), F32)],
        compiler_params=_params(("arbitrary", "arbitrary")),
        name="out1",
    )(yc, z, z, z, bg, x, mods, gain2.reshape(1, d), mods, mods,
      conv_w.reshape(3, CONV_WIDTH), w_out.astype(BF16), rw_both, rb, _bucket_selector())


def _plan_pos_kernel(info_ref, sel_ref, cnt_ref, ltri_ref, utri_ref, pos_ref, meta_ref, start_sc, run_sc):
    @pl.when(pl.program_id(0) == 0)
    def _():
        padded = jnp.ceil(cnt_ref[...] * (1.0 / SORT_TILE)) * SORT_TILE
        incl = jnp.dot(ltri_ref[...], padded, precision=lax.Precision.HIGHEST, preferred_element_type=F32)
        start_sc[...] = incl - padded
        run_sc[...] = jnp.zeros_like(run_sc)
        ends = jnp.broadcast_to(incl[:, 0:1], (BUCKET_ROWS, META_LANES))
        bid = lax.broadcasted_iota(jnp.int32, ends.shape, 0)
        tile = lax.broadcasted_iota(jnp.int32, (1, META_LANES), 1)

        def bucket_of(row0):
            done = jnp.where((bid < N_BUCKETS) & (ends <= row0), 1.0, 0.0)
            return jnp.minimum(jnp.sum(done, axis=0, keepdims=True), N_BUCKETS - 1.0).astype(jnp.int32)

        def experts_of(bucket):
            grp = ((bucket >= PAIRS_PER_GROUP).astype(jnp.int32)
                   + (bucket >= 2 * PAIRS_PER_GROUP).astype(jnp.int32)
                   + (bucket >= 3 * PAIRS_PER_GROUP).astype(jnp.int32))
            pair = bucket - PAIRS_PER_GROUP * grp
            lo = (pair >= 3).astype(jnp.int32) + (pair >= 5).astype(jnp.int32)
            hi = jnp.where(pair == 0, 1, jnp.where((pair == 1) | (pair == 3), 2, 3))
            return EXPERTS_PER_GROUP * grp + lo, EXPERTS_PER_GROUP * grp + hi

        row0 = (tile * SORT_TILE).astype(F32)
        tb = bucket_of(row0)
        n_used = (incl[N_BUCKETS - 1:N_BUCKETS, 0:1] * (1.0 / SORT_TILE)).astype(jnp.int32)
        fill = (tile >= n_used - 1) | (tb != bucket_of(row0 + SORT_TILE))
        first = (tile == 0) | (tb != bucket_of(row0 - SORT_TILE))
        own_end = jnp.sum(jnp.where(bid == tb, ends, 0.0), axis=0, keepdims=True)
        nonempty = jnp.broadcast_to(padded[:, 0:1], ends.shape) > 0.0
        ordinal = jnp.sum(jnp.where((bid < tb) & nonempty, 1.0, 0.0), axis=0, keepdims=True)
        rows = [None] * 8
        rows[META_EXPERT_LO], rows[META_EXPERT_HI] = experts_of(tb)
        rows[META_N_USED] = jnp.broadcast_to(n_used, (1, META_LANES))
        rows[META_FILL] = fill.astype(jnp.int32)
        rows[META_FIRST] = first.astype(jnp.int32)
        rows[META_WSLOT] = (ordinal - 2.0 * jnp.floor(ordinal * 0.5)).astype(jnp.int32)
        rows[META_NEXT_LO], rows[META_NEXT_HI] = experts_of(bucket_of(own_end))
        for r, row in enumerate(rows):
            meta_ref[r:r + 1, :] = row

    tm = utri_ref.shape[0]
    base = start_sc[:, 0:1] + run_sc[:, 0:1]
    for k in range(pos_ref.shape[0]):
        oh = _bucket_onehot(info_ref[k * tm:(k + 1) * tm], sel_ref[...])
        before = jnp.dot(oh.astype(BF16), utri_ref[...], preferred_element_type=F32)
        pos_ref[k] = jnp.sum(oh * (before + base), axis=0, keepdims=True).astype(jnp.int32)
        base = base + jnp.sum(oh, axis=1, keepdims=True)
    run_sc[...] = jnp.broadcast_to(base - start_sc[:, 0:1], run_sc.shape)


def _sort_plan(info, cnt, tm):
    n = info.shape[0]
    sel = _bucket_selector()
    sub = 4 if (n // tm) % 4 == 0 else 1
    info_spec = pl.BlockSpec((sub * tm, ROUTER_LANES), lambda i: (i, 0))
    r = jnp.arange(BUCKET_ROWS)
    ltri = (r[:, None] >= r[None, :]).astype(F32)
    t = jnp.arange(tm)
    utri = (t[:, None] < t[None, :]).astype(BF16)
    return pl.pallas_call(
        _plan_pos_kernel,
        grid=(n // (sub * tm),),
        in_specs=[info_spec, _full(sel.shape), _full(cnt.shape), _full(ltri.shape), _full(utri.shape)],
        out_specs=[pl.BlockSpec((sub, 1, tm), lambda i: (i, 0, 0)), _full((8, META_LANES))],
        out_shape=[jax.ShapeDtypeStruct((n // tm, 1, tm), jnp.int32),
                   jax.ShapeDtypeStruct((8, META_LANES), jnp.int32)],
        scratch_shapes=[pltpu.VMEM((BUCKET_ROWS, LANES), F32), pltpu.VMEM((BUCKET_ROWS, LANES), F32)],
        compiler_params=_params(("arbitrary",)),
        name="plan_pos",
    )(info, sel, cnt, ltri, utri)


def _dispatch_kernel(pos_ref, fill_ref, h1_ref, gain2_ref, sh2_ref, sc2_ref, info_ref, xs_ref,
                     rowbuf, zbuf, sem, zsem, *, tm, n_steps, n_tiles):
    step = pl.program_id(0) * pl.num_programs(1) + pl.program_id(1)
    slot = step % 2

    @pl.when(step == 0)
    def _():
        zbuf[...] = jnp.zeros_like(zbuf)
        fill = lambda j: pltpu.make_async_copy(zbuf, xs_ref.at[pl.ds(j * SORT_TILE, SORT_TILE), 0], zsem)
        for j in range(n_tiles):
            pl.when(fill_ref[0, j] == 1)(lambda j=j: fill(j).start())
        for j in range(n_tiles):
            pl.when(fill_ref[0, j] == 1)(lambda j=j: fill(j).wait())

    def wait(s):
        pltpu.make_async_copy(rowbuf.at[s], xs_ref.at[pl.ds(0, tm), 0], sem.at[s]).wait()

    def send(s):
        @pl.when(step >= 2)
        def _():
            wait(s)

        rowbuf[s, :, 0:D_MODEL] = _modulate(h1_ref[0], gain2_ref[...], sh2_ref[...], sc2_ref[...])
        rowbuf[s, :, D_MODEL:] = info_ref[0]
        for r in range(tm):
            pltpu.make_async_copy(rowbuf.at[s, pl.ds(r, 1)], xs_ref.at[pos_ref[0, 0, r]],
                                  sem.at[s]).start(priority=r % 2)

    for s in range(2):
        pl.when(slot == s)(functools.partial(send, s))

    @pl.when(step == n_steps - 1)
    def _():
        wait(slot)
        if n_steps > 1:
            wait(1 - slot)


def _gmoe_kernel(ea_ref, eb_ref, nu_ref, first_ref, wslot_ref, na_ref, nb_ref, xs_ref, wg_hbm, wu_hbm, wd_hbm,
                 ys_ref, xbuf, ybuf, zbuf, wg_buf, wu_buf, wd_buf, sem_in, sem_out, zsem, wsem, *, layer):
    j = pl.program_id(0)
    n_used = nu_ref[0]
    slot = j % 2
    wslot = wslot_ref[j]
    tile = lambda ref, t: ref.at[pl.ds(t * SORT_TILE, SORT_TILE), 0]
    in_copy = lambda t, s: pltpu.make_async_copy(tile(xs_ref, t), xbuf.at[s], sem_in.at[s])
    out_copy = lambda t, s: pltpu.make_async_copy(ybuf.at[s], tile(ys_ref, t), sem_out.at[s])

    def weight_copies(e_lo, e_hi, s):
        return [pltpu.make_async_copy(hbm.at[layer, e], buf.at[s, which], wsem.at[s])
                for hbm, buf in ((wg_hbm, wg_buf), (wu_hbm, wu_buf), (wd_hbm, wd_buf))
                for which, e in ((0, e_lo), (1, e_hi))]

    @pl.when(j == 0)
    def _():
        in_copy(0, 0).start()
        for c in weight_copies(ea_ref[0], eb_ref[0], 0):
            c.start()

    @pl.when(j + 1 < n_used)
    def _():
        in_copy(j + 1, 1 - slot).start()

    @pl.when(j < n_used)
    def _():
        @pl.when(first_ref[j] == 1)
        def _():
            for c in weight_copies(ea_ref[j], eb_ref[j], wslot):
                c.wait()
            for c in weight_copies(na_ref[j], nb_ref[j], 1 - wslot):
                c.start()

        in_copy(j, slot).wait()

        @pl.when(j >= 2)
        def _():
            out_copy(j - 2, slot).wait()

        x = xbuf[slot, :, 0:D_MODEL].astype(BF16)

        def expert(which, w):
            gt = jnp.dot(x, wg_buf[wslot, which].astype(BF16), preferred_element_type=F32)
            up = jnp.dot(x, wu_buf[wslot, which].astype(BF16), preferred_element_type=F32)
            h = (gt * jax.nn.sigmoid(gt)) * up * w
            return jnp.dot(h.astype(BF16), wd_buf[wslot, which].astype(BF16), preferred_element_type=F32)

        w_lo = xbuf[slot, :, D_MODEL + INFO_W_LO:D_MODEL + INFO_W_LO + 1]
        w_hi = xbuf[slot, :, D_MODEL + INFO_W_HI:D_MODEL + INFO_W_HI + 1]
        ybuf[slot] = expert(0, w_lo) + expert(1, w_hi)
        out_copy(j, slot).start()

        @pl.when(j == n_used - 1)
        def _():
            out_copy(j, slot).wait()

            @pl.when(j >= 1)
            def _():
                out_copy(j - 1, 1 - slot).wait()

            for c in weight_copies(ea_ref[j], eb_ref[j], 1 - wslot):
                c.wait()

    @pl.when(j >= n_used)
    def _():
        @pl.when(j == n_used)
        def _():
            zbuf[...] = jnp.zeros_like(zbuf)

        fill = pltpu.make_async_copy(zbuf, tile(ys_ref, j), zsem)
        fill.start()
        fill.wait()


def _fetch_sorted_rows(ys_ref, pos_ref, posn_ref, ybuf, sem, tm, n_steps):
    step = pl.program_id(0) * pl.num_programs(1) + pl.program_id(1)
    slot = step % 2

    def issue(p_ref, s):
        for r in range(tm):
            pltpu.make_async_copy(ys_ref.at[p_ref[0, 0, r]], ybuf.at[s, pl.ds(r, 1)],
                                  sem.at[s]).start(priority=r % 2)

    pl.when(step == 0)(functools.partial(issue, pos_ref, 0))
    for s in range(2):
        pl.when((step + 1 < n_steps) & (slot == s))(functools.partial(issue, posn_ref, 1 - s))
    pltpu.make_async_copy(ys_ref.at[pl.ds(0, tm), 0], ybuf.at[slot], sem.at[slot]).wait()
    return ybuf[slot]


def _combine_kernel(pos_ref, posn_ref, h1_ref, g2_ref, ys_ref, o_ref, ybuf, sem, *, tm, n_steps):
    rows = _fetch_sorted_rows(ys_ref, pos_ref, posn_ref, ybuf, sem, tm, n_steps)
    o_ref[0] = h1_ref[0] + g2_ref[...] * rows


def _experts_sorted(h1, info, cnt, mods, layer, gain2, w_gate, w_up, w_down, tm):
    bsz, seq, d = h1.shape
    nt = seq // tm
    n_steps = bsz * nt
    n = bsz * seq
    n_sorted = n + N_BUCKETS * SORT_TILE
    n_tiles = n_sorted // SORT_TILE
    assert n % SORT_TILE == 0 and n_tiles <= META_LANES
    pos, meta = _sort_plan(info.reshape(n, ROUTER_LANES), cnt, tm)
    tile = lambda w: pl.BlockSpec((1, tm, w), lambda b, i: (b, i, 0))
    any_spec = pl.BlockSpec(memory_space=pl.ANY)
    by_batch = lambda b, i: b

    xs = pl.pallas_call(
        functools.partial(_dispatch_kernel, tm=tm, n_steps=n_steps, n_tiles=n_tiles),
        grid=(bsz, nt),
        in_specs=[_pos_spec(tm, nt, n_steps, 0),
                  pl.BlockSpec((1, META_LANES), lambda b, i: (0, 0), memory_space=pltpu.SMEM),
                  tile(d), _full((1, d)), _mod_spec(layer, by_batch, 3),
                  _mod_spec(layer, by_batch, 4), tile(ROUTER_LANES)],
        out_specs=any_spec,
        out_shape=jax.ShapeDtypeStruct((n_sorted, 1, ROW_WIDTH), F32),
        scratch_shapes=[pltpu.VMEM((2, tm, ROW_WIDTH), F32), pltpu.VMEM((SORT_TILE, ROW_WIDTH), F32),
                        pltpu.SemaphoreType.DMA((2,)), pltpu.SemaphoreType.DMA(())],
        compiler_params=_params(("arbitrary", "arbitrary")),
        name=f"dispatch{layer}",
    )(pos, meta[META_FILL:META_FILL + 1], h1, gain2.reshape(1, d), mods, mods, info)

    ys = pl.pallas_call(
        functools.partial(_gmoe_kernel, layer=layer),
        grid_spec=pltpu.PrefetchScalarGridSpec(
            num_scalar_prefetch=7,
            grid=(n_tiles,),
            in_specs=[any_spec] * 4,
            out_specs=any_spec,
            scratch_shapes=[pltpu.VMEM((2, SORT_TILE, ROW_WIDTH), F32), pltpu.VMEM((2, SORT_TILE, d), F32),
                            pltpu.VMEM((SORT_TILE, d), F32),
                            pltpu.VMEM((2, 2, d, D_EXPERT), F32), pltpu.VMEM((2, 2, d, D_EXPERT), F32),
                            pltpu.VMEM((2, 2, D_EXPERT, d), F32),
                            pltpu.SemaphoreType.DMA((2,)), pltpu.SemaphoreType.DMA((2,)),
                            pltpu.SemaphoreType.DMA(()), pltpu.SemaphoreType.DMA((2,))]),
        out_shape=jax.ShapeDtypeStruct((n_sorted, 1, d), F32),
        compiler_params=_params(("arbitrary",)),
        name=f"experts{layer}",
    )(meta[META_EXPERT_LO], meta[META_EXPERT_HI], meta[META_N_USED, :1], meta[META_FIRST], meta[META_WSLOT],
      meta[META_NEXT_LO], meta[META_NEXT_HI], xs, w_gate, w_up, w_down)
    return ys, pos


def _combine(h1, ys, pos, mods, layer, tm):
    bsz, seq, d = h1.shape
    nt = seq // tm
    n_steps = bsz * nt
    tile = lambda w: pl.BlockSpec((1, tm, w), lambda b, i: (b, i, 0))
    return pl.pallas_call(
        functools.partial(_combine_kernel, tm=tm, n_steps=n_steps),
        grid=(bsz, nt),
        in_specs=[_pos_spec(tm, nt, n_steps, 0), _pos_spec(tm, nt, n_steps, 1), tile(d),
                  _mod_spec(layer, lambda b, i: b, 5), pl.BlockSpec(memory_space=pl.ANY)],
        out_specs=tile(d),
        out_shape=jax.ShapeDtypeStruct((bsz, seq, d), F32),
        scratch_shapes=[pltpu.VMEM((2, tm, d), F32), pltpu.SemaphoreType.DMA((2,))],
        compiler_params=_params(("arbitrary", "arbitrary")),
        name=f"combine{layer}",
    )(pos, pos, h1, mods, ys)


def kernel(x, c, ctx, c_ctx, mod_w, mod_b, norm1_g, norm2_g, even_w_in, q_gain, k_gain, pool_w, pool_scale,
           even_w_out, odd_w_in, sg_gain, sg_w, sg_b, conv_w, odd_w_out, router_g_w, router_g_b,
           router_e_w, router_e_b, w_gate, w_up, w_down):
    bsz, seq, d = x.shape
    tm = min(512, seq)
    tm_proj = min(1024, seq)
    cond = jnp.zeros((MOD_ROWS, d), F32).at[:bsz].set(c).at[bsz].set(c_ctx)
    mods = _adaln(cond, mod_w, mod_b).reshape(mod_w.shape[0], MOD_ROWS, 6, 1, d)

    q, k, v, p = _inproj0(x, mods, norm1_g[0], even_w_in[0], q_gain[0], k_gain[0], tm_proj)
    kc, vc = _inproj0_ctx(ctx, mods, bsz, norm1_g[0], even_w_in[0][:, ATTN_WIDTH:ATTN_WIDTH + 2 * KV_WIDTH],
                          k_gain[0])
    o = _attention(q, k, v, kc, vc, q_gain[0], k_gain[0], tq=min(512, seq), tk=min(2048, seq))
    router0 = _router_operands(router_g_w[0], router_g_b[0], router_e_w[0], router_e_b[0])
    h1, info, cnt = _out0(o, p, x, mods, norm2_g[0], pool_w[0], pool_scale[0], even_w_out[0], router0, tm_proj)
    ys, pos = _experts_sorted(h1, info, cnt, mods, 0, norm2_g[0], w_gate, w_up, w_down, tm)

    h, yc, z, bg = _inproj1(h1, ys, pos, mods, norm1_g[1], odd_w_in[0], sg_gain[0], sg_w[0], sg_b[0], tm)
    router1 = _router_operands(router_g_w[1], router_g_b[1], router_e_w[1], router_e_b[1])
    h1, info, cnt = _out1(yc, z, bg, h, mods, norm2_g[1], conv_w[0], odd_w_out[0], router1, tm_proj)
    ys, pos = _experts_sorted(h1, info, cnt, mods, 1, norm2_g[1], w_gate, w_up, w_down, tm)
    return _combine(h1, ys, pos, mods, 1, tm)
```

```python
import functools

import jax
import jax.numpy as jnp
from jax import lax
from jax.experimental import pallas as pl
from jax.experimental.pallas import tpu as pltpu

F32 = jnp.float32
BF16 = jnp.bfloat16

D_MODEL = 1024
GRID_W = 64
EPS = 1e-6
N_Q_HEADS = 8
N_KV_HEADS = 2
HEAD_DIM = 64
Q_PER_KV = N_Q_HEADS // N_KV_HEADS
ATTN_WIDTH = N_Q_HEADS * HEAD_DIM
KV_WIDTH = N_KV_HEADS * HEAD_DIM
ROPE_THETA = 10000.0
POOL_WINDOWS = (2, 4, 8, 16)
POOL_GROUP = 128
POOL_WIDTH = POOL_GROUP * len(POOL_WINDOWS)
SG_GROUPS = 4
SG_CHUNK = 128
SG_WIDTH = 512
CONV_WIDTH = 512
EVEN_IN = ATTN_WIDTH + 2 * KV_WIDTH + POOL_WIDTH
ODD_IN = 2 * SG_WIDTH + 3 * CONV_WIDTH
N_GROUPS = 4
EXPERTS_PER_GROUP = 4
N_EXPERTS = 16
D_EXPERT = 256

Q_SCALE = HEAD_DIM ** -0.5 * 1.4426950408889634
SAFE_SOFTMAX_SHIFT = 60.0
SCORE_BOUND_MARGIN = 1.02
LANES = 128
HALO = 16
POOL_BLOCK = 128
ROUTER_LANES = 128
MOD_ROWS = 16
VMEM_LIMIT = 48 * 1024 * 1024

PAIRS_PER_GROUP = 6
N_BUCKETS = N_GROUPS * PAIRS_PER_GROUP
BUCKET_ROWS = 32
SORT_TILE = 512
META_LANES = 256
META_EXPERT_LO, META_EXPERT_HI, META_N_USED, META_FILL, META_FIRST, META_WSLOT, META_NEXT_LO, META_NEXT_HI = range(8)
ROW_WIDTH = D_MODEL + ROUTER_LANES
INFO_BUCKET, INFO_W_LO, INFO_W_HI = 0, 1, 2


def _params(sem):
    return pltpu.CompilerParams(dimension_semantics=sem, vmem_limit_bytes=VMEM_LIMIT)


def _modulate(x, gain, shift, scale):
    ms = jnp.mean(x * x, axis=-1, keepdims=True)
    return (x * lax.rsqrt(ms + EPS) * gain) * (1.0 + scale) + shift


def _mod_spec(layer, row_fn, which):
    return pl.BlockSpec((None, None, None, 1, D_MODEL),
                        lambda *idx: (layer, row_fn(*idx), which, 0, 0))


def _full(shape):
    return pl.BlockSpec(shape, lambda *idx: (0,) * len(shape))


def _adaln_kernel(c_ref, w_ref, b_ref, o_ref):
    c = c_ref[...]
    s = c * jax.nn.sigmoid(c)
    o_ref[0] = jnp.dot(s, w_ref[0], precision=lax.Precision.HIGHEST,
                       preferred_element_type=F32) + b_ref[0]


def _adaln(cond, mod_w, mod_b):
    depth, d, n = mod_w.shape
    tn = 1024
    return pl.pallas_call(
        _adaln_kernel,
        grid=(depth, n // tn),
        in_specs=[_full((MOD_ROWS, d)),
                  pl.BlockSpec((1, d, tn), lambda l, j: (l, 0, j)),
                  pl.BlockSpec((1, 1, tn), lambda l, j: (l, 0, j))],
        out_specs=pl.BlockSpec((1, MOD_ROWS, tn), lambda l, j: (l, 0, j)),
        out_shape=jax.ShapeDtypeStruct((depth, MOD_ROWS, n), F32),
        compiler_params=_params(("arbitrary", "arbitrary")),
        name="adaln",
    )(cond, mod_w, mod_b.reshape(depth, 1, n))


def _head_mean_square(z, ones_bd):
    sq = z * z
    hi = sq.astype(BF16)
    lo = (sq - hi.astype(F32)).astype(BF16)
    return jnp.dot(jnp.concatenate([hi, lo], axis=1), ones_bd, preferred_element_type=F32)


def _head_norm_rope(z, gain, ones_bd, cos, sin, first_half):
    zn = z * lax.rsqrt(_head_mean_square(z, ones_bd) + EPS) * gain
    partner = jnp.where(first_half, pltpu.roll(zn, LANES - 16, 1), pltpu.roll(zn, 16, 1))
    return zn * cos + partner * sin


def _inproj0_kernel(x_ref, gain_ref, sh_ref, sc_ref, w_ref, cos_ref, sin_ref, qg_ref, kg_ref, ones_ref,
                    q_ref, k_ref, v_ref, p_ref):
    a = _modulate(x_ref[0], gain_ref[...], sh_ref[...], sc_ref[...])
    y = jnp.dot(a.astype(BF16), w_ref[...], preferred_element_type=F32)
    cos, sin, ones_bd = cos_ref[...], sin_ref[...], ones_ref[...]
    lane = lax.broadcasted_iota(jnp.int32, cos.shape, 1)
    first_half = (lane % 32) < 16
    for s in range(ATTN_WIDTH // LANES):
        r = _head_norm_rope(y[:, s * LANES:(s + 1) * LANES], qg_ref[...], ones_bd, cos, sin, first_half)
        r = (r * Q_SCALE).astype(BF16)
        q_ref[0, 2 * s] = r[:, :HEAD_DIM]
        q_ref[0, 2 * s + 1] = r[:, HEAD_DIM:]
    kr = _head_norm_rope(y[:, ATTN_WIDTH:ATTN_WIDTH + KV_WIDTH], kg_ref[...], ones_bd, cos, sin,
                         first_half).astype(BF16)
    k_ref[0, 0] = kr[:, :HEAD_DIM]
    k_ref[0, 1] = kr[:, HEAD_DIM:]
    vv = y[:, ATTN_WIDTH + KV_WIDTH:ATTN_WIDTH + 2 * KV_WIDTH].astype(BF16)
    v_ref[0, 0] = vv[:, :HEAD_DIM]
    v_ref[0, 1] = vv[:, HEAD_DIM:]
    p_ref[0] = y[:, ATTN_WIDTH + 2 * KV_WIDTH:].astype(BF16)


def _inproj0_ctx_kernel(x_ref, gain_ref, sh_ref, sc_ref, w_ref, kg_ref, ones_ref, k_ref, v_ref):
    a = _modulate(x_ref[0], gain_ref[...], sh_ref[...], sc_ref[...])
    y = jnp.dot(a.astype(BF16), w_ref[...], preferred_element_type=F32)
    z = y[:, :KV_WIDTH]
    kr = (z * lax.rsqrt(_head_mean_square(z, ones_ref[...]) + EPS) * kg_ref[...]).astype(BF16)
    k_ref[0, 0] = kr[:, :HEAD_DIM]
    k_ref[0, 1] = kr[:, HEAD_DIM:]
    vv = y[:, KV_WIDTH:].astype(BF16)
    v_ref[0, 0] = vv[:, :HEAD_DIM]
    v_ref[0, 1] = vv[:, HEAD_DIM:]


def _rope_tables(seq):
    t = jnp.arange(seq)
    row = (t // GRID_W).astype(F32)
    col = (t % GRID_W).astype(F32)
    half = HEAD_DIM // 2
    inv = ROPE_THETA ** (-jnp.arange(0, half, 2, dtype=F32) / half)
    ar, ac = row[:, None] * inv, col[:, None] * inv
    cos = jnp.concatenate([jnp.cos(ar), jnp.cos(ar), jnp.cos(ac), jnp.cos(ac)], axis=-1)
    sin = jnp.concatenate([-jnp.sin(ar), jnp.sin(ar), -jnp.sin(ac), jnp.sin(ac)], axis=-1)
    return jnp.tile(cos, (1, LANES // HEAD_DIM)), jnp.tile(sin, (1, LANES // HEAD_DIM))


def _head_mean_matrix():
    r = jnp.arange(LANES)
    same = (r[:, None] // HEAD_DIM) == (r[None, :] // HEAD_DIM)
    block = jnp.where(same, 1.0 / HEAD_DIM, 0.0).astype(BF16)
    return jnp.concatenate([block, block], axis=0)


def _inproj0(x, mods, gain, w_in, q_gain, k_gain, tm):
    bsz, seq, d = x.shape
    cos, sin = _rope_tables(seq)
    qg = jnp.tile(q_gain, LANES // HEAD_DIM).reshape(1, LANES)
    kg = jnp.tile(k_gain, LANES // HEAD_DIM).reshape(1, LANES)
    head = lambda n: pl.BlockSpec((1, n, tm, HEAD_DIM), lambda b, i: (b, 0, i, 0))
    return pl.pallas_call(
        _inproj0_kernel,
        grid=(bsz, seq // tm),
        in_specs=[pl.BlockSpec((1, tm, d), lambda b, i: (b, i, 0)),
                  _full((1, d)),
                  _mod_spec(0, lambda b, i: b, 0),
                  _mod_spec(0, lambda b, i: b, 1),
                  _full((d, EVEN_IN)),
                  pl.BlockSpec((tm, LANES), lambda b, i: (i, 0)),
                  pl.BlockSpec((tm, LANES), lambda b, i: (i, 0)),
                  _full((1, LANES)), _full((1, LANES)), _full((2 * LANES, LANES))],
        out_specs=[head(N_Q_HEADS), head(N_KV_HEADS), head(N_KV_HEADS),
                   pl.BlockSpec((1, tm, POOL_WIDTH), lambda b, i: (b, i, 0))],
        out_shape=[jax.ShapeDtypeStruct((bsz, N_Q_HEADS, seq, HEAD_DIM), BF16),
                   jax.ShapeDtypeStruct((bsz, N_KV_HEADS, seq, HEAD_DIM), BF16),
                   jax.ShapeDtypeStruct((bsz, N_KV_HEADS, seq, HEAD_DIM), BF16),
                   jax.ShapeDtypeStruct((bsz, seq, POOL_WIDTH), BF16)],
        compiler_params=_params(("parallel", "parallel")),
        name="inproj0",
    )(x, gain.reshape(1, d), mods, mods, w_in.astype(BF16), cos, sin, qg, kg, _head_mean_matrix())


def _inproj0_ctx(ctx, mods, ctx_row, gain, w_kv, k_gain):
    bsz, n_ctx, d = ctx.shape
    kg = jnp.tile(k_gain, LANES // HEAD_DIM).reshape(1, LANES)
    head = pl.BlockSpec((1, N_KV_HEADS, n_ctx, HEAD_DIM), lambda b: (b, 0, 0, 0))
    return pl.pallas_call(
        _inproj0_ctx_kernel,
        grid=(bsz,),
        in_specs=[pl.BlockSpec((1, n_ctx, d), lambda b: (b, 0, 0)),
                  _full((1, d)),
                  _mod_spec(0, lambda b: ctx_row, 0),
                  _mod_spec(0, lambda b: ctx_row, 1),
                  _full((d, 2 * KV_WIDTH)),
                  _full((1, LANES)), _full((2 * LANES, LANES))],
        out_specs=[head, head],
        out_shape=[jax.ShapeDtypeStruct((bsz, N_KV_HEADS, n_ctx, HEAD_DIM), BF16)] * 2,
        compiler_params=_params(("parallel",)),
        name="inproj0_ctx",
    )(ctx, gain.reshape(1, d), mods, mods, w_kv.astype(BF16), kg, _head_mean_matrix())


def _attn_kernel(shift_ref, q_ref, kl_ref, vl_ref, kc_ref, vc_ref, o_ref, *, tq, tk):
    rows = Q_PER_KV * tq
    q = q_ref[0].reshape(rows, HEAD_DIM)
    seq = kl_ref.shape[2]
    chunks = [(kc_ref, vc_ref, 0, kc_ref.shape[2])]
    chunks += [(kl_ref, vl_ref, c * tk, tk) for c in range(seq // tk)]
    scores = lambda k: lax.dot_general(q, k, (((1,), (1,)), ((), ())), preferred_element_type=F32)

    bound = shift_ref[0, 0]
    safe = bound <= SAFE_SOFTMAX_SHIFT

    def finish(acc, l):
        o = acc / l
        o_ref[0] = jnp.concatenate([o[h * tq:(h + 1) * tq] for h in range(Q_PER_KV)], axis=1).astype(BF16)

    @pl.when(safe)
    def _():
        l = jnp.zeros((rows, 1), F32)
        acc = jnp.zeros((rows, HEAD_DIM), F32)
        for k_ref, v_ref, start, size in chunks:
            p = jnp.exp2(scores(k_ref[0, 0, start:start + size, :]) - bound)
            l = l + jnp.sum(p, axis=1, keepdims=True)
            acc = acc + jnp.dot(p.astype(BF16), v_ref[0, 0, start:start + size, :], preferred_element_type=F32)
        finish(acc, l)

    @pl.when(jnp.logical_not(safe))
    def _():
        m = jnp.full((rows, 1), -jnp.inf, F32)
        l = jnp.zeros((rows, 1), F32)
        acc = jnp.zeros((rows, HEAD_DIM), F32)
        for k_ref, v_ref, start, size in chunks:
            s = scores(k_ref[0, 0, start:start + size, :])
            m_new = jnp.maximum(m, jnp.max(s, axis=1, keepdims=True))
            alpha = jnp.exp2(m - m_new)
            p = jnp.exp2(s - m_new)
            l = alpha * l + jnp.sum(p, axis=1, keepdims=True)
            acc = alpha * acc + jnp.dot(p.astype(BF16), v_ref[0, 0, start:start + size, :],
                                        preferred_element_type=F32)
            m = m_new
        finish(acc, l)


def _score_bound(q_gain, k_gain):
    bound = HEAD_DIM * Q_SCALE * jnp.max(jnp.abs(q_gain)) * jnp.max(jnp.abs(k_gain)) * SCORE_BOUND_MARGIN
    return bound.reshape(1, 1).astype(F32)


def _attention(q, k, v, kc, vc, q_gain, k_gain, tq, tk):
    bsz, _, seq, _ = q.shape
    n_ctx = kc.shape[2]
    kv_spec = lambda n: pl.BlockSpec((1, 1, n, HEAD_DIM), lambda b, g, i: (b, g, 0, 0))
    return pl.pallas_call(
        functools.partial(_attn_kernel, tq=tq, tk=tk),
        grid=(bsz, N_KV_HEADS, seq // tq),
        in_specs=[pl.BlockSpec((1, 1), lambda b, g, i: (0, 0), memory_space=pltpu.SMEM),
                  pl.BlockSpec((1, Q_PER_KV, tq, HEAD_DIM), lambda b, g, i: (b, g, i, 0)),
                  kv_spec(seq), kv_spec(seq), kv_spec(n_ctx), kv_spec(n_ctx)],
        out_specs=pl.BlockSpec((1, tq, Q_PER_KV * HEAD_DIM), lambda b, g, i: (b, i, g)),
        out_shape=jax.ShapeDtypeStruct((bsz, seq, ATTN_WIDTH), BF16),
        compiler_params=_params(("parallel", "parallel", "parallel")),
        name="attention",
    )(_score_bound(q_gain, k_gain), q, k, v, kc, vc)


def _route(logits):
    lane = lax.broadcasted_iota(jnp.int32, logits.shape, 1).astype(F32)
    neg = -jnp.inf
    big = float(ROUTER_LANES)
    first_index = lambda mask: jnp.min(jnp.where(mask, lane, big), axis=1, keepdims=True)
    is_g = lane < N_GROUPS
    gm = jnp.max(jnp.where(is_g, logits, neg), axis=1, keepdims=True)
    gidx = first_index(is_g & (logits == gm))
    gden = jnp.sum(jnp.where(is_g, jnp.exp(logits - gm), 0.0), axis=1, keepdims=True)
    g_p = 1.0 / gden
    first = N_GROUPS + EXPERTS_PER_GROUP * gidx
    sel = (lane >= first) & (lane < first + EXPERTS_PER_GROUP)
    e1 = jnp.max(jnp.where(sel, logits, neg), axis=1, keepdims=True)
    i1 = first_index(sel & (logits == e1))
    rest = sel & (lane != i1)
    e2 = jnp.max(jnp.where(rest, logits, neg), axis=1, keepdims=True)
    i2 = first_index(rest & (logits == e2))
    p2 = jnp.exp(e2 - e1)
    w1 = g_p * (1.0 / (1.0 + p2))
    w2 = g_p * (p2 / (1.0 + p2))
    lo = jnp.minimum(i1, i2) - first
    hi = jnp.maximum(i1, i2) - first
    pair = jnp.where(lo == 0, hi - 1, jnp.where(lo == 1, hi + 1, PAIRS_PER_GROUP - 1.0))
    bucket = PAIRS_PER_GROUP * gidx + pair
    w_lo = jnp.where(i1 < i2, w1, w2)
    w_hi = jnp.where(i1 < i2, w2, w1)
    return jnp.where(lane == INFO_BUCKET, bucket,
                     jnp.where(lane == INFO_W_LO, w_lo, jnp.where(lane == INFO_W_HI, w_hi, 0.0)))


def _bucket_onehot(info, sel):
    brow = lax.dot_general(sel, info.astype(BF16), (((1,), (1,)), ((), ())),
                           preferred_element_type=F32)[0:1]
    bid = lax.broadcasted_iota(jnp.int32, (BUCKET_ROWS, info.shape[0]), 0)
    return (bid == brow.astype(jnp.int32)).astype(F32)


def _bucket_selector():
    return jnp.zeros((8, ROUTER_LANES), F32).at[0, INFO_BUCKET].set(1.0).astype(BF16)


def _tail(y, x_res, gate1, gain2, shift2, scale2, rw_both, rbias, sel, h1_ref, info_ref, cnt_ref):
    h1 = x_res + gate1 * y
    h1_ref[0] = h1
    t = _modulate(h1, gain2, shift2, scale2)
    t_hi = t.astype(BF16)
    t_lo = (t - t_hi.astype(F32)).astype(BF16)
    both = jnp.dot(t_hi, rw_both, preferred_element_type=F32)
    logits = (both[:, :ROUTER_LANES] + both[:, ROUTER_LANES:]
              + jnp.dot(t_lo, rw_both[:, :ROUTER_LANES], preferred_element_type=F32)) + rbias
    info = _route(logits)
    info_ref[0] = info

    @pl.when((pl.program_id(0) == 0) & (pl.program_id(1) == 0))
    def _():
        cnt_ref[...] = jnp.zeros_like(cnt_ref)

    cnt_ref[...] += jnp.sum(_bucket_onehot(info, sel), axis=1, keepdims=True)


def _router_operands(rg_w, rg_b, re_w, re_b):
    d = rg_w.shape[0]
    w = jnp.concatenate([rg_w, re_w, jnp.zeros((d, ROUTER_LANES - N_GROUPS - N_EXPERTS), F32)], axis=1)
    b = jnp.concatenate([rg_b, re_b, jnp.zeros((ROUTER_LANES - N_GROUPS - N_EXPERTS,), F32)])
    w_hi = w.astype(BF16)
    w_lo = (w - w_hi.astype(F32)).astype(BF16)
    return jnp.concatenate([w_hi, w_lo], axis=1), b.reshape(1, ROUTER_LANES)


def _fill_halo(buf, main_ref, prev_ref, next_ref, tm, i, n_tiles, halo=HALO):
    zero = jnp.zeros((), buf.dtype)
    buf[halo:halo + tm] = main_ref[0].astype(buf.dtype)
    buf[0:halo] = jnp.where(i > 0, prev_ref[0].astype(buf.dtype), zero)
    buf[halo + tm:2 * halo + tm] = jnp.where(i < n_tiles - 1, next_ref[0].astype(buf.dtype), zero)


def _halo_specs(tm, seq, width, halo=HALO):
    per = tm // halo
    last = seq // halo - 1
    return [pl.BlockSpec((1, tm, width), lambda b, i: (b, i, 0)),
            pl.BlockSpec((1, halo, width), lambda b, i: (b, jnp.maximum(i * per - 1, 0), 0)),
            pl.BlockSpec((1, halo, width), lambda b, i: (b, jnp.minimum((i + 1) * per, last), 0))]


def _pool_bands():
    r = jnp.arange(POOL_BLOCK)[:, None]
    c = jnp.arange(POOL_BLOCK + 2 * HALO)[None, :]
    return jnp.stack([((c >= r + HALO - w // 2) & (c < r + HALO + w - w // 2)) for w in POOL_WINDOWS]).astype(BF16)


def _out0_kernel(o_ref, p_ref, pprev_ref, pnext_ref, x_ref, g1_ref, gain2_ref, sh2_ref, sc2_ref,
                 poolw_ref, pscale_ref, band_ref, wout_ref, rw_ref, rb_ref, sel_ref,
                 h1_ref, info_ref, cnt_ref, pbuf, *, tm, seq):
    i = pl.program_id(1)
    _fill_halo(pbuf, p_ref, pprev_ref, pnext_ref, tm, i, seq // tm)
    pos = i * tm + lax.broadcasted_iota(jnp.int32, (tm, 1), 0)
    pooled = []
    for g, w in enumerate(POOL_WINDOWS):
        sl = slice(g * POOL_GROUP, (g + 1) * POOL_GROUP)
        acc = jnp.concatenate(
            [jnp.dot(band_ref[g], pbuf[b:b + POOL_BLOCK + 2 * HALO, sl], preferred_element_type=F32)
             for b in range(0, tm, POOL_BLOCK)], axis=0)
        lo = jnp.clip(pos - w // 2, 0, seq)
        hi = jnp.clip(pos + w - w // 2, 0, seq)
        mean = acc * (1.0 / (hi - lo).astype(F32))
        dlt = (mean - p_ref[0, :, sl].astype(F32)).astype(BF16)
        pooled.append((jnp.dot(dlt, poolw_ref[g], preferred_element_type=F32) * pscale_ref[:, sl]).astype(BF16))
    mixed = jnp.concatenate([o_ref[0]] + pooled, axis=1)
    y = jnp.dot(mixed, wout_ref[...], preferred_element_type=F32)
    _tail(y, x_ref[0], g1_ref[...], gain2_ref[...], sh2_ref[...], sc2_ref[...],
          rw_ref[...], rb_ref[...], sel_ref[...], h1_ref, info_ref, cnt_ref)


def _tail_specs(layer, d):
    by_batch = lambda b, i: b
    ins = [_mod_spec(layer, by_batch, 2), _full((1, d)), _mod_spec(layer, by_batch, 3),
           _mod_spec(layer, by_batch, 4)]
    return ins


def _tail_outs(bsz, seq, d, tm):
    specs = [pl.BlockSpec((1, tm, d), lambda b, i: (b, i, 0)),
             pl.BlockSpec((1, tm, ROUTER_LANES), lambda b, i: (b, i, 0)),
             _full((BUCKET_ROWS, LANES))]
    shapes = [jax.ShapeDtypeStruct((bsz, seq, d), F32),
              jax.ShapeDtypeStruct((bsz, seq, ROUTER_LANES), F32),
              jax.ShapeDtypeStruct((BUCKET_ROWS, LANES), F32)]
    return specs, shapes


def _out0(o, p, x, mods, gain2, pool_w, pool_scale, w_out, router, tm):
    bsz, seq, d = x.shape
    rw_both, rb = router
    bands = _pool_bands()
    out_specs, out_shapes = _tail_outs(bsz, seq, d, tm)
    return pl.pallas_call(
        functools.partial(_out0_kernel, tm=tm, seq=seq),
        grid=(bsz, seq // tm),
        in_specs=[pl.BlockSpec((1, tm, ATTN_WIDTH), lambda b, i: (b, i, 0))]
        + _halo_specs(tm, seq, POOL_WIDTH)
        + [pl.BlockSpec((1, tm, d), lambda b, i: (b, i, 0))]
        + _tail_specs(0, d)
        + [_full(pool_w.shape), _full((1, POOL_WIDTH)), _full(bands.shape), _full(w_out.shape),
           _full(rw_both.shape), _full(rb.shape), _full((8, ROUTER_LANES))],
        out_specs=out_specs,
        out_shape=out_shapes,
        scratch_shapes=[pltpu.VMEM((tm + 2 * HALO, POOL_WIDTH), BF16)],
        compiler_params=_params(("arbitrary", "arbitrary")),
        name="out0",
    )(o, p, p, p, x, mods, gain2.reshape(1, d), mods, mods,
      pool_w.astype(BF16), pool_scale.reshape(1, POOL_WIDTH), bands, w_out.astype(BF16), rw_both, rb,
      _bucket_selector())


def _inproj1_kernel(pos0_ref, pos1_ref, pos2_ref, h1_ref, g2_ref, gain_ref, sh_ref, sc_ref, w_ref,
                    sgg_ref, sgw_ref, sgb_ref,
                    ys_ref, h_ref, yc_ref, z_ref, bg_ref, ybuf, sem, *, tm, n_steps):
    x = h1_ref[0] + g2_ref[...] * _fetch_sorted_rows(ys_ref, (pos0_ref, pos1_ref, pos2_ref), ybuf, sem,
                                                     tm, n_steps)
    h_ref[0] = x
    a = _modulate(x, gain_ref[...], sh_ref[...], sc_ref[...])
    y = jnp.dot(a.astype(BF16), w_ref[...], preferred_element_type=F32)
    for g in range(SG_GROUPS):
        sl = slice(g * LANES, (g + 1) * LANES)
        u = y[:, sl]
        vg = y[:, SG_WIDTH + g * LANES:SG_WIDTH + (g + 1) * LANES]
        ms = jnp.mean(vg * vg, axis=-1, keepdims=True)
        vn = (vg * lax.rsqrt(ms + EPS) * sgg_ref[:, sl]).astype(BF16)
        for c in range(tm // SG_CHUNK):
            rows = slice(c * SG_CHUNK, (c + 1) * SG_CHUNK)
            s = jnp.dot(sgw_ref[g], vn[rows], preferred_element_type=F32) + sgb_ref[g]
            yc_ref[0, rows, sl] = (u[rows] * s).astype(BF16)
    hx = y[:, 2 * SG_WIDTH:2 * SG_WIDTH + CONV_WIDTH]
    bg_ref[0] = y[:, 2 * SG_WIDTH + CONV_WIDTH:2 * SG_WIDTH + 2 * CONV_WIDTH].astype(BF16)
    cg = y[:, 2 * SG_WIDTH + 2 * CONV_WIDTH:]
    z_ref[0] = (cg * hx).astype(BF16)


def _pos_spec(tm, nt, n_steps, ahead):
    return pl.BlockSpec((1, 1, tm), lambda b, i: (jnp.minimum(b * nt + i + ahead, n_steps - 1), 0, 0),
                        memory_space=pltpu.SMEM)


def _inproj1(h1, ys, pos, mods, gain, w_in, sg_gain, sg_w, sg_b, tm):
    bsz, seq, d = h1.shape
    nt = seq // tm
    n_steps = bsz * nt
    sgb = jnp.broadcast_to(sg_b[:, :, None], (SG_GROUPS, SG_CHUNK, LANES))
    by_batch = lambda b, i: b
    wide = lambda w, dt: (pl.BlockSpec((1, tm, w), lambda b, i: (b, i, 0)),
                          jax.ShapeDtypeStruct((bsz, seq, w), dt))
    outs = [wide(d, F32), wide(SG_WIDTH, BF16), wide(CONV_WIDTH, BF16), wide(CONV_WIDTH, BF16)]
    return pl.pallas_call(
        functools.partial(_inproj1_kernel, tm=tm, n_steps=n_steps),
        grid=(bsz, nt),
        in_specs=[_pos_spec(tm, nt, n_steps, 0), _pos_spec(tm, nt, n_steps, 1), _pos_spec(tm, nt, n_steps, 2),
                  pl.BlockSpec((1, tm, d), lambda b, i: (b, i, 0)),
                  _mod_spec(0, by_batch, 5),
                  _full((1, d)),
                  _mod_spec(1, by_batch, 0),
                  _mod_spec(1, by_batch, 1),
                  _full((d, ODD_IN)),
                  _full((1, SG_WIDTH)), _full(sg_w.shape), _full(sgb.shape),
                  pl.BlockSpec(memory_space=pl.ANY)],
        out_specs=[s for s, _ in outs],
        out_shape=[s for _, s in outs],
        scratch_shapes=[pltpu.VMEM((3, tm, d), F32), pltpu.SemaphoreType.DMA((3,))],
        compiler_params=_params(("arbitrary", "arbitrary")),
        name="inproj1",
    )(pos, pos, pos, h1, mods, gain.reshape(1, d), mods, mods, w_in.astype(BF16), sg_gain.reshape(1, SG_WIDTH),
      sg_w.astype(BF16), sgb, ys)


def _out1_kernel(yc_ref, z_ref, zprev_ref, znext_ref, bg_ref, x_ref, g1_ref, gain2_ref, sh2_ref, sc2_ref,
                 convw_ref, wout_ref, rw_ref, rb_ref, sel_ref,
                 h1_ref, info_ref, cnt_ref, zbuf, *, tm, seq):
    i = pl.program_id(1)
    _fill_halo(zbuf, z_ref, zprev_ref, znext_ref, tm, i, seq // tm, HALO)
    zc = (zbuf[HALO - 1:HALO - 1 + tm] * convw_ref[0:1, :]
          + zbuf[HALO:HALO + tm] * convw_ref[1:2, :]
          + zbuf[HALO + 1:HALO + 1 + tm] * convw_ref[2:3, :])
    yd = (bg_ref[0].astype(F32) * zc).astype(BF16)
    y = (jnp.dot(yc_ref[0], wout_ref[0:SG_WIDTH, :], preferred_element_type=F32)
         + jnp.dot(yd, wout_ref[SG_WIDTH:, :], preferred_element_type=F32))
    _tail(y, x_ref[0], g1_ref[...], gain2_ref[...], sh2_ref[...], sc2_ref[...],
          rw_ref[...], rb_ref[...], sel_ref[...], h1_ref, info_ref, cnt_ref)


def _out1(yc, z, bg, x, mods, gain2, conv_w, w_out, router, tm):
    bsz, seq, d = x.shape
    rw_both, rb = router
    out_specs, out_shapes = _tail_outs(bsz, seq, d, tm)
    wide = pl.BlockSpec((1, tm, CONV_WIDTH), lambda b, i: (b, i, 0))
    return pl.pallas_call(
        functools.partial(_out1_kernel, tm=tm, seq=seq),
        grid=(bsz, seq // tm),
        in_specs=[wide] + _halo_specs(tm, seq, CONV_WIDTH, HALO) + [wide]
        + [pl.BlockSpec((1, tm, d), lambda b, i: (b, i, 0))]
        + _tail_specs(1, d)
        + [_full((3, CONV_WIDTH)), _full(w_out.shape),
           _full(rw_both.shape), _full(rb.shape), _full((8, ROUTER_LANES))],
        out_specs=out_specs,
        out_shape=out_shapes,
        scratch_shapes=[pltpu.VMEM((tm + 2 * HALO, CONV_WIDTH), F32)],
        compiler_params=_params(("arbitrary", "arbitrary")),
        name="out1",
    )(yc, z, z, z, bg, x, mods, gain2.reshape(1, d), mods, mods,
      conv_w.reshape(3, CONV_WIDTH), w_out.astype(BF16), rw_both, rb, _bucket_selector())


def _plan_pos_kernel(info_ref, sel_ref, cnt_ref, ltri_ref, utri_ref, pos_ref, meta_ref, start_sc, run_sc):
    @pl.when(pl.program_id(0) == 0)
    def _():
        padded = jnp.ceil(cnt_ref[...] * (1.0 / SORT_TILE)) * SORT_TILE
        incl = jnp.dot(ltri_ref[...], padded, precision=lax.Precision.HIGHEST, preferred_element_type=F32)
        start_sc[...] = incl - padded
        run_sc[...] = jnp.zeros_like(run_sc)
        ends = jnp.broadcast_to(incl[:, 0:1], (BUCKET_ROWS, META_LANES))
        bid = lax.broadcasted_iota(jnp.int32, ends.shape, 0)
        tile = lax.broadcasted_iota(jnp.int32, (1, META_LANES), 1)

        def bucket_of(row0):
            done = jnp.where((bid < N_BUCKETS) & (ends <= row0), 1.0, 0.0)
            return jnp.minimum(jnp.sum(done, axis=0, keepdims=True), N_BUCKETS - 1.0).astype(jnp.int32)

        def experts_of(bucket):
            grp = ((bucket >= PAIRS_PER_GROUP).astype(jnp.int32)
                   + (bucket >= 2 * PAIRS_PER_GROUP).astype(jnp.int32)
                   + (bucket >= 3 * PAIRS_PER_GROUP).astype(jnp.int32))
            pair = bucket - PAIRS_PER_GROUP * grp
            lo = (pair >= 3).astype(jnp.int32) + (pair >= 5).astype(jnp.int32)
            hi = jnp.where(pair == 0, 1, jnp.where((pair == 1) | (pair == 3), 2, 3))
            return EXPERTS_PER_GROUP * grp + lo, EXPERTS_PER_GROUP * grp + hi

        row0 = (tile * SORT_TILE).astype(F32)
        tb = bucket_of(row0)
        n_used = (incl[N_BUCKETS - 1:N_BUCKETS, 0:1] * (1.0 / SORT_TILE)).astype(jnp.int32)
        fill = (tile >= n_used - 1) | (tb != bucket_of(row0 + SORT_TILE))
        first = (tile == 0) | (tb != bucket_of(row0 - SORT_TILE))
        own_end = jnp.sum(jnp.where(bid == tb, ends, 0.0), axis=0, keepdims=True)
        nonempty = jnp.broadcast_to(padded[:, 0:1], ends.shape) > 0.0
        ordinal = jnp.sum(jnp.where((bid < tb) & nonempty, 1.0, 0.0), axis=0, keepdims=True)
        rows = [None] * 8
        rows[META_EXPERT_LO], rows[META_EXPERT_HI] = experts_of(tb)
        rows[META_N_USED] = jnp.broadcast_to(n_used, (1, META_LANES))
        rows[META_FILL] = fill.astype(jnp.int32)
        rows[META_FIRST] = first.astype(jnp.int32)
        rows[META_WSLOT] = (ordinal - 2.0 * jnp.floor(ordinal * 0.5)).astype(jnp.int32)
        rows[META_NEXT_LO], rows[META_NEXT_HI] = experts_of(bucket_of(own_end))
        for r, row in enumerate(rows):
            meta_ref[r:r + 1, :] = row

    tm = utri_ref.shape[0]
    base = start_sc[:, 0:1] + run_sc[:, 0:1]
    for k in range(pos_ref.shape[0]):
        oh = _bucket_onehot(info_ref[k * tm:(k + 1) * tm], sel_ref[...])
        before = jnp.dot(oh.astype(BF16), utri_ref[...], preferred_element_type=F32)
        pos_ref[k] = jnp.sum(oh * (before + base), axis=0, keepdims=True).astype(jnp.int32)
        base = base + jnp.sum(oh, axis=1, keepdims=True)
    run_sc[...] = jnp.broadcast_to(base - start_sc[:, 0:1], run_sc.shape)


def _sort_plan(info, cnt, tm):
    n = info.shape[0]
    sel = _bucket_selector()
    sub = 4 if (n // tm) % 4 == 0 else 1
    info_spec = pl.BlockSpec((sub * tm, ROUTER_LANES), lambda i: (i, 0))
    r = jnp.arange(BUCKET_ROWS)
    ltri = (r[:, None] >= r[None, :]).astype(F32)
    t = jnp.arange(tm)
    utri = (t[:, None] < t[None, :]).astype(BF16)
    return pl.pallas_call(
        _plan_pos_kernel,
        grid=(n // (sub * tm),),
        in_specs=[info_spec, _full(sel.shape), _full(cnt.shape), _full(ltri.shape), _full(utri.shape)],
        out_specs=[pl.BlockSpec((sub, 1, tm), lambda i: (i, 0, 0)), _full((8, META_LANES))],
        out_shape=[jax.ShapeDtypeStruct((n // tm, 1, tm), jnp.int32),
                   jax.ShapeDtypeStruct((8, META_LANES), jnp.int32)],
        scratch_shapes=[pltpu.VMEM((BUCKET_ROWS, LANES), F32), pltpu.VMEM((BUCKET_ROWS, LANES), F32)],
        compiler_params=_params(("arbitrary",)),
        name="plan_pos",
    )(info, sel, cnt, ltri, utri)


def _dispatch_kernel(pos_ref, fill_ref, h1_ref, gain2_ref, sh2_ref, sc2_ref, info_ref, xs_ref,
                     rowbuf, zbuf, sem, zsem, *, tm, n_steps, n_tiles):
    step = pl.program_id(0) * pl.num_programs(1) + pl.program_id(1)
    slot = step % 2

    @pl.when(step == 0)
    def _():
        zbuf[...] = jnp.zeros_like(zbuf)
        fill = lambda j: pltpu.make_async_copy(zbuf, xs_ref.at[pl.ds(j * SORT_TILE, SORT_TILE), 0], zsem)
        for j in range(n_tiles):
            pl.when(fill_ref[0, j] == 1)(lambda j=j: fill(j).start())
        for j in range(n_tiles):
            pl.when(fill_ref[0, j] == 1)(lambda j=j: fill(j).wait())

    def wait(s):
        pltpu.make_async_copy(rowbuf.at[s], xs_ref.at[pl.ds(0, tm), 0], sem.at[s]).wait()

    def send(s):
        @pl.when(step >= 2)
        def _():
            wait(s)

        rowbuf[s, :, 0:D_MODEL] = _modulate(h1_ref[0], gain2_ref[...], sh2_ref[...], sc2_ref[...])
        rowbuf[s, :, D_MODEL:] = info_ref[0]
        for r in range(tm):
            pltpu.make_async_copy(rowbuf.at[s, pl.ds(r, 1)], xs_ref.at[pos_ref[0, 0, r]],
                                  sem.at[s]).start(priority=r % 2)

    for s in range(2):
        pl.when(slot == s)(functools.partial(send, s))

    @pl.when(step == n_steps - 1)
    def _():
        wait(slot)
        if n_steps > 1:
            wait(1 - slot)


def _gmoe_kernel(ea_ref, eb_ref, nu_ref, first_ref, wslot_ref, na_ref, nb_ref, xs_ref, wg_hbm, wu_hbm, wd_hbm,
                 ys_ref, xbuf, ybuf, zbuf, wg_buf, wu_buf, wd_buf, sem_in, sem_out, zsem, wsem, *, layer):
    j = pl.program_id(0)
    n_used = nu_ref[0]
    slot = j % 2
    wslot = wslot_ref[j]
    tile = lambda ref, t: ref.at[pl.ds(t * SORT_TILE, SORT_TILE), 0]
    in_copy = lambda t, s: pltpu.make_async_copy(tile(xs_ref, t), xbuf.at[s], sem_in.at[s])
    out_copy = lambda t, s: pltpu.make_async_copy(ybuf.at[s], tile(ys_ref, t), sem_out.at[s])

    def weight_copies(e_lo, e_hi, s):
        return [pltpu.make_async_copy(hbm.at[layer, e], buf.at[s, which], wsem.at[s])
                for hbm, buf in ((wg_hbm, wg_buf), (wu_hbm, wu_buf), (wd_hbm, wd_buf))
                for which, e in ((0, e_lo), (1, e_hi))]

    @pl.when(j == 0)
    def _():
        in_copy(0, 0).start()
        for c in weight_copies(ea_ref[0], eb_ref[0], 0):
            c.start()

    @pl.when(j + 1 < n_used)
    def _():
        in_copy(j + 1, 1 - slot).start()

    @pl.when(j < n_used)
    def _():
        @pl.when(first_ref[j] == 1)
        def _():
            for c in weight_copies(ea_ref[j], eb_ref[j], wslot):
                c.wait()
            for c in weight_copies(na_ref[j], nb_ref[j], 1 - wslot):
                c.start()

        in_copy(j, slot).wait()

        @pl.when(j >= 2)
        def _():
            out_copy(j - 2, slot).wait()

        x = xbuf[slot, :, 0:D_MODEL].astype(BF16)

        def expert(which, w):
            gt = jnp.dot(x, wg_buf[wslot, which].astype(BF16), preferred_element_type=F32)
            up = jnp.dot(x, wu_buf[wslot, which].astype(BF16), preferred_element_type=F32)
            h = (gt * jax.nn.sigmoid(gt)) * up * w
            return jnp.dot(h.astype(BF16), wd_buf[wslot, which].astype(BF16), preferred_element_type=F32)

        w_lo = xbuf[slot, :, D_MODEL + INFO_W_LO:D_MODEL + INFO_W_LO + 1]
        w_hi = xbuf[slot, :, D_MODEL + INFO_W_HI:D_MODEL + INFO_W_HI + 1]
        ybuf[slot] = expert(0, w_lo) + expert(1, w_hi)
        out_copy(j, slot).start()

        @pl.when(j == n_used - 1)
        def _():
            out_copy(j, slot).wait()

            @pl.when(j >= 1)
            def _():
                out_copy(j - 1, 1 - slot).wait()

            for c in weight_copies(ea_ref[j], eb_ref[j], 1 - wslot):
                c.wait()

    @pl.when(j >= n_used)
    def _():
        @pl.when(j == n_used)
        def _():
            zbuf[...] = jnp.zeros_like(zbuf)

        fill = pltpu.make_async_copy(zbuf, tile(ys_ref, j), zsem)
        fill.start()
        fill.wait()


def _fetch_sorted_rows(ys_ref, pos_refs, ybuf, sem, tm, n_steps):
    depth = len(pos_refs)
    step = pl.program_id(0) * pl.num_programs(1) + pl.program_id(1)
    slot = step % depth

    def issue(p_ref, s):
        for r in range(tm):
            pltpu.make_async_copy(ys_ref.at[p_ref[0, 0, r]], ybuf.at[s, pl.ds(r, 1)],
                                  sem.at[s]).start(priority=r % 2)

    for a in range(depth - 1):
        pl.when((step == 0) & (a < n_steps))(functools.partial(issue, pos_refs[a], a))
    for s in range(depth):
        pl.when((step + depth - 1 < n_steps) & (slot == s))(
            functools.partial(issue, pos_refs[depth - 1], (s + depth - 1) % depth))
    pltpu.make_async_copy(ys_ref.at[pl.ds(0, tm), 0], ybuf.at[slot], sem.at[slot]).wait()
    return ybuf[slot]


def _combine_kernel(pos_ref, posn_ref, h1_ref, g2_ref, ys_ref, o_ref, ybuf, sem, *, tm, n_steps):
    rows = _fetch_sorted_rows(ys_ref, (pos_ref, posn_ref), ybuf, sem, tm, n_steps)
    o_ref[0] = h1_ref[0] + g2_ref[...] * rows


def _experts_sorted(h1, info, cnt, mods, layer, gain2, w_gate, w_up, w_down, tm):
    bsz, seq, d = h1.shape
    nt = seq // tm
    n_steps = bsz * nt
    n = bsz * seq
    n_sorted = n + N_BUCKETS * SORT_TILE
    n_tiles = n_sorted // SORT_TILE
    assert n % SORT_TILE == 0 and n_tiles <= META_LANES
    pos, meta = _sort_plan(info.reshape(n, ROUTER_LANES), cnt, tm)
    tile = lambda w: pl.BlockSpec((1, tm, w), lambda b, i: (b, i, 0))
    any_spec = pl.BlockSpec(memory_space=pl.ANY)
    by_batch = lambda b, i: b

    xs = pl.pallas_call(
        functools.partial(_dispatch_kernel, tm=tm, n_steps=n_steps, n_tiles=n_tiles),
        grid=(bsz, nt),
        in_specs=[_pos_spec(tm, nt, n_steps, 0),
                  pl.BlockSpec((1, META_LANES), lambda b, i: (0, 0), memory_space=pltpu.SMEM),
                  tile(d), _full((1, d)), _mod_spec(layer, by_batch, 3),
                  _mod_spec(layer, by_batch, 4), tile(ROUTER_LANES)],
        out_specs=any_spec,
        out_shape=jax.ShapeDtypeStruct((n_sorted, 1, ROW_WIDTH), F32),
        scratch_shapes=[pltpu.VMEM((2, tm, ROW_WIDTH), F32), pltpu.VMEM((SORT_TILE, ROW_WIDTH), F32),
                        pltpu.SemaphoreType.DMA((2,)), pltpu.SemaphoreType.DMA(())],
        compiler_params=_params(("arbitrary", "arbitrary")),
        name=f"dispatch{layer}",
    )(pos, meta[META_FILL:META_FILL + 1], h1, gain2.reshape(1, d), mods, mods, info)

    ys = pl.pallas_call(
        functools.partial(_gmoe_kernel, layer=layer),
        grid_spec=pltpu.PrefetchScalarGridSpec(
            num_scalar_prefetch=7,
            grid=(n_tiles,),
            in_specs=[any_spec] * 4,
            out_specs=any_spec,
            scratch_shapes=[pltpu.VMEM((2, SORT_TILE, ROW_WIDTH), F32), pltpu.VMEM((2, SORT_TILE, d), F32),
                            pltpu.VMEM((SORT_TILE, d), F32),
                            pltpu.VMEM((2, 2, d, D_EXPERT), F32), pltpu.VMEM((2, 2, d, D_EXPERT), F32),
                            pltpu.VMEM((2, 2, D_EXPERT, d), F32),
                            pltpu.SemaphoreType.DMA((2,)), pltpu.SemaphoreType.DMA((2,)),
                            pltpu.SemaphoreType.DMA(()), pltpu.SemaphoreType.DMA((2,))]),
        out_shape=jax.ShapeDtypeStruct((n_sorted, 1, d), F32),
        compiler_params=_params(("arbitrary",)),
        name=f"experts{layer}",
    )(meta[META_EXPERT_LO], meta[META_EXPERT_HI], meta[META_N_USED, :1], meta[META_FIRST], meta[META_WSLOT],
      meta[META_NEXT_LO], meta[META_NEXT_HI], xs, w_gate, w_up, w_down)
    return ys, pos


def _combine(h1, ys, pos, mods, layer, tm):
    bsz, seq, d = h1.shape
    nt = seq // tm
    n_steps = bsz * nt
    tile = lambda w: pl.BlockSpec((1, tm, w), lambda b, i: (b, i, 0))
    return pl.pallas_call(
        functools.partial(_combine_kernel, tm=tm, n_steps=n_steps),
        grid=(bsz, nt),
        in_specs=[_pos_spec(tm, nt, n_steps, 0), _pos_spec(tm, nt, n_steps, 1), tile(d),
                  _mod_spec(layer, lambda b, i: b, 5), pl.BlockSpec(memory_space=pl.ANY)],
        out_specs=tile(d),
        out_shape=jax.ShapeDtypeStruct((bsz, seq, d), F32),
        scratch_shapes=[pltpu.VMEM((2, tm, d), F32), pltpu.SemaphoreType.DMA((2,))],
        compiler_params=_params(("arbitrary", "arbitrary")),
        name=f"combine{layer}",
    )(pos, pos, h1, mods, ys)


def kernel(x, c, ctx, c_ctx, mod_w, mod_b, norm1_g, norm2_g, even_w_in, q_gain, k_gain, pool_w, pool_scale,
           even_w_out, odd_w_in, sg_gain, sg_w, sg_b, conv_w, odd_w_out, router_g_w, router_g_b,
           router_e_w, router_e_b, w_gate, w_up, w_down):
    bsz, seq, d = x.shape
    tm = min(512, seq)
    tm_proj = min(1024, seq)
    cond = jnp.zeros((MOD_ROWS, d), F32).at[:bsz].set(c).at[bsz].set(c_ctx)
    mods = _adaln(cond, mod_w, mod_b).reshape(mod_w.shape[0], MOD_ROWS, 6, 1, d)

    q, k, v, p = _inproj0(x, mods, norm1_g[0], even_w_in[0], q_gain[0], k_gain[0], tm_proj)
    kc, vc = _inproj0_ctx(ctx, mods, bsz, norm1_g[0], even_w_in[0][:, ATTN_WIDTH:ATTN_WIDTH + 2 * KV_WIDTH],
                          k_gain[0])
    o = _attention(q, k, v, kc, vc, q_gain[0], k_gain[0], tq=min(512, seq), tk=min(2048, seq))
    router0 = _router_operands(router_g_w[0], router_g_b[0], router_e_w[0], router_e_b[0])
    h1, info, cnt = _out0(o, p, x, mods, norm2_g[0], pool_w[0], pool_scale[0], even_w_out[0], router0, tm_proj)
    ys, pos = _experts_sorted(h1, info, cnt, mods, 0, norm2_g[0], w_gate, w_up, w_down, tm)

    h, yc, z, bg = _inproj1(h1, ys, pos, mods, norm1_g[1], odd_w_in[0], sg_gain[0], sg_w[0], sg_b[0], tm)
    router1 = _router_operands(router_g_w[1], router_g_b[1], router_e_w[1], router_e_b[1])
    h1, info, cnt = _out1(yc, z, bg, h, mods, norm2_g[1], conv_w[0], odd_w_out[0], router1, tm_proj)
    ys, pos = _experts_sorted(h1, info, cnt, mods, 1, norm2_g[1], w_gate, w_up, w_down, tm)
    return _combine(h1, ys, pos, mods, 1, tm)
```

```python
import functools

import jax
import jax.numpy as jnp
from jax import lax
from jax.experimental import pallas as pl
from jax.experimental.pallas import tpu as pltpu

F32 = jnp.float32
BF16 = jnp.bfloat16

D_MODEL = 1024
GRID_W = 64
EPS = 1e-6
N_Q_HEADS = 8
N_KV_HEADS = 2
HEAD_DIM = 64
Q_PER_KV = N_Q_HEADS // N_KV_HEADS
ATTN_WIDTH = N_Q_HEADS * HEAD_DIM
KV_WIDTH = N_KV_HEADS * HEAD_DIM
ROPE_THETA = 10000.0
POOL_WINDOWS = (2, 4, 8, 16)
POOL_GROUP = 128
POOL_WIDTH = POOL_GROUP * len(POOL_WINDOWS)
SG_GROUPS = 4
SG_CHUNK = 128
SG_WIDTH = 512
CONV_WIDTH = 512
EVEN_IN = ATTN_WIDTH + 2 * KV_WIDTH + POOL_WIDTH
ODD_IN = 2 * SG_WIDTH + 3 * CONV_WIDTH
N_GROUPS = 4
EXPERTS_PER_GROUP = 4
N_EXPERTS = 16
D_EXPERT = 256

Q_SCALE = HEAD_DIM ** -0.5 * 1.4426950408889634
SAFE_SOFTMAX_SHIFT = 60.0
SCORE_BOUND_MARGIN = 1.02
LANES = 128
HALO = 16
POOL_BLOCK = 128
ROUTER_LANES = 128
MOD_ROWS = 16
VMEM_LIMIT = 48 * 1024 * 1024

PAIRS_PER_GROUP = 6
N_BUCKETS = N_GROUPS * PAIRS_PER_GROUP
BUCKET_ROWS = 32
SORT_TILE = 512
META_LANES = 256
META_EXPERT_LO, META_EXPERT_HI, META_N_USED, META_FILL, META_FIRST, META_WSLOT, META_NEXT_LO, META_NEXT_HI = range(8)
ROW_WIDTH = D_MODEL + ROUTER_LANES
INFO_BUCKET, INFO_W_LO, INFO_W_HI = 0, 1, 2


def _params(sem):
    return pltpu.CompilerParams(dimension_semantics=sem, vmem_limit_bytes=VMEM_LIMIT)


def _modulate(x, gain, shift, scale):
    ms = jnp.mean(x * x, axis=-1, keepdims=True)
    return (x * lax.rsqrt(ms + EPS) * gain) * (1.0 + scale) + shift


def _mod_spec(layer, row_fn, which):
    return pl.BlockSpec((None, None, None, 1, D_MODEL),
                        lambda *idx: (layer, row_fn(*idx), which, 0, 0))


def _full(shape):
    return pl.BlockSpec(shape, lambda *idx: (0,) * len(shape))


def _adaln_kernel(c_ref, w_ref, b_ref, o_ref):
    c = c_ref[...]
    s = c * jax.nn.sigmoid(c)
    o_ref[0] = jnp.dot(s, w_ref[0], precision=lax.Precision.HIGHEST,
                       preferred_element_type=F32) + b_ref[0]


def _adaln(cond, mod_w, mod_b):
    depth, d, n = mod_w.shape
    tn = 1024
    return pl.pallas_call(
        _adaln_kernel,
        grid=(depth, n // tn),
        in_specs=[_full((MOD_ROWS, d)),
                  pl.BlockSpec((1, d, tn), lambda l, j: (l, 0, j)),
                  pl.BlockSpec((1, 1, tn), lambda l, j: (l, 0, j))],
        out_specs=pl.BlockSpec((1, MOD_ROWS, tn), lambda l, j: (l, 0, j)),
        out_shape=jax.ShapeDtypeStruct((depth, MOD_ROWS, n), F32),
        compiler_params=_params(("arbitrary", "arbitrary")),
        name="adaln",
    )(cond, mod_w, mod_b.reshape(depth, 1, n))


def _head_mean_square(z, ones_bd):
    sq = z * z
    hi = sq.astype(BF16)
    lo = (sq - hi.astype(F32)).astype(BF16)
    return jnp.dot(jnp.concatenate([hi, lo], axis=1), ones_bd, preferred_element_type=F32)


def _head_norm_rope(z, gain, ones_bd, cos, sin, first_half):
    zn = z * lax.rsqrt(_head_mean_square(z, ones_bd) + EPS) * gain
    partner = jnp.where(first_half, pltpu.roll(zn, LANES - 16, 1), pltpu.roll(zn, 16, 1))
    return zn * cos + partner * sin


def _inproj0_kernel(x_ref, gain_ref, sh_ref, sc_ref, w_ref, cos_ref, sin_ref, qg_ref, kg_ref, ones_ref,
                    q_ref, k_ref, v_ref, p_ref):
    a = _modulate(x_ref[0], gain_ref[...], sh_ref[...], sc_ref[...])
    y = jnp.dot(a.astype(BF16), w_ref[...], preferred_element_type=F32)
    cos, sin, ones_bd = cos_ref[...], sin_ref[...], ones_ref[...]
    lane = lax.broadcasted_iota(jnp.int32, cos.shape, 1)
    first_half = (lane % 32) < 16
    for s in range(ATTN_WIDTH // LANES):
        r = _head_norm_rope(y[:, s * LANES:(s + 1) * LANES], qg_ref[...], ones_bd, cos, sin, first_half)
        r = (r * Q_SCALE).astype(BF16)
        q_ref[0, 2 * s] = r[:, :HEAD_DIM]
        q_ref[0, 2 * s + 1] = r[:, HEAD_DIM:]
    kr = _head_norm_rope(y[:, ATTN_WIDTH:ATTN_WIDTH + KV_WIDTH], kg_ref[...], ones_bd, cos, sin,
                         first_half).astype(BF16)
    k_ref[0, 0] = kr[:, :HEAD_DIM]
    k_ref[0, 1] = kr[:, HEAD_DIM:]
    vv = y[:, ATTN_WIDTH + KV_WIDTH:ATTN_WIDTH + 2 * KV_WIDTH].astype(BF16)
    v_ref[0, 0] = vv[:, :HEAD_DIM]
    v_ref[0, 1] = vv[:, HEAD_DIM:]
    p_ref[0] = y[:, ATTN_WIDTH + 2 * KV_WIDTH:].astype(BF16)


def _inproj0_ctx_kernel(x_ref, gain_ref, sh_ref, sc_ref, w_ref, kg_ref, ones_ref, k_ref, v_ref):
    a = _modulate(x_ref[0], gain_ref[...], sh_ref[...], sc_ref[...])
    y = jnp.dot(a.astype(BF16), w_ref[...], preferred_element_type=F32)
    z = y[:, :KV_WIDTH]
    kr = (z * lax.rsqrt(_head_mean_square(z, ones_ref[...]) + EPS) * kg_ref[...]).astype(BF16)
    k_ref[0, 0] = kr[:, :HEAD_DIM]
    k_ref[0, 1] = kr[:, HEAD_DIM:]
    vv = y[:, KV_WIDTH:].astype(BF16)
    v_ref[0, 0] = vv[:, :HEAD_DIM]
    v_ref[0, 1] = vv[:, HEAD_DIM:]


def _rope_tables(seq):
    t = jnp.arange(seq)
    row = (t // GRID_W).astype(F32)
    col = (t % GRID_W).astype(F32)
    half = HEAD_DIM // 2
    inv = ROPE_THETA ** (-jnp.arange(0, half, 2, dtype=F32) / half)
    ar, ac = row[:, None] * inv, col[:, None] * inv
    cos = jnp.concatenate([jnp.cos(ar), jnp.cos(ar), jnp.cos(ac), jnp.cos(ac)], axis=-1)
    sin = jnp.concatenate([-jnp.sin(ar), jnp.sin(ar), -jnp.sin(ac), jnp.sin(ac)], axis=-1)
    return jnp.tile(cos, (1, LANES // HEAD_DIM)), jnp.tile(sin, (1, LANES // HEAD_DIM))


def _head_mean_matrix():
    r = jnp.arange(LANES)
    same = (r[:, None] // HEAD_DIM) == (r[None, :] // HEAD_DIM)
    block = jnp.where(same, 1.0 / HEAD_DIM, 0.0).astype(BF16)
    return jnp.concatenate([block, block], axis=0)


def _inproj0(x, mods, gain, w_in, q_gain, k_gain, tm):
    bsz, seq, d = x.shape
    cos, sin = _rope_tables(seq)
    qg = jnp.tile(q_gain, LANES // HEAD_DIM).reshape(1, LANES)
    kg = jnp.tile(k_gain, LANES // HEAD_DIM).reshape(1, LANES)
    head = lambda n: pl.BlockSpec((1, n, tm, HEAD_DIM), lambda b, i: (b, 0, i, 0))
    return pl.pallas_call(
        _inproj0_kernel,
        grid=(bsz, seq // tm),
        in_specs=[pl.BlockSpec((1, tm, d), lambda b, i: (b, i, 0)),
                  _full((1, d)),
                  _mod_spec(0, lambda b, i: b, 0),
                  _mod_spec(0, lambda b, i: b, 1),
                  _full((d, EVEN_IN)),
                  pl.BlockSpec((tm, LANES), lambda b, i: (i, 0)),
                  pl.BlockSpec((tm, LANES), lambda b, i: (i, 0)),
                  _full((1, LANES)), _full((1, LANES)), _full((2 * LANES, LANES))],
        out_specs=[head(N_Q_HEADS), head(N_KV_HEADS), head(N_KV_HEADS),
                   pl.BlockSpec((1, tm, POOL_WIDTH), lambda b, i: (b, i, 0))],
        out_shape=[jax.ShapeDtypeStruct((bsz, N_Q_HEADS, seq, HEAD_DIM), BF16),
                   jax.ShapeDtypeStruct((bsz, N_KV_HEADS, seq, HEAD_DIM), BF16),
                   jax.ShapeDtypeStruct((bsz, N_KV_HEADS, seq, HEAD_DIM), BF16),
                   jax.ShapeDtypeStruct((bsz, seq, POOL_WIDTH), BF16)],
        compiler_params=_params(("parallel", "parallel")),
        name="inproj0",
    )(x, gain.reshape(1, d), mods, mods, w_in.astype(BF16), cos, sin, qg, kg, _head_mean_matrix())


def _inproj0_ctx(ctx, mods, ctx_row, gain, w_kv, k_gain):
    bsz, n_ctx, d = ctx.shape
    kg = jnp.tile(k_gain, LANES // HEAD_DIM).reshape(1, LANES)
    head = pl.BlockSpec((1, N_KV_HEADS, n_ctx, HEAD_DIM), lambda b: (b, 0, 0, 0))
    return pl.pallas_call(
        _inproj0_ctx_kernel,
        grid=(bsz,),
        in_specs=[pl.BlockSpec((1, n_ctx, d), lambda b: (b, 0, 0)),
                  _full((1, d)),
                  _mod_spec(0, lambda b: ctx_row, 0),
                  _mod_spec(0, lambda b: ctx_row, 1),
                  _full((d, 2 * KV_WIDTH)),
                  _full((1, LANES)), _full((2 * LANES, LANES))],
        out_specs=[head, head],
        out_shape=[jax.ShapeDtypeStruct((bsz, N_KV_HEADS, n_ctx, HEAD_DIM), BF16)] * 2,
        compiler_params=_params(("parallel",)),
        name="inproj0_ctx",
    )(ctx, gain.reshape(1, d), mods, mods, w_kv.astype(BF16), kg, _head_mean_matrix())


def _attn_kernel(shift_ref, q_ref, kl_ref, vl_ref, kc_ref, vc_ref, o_ref, *, tq, tk):
    rows = Q_PER_KV * tq
    q = q_ref[0].reshape(rows, HEAD_DIM)
    seq = kl_ref.shape[2]
    chunks = [(kc_ref, vc_ref, 0, kc_ref.shape[2])]
    chunks += [(kl_ref, vl_ref, c * tk, tk) for c in range(seq // tk)]
    scores = lambda k: lax.dot_general(q, k, (((1,), (1,)), ((), ())), preferred_element_type=F32)

    bound = shift_ref[0, 0]
    safe = bound <= SAFE_SOFTMAX_SHIFT

    def finish(acc, l):
        o = acc / l
        o_ref[0] = jnp.concatenate([o[h * tq:(h + 1) * tq] for h in range(Q_PER_KV)], axis=1).astype(BF16)

    @pl.when(safe)
    def _():
        l = jnp.zeros((rows, 1), F32)
        acc = jnp.zeros((rows, HEAD_DIM), F32)
        for k_ref, v_ref, start, size in chunks:
            p = jnp.exp2(scores(k_ref[0, 0, start:start + size, :]) - bound)
            l = l + jnp.sum(p, axis=1, keepdims=True)
            acc = acc + jnp.dot(p.astype(BF16), v_ref[0, 0, start:start + size, :], preferred_element_type=F32)
        finish(acc, l)

    @pl.when(jnp.logical_not(safe))
    def _():
        m = jnp.full((rows, 1), -jnp.inf, F32)
        l = jnp.zeros((rows, 1), F32)
        acc = jnp.zeros((rows, HEAD_DIM), F32)
        for k_ref, v_ref, start, size in chunks:
            s = scores(k_ref[0, 0, start:start + size, :])
            m_new = jnp.maximum(m, jnp.max(s, axis=1, keepdims=True))
            alpha = jnp.exp2(m - m_new)
            p = jnp.exp2(s - m_new)
            l = alpha * l + jnp.sum(p, axis=1, keepdims=True)
            acc = alpha * acc + jnp.dot(p.astype(BF16), v_ref[0, 0, start:start + size, :],
                                        preferred_element_type=F32)
            m = m_new
        finish(acc, l)


def _score_bound(q_gain, k_gain):
    bound = HEAD_DIM * Q_SCALE * jnp.max(jnp.abs(q_gain)) * jnp.max(jnp.abs(k_gain)) * SCORE_BOUND_MARGIN
    return bound.reshape(1, 1).astype(F32)


def _attention(q, k, v, kc, vc, q_gain, k_gain, tq, tk):
    bsz, _, seq, _ = q.shape
    n_ctx = kc.shape[2]
    kv_spec = lambda n: pl.BlockSpec((1, 1, n, HEAD_DIM), lambda b, g, i: (b, g, 0, 0))
    return pl.pallas_call(
        functools.partial(_attn_kernel, tq=tq, tk=tk),
        grid=(bsz, N_KV_HEADS, seq // tq),
        in_specs=[pl.BlockSpec((1, 1), lambda b, g, i: (0, 0), memory_space=pltpu.SMEM),
                  pl.BlockSpec((1, Q_PER_KV, tq, HEAD_DIM), lambda b, g, i: (b, g, i, 0)),
                  kv_spec(seq), kv_spec(seq), kv_spec(n_ctx), kv_spec(n_ctx)],
        out_specs=pl.BlockSpec((1, tq, Q_PER_KV * HEAD_DIM), lambda b, g, i: (b, i, g)),
        out_shape=jax.ShapeDtypeStruct((bsz, seq, ATTN_WIDTH), BF16),
        compiler_params=_params(("parallel", "parallel", "parallel")),
        name="attention",
    )(_score_bound(q_gain, k_gain), q, k, v, kc, vc)


def _route(logits):
    lane = lax.broadcasted_iota(jnp.int32, logits.shape, 1).astype(F32)
    neg = -jnp.inf
    big = float(ROUTER_LANES)
    first_index = lambda mask: jnp.min(jnp.where(mask, lane, big), axis=1, keepdims=True)
    is_g = lane < N_GROUPS
    gm = jnp.max(jnp.where(is_g, logits, neg), axis=1, keepdims=True)
    gidx = first_index(is_g & (logits == gm))
    gden = jnp.sum(jnp.where(is_g, jnp.exp(logits - gm), 0.0), axis=1, keepdims=True)
    g_p = 1.0 / gden
    first = N_GROUPS + EXPERTS_PER_GROUP * gidx
    sel = (lane >= first) & (lane < first + EXPERTS_PER_GROUP)
    e1 = jnp.max(jnp.where(sel, logits, neg), axis=1, keepdims=True)
    i1 = first_index(sel & (logits == e1))
    rest = sel & (lane != i1)
    e2 = jnp.max(jnp.where(rest, logits, neg), axis=1, keepdims=True)
    i2 = first_index(rest & (logits == e2))
    p2 = jnp.exp(e2 - e1)
    w1 = g_p * (1.0 / (1.0 + p2))
    w2 = g_p * (p2 / (1.0 + p2))
    lo = jnp.minimum(i1, i2) - first
    hi = jnp.maximum(i1, i2) - first
    pair = jnp.where(lo == 0, hi - 1, jnp.where(lo == 1, hi + 1, PAIRS_PER_GROUP - 1.0))
    bucket = PAIRS_PER_GROUP * gidx + pair
    w_lo = jnp.where(i1 < i2, w1, w2)
    w_hi = jnp.where(i1 < i2, w2, w1)
    return jnp.where(lane == INFO_BUCKET, bucket,
                     jnp.where(lane == INFO_W_LO, w_lo, jnp.where(lane == INFO_W_HI, w_hi, 0.0)))


def _bucket_onehot(info, sel):
    brow = lax.dot_general(sel, info.astype(BF16), (((1,), (1,)), ((), ())),
                           preferred_element_type=F32)[0:1]
    bid = lax.broadcasted_iota(jnp.int32, (BUCKET_ROWS, info.shape[0]), 0)
    return (bid == brow.astype(jnp.int32)).astype(F32)


def _bucket_selector():
    return jnp.zeros((8, ROUTER_LANES), F32).at[0, INFO_BUCKET].set(1.0).astype(BF16)


def _tail(y, x_res, gate1, gain2, shift2, scale2, rw_both, rbias, sel, h1_ref, info_ref, cnt_ref):
    h1 = x_res + gate1 * y
    h1_ref[0] = h1
    t = _modulate(h1, gain2, shift2, scale2)
    t_hi = t.astype(BF16)
    t_lo = (t - t_hi.astype(F32)).astype(BF16)
    both = jnp.dot(t_hi, rw_both, preferred_element_type=F32)
    logits = (both[:, :ROUTER_LANES] + both[:, ROUTER_LANES:]
              + jnp.dot(t_lo, rw_both[:, :ROUTER_LANES], preferred_element_type=F32)) + rbias
    info = _route(logits)
    info_ref[0] = info

    @pl.when((pl.program_id(0) == 0) & (pl.program_id(1) == 0))
    def _():
        cnt_ref[...] = jnp.zeros_like(cnt_ref)

    cnt_ref[...] += jnp.sum(_bucket_onehot(info, sel), axis=1, keepdims=True)


def _router_operands(rg_w, rg_b, re_w, re_b):
    d = rg_w.shape[0]
    w = jnp.concatenate([rg_w, re_w, jnp.zeros((d, ROUTER_LANES - N_GROUPS - N_EXPERTS), F32)], axis=1)
    b = jnp.concatenate([rg_b, re_b, jnp.zeros((ROUTER_LANES - N_GROUPS - N_EXPERTS,), F32)])
    w_hi = w.astype(BF16)
    w_lo = (w - w_hi.astype(F32)).astype(BF16)
    return jnp.concatenate([w_hi, w_lo], axis=1), b.reshape(1, ROUTER_LANES)


def _fill_halo(buf, main_ref, prev_ref, next_ref, tm, i, n_tiles, halo=HALO):
    zero = jnp.zeros((), buf.dtype)
    buf[halo:halo + tm] = main_ref[0].astype(buf.dtype)
    buf[0:halo] = jnp.where(i > 0, prev_ref[0].astype(buf.dtype), zero)
    buf[halo + tm:2 * halo + tm] = jnp.where(i < n_tiles - 1, next_ref[0].astype(buf.dtype), zero)


def _halo_specs(tm, seq, width, halo=HALO):
    per = tm // halo
    last = seq // halo - 1
    return [pl.BlockSpec((1, tm, width), lambda b, i: (b, i, 0)),
            pl.BlockSpec((1, halo, width), lambda b, i: (b, jnp.maximum(i * per - 1, 0), 0)),
            pl.BlockSpec((1, halo, width), lambda b, i: (b, jnp.minimum((i + 1) * per, last), 0))]


def _pool_bands():
    r = jnp.arange(POOL_BLOCK)[:, None]
    c = jnp.arange(POOL_BLOCK + 2 * HALO)[None, :]
    return jnp.stack([((c >= r + HALO - w // 2) & (c < r + HALO + w - w // 2)) for w in POOL_WINDOWS]).astype(BF16)


def _out0_kernel(o_ref, p_ref, pprev_ref, pnext_ref, x_ref, g1_ref, gain2_ref, sh2_ref, sc2_ref,
                 poolw_ref, pscale_ref, band_ref, wout_ref, rw_ref, rb_ref, sel_ref,
                 h1_ref, info_ref, cnt_ref, pbuf, *, tm, seq):
    i = pl.program_id(1)
    _fill_halo(pbuf, p_ref, pprev_ref, pnext_ref, tm, i, seq // tm)
    pos = i * tm + lax.broadcasted_iota(jnp.int32, (tm, 1), 0)
    pooled = []
    for g, w in enumerate(POOL_WINDOWS):
        sl = slice(g * POOL_GROUP, (g + 1) * POOL_GROUP)
        acc = jnp.concatenate(
            [jnp.dot(band_ref[g], pbuf[b:b + POOL_BLOCK + 2 * HALO, sl], preferred_element_type=F32)
             for b in range(0, tm, POOL_BLOCK)], axis=0)
        lo = jnp.clip(pos - w // 2, 0, seq)
        hi = jnp.clip(pos + w - w // 2, 0, seq)
        mean = acc * (1.0 / (hi - lo).astype(F32))
        dlt = (mean - p_ref[0, :, sl].astype(F32)).astype(BF16)
        pooled.append((jnp.dot(dlt, poolw_ref[g], preferred_element_type=F32) * pscale_ref[:, sl]).astype(BF16))
    mixed = jnp.concatenate([o_ref[0]] + pooled, axis=1)
    y = jnp.dot(mixed, wout_ref[...], preferred_element_type=F32)
    _tail(y, x_ref[0], g1_ref[...], gain2_ref[...], sh2_ref[...], sc2_ref[...],
          rw_ref[...], rb_ref[...], sel_ref[...], h1_ref, info_ref, cnt_ref)


def _tail_specs(layer, d):
    by_batch = lambda b, i: b
    ins = [_mod_spec(layer, by_batch, 2), _full((1, d)), _mod_spec(layer, by_batch, 3),
           _mod_spec(layer, by_batch, 4)]
    return ins


def _tail_outs(bsz, seq, d, tm):
    specs = [pl.BlockSpec((1, tm, d), lambda b, i: (b, i, 0)),
             pl.BlockSpec((1, tm, ROUTER_LANES), lambda b, i: (b, i, 0)),
             _full((BUCKET_ROWS, LANES))]
    shapes = [jax.ShapeDtypeStruct((bsz, seq, d), F32),
              jax.ShapeDtypeStruct((bsz, seq, ROUTER_LANES), F32),
              jax.ShapeDtypeStruct((BUCKET_ROWS, LANES), F32)]
    return specs, shapes


def _out0(o, p, x, mods, gain2, pool_w, pool_scale, w_out, router, tm):
    bsz, seq, d = x.shape
    rw_both, rb = router
    bands = _pool_bands()
    out_specs, out_shapes = _tail_outs(bsz, seq, d, tm)
    return pl.pallas_call(
        functools.partial(_out0_kernel, tm=tm, seq=seq),
        grid=(bsz, seq // tm),
        in_specs=[pl.BlockSpec((1, tm, ATTN_WIDTH), lambda b, i: (b, i, 0))]
        + _halo_specs(tm, seq, POOL_WIDTH)
        + [pl.BlockSpec((1, tm, d), lambda b, i: (b, i, 0))]
        + _tail_specs(0, d)
        + [_full(pool_w.shape), _full((1, POOL_WIDTH)), _full(bands.shape), _full(w_out.shape),
           _full(rw_both.shape), _full(rb.shape), _full((8, ROUTER_LANES))],
        out_specs=out_specs,
        out_shape=out_shapes,
        scratch_shapes=[pltpu.VMEM((tm + 2 * HALO, POOL_WIDTH), BF16)],
        compiler_params=_params(("arbitrary", "arbitrary")),
        name="out0",
    )(o, p, p, p, x, mods, gain2.reshape(1, d), mods, mods,
      pool_w.astype(BF16), pool_scale.reshape(1, POOL_WIDTH), bands, w_out.astype(BF16), rw_both, rb,
      _bucket_selector())


def _inproj1_kernel(pos_ref, posn_ref, h1_ref, g2_ref, gain_ref, sh_ref, sc_ref, w_ref, sgg_ref, sgw_ref, sgb_ref,
                    ys_ref, h_ref, yc_ref, z_ref, bg_ref, ybuf, sem, *, tm, n_steps):
    x = h1_ref[0] + g2_ref[...] * _fetch_sorted_rows(ys_ref, pos_ref, posn_ref, ybuf, sem, tm, n_steps)
    h_ref[0] = x
    a = _modulate(x, gain_ref[...], sh_ref[...], sc_ref[...])
    y = jnp.dot(a.astype(BF16), w_ref[...], preferred_element_type=F32)
    for g in range(SG_GROUPS):
        sl = slice(g * LANES, (g + 1) * LANES)
        u = y[:, sl]
        vg = y[:, SG_WIDTH + g * LANES:SG_WIDTH + (g + 1) * LANES]
        ms = jnp.mean(vg * vg, axis=-1, keepdims=True)
        vn = (vg * lax.rsqrt(ms + EPS) * sgg_ref[:, sl]).astype(BF16)
        for c in range(tm // SG_CHUNK):
            rows = slice(c * SG_CHUNK, (c + 1) * SG_CHUNK)
            s = jnp.dot(sgw_ref[g], vn[rows], preferred_element_type=F32) + sgb_ref[g]
            yc_ref[0, rows, sl] = (u[rows] * s).astype(BF16)
    hx = y[:, 2 * SG_WIDTH:2 * SG_WIDTH + CONV_WIDTH]
    bg_ref[0] = y[:, 2 * SG_WIDTH + CONV_WIDTH:2 * SG_WIDTH + 2 * CONV_WIDTH].astype(BF16)
    cg = y[:, 2 * SG_WIDTH + 2 * CONV_WIDTH:]
    z_ref[0] = (cg * hx).astype(BF16)


def _pos_spec(tm, nt, n_steps, ahead):
    return pl.BlockSpec((1, 1, tm), lambda b, i: (jnp.minimum(b * nt + i + ahead, n_steps - 1), 0, 0),
                        memory_space=pltpu.SMEM)


def _inproj1(h1, ys, pos, mods, gain, w_in, sg_gain, sg_w, sg_b, tm):
    bsz, seq, d = h1.shape
    nt = seq // tm
    n_steps = bsz * nt
    sgb = jnp.broadcast_to(sg_b[:, :, None], (SG_GROUPS, SG_CHUNK, LANES))
    by_batch = lambda b, i: b
    wide = lambda w, dt: (pl.BlockSpec((1, tm, w), lambda b, i: (b, i, 0)),
                          jax.ShapeDtypeStruct((bsz, seq, w), dt))
    outs = [wide(d, F32), wide(SG_WIDTH, BF16), wide(CONV_WIDTH, BF16), wide(CONV_WIDTH, BF16)]
    return pl.pallas_call(
        functools.partial(_inproj1_kernel, tm=tm, n_steps=n_steps),
        grid=(bsz, nt),
        in_specs=[_pos_spec(tm, nt, n_steps, 0), _pos_spec(tm, nt, n_steps, 1),
                  pl.BlockSpec((1, tm, d), lambda b, i: (b, i, 0)),
                  _mod_spec(0, by_batch, 5),
                  _full((1, d)),
                  _mod_spec(1, by_batch, 0),
                  _mod_spec(1, by_batch, 1),
                  _full((d, ODD_IN)),
                  _full((1, SG_WIDTH)), _full(sg_w.shape), _full(sgb.shape),
                  pl.BlockSpec(memory_space=pl.ANY)],
        out_specs=[s for s, _ in outs],
        out_shape=[s for _, s in outs],
        scratch_shapes=[pltpu.VMEM((2, tm, d), F32), pltpu.SemaphoreType.DMA((2,))],
        compiler_params=_params(("arbitrary", "arbitrary")),
        name="inproj1",
    )(pos, pos, h1, mods, gain.reshape(1, d), mods, mods, w_in.astype(BF16), sg_gain.reshape(1, SG_WIDTH),
      sg_w.astype(BF16), sgb, ys)


def _out1_kernel(yc_ref, z_ref, zprev_ref, znext_ref, bg_ref, x_ref, g1_ref, gain2_ref, sh2_ref, sc2_ref,
                 convw_ref, wout_ref, rw_ref, rb_ref, sel_ref,
                 h1_ref, info_ref, cnt_ref, zbuf, *, tm, seq):
    i = pl.program_id(1)
    _fill_halo(zbuf, z_ref, zprev_ref, znext_ref, tm, i, seq // tm, HALO)
    zc = (zbuf[HALO - 1:HALO - 1 + tm] * convw_ref[0:1, :]
          + zbuf[HALO:HALO + tm] * convw_ref[1:2, :]
          + zbuf[HALO + 1:HALO + 1 + tm] * convw_ref[2:3, :])
    yd = (bg_ref[0].astype(F32) * zc).astype(BF16)
    y = (jnp.dot(yc_ref[0], wout_ref[0:SG_WIDTH, :], preferred_element_type=F32)
         + jnp.dot(yd, wout_ref[SG_WIDTH:, :], preferred_element_type=F32))
    _tail(y, x_ref[0], g1_ref[...], gain2_ref[...], sh2_ref[...], sc2_ref[...],
          rw_ref[...], rb_ref[...], sel_ref[...], h1_ref, info_ref, cnt_ref)


def _out1(yc, z, bg, x, mods, gain2, conv_w, w_out, router, tm):
    bsz, seq, d = x.shape
    rw_both, rb = router
    out_specs, out_shapes = _tail_outs(bsz, seq, d, tm)
    wide = pl.BlockSpec((1, tm, CONV_WIDTH), lambda b, i: (b, i, 0))
    return pl.pallas_call(
        functools.partial(_out1_kernel, tm=tm, seq=seq),
        grid=(bsz, seq // tm),
        in_specs=[wide] + _halo_specs(tm, seq, CONV_WIDTH, HALO) + [wide]
        + [pl.BlockSpec((1, tm, d), lambda b, i: (b, i, 0))]
        + _tail_specs(1, d)
        + [_full((3, CONV_WIDTH)), _full(w_out.shape),
           _full(rw_both.shape), _full(rb.shape), _full((8, ROUTER_LANES))],
        out_specs=out_specs,
        out_shape=out_shapes,
        scratch_shapes=[pltpu.VMEM((tm + 2 * HALO, CONV_WIDTH), F32)],
        compiler_params=_params(("arbitrary", "arbitrary")),
        name="out1",
    )(yc, z, z, z, bg, x, mods, gain2.reshape(1, d), mods, mods,
      conv_w.reshape(3, CONV_WIDTH), w_out.astype(BF16), rw_both, rb, _bucket_selector())


def _plan_pos_kernel(info_ref, sel_ref, cnt_ref, ltri_ref, utri_ref, pos_ref, meta_ref, start_sc, run_sc):
    @pl.when(pl.program_id(0) == 0)
    def _():
        padded = jnp.ceil(cnt_ref[...] * (1.0 / SORT_TILE)) * SORT_TILE
        incl = jnp.dot(ltri_ref[...], padded, precision=lax.Precision.HIGHEST, preferred_element_type=F32)
        start_sc[...] = incl - padded
        run_sc[...] = jnp.zeros_like(run_sc)
        ends = jnp.broadcast_to(incl[:, 0:1], (BUCKET_ROWS, META_LANES))
        bid = lax.broadcasted_iota(jnp.int32, ends.shape, 0)
        tile = lax.broadcasted_iota(jnp.int32, (1, META_LANES), 1)

        def bucket_of(row0):
            done = jnp.where((bid < N_BUCKETS) & (ends <= row0), 1.0, 0.0)
            return jnp.minimum(jnp.sum(done, axis=0, keepdims=True), N_BUCKETS - 1.0).astype(jnp.int32)

        def experts_of(bucket):
            grp = ((bucket >= PAIRS_PER_GROUP).astype(jnp.int32)
                   + (bucket >= 2 * PAIRS_PER_GROUP).astype(jnp.int32)
                   + (bucket >= 3 * PAIRS_PER_GROUP).astype(jnp.int32))
            pair = bucket - PAIRS_PER_GROUP * grp
            lo = (pair >= 3).astype(jnp.int32) + (pair >= 5).astype(jnp.int32)
            hi = jnp.where(pair == 0, 1, jnp.where((pair == 1) | (pair == 3), 2, 3))
            return EXPERTS_PER_GROUP * grp + lo, EXPERTS_PER_GROUP * grp + hi

        row0 = (tile * SORT_TILE).astype(F32)
        tb = bucket_of(row0)
        n_used = (incl[N_BUCKETS - 1:N_BUCKETS, 0:1] * (1.0 / SORT_TILE)).astype(jnp.int32)
        fill = (tile >= n_used - 1) | (tb != bucket_of(row0 + SORT_TILE))
        first = (tile == 0) | (tb != bucket_of(row0 - SORT_TILE))
        own_end = jnp.sum(jnp.where(bid == tb, ends, 0.0), axis=0, keepdims=True)
        nonempty = jnp.broadcast_to(padded[:, 0:1], ends.shape) > 0.0
        ordinal = jnp.sum(jnp.where((bid < tb) & nonempty, 1.0, 0.0), axis=0, keepdims=True)
        rows = [None] * 8
        rows[META_EXPERT_LO], rows[META_EXPERT_HI] = experts_of(tb)
        rows[META_N_USED] = jnp.broadcast_to(n_used, (1, META_LANES))
        rows[META_FILL] = fill.astype(jnp.int32)
        rows[META_FIRST] = first.astype(jnp.int32)
        rows[META_WSLOT] = (ordinal - 2.0 * jnp.floor(ordinal * 0.5)).astype(jnp.int32)
        rows[META_NEXT_LO], rows[META_NEXT_HI] = experts_of(bucket_of(own_end))
        for r, row in enumerate(rows):
            meta_ref[r:r + 1, :] = row

    tm = utri_ref.shape[0]
    base = start_sc[:, 0:1] + run_sc[:, 0:1]
    for k in range(pos_ref.shape[0]):
        oh = _bucket_onehot(info_ref[k * tm:(k + 1) * tm], sel_ref[...])
        before = jnp.dot(oh.astype(BF16), utri_ref[...], preferred_element_type=F32)
        pos_ref[k] = jnp.sum(oh * (before + base), axis=0, keepdims=True).astype(jnp.int32)
        base = base + jnp.sum(oh, axis=1, keepdims=True)
    run_sc[...] = jnp.broadcast_to(base - start_sc[:, 0:1], run_sc.shape)


def _sort_plan(info, cnt, tm):
    n = info.shape[0]
    sel = _bucket_selector()
    sub = 4 if (n // tm) % 4 == 0 else 1
    info_spec = pl.BlockSpec((sub * tm, ROUTER_LANES), lambda i: (i, 0))
    r = jnp.arange(BUCKET_ROWS)
    ltri = (r[:, None] >= r[None, :]).astype(F32)
    t = jnp.arange(tm)
    utri = (t[:, None] < t[None, :]).astype(BF16)
    return pl.pallas_call(
        _plan_pos_kernel,
        grid=(n // (sub * tm),),
        in_specs=[info_spec, _full(sel.shape), _full(cnt.shape), _full(ltri.shape), _full(utri.shape)],
        out_specs=[pl.BlockSpec((sub, 1, tm), lambda i: (i, 0, 0)), _full((8, META_LANES))],
        out_shape=[jax.ShapeDtypeStruct((n // tm, 1, tm), jnp.int32),
                   jax.ShapeDtypeStruct((8, META_LANES), jnp.int32)],
        scratch_shapes=[pltpu.VMEM((BUCKET_ROWS, LANES), F32), pltpu.VMEM((BUCKET_ROWS, LANES), F32)],
        compiler_params=_params(("arbitrary",)),
        name="plan_pos",
    )(info, sel, cnt, ltri, utri)


def _dispatch_kernel(pos_ref, fill_ref, h1_ref, gain2_ref, sh2_ref, sc2_ref, info_ref, xs_ref,
                     rowbuf, zbuf, sem, zsem, *, tm, n_steps, n_tiles):
    step = pl.program_id(0) * pl.num_programs(1) + pl.program_id(1)
    slot = step % 2

    @pl.when(step == 0)
    def _():
        zbuf[...] = jnp.zeros_like(zbuf)
        fill = lambda j: pltpu.make_async_copy(zbuf, xs_ref.at[pl.ds(j * SORT_TILE, SORT_TILE), 0], zsem)
        for j in range(n_tiles):
            pl.when(fill_ref[0, j] == 1)(lambda j=j: fill(j).start())
        for j in range(n_tiles):
            pl.when(fill_ref[0, j] == 1)(lambda j=j: fill(j).wait())

    def wait(s):
        pltpu.make_async_copy(rowbuf.at[s], xs_ref.at[pl.ds(0, tm), 0], sem.at[s]).wait()

    def send(s):
        @pl.when(step >= 2)
        def _():
            wait(s)

        rowbuf[s, :, 0:D_MODEL] = _modulate(h1_ref[0], gain2_ref[...], sh2_ref[...], sc2_ref[...])
        rowbuf[s, :, D_MODEL:] = info_ref[0]
        for r in range(tm):
            pltpu.make_async_copy(rowbuf.at[s, pl.ds(r, 1)], xs_ref.at[pos_ref[0, 0, r]],
                                  sem.at[s]).start(priority=r % 2)

    for s in range(2):
        pl.when(slot == s)(functools.partial(send, s))

    @pl.when(step == n_steps - 1)
    def _():
        wait(slot)
        if n_steps > 1:
            wait(1 - slot)


def _gmoe_kernel(ea_ref, eb_ref, nu_ref, first_ref, wslot_ref, na_ref, nb_ref, xs_ref, wg_hbm, wu_hbm, wd_hbm,
                 ys_ref, xbuf, ybuf, zbuf, wg_buf, wu_buf, wd_buf, sem_in, sem_out, zsem, wsem, *, layer):
    j = pl.program_id(0)
    n_used = nu_ref[0]
    slot = j % 2
    wslot = wslot_ref[j]
    tile = lambda ref, t: ref.at[pl.ds(t * SORT_TILE, SORT_TILE), 0]
    in_copy = lambda t, s: pltpu.make_async_copy(tile(xs_ref, t), xbuf.at[s], sem_in.at[s])
    out_copy = lambda t, s: pltpu.make_async_copy(ybuf.at[s], tile(ys_ref, t), sem_out.at[s])

    def weight_copies(e_lo, e_hi, s):
        return [pltpu.make_async_copy(hbm.at[layer, e], buf.at[s, which], wsem.at[s])
                for hbm, buf in ((wg_hbm, wg_buf), (wu_hbm, wu_buf), (wd_hbm, wd_buf))
                for which, e in ((0, e_lo), (1, e_hi))]

    @pl.when(j == 0)
    def _():
        in_copy(0, 0).start()
        for c in weight_copies(ea_ref[0], eb_ref[0], 0):
            c.start()

    @pl.when(j + 1 < n_used)
    def _():
        in_copy(j + 1, 1 - slot).start()

    @pl.when(j < n_used)
    def _():
        @pl.when(first_ref[j] == 1)
        def _():
            for c in weight_copies(ea_ref[j], eb_ref[j], wslot):
                c.wait()
            for c in weight_copies(na_ref[j], nb_ref[j], 1 - wslot):
                c.start()

        in_copy(j, slot).wait()

        @pl.when(j >= 2)
        def _():
            out_copy(j - 2, slot).wait()

        x = xbuf[slot, :, 0:D_MODEL].astype(BF16)

        def expert(which, w):
            gt = jnp.dot(x, wg_buf[wslot, which].astype(BF16), preferred_element_type=F32)
            up = jnp.dot(x, wu_buf[wslot, which].astype(BF16), preferred_element_type=F32)
            h = (gt * jax.nn.sigmoid(gt)) * up * w
            return jnp.dot(h.astype(BF16), wd_buf[wslot, which].astype(BF16), preferred_element_type=F32)

        w_lo = xbuf[slot, :, D_MODEL + INFO_W_LO:D_MODEL + INFO_W_LO + 1]
        w_hi = xbuf[slot, :, D_MODEL + INFO_W_HI:D_MODEL + INFO_W_HI + 1]
        ybuf[slot] = expert(0, w_lo) + expert(1, w_hi)
        out_copy(j, slot).start()

        @pl.when(j == n_used - 1)
        def _():
            out_copy(j, slot).wait()

            @pl.when(j >= 1)
            def _():
                out_copy(j - 1, 1 - slot).wait()

            for c in weight_copies(ea_ref[j], eb_ref[j], 1 - wslot):
                c.wait()

    @pl.when(j >= n_used)
    def _():
        @pl.when(j == n_used)
        def _():
            zbuf[...] = jnp.zeros_like(zbuf)

        fill = pltpu.make_async_copy(zbuf, tile(ys_ref, j), zsem)
        fill.start()
        fill.wait()


def _fetch_sorted_rows(ys_ref, pos_ref, posn_ref, ybuf, sem, tm, n_steps):
    step = pl.program_id(0) * pl.num_programs(1) + pl.program_id(1)
    slot = step % 2

    def issue(p_ref, s):
        for r in range(tm):
            pltpu.make_async_copy(ys_ref.at[p_ref[0, 0, r]], ybuf.at[s, pl.ds(r, 1)], sem.at[s]).start()

    pl.when(step == 0)(functools.partial(issue, pos_ref, 0))
    for s in range(2):
        pl.when((step + 1 < n_steps) & (slot == s))(functools.partial(issue, posn_ref, 1 - s))
    pltpu.make_async_copy(ys_ref.at[pl.ds(0, tm), 0], ybuf.at[slot], sem.at[slot]).wait()
    return ybuf[slot]


def _combine_kernel(pos_ref, posn_ref, h1_ref, g2_ref, ys_ref, o_ref, ybuf, sem, *, tm, n_steps):
    rows = _fetch_sorted_rows(ys_ref, pos_ref, posn_ref, ybuf, sem, tm, n_steps)
    o_ref[0] = h1_ref[0] + g2_ref[...] * rows


def _experts_sorted(h1, info, cnt, mods, layer, gain2, w_gate, w_up, w_down, tm):
    bsz, seq, d = h1.shape
    nt = seq // tm
    n_steps = bsz * nt
    n = bsz * seq
    n_sorted = n + N_BUCKETS * SORT_TILE
    n_tiles = n_sorted // SORT_TILE
    assert n % SORT_TILE == 0 and n_tiles <= META_LANES
    pos, meta = _sort_plan(info.reshape(n, ROUTER_LANES), cnt, tm)
    tile = lambda w: pl.BlockSpec((1, tm, w), lambda b, i: (b, i, 0))
    any_spec = pl.BlockSpec(memory_space=pl.ANY)
    by_batch = lambda b, i: b

    xs = pl.pallas_call(
        functools.partial(_dispatch_kernel, tm=tm, n_steps=n_steps, n_tiles=n_tiles),
        grid=(bsz, nt),
        in_specs=[_pos_spec(tm, nt, n_steps, 0),
                  pl.BlockSpec((1, META_LANES), lambda b, i: (0, 0), memory_space=pltpu.SMEM),
                  tile(d), _full((1, d)), _mod_spec(layer, by_batch, 3),
                  _mod_spec(layer, by_batch, 4), tile(ROUTER_LANES)],
        out_specs=any_spec,
        out_shape=jax.ShapeDtypeStruct((n_sorted, 1, ROW_WIDTH), F32),
        scratch_shapes=[pltpu.VMEM((2, tm, ROW_WIDTH), F32), pltpu.VMEM((SORT_TILE, ROW_WIDTH), F32),
                        pltpu.SemaphoreType.DMA((2,)), pltpu.SemaphoreType.DMA(())],
        compiler_params=_params(("arbitrary", "arbitrary")),
        name=f"dispatch{layer}",
    )(pos, meta[META_FILL:META_FILL + 1], h1, gain2.reshape(1, d), mods, mods, info)

    ys = pl.pallas_call(
        functools.partial(_gmoe_kernel, layer=layer),
        grid_spec=pltpu.PrefetchScalarGridSpec(
            num_scalar_prefetch=7,
            grid=(n_tiles,),
            in_specs=[any_spec] * 4,
            out_specs=any_spec,
            scratch_shapes=[pltpu.VMEM((2, SORT_TILE, ROW_WIDTH), F32), pltpu.VMEM((2, SORT_TILE, d), F32),
                            pltpu.VMEM((SORT_TILE, d), F32),
                            pltpu.VMEM((2, 2, d, D_EXPERT), F32), pltpu.VMEM((2, 2, d, D_EXPERT), F32),
                            pltpu.VMEM((2, 2, D_EXPERT, d), F32),
                            pltpu.SemaphoreType.DMA((2,)), pltpu.SemaphoreType.DMA((2,)),
                            pltpu.SemaphoreType.DMA(()), pltpu.SemaphoreType.DMA((2,))]),
        out_shape=jax.ShapeDtypeStruct((n_sorted, 1, d), F32),
        compiler_params=_params(("arbitrary",)),
        name=f"experts{layer}",
    )(meta[META_EXPERT_LO], meta[META_EXPERT_HI], meta[META_N_USED, :1], meta[META_FIRST], meta[META_WSLOT],
      meta[META_NEXT_LO], meta[META_NEXT_HI], xs, w_gate, w_up, w_down)
    return ys, pos


def _combine(h1, ys, pos, mods, layer, tm):
    bsz, seq, d = h1.shape
    nt = seq // tm
    n_steps = bsz * nt
    tile = lambda w: pl.BlockSpec((1, tm, w), lambda b, i: (b, i, 0))
    return pl.pallas_call(
        functools.partial(_combine_kernel, tm=tm, n_steps=n_steps),
        grid=(bsz, nt),
        in_specs=[_pos_spec(tm, nt, n_steps, 0), _pos_spec(tm, nt, n_steps, 1), tile(d),
                  _mod_spec(layer, lambda b, i: b, 5), pl.BlockSpec(memory_space=pl.ANY)],
        out_specs=tile(d),
        out_shape=jax.ShapeDtypeStruct((bsz, seq, d), F32),
        scratch_shapes=[pltpu.VMEM((2, tm, d), F32), pltpu.SemaphoreType.DMA((2,))],
        compiler_params=_params(("arbitrary", "arbitrary")),
        name=f"combine{layer}",
    )(pos, pos, h1, mods, ys)


def kernel(x, c, ctx, c_ctx, mod_w, mod_b, norm1_g, norm2_g, even_w_in, q_gain, k_gain, pool_w, pool_scale,
           even_w_out, odd_w_in, sg_gain, sg_w, sg_b, conv_w, odd_w_out, router_g_w, router_g_b,
           router_e_w, router_e_b, w_gate, w_up, w_down):
    bsz, seq, d = x.shape
    tm = min(512, seq)
    tm_proj = min(1024, seq)
    cond = jnp.zeros((MOD_ROWS, d), F32).at[:bsz].set(c).at[bsz].set(c_ctx)
    mods = _adaln(cond, mod_w, mod_b).reshape(mod_w.shape[0], MOD_ROWS, 6, 1, d)

    q, k, v, p = _inproj0(x, mods, norm1_g[0], even_w_in[0], q_gain[0], k_gain[0], tm_proj)
    kc, vc = _inproj0_ctx(ctx, mods, bsz, norm1_g[0], even_w_in[0][:, ATTN_WIDTH:ATTN_WIDTH + 2 * KV_WIDTH],
                          k_gain[0])
    o = _attention(q, k, v, kc, vc, q_gain[0], k_gain[0], tq=min(512, seq), tk=min(2048, seq))
    router0 = _router_operands(router_g_w[0], router_g_b[0], router_e_w[0], router_e_b[0])
    h1, info, cnt = _out0(o, p, x, mods, norm2_g[0], pool_w[0], pool_scale[0], even_w_out[0], router0, tm_proj)
    ys, pos = _experts_sorted(h1, info, cnt, mods, 0, norm2_g[0], w_gate, w_up, w_down, tm)

    h, yc, z, bg = _inproj1(h1, ys, pos, mods, norm1_g[1], odd_w_in[0], sg_gain[0], sg_w[0], sg_b[0], tm)
    router1 = _router_operands(router_g_w[1], router_g_b[1], router_e_w[1], router_e_b[1])
    h1, info, cnt = _out1(yc, z, bg, h, mods, norm2_g[1], conv_w[0], odd_w_out[0], router1, tm_proj)
    ys, pos = _experts_sorted(h1, info, cnt, mods, 1, norm2_g[1], w_gate, w_up, w_down, tm)
    return _combine(h1, ys, pos, mods, 1, tm)
```
